```python
import jax
import jax.numpy as jnp
from jax import lax
import numpy as np


D_MODEL = 1024
BATCH = 8
SEQ = 8192
DEPTH = 2

GRID_W = 64
CTX_LEN = 256
N_BRANCH = 4
BRANCH_W = 256
EPS = 1e-6

ML_HEADS = 4
ML_DH = BRANCH_W // ML_HEADS
ML_CHUNK = 128
ML_COLS = 4 * BRANCH_W + 4 * ML_HEADS

MLA_HEADS = 4
Q_LORA = 256
KV_LORA = 128
QK_NOPE = 64
QK_ROPE = 32
V_DIM = BRANCH_W // MLA_HEADS
ATT_BLOCK = 128
ROPE_THETA = 10000.0
MLA_COLS = Q_LORA + KV_LORA + QK_ROPE

POOL_WINDOWS = (2, 4, 8, 16)
POOL_GROUP = BRANCH_W // len(POOL_WINDOWS)
POOL_COLS = BRANCH_W

CONV_WIDTH = 3
CONV_COLS = 3 * BRANCH_W

IN_COLS = ML_COLS + MLA_COLS + POOL_COLS + CONV_COLS

N_EXPERTS = 64
TOP_K = 6
EXPERT_FF = 256
SHARED_FF = 256
ROUTE_SCALE = 2.5
MOE_BLOCK = 128

kernel_name = 'hybrid_gated_mlstm_mla_pool_conv_moe_dit'


def rms_norm(x, g):
    xf = x.astype(jnp.float32)
    y = xf * lax.rsqrt(jnp.mean(xf * xf, axis=-1, keepdims=True) + EPS)
    return (y * g.astype(jnp.float32)).astype(x.dtype)


def rope_angles(rows):
    row = jnp.repeat(jnp.arange(rows), GRID_W).astype(jnp.float32)
    col = jnp.tile(jnp.arange(GRID_W), rows).astype(jnp.float32)
    n_freq = QK_ROPE // 4
    inv = ROPE_THETA ** (-jnp.arange(n_freq, dtype=jnp.float32) / n_freq)
    return row[:, None] * inv, col[:, None] * inv


def rotate_half(x, ang):
    n = ang.shape[-1]
    cos = jnp.cos(ang)[None, :, None, :]
    sin = jnp.sin(ang)[None, :, None, :]
    x1, x2 = x[..., :n], x[..., n:]
    return jnp.concatenate([x1 * cos - x2 * sin, x2 * cos + x1 * sin], axis=-1).astype(x.dtype)


def axial_rope(x, ang):
    half = x.shape[-1] // 2
    return jnp.concatenate([rotate_half(x[..., :half], ang[0]), rotate_half(x[..., half:], ang[1])], axis=-1)


def mlstm_scan(q, k, v, li, lf, state, emit):
    bsz, nh, t_len, dh = q.shape
    n_chunks = t_len // ML_CHUNK

    def to_chunks(a):
        a = a.reshape(a.shape[:2] + (n_chunks, ML_CHUNK) + a.shape[3:])
        return jnp.moveaxis(a, 2, 0)

    tril = jnp.tril(jnp.ones((ML_CHUNK, ML_CHUNK), dtype=bool))

    def step(carry, inp):
        c_st, n_st, m_st = carry
        qc, kc, vc, lic, lfc = inp
        b = jnp.cumsum(lfc, axis=-1)
        b_end = b[..., -1]
        w_log = b_end[..., None] - b + lic
        m_new = jnp.maximum(b_end + m_st, jnp.max(w_log, axis=-1))
        decay = jnp.exp(b_end + m_st - m_new)
        w_s = jnp.exp(w_log - m_new[..., None])
        c_new = decay[..., None, None] * c_st + jnp.einsum('bhs,bhsd,bhse->bhde', w_s, kc, vc)
        n_new = decay[..., None] * n_st + jnp.einsum('bhs,bhsd->bhd', w_s, kc)
        if not emit:
            return (c_new, n_new, m_new), None
        d_log = jnp.where(tril, b[..., :, None] - b[..., None, :] + lic[..., None, :], -jnp.inf)
        inter = b + m_st[..., None]
        m_t = jnp.maximum(inter, jnp.max(d_log, axis=-1))
        a_inter = jnp.exp(inter - m_t)
        w_ts = jnp.exp(d_log - m_t[..., None]) * jnp.einsum('bhtd,bhsd->bhts', qc, kc)
        num = a_inter[..., None] * jnp.einsum('bhtd,bhde->bhte', qc, c_st) + jnp.einsum('bhts,bhse->bhte', w_ts, vc)
        den = a_inter * jnp.einsum('bhtd,bhd->bht', qc, n_st) + jnp.sum(w_ts, axis=-1)
        h = num / jnp.maximum(jnp.abs(den), jnp.exp(-m_t))[..., None]
        return (c_new, n_new, m_new), h

    state, hs = lax.scan(step, state, tuple(to_chunks(a) for a in (q, k, v, li, lf)))
    if not emit:
        return state, None
    return state, jnp.moveaxis(hs, 0, 2).reshape(bsz, nh, t_len, dh)


def mlstm_branch(p_lat, p_ctx, gate_bias, head_gain, emit_ctx):
    def prep(p):
        bsz, t_len, _ = p.shape
        q, k, v, o, g = jnp.split(p, [BRANCH_W, 2 * BRANCH_W, 3 * BRANCH_W, 4 * BRANCH_W], axis=-1)

        def heads(a):
            return a.reshape(bsz, t_len, ML_HEADS, ML_DH).transpose(0, 2, 1, 3).astype(jnp.float32)

        g = (g.astype(jnp.float32) + gate_bias).reshape(bsz, t_len, 4, ML_HEADS).transpose(2, 0, 3, 1)
        return heads(q), heads(k) * (ML_DH ** -0.5), heads(v), o, g

    ql, kl, vl, ol, gl = prep(p_lat)
    qc, kc, vc, oc, gc = prep(p_ctx)
    bsz = ql.shape[0]
    zero_state = (jnp.zeros((bsz, ML_HEADS, ML_DH, ML_DH), jnp.float32),
                  jnp.zeros((bsz, ML_HEADS, ML_DH), jnp.float32),
                  jnp.zeros((bsz, ML_HEADS), jnp.float32))
    h_lat, h_ctx = None, None
    for rev in (False, True):
        gi, gf = (2, 3) if rev else (0, 1)
        tf = (lambda a: jnp.flip(a, axis=2)) if rev else (lambda a: a)
        st_ctx, hc = mlstm_scan(tf(qc), tf(kc), tf(vc), tf(gc[gi]), tf(jax.nn.log_sigmoid(gc[gf])), zero_state, emit_ctx)
        _, hl = mlstm_scan(tf(ql), tf(kl), tf(vl), tf(gl[gi]), tf(jax.nn.log_sigmoid(gl[gf])), st_ctx, True)
        h_lat = tf(hl) if h_lat is None else h_lat + tf(hl)
        if emit_ctx:
            h_ctx = tf(hc) if h_ctx is None else h_ctx + tf(hc)

    def finish(h, o):
        bsz_, nh, t_len, dh = h.shape
        h = rms_norm(h.transpose(0, 2, 1, 3), head_gain.reshape(ML_HEADS, ML_DH)).reshape(bsz_, t_len, BRANCH_W)
        return (h * jax.nn.sigmoid(o.astype(jnp.float32))).astype(o.dtype)

    y_lat = finish(h_lat, ol)
    y_ctx = finish(h_ctx, oc) if emit_ctx else None
    return y_lat, y_ctx


def mla_project(p, ang, want_q, g_cq, g_ckv, w_uq, w_ukv, g_qn, g_kn):
    bsz, t_len, _ = p.shape
    cq, ckv, kr = jnp.split(p, [Q_LORA, Q_LORA + KV_LORA], axis=-1)
    kv = (rms_norm(ckv, g_ckv) @ w_ukv).reshape(bsz, t_len, MLA_HEADS, QK_NOPE + V_DIM)
    k_nope, v = kv[..., :QK_NOPE], kv[..., QK_NOPE:]
    k_rope = jnp.broadcast_to(kr[:, :, None, :], (bsz, t_len, MLA_HEADS, QK_ROPE))
    k = rms_norm(jnp.concatenate([k_nope, k_rope], axis=-1), g_kn)
    if ang is not None:
        k = jnp.concatenate([k[..., :QK_NOPE], axial_rope(k[..., QK_NOPE:], ang)], axis=-1)
    q = None
    if want_q:
        q = (rms_norm(cq, g_cq) @ w_uq).reshape(bsz, t_len, MLA_HEADS, QK_NOPE + QK_ROPE)
        q = rms_norm(q, g_qn)
        if ang is not None:
            q = jnp.concatenate([q[..., :QK_NOPE], axial_rope(q[..., QK_NOPE:], ang)], axis=-1)
    return q, k, v


def attend_blocks(q, k, v):
    bsz, tq, nh, dq = q.shape
    nb = tq // ATT_BLOCK
    qb = jnp.moveaxis(q.reshape(bsz, nb, ATT_BLOCK, nh, dq), 1, 0)
    scale = dq ** -0.5

    def one(q_blk):
        s = jnp.einsum('bqhd,bkhd->bhqk', q_blk, k, preferred_element_type=jnp.float32) * scale
        p = jax.nn.softmax(s, axis=-1).astype(v.dtype)
        return jnp.einsum('bhqk,bkhd->bqhd', p, v)

    out = lax.map(one, qb)
    return jnp.moveaxis(out, 0, 1).reshape(bsz, tq, nh, v.shape[-1])


def pool_mix(u, w_pool, pool_scale):
    bsz, t_len, cw = u.shape
    uf = u.astype(jnp.float32)
    cs = jnp.concatenate([jnp.zeros((bsz, 1, cw), jnp.float32), jnp.cumsum(uf, axis=1)], axis=1)
    t = jnp.arange(t_len)
    outs = []
    for gi, w in enumerate(POOL_WINDOWS):
        lo = jnp.clip(t - w // 2, 0, t_len)
        hi = jnp.clip(t + w // 2, 0, t_len)
        sl = slice(gi * POOL_GROUP, (gi + 1) * POOL_GROUP)
        seg = cs[:, :, sl]
        mean = (seg[:, hi] - seg[:, lo]) / (hi - lo).astype(jnp.float32)[None, :, None]
        outs.append(mean - uf[:, :, sl])
    y = jnp.stack(outs, axis=2)
    y = jnp.einsum('btgc,gcd->btgd', y, w_pool.astype(jnp.float32)).reshape(bsz, t_len, cw)
    return (y * pool_scale).astype(u.dtype)


def short_conv_mix(u, gate_b, gate_c, w_conv, b_conv):
    t_len = u.shape[1]
    pad = CONV_WIDTH // 2
    z = jnp.pad(gate_c * u, ((0, 0), (pad, pad), (0, 0)))
    y = b_conv
    for j in range(CONV_WIDTH):
        y = y + z[:, j:j + t_len] * w_conv[j]
    return gate_b * y


def gated_merge(h, ys, w_gate, b_gate, w_branch, w_out):
    acc = None
    for i, y in enumerate(ys):
        term = jax.nn.sigmoid(h @ w_gate[i] + b_gate[i]) * (y @ w_branch[i])
        acc = term if acc is None else acc + term
    return acc @ w_out


def moe_ffn(h, w_router, b_router, w1, w3, w2, ws1, ws3, ws2):
    n_tok, d = h.shape
    scores = jax.nn.sigmoid((h @ w_router).astype(jnp.float32))
    _, idx = lax.top_k(scores + b_router.astype(jnp.float32), TOP_K)
    s_sel = jnp.take_along_axis(scores, idx, axis=-1)
    wts = s_sel / jnp.sum(s_sel, axis=-1, keepdims=True) * ROUTE_SCALE

    n_assign = n_tok * TOP_K
    e_flat = idx.reshape(-1)
    tok_flat = jnp.repeat(jnp.arange(n_tok, dtype=jnp.int32), TOP_K)
    order = jnp.argsort(e_flat)
    e_s, tok_s, w_s = e_flat[order], tok_flat[order], wts.reshape(-1)[order]
    counts = jnp.bincount(e_flat, length=N_EXPERTS)
    padded = (counts + MOE_BLOCK - 1) // MOE_BLOCK * MOE_BLOCK
    pad_end = jnp.cumsum(padded)
    pad_start = pad_end - padded
    grp_start = jnp.cumsum(counts) - counts
    dest = pad_start[e_s] + (jnp.arange(n_assign) - grp_start[e_s])
    n_blocks = -(-n_assign // MOE_BLOCK) + N_EXPERTS
    n_rows = n_blocks * MOE_BLOCK
    row_tok = jnp.full((n_rows,), n_tok, jnp.int32).at[dest].set(tok_s)
    row_w = jnp.zeros((n_rows,), jnp.float32).at[dest].set(w_s)
    blk_e = jnp.clip(jnp.searchsorted(pad_end, jnp.arange(n_blocks) * MOE_BLOCK, side='right'), 0, N_EXPERTS - 1)
    h_pad = jnp.concatenate([h, jnp.zeros((1, d), h.dtype)], axis=0)

    def step(acc, inp):
        toks, wr, e = inp
        xb = h_pad[toks]
        a = jax.nn.silu(xb @ w1[e]) * (xb @ w3[e])
        yb = (a @ w2[e]) * wr[:, None]
        return acc.at[toks].add(yb.astype(jnp.float32)), None

    acc0 = jnp.zeros((n_tok + 1, d), jnp.float32)
    acc, _ = lax.scan(step, acc0, (row_tok.reshape(n_blocks, MOE_BLOCK), row_w.reshape(n_blocks, MOE_BLOCK), blk_e))
    shared = (jax.nn.silu(h @ ws1) * (h @ ws3)) @ ws2
    return (acc[:n_tok] + shared.astype(jnp.float32)).astype(h.dtype)


def hybrid_layer(x, ctx, c, c_ctx, ang, emit_ctx, lp):
    bsz, t_len, d = x.shape
    mod = (jax.nn.silu(c) @ lp['w_mod'] + lp['b_mod'])[:, None, :]
    mod_c = (jax.nn.silu(c_ctx) @ lp['w_mod'] + lp['b_mod'])[None, None, :]
    sh1, sc1, gt1, sh2, sc2, gt2 = jnp.split(mod, 6, axis=-1)
    csh1, csc1, cgt1, csh2, csc2, cgt2 = jnp.split(mod_c, 6, axis=-1)

    hx = rms_norm(x, lp['g_mix']) * (1 + sc1) + sh1
    hc = rms_norm(ctx, lp['g_mix']) * (1 + csc1) + csh1
    px = hx @ lp['w_in']
    pc = hc @ lp['w_in']
    cuts = [ML_COLS, ML_COLS + MLA_COLS, ML_COLS + MLA_COLS + POOL_COLS]
    ml_x, mla_x, pool_x, conv_x = jnp.split(px, cuts, axis=-1)
    ml_c, mla_c, pool_c, conv_c = jnp.split(pc, cuts, axis=-1)

    y_ml, yc_ml = mlstm_branch(ml_x, ml_c, lp['ml_gate_bias'], lp['ml_head_gain'], emit_ctx)

    mla_w = (lp['mla_g_cq'], lp['mla_g_ckv'], lp['mla_w_uq'], lp['mla_w_ukv'], lp['mla_g_qn'], lp['mla_g_kn'])
    q_l, k_l, v_l = mla_project(mla_x, ang, True, *mla_w)
    q_c, k_c, v_c = mla_project(mla_c, None, emit_ctx, *mla_w)
    k_all = jnp.concatenate([k_c, k_l], axis=1)
    v_all = jnp.concatenate([v_c, v_l], axis=1)
    y_mla = attend_blocks(q_l, k_all, v_all).reshape(bsz, t_len, BRANCH_W)

    y_pool = pool_mix(pool_x, lp['pool_w'], lp['pool_scale'])
    u_x, b_x, c_x = jnp.split(conv_x, 3, axis=-1)
    y_conv = short_conv_mix(u_x, b_x, c_x, lp['conv_w'], lp['conv_b'])

    merge_w = (lp['w_gate'], lp['b_gate'], lp['w_branch'], lp['w_out'])
    x = x + gt1 * gated_merge(hx, (y_ml, y_mla, y_pool, y_conv), *merge_w)
    if emit_ctx:
        yc_mla = attend_blocks(q_c, k_c, v_c).reshape(bsz, -1, BRANCH_W)
        yc_pool = pool_mix(pool_c, lp['pool_w'], lp['pool_scale'])
        u_c, b_c, c_c = jnp.split(conv_c, 3, axis=-1)
        yc_conv = short_conv_mix(u_c, b_c, c_c, lp['conv_w'], lp['conv_b'])
        ctx = ctx + cgt1 * gated_merge(hc, (yc_ml, yc_mla, yc_pool, yc_conv), *merge_w)

    hx2 = rms_norm(x, lp['g_ffn']) * (1 + sc2) + sh2
    tokens = hx2.reshape(-1, d)
    if emit_ctx:
        hc2 = rms_norm(ctx, lp['g_ffn']) * (1 + csc2) + csh2
        tokens = jnp.concatenate([tokens, hc2.reshape(-1, d)], axis=0)
    f = moe_ffn(tokens, lp['moe_w_router'], lp['moe_b_router'], lp['moe_w1'], lp['moe_w3'], lp['moe_w2'],
                lp['moe_ws1'], lp['moe_ws3'], lp['moe_ws2'])
    x = x + gt2 * f[:bsz * t_len].reshape(bsz, t_len, d)
    if emit_ctx:
        ctx = ctx + cgt2 * f[bsz * t_len:].reshape(ctx.shape)
    return x, ctx


def setup_inputs(seed: int = 0) -> dict:
    key = jax.random.key(seed)
    ks = iter(jax.random.split(key, 40))

    def nrm(shape, scale):
        return jax.random.normal(next(ks), shape, jnp.float32) * scale

    def gain(shape):
        return 1.0 + nrm(shape, 0.02)

    L, D = DEPTH, D_MODEL
    f_bias = jnp.linspace(3.0, 6.0, ML_HEADS, dtype=jnp.float32)
    z_bias = jnp.zeros((ML_HEADS,), jnp.float32)
    gate_base = jnp.concatenate([z_bias, f_bias, z_bias, f_bias])
    return {
        'x': nrm((BATCH, SEQ, D), 1.0),
        'c': nrm((BATCH, D), 1.0),
        'ctx': nrm((BATCH, CTX_LEN, D), 1.0),
        'c_ctx': nrm((D,), 1.0),
        'w_mod': nrm((L, D, 6 * D), 0.5 * D ** -0.5),
        'b_mod': nrm((L, 6 * D), 0.02),
        'g_mix': gain((L, D)),
        'g_ffn': gain((L, D)),
        'w_in': nrm((L, D, IN_COLS), D ** -0.5),
        'ml_gate_bias': gate_base + nrm((L, 4 * ML_HEADS), 0.1),
        'ml_head_gain': gain((L, BRANCH_W)),
        'mla_g_cq': gain((L, Q_LORA)),
        'mla_g_ckv': gain((L, KV_LORA)),
        'mla_w_uq': nrm((L, Q_LORA, MLA_HEADS * (QK_NOPE + QK_ROPE)), Q_LORA ** -0.5),
        'mla_w_ukv': nrm((L, KV_LORA, MLA_HEADS * (QK_NOPE + V_DIM)), KV_LORA ** -0.5),
        'mla_g_qn': gain((L, QK_NOPE + QK_ROPE)),
        'mla_g_kn': gain((L, QK_NOPE + QK_ROPE)),
        'pool_w': nrm((L, len(POOL_WINDOWS), POOL_GROUP, POOL_GROUP), POOL_GROUP ** -0.5),
        'pool_scale': gain((L, BRANCH_W)),
        'conv_w': nrm((L, CONV_WIDTH, BRANCH_W), CONV_WIDTH ** -0.5),
        'conv_b': nrm((L, BRANCH_W), 0.02),
        'w_gate': nrm((L, N_BRANCH, D, D), D ** -0.5),
        'b_gate': nrm((L, N_BRANCH, D), 0.02),
        'w_branch': nrm((L, N_BRANCH, BRANCH_W, D), BRANCH_W ** -0.5),
        'w_out': nrm((L, D, D), D ** -0.5),
        'moe_w_router': nrm((L, D, N_EXPERTS), D ** -0.5),
        'moe_b_router': nrm((L, N_EXPERTS), 0.01),
        'moe_w1': nrm((L, N_EXPERTS, D, EXPERT_FF), D ** -0.5),
        'moe_w3': nrm((L, N_EXPERTS, D, EXPERT_FF), D ** -0.5),
        'moe_w2': nrm((L, N_EXPERTS, EXPERT_FF, D), EXPERT_FF ** -0.5),
        'moe_ws1': nrm((L, D, SHARED_FF), D ** -0.5),
        'moe_ws3': nrm((L, D, SHARED_FF), D ** -0.5),
        'moe_ws2': nrm((L, SHARED_FF, D), SHARED_FF ** -0.5),
    }


def reference(x, c, ctx, c_ctx, w_mod, b_mod, g_mix, g_ffn, w_in, ml_gate_bias, ml_head_gain,
              mla_g_cq, mla_g_ckv, mla_w_uq, mla_w_ukv, mla_g_qn, mla_g_kn, pool_w, pool_scale,
              conv_w, conv_b, w_gate, b_gate, w_branch, w_out, moe_w_router, moe_b_router,
              moe_w1, moe_w3, moe_w2, moe_ws1, moe_ws3, moe_ws2):
    rows = x.shape[1] // GRID_W
    ang = rope_angles(rows)
    for l in range(DEPTH):
        emit_ctx = l < DEPTH - 1
        lp = {
            'w_mod': w_mod[l], 'b_mod': b_mod[l], 'g_mix': g_mix[l], 'g_ffn': g_ffn[l], 'w_in': w_in[l],
            'ml_gate_bias': ml_gate_bias[l], 'ml_head_gain': ml_head_gain[l],
            'mla_g_cq': mla_g_cq[l], 'mla_g_ckv': mla_g_ckv[l], 'mla_w_uq': mla_w_uq[l], 'mla_w_ukv': mla_w_ukv[l],
            'mla_g_qn': mla_g_qn[l], 'mla_g_kn': mla_g_kn[l],
            'pool_w': pool_w[l], 'pool_scale': pool_scale[l], 'conv_w': conv_w[l], 'conv_b': conv_b[l],
            'w_gate': w_gate[l], 'b_gate': b_gate[l], 'w_branch': w_branch[l], 'w_out': w_out[l],
            'moe_w_router': moe_w_router[l], 'moe_b_router': moe_b_router[l],
            'moe_w1': moe_w1[l], 'moe_w3': moe_w3[l], 'moe_w2': moe_w2[l],
            'moe_ws1': moe_ws1[l], 'moe_ws3': moe_ws3[l], 'moe_ws2': moe_ws2[l],
        }
        x, ctx = hybrid_layer(x, ctx, c, c_ctx, ang, emit_ctx, lp)
    return x
```

```python
import functools
import math

import jax
import jax.numpy as jnp
from jax import lax
from jax.experimental import pallas as pl
from jax.experimental.pallas import tpu as pltpu

GRID_W = 64
BRANCH_W = 256
EPS = 1e-6
ML_HEADS = 4
ML_DH = BRANCH_W // ML_HEADS
ML_CHUNK = 128
MLA_HEADS = 4
Q_LORA = 256
KV_LORA = 128
QK_NOPE = 64
QK_ROPE = 32
QK_DIM = QK_NOPE + QK_ROPE
V_DIM = BRANCH_W // MLA_HEADS
ROPE_THETA = 10000.0
POOL_WINDOWS = (2, 4, 8, 16)
POOL_GROUP = BRANCH_W // len(POOL_WINDOWS)
POOL_HALO = max(POOL_WINDOWS) // 2
N_EXPERTS = 64
TOP_K = 6
EXPERT_FF = 256
ROUTE_SCALE = 2.5

LANES = 128
SUBLANES = 8
HEAD_SLOT = LANES
ROW_TILE = 256
ATT_TQ = 256
ATT_TK = 1024
EXPERT_ROWS = 512
VMEM_LIMIT = 56 * 1024 * 1024

F32 = jnp.float32
BF16 = jnp.bfloat16
NT_DIMS = (((1,), (1,)), ((), ()))
TN_DIMS = (((0,), (0,)), ((), ()))


def _cparams(*sem):
    return pltpu.CompilerParams(dimension_semantics=sem, vmem_limit_bytes=VMEM_LIMIT)


def _const_spec(shape):
    nd = len(shape)
    return pl.BlockSpec(shape, lambda *_: (0,) * nd, pipeline_mode=pl.Buffered(1))


def _dot(a, b):
    return jnp.dot(a, b, preferred_element_type=F32)


def _split_dot(a_f32, ones_bf16):
    hi = a_f32.astype(BF16)
    r1 = a_f32 - hi.astype(F32)
    mid = r1.astype(BF16)
    lo = (r1 - mid.astype(F32)).astype(BF16)
    return _dot(hi, ones_bf16) + _dot(mid, ones_bf16) + _dot(lo, ones_bf16)


def _split_dot_left(ones_bf16, a_f32):
    hi = a_f32.astype(BF16)
    r1 = a_f32 - hi.astype(F32)
    mid = r1.astype(BF16)
    lo = (r1 - mid.astype(F32)).astype(BF16)
    return _dot(ones_bf16, hi) + _dot(ones_bf16, mid) + _dot(ones_bf16, lo)


def _rms_rows(x):
    return x * lax.rsqrt(jnp.mean(x * x, axis=-1, keepdims=True) + EPS)


def _mod_kernel(c_ref, w_ref, b_ref, o_ref):
    c = c_ref[...]
    a = (c * jax.nn.sigmoid(c)).astype(BF16)
    o_ref[0] = _dot(a, w_ref[0].astype(BF16)) + b_ref[0]


def _modulation(cc, w_mod, b_mod):
    depth, d, d6 = w_mod.shape
    rows = cc.shape[0]
    tn = 1536
    return pl.pallas_call(
        _mod_kernel,
        grid=(depth, d6 // tn),
        in_specs=[
            pl.BlockSpec((rows, d), lambda l, n: (0, 0)),
            pl.BlockSpec((1, d, tn), lambda l, n: (l, 0, n)),
            pl.BlockSpec((1, 1, tn), lambda l, n: (l, 0, n)),
        ],
        out_specs=pl.BlockSpec((1, rows, tn), lambda l, n: (l, 0, n)),
        out_shape=jax.ShapeDtypeStruct((depth, rows, d6), F32),
        compiler_params=_cparams("parallel", "parallel"),
        name="modulation",
    )(cc, w_mod, b_mod.reshape(depth, 1, d6))


def _group_mean_sq(x, ones_bd, width):
    return _split_dot(x * x, ones_bd) * (1.0 / width)


def _in_kernel(x_ref, mod_ref, gmix_ref, w_ref, gcq_ref, gckv_ref, wuq_ref, wuk_ref, wuv_ref,
               gq_ref, gk_ref, ones_ref, cos_ref, sa_ref, sb_ref,
               hx_ref, qkv_ref, o_ref, g_ref, pc_ref, q_ref, k_ref, v_ref):
    x = x_ref[0]
    shift = mod_ref[0, 0:1, :]
    scale = mod_ref[0, 1:2, :]
    hx = (_rms_rows(x) * gmix_ref[...]) * (1.0 + scale) + shift
    hxb = hx.astype(BF16)
    hx_ref[0] = hxb
    p = _dot(hxb, w_ref[...])
    bw = BRANCH_W
    qkv_ref[0, :, 0:bw] = p[:, 0:bw].astype(BF16)
    qkv_ref[0, :, bw:2 * bw] = (p[:, bw:2 * bw] * (ML_DH ** -0.5)).astype(BF16)
    qkv_ref[0, :, 2 * bw:3 * bw] = p[:, 2 * bw:3 * bw].astype(BF16)
    o_ref[0] = p[:, 3 * bw:4 * bw].astype(BF16)
    pc_ref[0] = p[:, 1024:2048]
    cq = p[:, 2048:2048 + Q_LORA]
    ckv = p[:, 2304:2304 + KV_LORA]
    kr4 = p[:, 2560:3072]
    g_ref[0] = p[:, 3072:3200]

    cqn = (_rms_rows(cq) * gcq_ref[...]).astype(BF16)
    ckvn = (_rms_rows(ckv) * gckv_ref[...]).astype(BF16)
    q_pre = _dot(cqn, wuq_ref[...])
    k_pre = _dot(ckvn, wuk_ref[...]) + kr4
    v_ref[0] = _dot(ckvn, wuv_ref[...]).astype(BF16)

    ones_bd = ones_ref[...]
    cos = jnp.concatenate([cos_ref[...]] * MLA_HEADS, axis=1)
    sa = jnp.concatenate([sa_ref[...]] * MLA_HEADS, axis=1)
    sb = jnp.concatenate([sb_ref[...]] * MLA_HEADS, axis=1)
    width = MLA_HEADS * HEAD_SLOT
    half = QK_ROPE // 4

    def norm_rope(t, gain):
        t = t * lax.rsqrt(_group_mean_sq(t, ones_bd, QK_DIM) + EPS) * gain
        return t * cos + pltpu.roll(t, width - half, 1) * sa + pltpu.roll(t, half, 1) * sb

    q = norm_rope(q_pre, gq_ref[...]) * (QK_DIM ** -0.5 * math.log2(math.e))
    q_ref[0] = q.astype(BF16)
    k_ref[0] = norm_rope(k_pre, gk_ref[...]).astype(BF16)


def _in_proj(xs, mod, lw, rope, n_lat_tiles):
    bsz, s, d = xs.shape
    tm = ROW_TILE
    nt = s // tm
    wcols = lw["w_big"].shape[1]
    slot_w = MLA_HEADS * HEAD_SLOT

    def row_spec(width):
        return pl.BlockSpec((1, tm, width), lambda t, b: (b, t, 0))

    def tab_spec():
        return pl.BlockSpec((tm, HEAD_SLOT), lambda t, b: (t, 0))

    outs = [
        (d, BF16), (3 * BRANCH_W, BF16), (BRANCH_W, BF16), (LANES, F32), (1024, F32),
        (slot_w, BF16), (slot_w, BF16), (slot_w, BF16),
    ]
    return pl.pallas_call(
        _in_kernel,
        grid=(nt, bsz),
        in_specs=[
            row_spec(d),
            pl.BlockSpec((1, 6, d), lambda t, b: (jnp.where(t >= n_lat_tiles, bsz, b), 0, 0)),
            _const_spec((1, d)),
            _const_spec((d, wcols)),
            _const_spec((1, Q_LORA)),
            _const_spec((1, KV_LORA)),
            _const_spec((Q_LORA, slot_w)),
            _const_spec((KV_LORA, slot_w)),
            _const_spec((KV_LORA, slot_w)),
            _const_spec((1, slot_w)),
            _const_spec((1, slot_w)),
            _const_spec((slot_w, slot_w)),
            tab_spec(), tab_spec(), tab_spec(),
        ],
        out_specs=[row_spec(w) for w, _ in outs],
        out_shape=[jax.ShapeDtypeStruct((bsz, s, w), dt) for w, dt in outs],
        compiler_params=_cparams("parallel", "parallel"),
        name="norm_in_proj",
    )(xs, mod, lw["g_mix"], lw["w_big"], lw["g_cq"], lw["g_ckv"], lw["w_uq"], lw["w_uk"], lw["w_uv"],
      lw["g_qn"], lw["g_kn"], rope["ones_slot"], rope["cos"], rope["sin_a"], rope["sin_b"])


def _log_sigmoid(x):
    return jnp.minimum(x, 0.0) - jnp.log1p(jnp.exp(-jnp.abs(x)))


def _mlstm_kernel(qkv_f, qkv_b, g_f, g_b, bias_ref, hf_ref, hb_ref, c_sc, m_sc):
    @pl.when(pl.program_id(1) == 0)
    def _():
        c_sc[...] = jnp.zeros_like(c_sc)
        m_sc[...] = jnp.zeros_like(m_sc)

    L = ML_CHUNK
    dh = ML_DH
    row = lax.broadcasted_iota(jnp.int32, (L, L), 0)
    col = lax.broadcasted_iota(jnp.int32, (L, L), 1)
    lane = lax.broadcasted_iota(jnp.int32, (L, dh), 1)
    ones_col = jnp.where(lane == 0, 1.0, 0.0).astype(BF16)

    for d, (qkv_ref, g_ref, out_ref) in enumerate(((qkv_f, g_f, hf_ref), (qkv_b, g_b, hb_ref))):
        a = g_ref[0] + bias_ref[...]
        lf = _log_sigmoid(a)
        seen = (col <= row) if d == 0 else (col >= row)
        b_cols = _split_dot_left(seen.astype(BF16), lf)
        b_rows = b_cols.T
        a_rows = a.T
        last = L - 1 if d == 0 else 0
        outs = []
        for h in range(ML_HEADS):
            ci = d * 2 * ML_HEADS + h
            cf = ci + ML_HEADS
            st = d * ML_HEADS + h
            b_col = b_cols[:, cf:cf + 1]
            b_row = b_rows[cf:cf + 1, :]
            li_col = a[:, ci:ci + 1]
            li_row = a_rows[ci:ci + 1, :]
            b_end = b_cols[last:last + 1, cf:cf + 1]
            m_st = m_sc[st:st + 1, 0:1]
            q = qkv_ref[0, :, h * dh:(h + 1) * dh]
            k = qkv_ref[0, :, BRANCH_W + h * dh:BRANCH_W + (h + 1) * dh]
            v = qkv_ref[0, :, 2 * BRANCH_W + h * dh:2 * BRANCH_W + (h + 1) * dh]
            c_aug = c_sc[st]

            d_log = jnp.where(seen, b_col - b_row + li_row, -jnp.inf)
            inter = b_col + m_st
            m_t = jnp.maximum(inter, jnp.max(d_log, axis=1, keepdims=True))
            a_inter = jnp.exp(inter - m_t)
            s_qk = lax.dot_general(q, k, NT_DIMS, preferred_element_type=F32)
            w_ts = jnp.exp(d_log - m_t) * s_qk
            qc = _dot(q, c_aug.astype(BF16))
            num = a_inter * qc[:, 0:dh] + _dot(w_ts.astype(BF16), v)
            den = a_inter * qc[:, dh:dh + 1] + jnp.sum(w_ts, axis=1, keepdims=True)
            outs.append(num / jnp.maximum(jnp.abs(den), jnp.exp(-m_t)))

            w_log = b_end - b_col + li_col
            m_new = jnp.maximum(b_end + m_st, jnp.max(w_log, axis=0, keepdims=True))
            decay = jnp.exp(b_end + m_st - m_new)
            kw = (k.astype(F32) * jnp.exp(w_log - m_new)).astype(BF16)
            v_aug = jnp.concatenate([v, ones_col], axis=1)
            c_sc[st] = decay * c_aug + lax.dot_general(kw, v_aug, TN_DIMS, preferred_element_type=F32)
            m_sc[st:st + 1, :] = jnp.broadcast_to(m_new, (1, LANES))
        out_ref[0] = jnp.concatenate(outs, axis=1)


def _mlstm(qkv, g, gate_bias, n_lat_chunks):
    bsz, s, _ = qkv.shape
    nc = s // ML_CHUNK

    def fwd_map(b, j):
        return (b, (j + n_lat_chunks) % nc, 0)

    def bwd_map(b, j):
        return (b, nc - 1 - j, 0)

    def spec(width, imap):
        return pl.BlockSpec((1, ML_CHUNK, width), imap)

    return pl.pallas_call(
        _mlstm_kernel,
        grid=(bsz, nc),
        in_specs=[spec(3 * BRANCH_W, fwd_map), spec(3 * BRANCH_W, bwd_map),
                  spec(LANES, fwd_map), spec(LANES, bwd_map),
                  pl.BlockSpec((1, LANES), lambda b, j: (0, 0))],
        out_specs=[spec(BRANCH_W, fwd_map), spec(BRANCH_W, bwd_map)],
        out_shape=[jax.ShapeDtypeStruct((bsz, s, BRANCH_W), F32)] * 2,
        scratch_shapes=[pltpu.VMEM((2 * ML_HEADS, ML_DH, LANES), F32),
                        pltpu.VMEM((2 * ML_HEADS, LANES), F32)],
        compiler_params=_cparams("parallel", "arbitrary"),
        name="mlstm_scan",
    )(qkv, qkv, g, g, gate_bias)


def _attn_kernel(q_ref, k_ref, v_ref, o_ref, m_sc, l_sc, acc_sc, *, n_lat, n_ctx, tk):
    m_sc[...] = jnp.full_like(m_sc, -jnp.inf)
    l_sc[...] = jnp.zeros_like(l_sc)
    acc_sc[...] = jnp.zeros_like(acc_sc)

    def keys(start, size):
        for h in range(MLA_HEADS):
            sl = slice(h * HEAD_SLOT, (h + 1) * HEAD_SLOT)
            s = lax.dot_general(q_ref[0, :, sl], k_ref[0, pl.ds(start, size), sl], NT_DIMS,
                                preferred_element_type=F32)
            m_old = m_sc[h]
            m_new = jnp.maximum(m_old, jnp.max(s, axis=1, keepdims=True))
            p = jnp.exp2(s - m_new)
            alpha = jnp.exp2(m_old - m_new)
            l_sc[h] = alpha * l_sc[h] + jnp.sum(p, axis=1, keepdims=True)
            acc_sc[h] = alpha * acc_sc[h] + _dot(p.astype(BF16), v_ref[0, pl.ds(start, size), sl])
            m_sc[h] = m_new

    keys(n_lat, n_ctx)

    def chunk(c, carry):
        keys(pl.multiple_of(c * tk, tk), tk)
        return carry

    is_ctx_tile = pl.program_id(1) * ATT_TQ >= n_lat
    lax.fori_loop(0, jnp.where(is_ctx_tile, 0, n_lat // tk), chunk, 0)
    o_ref[0] = jnp.concatenate(
        [acc_sc[h][:, 0:V_DIM] / l_sc[h] for h in range(MLA_HEADS)], axis=1).astype(o_ref.dtype)


def _key_chunk(n_keys):
    return max(t for t in range(ATT_TQ, ATT_TK + 1, ATT_TQ) if n_keys % t == 0)


def _attention(q, k, v, n_lat):
    bsz, s, slot_w = q.shape
    tq = ATT_TQ
    kv_spec = pl.BlockSpec((1, s, slot_w), lambda b, t: (b, 0, 0), pipeline_mode=pl.Buffered(1))
    return pl.pallas_call(
        functools.partial(_attn_kernel, n_lat=n_lat, n_ctx=s - n_lat, tk=_key_chunk(n_lat)),
        grid=(bsz, s // tq),
        in_specs=[pl.BlockSpec((1, tq, slot_w), lambda b, t: (b, t, 0)), kv_spec, kv_spec],
        out_specs=pl.BlockSpec((1, tq, BRANCH_W), lambda b, t: (b, t, 0)),
        out_shape=jax.ShapeDtypeStruct((bsz, s, BRANCH_W), BF16),
        scratch_shapes=[pltpu.VMEM((MLA_HEADS, tq, 1), F32), pltpu.VMEM((MLA_HEADS, tq, 1), F32),
                        pltpu.VMEM((MLA_HEADS, tq, HEAD_SLOT), F32)],
        compiler_params=_cparams("parallel", "arbitrary"),
        name="mla_attention",
    )(q, k, v)


def _merge_kernel(hx_ref, hf_ref, hb_ref, o_ref, ya_ref, pc_ref, prev_ref, next_ref, x_ref, mod_ref,
                  hg_ref, ones64_ref, pw_ref, ps_ref, cw_ref, cb_ref, wg_ref, bg_ref, wb_ref, wo_ref,
                  gffn_ref, xo_ref, h2_ref, *, n_lat_tiles, n_tiles):
    t = pl.program_id(1)
    tm = ROW_TILE
    bw = BRANCH_W
    halo = POOL_HALO
    no_prev = jnp.logical_or(t == 0, t == n_lat_tiles)
    no_next = jnp.logical_or(t == n_lat_tiles - 1, t == n_tiles - 1)

    h = hf_ref[0] + hb_ref[0]
    ms = _split_dot(h * h, ones64_ref[...]) * (1.0 / ML_DH)
    y_ml = (h * lax.rsqrt(ms + EPS) * hg_ref[...]) * jax.nn.sigmoid(o_ref[0].astype(F32))

    prev = jnp.where(no_prev, 0.0, prev_ref[0])
    nxt = jnp.where(no_next, 0.0, next_ref[0])
    ext = jnp.concatenate([prev, pc_ref[0], nxt], axis=0)
    pe = ext[:, 0:bw]

    def rows(arr, off):
        return arr[halo + off:halo + off + tm, :]

    r = lax.broadcasted_iota(jnp.int32, (tm, 1), 0)
    lane = lax.broadcasted_iota(jnp.int32, (tm, bw), 1)
    centre = rows(pe, 0)
    acc = centre
    mean = jnp.zeros((tm, bw), F32)
    done = 0
    for gi, w in enumerate(POOL_WINDOWS):
        for off in list(range(-(w // 2), -done)) + list(range(max(done, 1), w // 2)):
            acc = acc + rows(pe, off)
        done = w // 2
        before = jnp.where(no_prev, jnp.minimum(r, w // 2), w // 2)
        after = jnp.where(no_next, jnp.minimum(tm - r, w // 2), w // 2)
        inv = 1.0 / (before + after).astype(F32)
        mean = jnp.where(lane >= gi * POOL_GROUP, acc * inv, mean)
    y_pool = _dot((mean - centre).astype(BF16), pw_ref[...]) * ps_ref[...]

    z = ext[:, 3 * bw:4 * bw] * ext[:, bw:2 * bw]
    conv = cb_ref[...] + rows(z, -1) * cw_ref[0:1, :] + rows(z, 0) * cw_ref[1:2, :] + rows(z, 1) * cw_ref[2:3, :]
    y_conv = rows(ext[:, 2 * bw:3 * bw], 0) * conv

    hxb = hx_ref[0]
    d = hxb.shape[1]
    gates = jax.nn.sigmoid(_dot(hxb, wg_ref[...]) + bg_ref[...])
    ys = (y_ml.astype(BF16), ya_ref[0], y_pool.astype(BF16), y_conv.astype(BF16))
    merged = None
    for i, y in enumerate(ys):
        term = gates[:, i * d:(i + 1) * d] * _dot(y, wb_ref[i])
        merged = term if merged is None else merged + term
    out = _dot(merged.astype(BF16), wo_ref[...])
    x_new = x_ref[0] + mod_ref[0, 2:3, :] * out
    xo_ref[0] = x_new
    h2_ref[0] = (_rms_rows(x_new) * gffn_ref[...]) * (1.0 + mod_ref[0, 4:5, :]) + mod_ref[0, 3:4, :]


def _merge(hx, hf, hb, o, y_mla, pc, xs, mod, lw, n_lat_tiles):
    bsz, s, d = xs.shape
    tm = ROW_TILE
    nt = s // tm
    halo = POOL_HALO
    per = tm // halo

    def row_spec(width):
        return pl.BlockSpec((1, tm, width), lambda b, t: (b, t, 0))

    return pl.pallas_call(
        functools.partial(_merge_kernel, n_lat_tiles=n_lat_tiles, n_tiles=nt),
        grid=(bsz, nt),
        in_specs=[
            row_spec(d), row_spec(BRANCH_W), row_spec(BRANCH_W), row_spec(BRANCH_W), row_spec(BRANCH_W),
            row_spec(1024),
            pl.BlockSpec((1, halo, 1024), lambda b, t: (b, jnp.maximum(t * per - 1, 0), 0)),
            pl.BlockSpec((1, halo, 1024), lambda b, t: (b, jnp.minimum((t + 1) * per, s // halo - 1), 0)),
            row_spec(d),
            pl.BlockSpec((1, 6, d), lambda b, t: (jnp.where(t >= n_lat_tiles, bsz, b), 0, 0)),
            _const_spec((1, BRANCH_W)), _const_spec((BRANCH_W, BRANCH_W)), _const_spec((BRANCH_W, BRANCH_W)),
            _const_spec((1, BRANCH_W)), _const_spec((3, BRANCH_W)), _const_spec((1, BRANCH_W)),
            _const_spec((d, 4 * d)), _const_spec((1, 4 * d)), _const_spec((4, BRANCH_W, d)),
            _const_spec((d, d)), _const_spec((1, d)),
        ],
        out_specs=[row_spec(d), row_spec(d)],
        out_shape=[jax.ShapeDtypeStruct((bsz, s, d), F32)] * 2,
        compiler_params=_cparams("parallel", "parallel"),
        name="mixer_merge",
    )(hx, hf, hb, o, y_mla, pc, pc, pc, xs, mod, lw["head_gain"], lw["ones64"], lw["pool_w"], lw["pool_scale"],
      lw["conv_w"], lw["conv_b"], lw["w_gate"], lw["b_gate"], lw["w_branch"], lw["w_out"], lw["g_ffn"])


def _router_kernel(h_ref, wr_ref, br_ref, idx_ref, rank_ref, wt_ref, cnt_ref, cnt_sc):
    @pl.when(pl.program_id(0) == 0)
    def _():
        cnt_sc[...] = jnp.zeros_like(cnt_sc)

    tm = h_ref.shape[0]
    scores = jax.nn.sigmoid(_dot(h_ref[...].astype(BF16), wr_ref[...]))
    sel = scores + br_ref[...]
    lane = lax.broadcasted_iota(jnp.int32, (tm, LANES), 1)
    member = jnp.zeros((tm, LANES), F32)
    picks = []
    for _ in range(TOP_K):
        mx = jnp.max(sel, axis=1, keepdims=True)
        ix = jnp.min(jnp.where(sel == mx, lane, LANES), axis=1, keepdims=True)
        hit = lane == ix
        picks.append((ix, hit, jnp.sum(jnp.where(hit, scores, 0.0), axis=1, keepdims=True)))
        member = jnp.where(hit, 1.0, member)
        sel = jnp.where(hit, -jnp.inf, sel)
    total = sum(p[2] for p in picks)
    row = lax.broadcasted_iota(jnp.int32, (tm, tm), 0)
    col = lax.broadcasted_iota(jnp.int32, (tm, tm), 1)
    earlier = (col < row).astype(BF16)
    ranks = _dot(earlier, member.astype(BF16)) + cnt_sc[...]
    idx_o = jnp.zeros((tm, LANES), jnp.int32)
    rank_o = jnp.zeros((tm, LANES), jnp.int32)
    wt_o = jnp.zeros((tm, LANES), F32)
    for kk, (ix, hit, sc) in enumerate(picks):
        rk = jnp.sum(jnp.where(hit, ranks, 0.0), axis=1, keepdims=True)
        idx_o = jnp.where(lane == kk, ix, idx_o)
        rank_o = jnp.where(lane == kk, rk.astype(jnp.int32), rank_o)
        wt_o = jnp.where(lane == kk, sc / total * ROUTE_SCALE, wt_o)
    idx_ref[...] = idx_o
    rank_ref[...] = rank_o
    wt_ref[...] = wt_o
    cnt_sc[...] = cnt_sc[...] + jnp.sum(member, axis=0, keepdims=True)
    cnt_ref[...] = cnt_sc[...]


def _router(h2, lw):
    n, d = h2.shape
    tm = ROW_TILE

    def spec():
        return pl.BlockSpec((tm, LANES), lambda i: (i, 0))

    return pl.pallas_call(
        _router_kernel,
        grid=(n // tm,),
        in_specs=[pl.BlockSpec((tm, d), lambda i: (i, 0)), _const_spec((d, LANES)), _const_spec((1, LANES))],
        out_specs=[spec(), spec(), spec(), pl.BlockSpec((1, LANES), lambda i: (0, 0))],
        out_shape=[jax.ShapeDtypeStruct((n, LANES), jnp.int32), jax.ShapeDtypeStruct((n, LANES), jnp.int32),
                   jax.ShapeDtypeStruct((n, LANES), F32), jax.ShapeDtypeStruct((1, LANES), F32)],
        scratch_shapes=[pltpu.VMEM((1, LANES), F32)],
        compiler_params=_cparams("arbitrary"),
        name="moe_router",
    )(h2, lw["w_router"], lw["b_router"])


def _dispatch_kernel(dest_ref, h_ref, zero_ref, xs_ref, sem):
    del zero_ref
    tm = h_ref.shape[0]

    def row_copy(r, kk):
        return pltpu.make_async_copy(h_ref.at[pl.ds(r, 1)],
                                     xs_ref.at[pl.ds(dest_ref[0, 0, r * TOP_K + kk], 1)], sem)

    def issue(r, carry):
        for kk in range(TOP_K):
            row_copy(r, kk).start()
        return carry

    def drain(r, carry):
        for kk in range(TOP_K):
            row_copy(r, kk).wait()
        return carry

    lax.fori_loop(0, tm, issue, 0)
    lax.fori_loop(0, tm, drain, 0)


def _dispatch(h2, dest, n_rows):
    n, d = h2.shape
    tm = ROW_TILE
    dest3 = dest.reshape(n // tm, 1, tm * TOP_K)
    return pl.pallas_call(
        _dispatch_kernel,
        grid=(n // tm,),
        in_specs=[pl.BlockSpec((1, 1, tm * TOP_K), lambda i: (i, 0, 0), memory_space=pltpu.SMEM),
                  pl.BlockSpec((tm, d), lambda i: (i, 0)),
                  pl.BlockSpec(memory_space=pl.ANY)],
        out_specs=pl.BlockSpec(memory_space=pl.ANY),
        out_shape=jax.ShapeDtypeStruct((n_rows, d), F32),
        scratch_shapes=[pltpu.SemaphoreType.DMA(())],
        input_output_aliases={2: 0},
        compiler_params=_cparams("arbitrary"),
        name="moe_dispatch",
    )(dest3, h2, jnp.zeros((n_rows, d), F32))


def _expert_kernel(blk_e_ref, n_act_ref, x_ref, w13_ref, w2_ref, y_ref):
    del blk_e_ref

    @pl.when(pl.program_id(0) < n_act_ref[0])
    def _():
        xb = x_ref[...].astype(BF16)
        up = _dot(xb, w13_ref[0])
        a = up[:, 0:EXPERT_FF]
        act = (a * jax.nn.sigmoid(a)) * up[:, EXPERT_FF:2 * EXPERT_FF]
        y_ref[...] = _dot(act.astype(BF16), w2_ref[0])


def _experts(xs_sorted, blk_e, n_act, lw):
    n_rows, d = xs_sorted.shape
    bm = EXPERT_ROWS

    def row_map(i, blk_e_ref, n_act_ref):
        return (jnp.minimum(i, n_act_ref[0] - 1), 0)

    def w_map(i, blk_e_ref, n_act_ref):
        return (blk_e_ref[i], 0, 0)

    return pl.pallas_call(
        _expert_kernel,
        grid_spec=pltpu.PrefetchScalarGridSpec(
            num_scalar_prefetch=2,
            grid=(n_rows // bm,),
            in_specs=[pl.BlockSpec((bm, d), row_map),
                      pl.BlockSpec((1, d, 2 * EXPERT_FF), w_map),
                      pl.BlockSpec((1, EXPERT_FF, d), w_map)],
            out_specs=pl.BlockSpec((bm, d), row_map),
        ),
        out_shape=jax.ShapeDtypeStruct((n_rows, d), F32),
        compiler_params=_cparams("arbitrary"),
        name="moe_experts",
    )(blk_e, n_act, xs_sorted, lw["w13"], lw["w2"])


def _combine_kernel(dest_ref, wt_ref, h_ref, x_ref, mod_ref, ws13_ref, ws2_ref, ys_ref, xo_ref, gbuf, sem):
    tm = h_ref.shape[0]

    def row_copy(r, kk):
        return pltpu.make_async_copy(ys_ref.at[pl.ds(dest_ref[0, 0, r * TOP_K + kk], 1)],
                                     gbuf.at[kk, pl.ds(r, 1)], sem)

    def issue(r, carry):
        for kk in range(TOP_K):
            row_copy(r, kk).start()
        return carry

    def drain(r, carry):
        for kk in range(TOP_K):
            row_copy(r, kk).wait()
        return carry

    lax.fori_loop(0, tm, issue, 0)
    hb = h_ref[...].astype(BF16)
    up = _dot(hb, ws13_ref[...])
    a = up[:, 0:EXPERT_FF]
    act = (a * jax.nn.sigmoid(a)) * up[:, EXPERT_FF:2 * EXPERT_FF]
    f = _dot(act.astype(BF16), ws2_ref[...])
    lax.fori_loop(0, tm, drain, 0)
    wt = wt_ref[...]
    for kk in range(TOP_K):
        f = f + wt[:, kk:kk + 1] * gbuf[kk]
    xo_ref[...] = x_ref[...] + mod_ref[0, 5:6, :] * f


def _combine(dest, wts, h2, xs, mod, ys_sorted, lw, tiles_per_sample, n_lat_tiles, bsz):
    n, d = h2.shape
    tm = ROW_TILE
    dest3 = dest.reshape(n // tm, 1, tm * TOP_K)

    def row_spec(width):
        return pl.BlockSpec((tm, width), lambda i: (i, 0))

    def mod_map(i):
        return (jnp.where(i % tiles_per_sample >= n_lat_tiles, bsz, i // tiles_per_sample), 0, 0)

    return pl.pallas_call(
        _combine_kernel,
        grid=(n // tm,),
        in_specs=[pl.BlockSpec((1, 1, tm * TOP_K), lambda i: (i, 0, 0), memory_space=pltpu.SMEM),
                  row_spec(LANES), row_spec(d), row_spec(d),
                  pl.BlockSpec((1, 6, d), mod_map),
                  _const_spec((d, 2 * EXPERT_FF)), _const_spec((EXPERT_FF, d)),
                  pl.BlockSpec(memory_space=pl.ANY)],
        out_specs=row_spec(d),
        out_shape=jax.ShapeDtypeStruct((n, d), F32),
        scratch_shapes=[pltpu.VMEM((TOP_K, tm, d), F32), pltpu.SemaphoreType.DMA(())],
        compiler_params=_cparams("arbitrary"),
        name="moe_combine",
    )(dest3, wts, h2, xs, mod, lw["ws13"], lw["ws2"], ys_sorted)


def _moe(h2, xs, mod, lw, tiles_per_sample, n_lat_tiles, bsz):
    n, d = h2.shape
    bm = EXPERT_ROWS
    idx, rank, wts, counts = _router(h2, lw)
    counts = counts[0, :N_EXPERTS].astype(jnp.int32)
    padded = (counts + bm - 1) // bm * bm
    pad_end = jnp.cumsum(padded)
    pad_start = pad_end - padded
    dest = pad_start[idx[:, :TOP_K]] + rank[:, :TOP_K]
    n_blocks = -(-(n * TOP_K) // bm) + N_EXPERTS
    n_act = (pad_end[-1] // bm).astype(jnp.int32)
    blk = jnp.minimum(jnp.arange(n_blocks, dtype=jnp.int32), n_act - 1)
    blk_e = jnp.clip(jnp.searchsorted(pad_end, blk * bm, side="right"), 0, N_EXPERTS - 1).astype(jnp.int32)
    xs_sorted = _dispatch(h2, dest, n_blocks * bm)
    ys_sorted = _experts(xs_sorted, blk_e, n_act.reshape(1), lw)
    return _combine(dest, wts, h2, xs, mod, ys_sorted, lw, tiles_per_sample, n_lat_tiles, bsz)


def _slots(w, head_w, real_w):
    rows = w.shape[0]
    w = w.reshape(rows, -1, head_w)[:, :, :real_w]
    return jnp.pad(w, ((0, 0), (0, 0), (0, HEAD_SLOT - real_w))).reshape(rows, -1)


def _block_diag(blocks):
    n, r, c = blocks.shape
    out = jnp.zeros((n * r, n * c), blocks.dtype)
    for i in range(n):
        out = out.at[i * r:(i + 1) * r, i * c:(i + 1) * c].set(blocks[i])
    return out


def _layer_weights(l, p):
    d = p["w_in"].shape[1]
    w_in = p["w_in"][l]
    ml = 4 * BRANCH_W
    o_mla = ml + 4 * ML_HEADS
    o_pool = o_mla + Q_LORA + KV_LORA + QK_ROPE
    kr = w_in[:, o_mla + Q_LORA + KV_LORA:o_pool]
    kr_slot = jnp.pad(kr, ((0, 0), (QK_NOPE, HEAD_SLOT - QK_DIM)))
    w_big = jnp.concatenate([
        w_in[:, :ml],
        w_in[:, o_pool:],
        w_in[:, o_mla:o_mla + Q_LORA + KV_LORA], jnp.zeros((d, LANES), F32),
        jnp.tile(kr_slot, (1, MLA_HEADS)),
        jnp.pad(w_in[:, ml:o_mla], ((0, 0), (0, LANES - 4 * ML_HEADS))),
    ], axis=1).astype(BF16)
    w_ukv = p["mla_w_ukv"][l].reshape(KV_LORA, MLA_HEADS, QK_NOPE + V_DIM)

    def gain_slots(g):
        return jnp.tile(jnp.pad(g, (0, HEAD_SLOT - QK_DIM)), MLA_HEADS)[None, :]

    return {
        "g_mix": p["g_mix"][l][None, :], "g_ffn": p["g_ffn"][l][None, :],
        "w_big": w_big,
        "g_cq": p["mla_g_cq"][l][None, :], "g_ckv": p["mla_g_ckv"][l][None, :],
        "w_uq": _slots(p["mla_w_uq"][l], QK_DIM, QK_DIM).astype(BF16),
        "w_uk": _slots(w_ukv[:, :, :QK_NOPE].reshape(KV_LORA, -1), QK_NOPE, QK_NOPE).astype(BF16),
        "w_uv": _slots(w_ukv[:, :, QK_NOPE:].reshape(KV_LORA, -1), V_DIM, V_DIM).astype(BF16),
        "g_qn": gain_slots(p["mla_g_qn"][l]), "g_kn": gain_slots(p["mla_g_kn"][l]),
        "gate_bias": jnp.pad(p["ml_gate_bias"][l], (0, LANES - 4 * ML_HEADS))[None, :],
        "head_gain": p["ml_head_gain"][l][None, :],
        "ones64": _block_diag(jnp.ones((ML_HEADS, ML_DH, ML_DH), BF16)),
        "pool_w": _block_diag(p["pool_w"][l]).astype(BF16),
        "pool_scale": p["pool_scale"][l][None, :],
        "conv_w": p["conv_w"][l], "conv_b": p["conv_b"][l][None, :],
        "w_gate": jnp.concatenate(list(p["w_gate"][l]), axis=1).astype(BF16),
        "b_gate": p["b_gate"][l].reshape(1, -1),
        "w_branch": p["w_branch"][l].astype(BF16),
        "w_out": p["w_out"][l].astype(BF16),
        "w_router": jnp.pad(p["moe_w_router"][l], ((0, 0), (0, LANES - N_EXPERTS))).astype(BF16),
        "b_router": jnp.pad(p["moe_b_router"][l], (0, LANES - N_EXPERTS), constant_values=-1e30)[None, :],
        "w13": jnp.concatenate([p["moe_w1"][l], p["moe_w3"][l]], axis=2).astype(BF16),
        "w2": p["moe_w2"][l].astype(BF16),
        "ws13": jnp.concatenate([p["moe_ws1"][l], p["moe_ws3"][l]], axis=1).astype(BF16),
        "ws2": p["moe_ws2"][l].astype(BF16),
    }


def _rope_tables(n_lat, n_ctx):
    t = jnp.arange(n_lat)
    n_freq = QK_ROPE // 4
    inv = ROPE_THETA ** (-jnp.arange(n_freq, dtype=F32) / n_freq)
    ang_r = (t // GRID_W).astype(F32)[:, None] * inv
    ang_c = (t % GRID_W).astype(F32)[:, None] * inv
    cos4 = jnp.concatenate([jnp.cos(ang_r)] * 2 + [jnp.cos(ang_c)] * 2, axis=1)
    zero = jnp.zeros_like(ang_r)
    sin_a = jnp.concatenate([-jnp.sin(ang_r), zero, -jnp.sin(ang_c), zero], axis=1)
    sin_b = jnp.concatenate([zero, jnp.sin(ang_r), zero, jnp.sin(ang_c)], axis=1)

    def slot(a, fill):
        a = jnp.pad(a, ((0, 0), (QK_NOPE, 0)), constant_values=fill)
        a = jnp.pad(a, ((0, 0), (0, HEAD_SLOT - QK_DIM)), constant_values=fill)
        return jnp.pad(a, ((0, n_ctx), (0, 0)), constant_values=fill)

    return {"cos": slot(cos4, 1.0), "sin_a": slot(sin_a, 0.0), "sin_b": slot(sin_b, 0.0),
            "ones_slot": _block_diag(jnp.ones((MLA_HEADS, HEAD_SLOT, HEAD_SLOT), BF16))}


def kernel(x, c, ctx, c_ctx, w_mod, b_mod, g_mix, g_ffn, w_in, ml_gate_bias, ml_head_gain, mla_g_cq, mla_g_ckv,
           mla_w_uq, mla_w_ukv, mla_g_qn, mla_g_kn, pool_w, pool_scale, conv_w, conv_b, w_gate, b_gate, w_branch,
           w_out, moe_w_router, moe_b_router, moe_w1, moe_w3, moe_w2, moe_ws1, moe_ws3, moe_ws2):
    p = dict(g_mix=g_mix, g_ffn=g_ffn, w_in=w_in, ml_gate_bias=ml_gate_bias, ml_head_gain=ml_head_gain,
             mla_g_cq=mla_g_cq, mla_g_ckv=mla_g_ckv, mla_w_uq=mla_w_uq, mla_w_ukv=mla_w_ukv, mla_g_qn=mla_g_qn,
             mla_g_kn=mla_g_kn, pool_w=pool_w, pool_scale=pool_scale, conv_w=conv_w, conv_b=conv_b,
             w_gate=w_gate, b_gate=b_gate, w_branch=w_branch, w_out=w_out, moe_w_router=moe_w_router,
             moe_b_router=moe_b_router, moe_w1=moe_w1, moe_w3=moe_w3, moe_w2=moe_w2, moe_ws1=moe_ws1,
             moe_ws3=moe_ws3, moe_ws2=moe_ws2)
    bsz, n_lat, d = x.shape
    n_ctx = ctx.shape[1]
    depth = w_mod.shape[0]
    s = n_lat + n_ctx
    assert n_ctx == ROW_TILE == ATT_TQ and n_lat % ROW_TILE == 0 and n_lat % GRID_W == 0
    n_lat_tiles = n_lat // ROW_TILE
    tiles_per_sample = s // ROW_TILE

    mod_rows = -(-(bsz + 1) // SUBLANES) * SUBLANES
    cc = jnp.concatenate([c, c_ctx[None, :], jnp.zeros((mod_rows - bsz - 1, d), F32)], axis=0)
    mods = _modulation(cc, w_mod, b_mod).reshape(depth, mod_rows, 6, d)
    rope = _rope_tables(n_lat, n_ctx)
    xs = jnp.concatenate([x, ctx], axis=1)

    for l in range(depth):
        lw = _layer_weights(l, p)
        mod = mods[l]
        hx, qkv, o, g, pc, q, k, v = _in_proj(xs, mod, lw, rope, n_lat_tiles)
        hf, hb = _mlstm(qkv, g, lw["gate_bias"], n_lat // ML_CHUNK)
        y_mla = _attention(q, k, v, n_lat)
        xs, h2 = _merge(hx, hf, hb, o, y_mla, pc, xs, mod, lw, n_lat_tiles)
        xs = _moe(h2.reshape(bsz * s, d), xs.reshape(bsz * s, d), mod, lw, tiles_per_sample, n_lat_tiles,
                  bsz).reshape(bsz, s, d)
    return xs[:, :n_lat]
```

```python
import functools
import math

import jax
import jax.numpy as jnp
from jax import lax
from jax.experimental import pallas as pl
from jax.experimental.pallas import tpu as pltpu

GRID_W = 64
BRANCH_W = 256
EPS = 1e-6
ML_HEADS = 4
ML_DH = BRANCH_W // ML_HEADS
ML_CHUNK = 128
MLA_HEADS = 4
Q_LORA = 256
KV_LORA = 128
QK_NOPE = 64
QK_ROPE = 32
QK_DIM = QK_NOPE + QK_ROPE
V_DIM = BRANCH_W // MLA_HEADS
V_SLOT = V_DIM + 16
ROPE_THETA = 10000.0
POOL_WINDOWS = (2, 4, 8, 16)
POOL_GROUP = BRANCH_W // len(POOL_WINDOWS)
POOL_HALO = max(POOL_WINDOWS) // 2
N_EXPERTS = 64
TOP_K = 6
EXPERT_FF = 256
ROUTE_SCALE = 2.5

LANES = 128
SUBLANES = 8
HEAD_SLOT = LANES
ROW_TILE = 256
ATT_TQ = 256
ATT_TK = 1024
ATT_SUB = 128
ATT_AHEAD = 8
EXPERT_ROWS = 512
VMEM_LIMIT = 56 * 1024 * 1024

F32 = jnp.float32
BF16 = jnp.bfloat16
NT_DIMS = (((1,), (1,)), ((), ()))
TN_DIMS = (((0,), (0,)), ((), ()))


def _cparams(*sem):
    return pltpu.CompilerParams(dimension_semantics=sem, vmem_limit_bytes=VMEM_LIMIT)


def _const_spec(shape):
    nd = len(shape)
    return pl.BlockSpec(shape, lambda *_: (0,) * nd, pipeline_mode=pl.Buffered(1))


def _dot(a, b):
    return jnp.dot(a, b, preferred_element_type=F32)


def _split_dot(a_f32, ones_bf16):
    hi = a_f32.astype(BF16)
    r1 = a_f32 - hi.astype(F32)
    mid = r1.astype(BF16)
    lo = (r1 - mid.astype(F32)).astype(BF16)
    return _dot(hi, ones_bf16) + _dot(mid, ones_bf16) + _dot(lo, ones_bf16)


def _split_dot_left(ones_bf16, a_f32):
    hi = a_f32.astype(BF16)
    r1 = a_f32 - hi.astype(F32)
    mid = r1.astype(BF16)
    lo = (r1 - mid.astype(F32)).astype(BF16)
    return _dot(ones_bf16, hi) + _dot(ones_bf16, mid) + _dot(ones_bf16, lo)


def _rms_rows(x):
    return x * lax.rsqrt(jnp.mean(x * x, axis=-1, keepdims=True) + EPS)


def _mod_kernel(c_ref, w_ref, b_ref, o_ref):
    c = c_ref[...]
    a = (c * jax.nn.sigmoid(c)).astype(BF16)
    o_ref[0] = _dot(a, w_ref[0].astype(BF16)) + b_ref[0]


def _modulation(cc, w_mod, b_mod):
    depth, d, d6 = w_mod.shape
    rows = cc.shape[0]
    tn = 1536
    return pl.pallas_call(
        _mod_kernel,
        grid=(depth, d6 // tn),
        in_specs=[
            pl.BlockSpec((rows, d), lambda l, n: (0, 0)),
            pl.BlockSpec((1, d, tn), lambda l, n: (l, 0, n)),
            pl.BlockSpec((1, 1, tn), lambda l, n: (l, 0, n)),
        ],
        out_specs=pl.BlockSpec((1, rows, tn), lambda l, n: (l, 0, n)),
        out_shape=jax.ShapeDtypeStruct((depth, rows, d6), F32),
        compiler_params=_cparams("parallel", "parallel"),
        name="modulation",
    )(cc, w_mod, b_mod.reshape(depth, 1, d6))


def _group_mean_sq(x, ones_bd, width):
    return _split_dot(x * x, ones_bd) * (1.0 / width)


def _in_kernel(x_ref, mod_ref, gmix_ref, w_ref, gcq_ref, gckv_ref, wuq_ref, wuk_ref, wuv_ref,
               gq_ref, gk_ref, ones_ref, cos_ref, sa_ref, sb_ref,
               hx_ref, qkv_ref, o_ref, g_ref, pc_ref, q_ref, k_ref, v_ref):
    x = x_ref[0]
    shift = mod_ref[0, 0:1, :]
    scale = mod_ref[0, 1:2, :]
    hx = (_rms_rows(x) * gmix_ref[...]) * (1.0 + scale) + shift
    hxb = hx.astype(BF16)
    hx_ref[0] = hxb
    p = _dot(hxb, w_ref[...])
    bw = BRANCH_W
    qkv_ref[0, :, 0:bw] = p[:, 0:bw].astype(BF16)
    qkv_ref[0, :, bw:2 * bw] = (p[:, bw:2 * bw] * (ML_DH ** -0.5)).astype(BF16)
    qkv_ref[0, :, 2 * bw:3 * bw] = p[:, 2 * bw:3 * bw].astype(BF16)
    o_ref[0] = p[:, 3 * bw:4 * bw].astype(BF16)
    pc_ref[0] = p[:, 1024:2048]
    cq = p[:, 2048:2048 + Q_LORA]
    ckv = p[:, 2304:2304 + KV_LORA]
    kr4 = p[:, 2560:3072]
    g_ref[0] = p[:, 3072:3200]

    cqn = (_rms_rows(cq) * gcq_ref[...]).astype(BF16)
    ckvn = (_rms_rows(ckv) * gckv_ref[...]).astype(BF16)
    q_pre = _dot(cqn, wuq_ref[...])
    k_pre = _dot(ckvn, wuk_ref[...]) + kr4
    v_t = _dot(ckvn, wuv_ref[...]).T
    tm = v_t.shape[1]
    extra = V_SLOT - V_DIM
    one_row = jnp.where(lax.broadcasted_iota(jnp.int32, (extra, tm), 0) == 0, 1.0, 0.0)
    v_ref[0] = jnp.concatenate(
        sum([[v_t[h * V_DIM:(h + 1) * V_DIM], one_row] for h in range(MLA_HEADS)], []), axis=0).astype(BF16)

    ones_bd = ones_ref[...]
    cos = jnp.concatenate([cos_ref[...]] * MLA_HEADS, axis=1)
    sa = jnp.concatenate([sa_ref[...]] * MLA_HEADS, axis=1)
    sb = jnp.concatenate([sb_ref[...]] * MLA_HEADS, axis=1)
    width = MLA_HEADS * HEAD_SLOT
    half = QK_ROPE // 4

    def norm_rope(t, gain):
        t = t * lax.rsqrt(_group_mean_sq(t, ones_bd, QK_DIM) + EPS) * gain
        return t * cos + pltpu.roll(t, width - half, 1) * sa + pltpu.roll(t, half, 1) * sb

    q = norm_rope(q_pre, gq_ref[...]) * (QK_DIM ** -0.5 * math.log2(math.e))
    q_ref[0] = q.T.astype(BF16)
    k_ref[0] = norm_rope(k_pre, gk_ref[...]).astype(BF16)


def _in_proj(xs, mod, lw, rope, n_lat_tiles):
    bsz, s, d = xs.shape
    tm = ROW_TILE
    nt = s // tm
    wcols = lw["w_big"].shape[1]
    slot_w = MLA_HEADS * HEAD_SLOT

    def row_spec(width):
        return pl.BlockSpec((1, tm, width), lambda t, b: (b, t, 0))

    def tab_spec():
        return pl.BlockSpec((tm, HEAD_SLOT), lambda t, b: (t, 0))

    def col_spec(height):
        return pl.BlockSpec((1, height, tm), lambda t, b: (b, 0, t))

    outs = [(d, BF16), (3 * BRANCH_W, BF16), (BRANCH_W, BF16), (LANES, F32), (1024, F32)]
    v_rows = MLA_HEADS * V_SLOT
    out_specs = [row_spec(w) for w, _ in outs] + [col_spec(slot_w), row_spec(slot_w), col_spec(v_rows)]
    out_shape = [jax.ShapeDtypeStruct((bsz, s, w), dt) for w, dt in outs] + [
        jax.ShapeDtypeStruct((bsz, slot_w, s), BF16), jax.ShapeDtypeStruct((bsz, s, slot_w), BF16),
        jax.ShapeDtypeStruct((bsz, v_rows, s), BF16)]
    return pl.pallas_call(
        _in_kernel,
        grid=(nt, bsz),
        in_specs=[
            row_spec(d),
            pl.BlockSpec((1, 6, d), lambda t, b: (jnp.where(t >= n_lat_tiles, bsz, b), 0, 0)),
            _const_spec((1, d)),
            _const_spec((d, wcols)),
            _const_spec((1, Q_LORA)),
            _const_spec((1, KV_LORA)),
            _const_spec((Q_LORA, slot_w)),
            _const_spec((KV_LORA, slot_w)),
            _const_spec((KV_LORA, BRANCH_W)),
            _const_spec((1, slot_w)),
            _const_spec((1, slot_w)),
            _const_spec((slot_w, slot_w)),
            tab_spec(), tab_spec(), tab_spec(),
        ],
        out_specs=out_specs,
        out_shape=out_shape,
        compiler_params=_cparams("parallel", "parallel"),
        name="norm_in_proj",
    )(xs, mod, lw["g_mix"], lw["w_big"], lw["g_cq"], lw["g_ckv"], lw["w_uq"], lw["w_uk"], lw["w_uv"],
      lw["g_qn"], lw["g_kn"], rope["ones_slot"], rope["cos"], rope["sin_a"], rope["sin_b"])


def _log_sigmoid(x):
    return jnp.minimum(x, 0.0) - jnp.log1p(jnp.exp(-jnp.abs(x)))


def _mlstm_kernel(qkv_f, qkv_b, g_f, g_b, bias_ref, hf_ref, hb_ref, c_sc, m_sc):
    @pl.when(pl.program_id(1) == 0)
    def _():
        c_sc[...] = jnp.zeros_like(c_sc)
        m_sc[...] = jnp.zeros_like(m_sc)

    L = ML_CHUNK
    dh = ML_DH
    row = lax.broadcasted_iota(jnp.int32, (L, L), 0)
    col = lax.broadcasted_iota(jnp.int32, (L, L), 1)
    lane = lax.broadcasted_iota(jnp.int32, (L, dh), 1)
    ones_col = jnp.where(lane == 0, 1.0, 0.0).astype(BF16)

    for d, (qkv_ref, g_ref, out_ref) in enumerate(((qkv_f, g_f, hf_ref), (qkv_b, g_b, hb_ref))):
        a = g_ref[0] + bias_ref[...]
        lf = _log_sigmoid(a)
        seen = (col <= row) if d == 0 else (col >= row)
        b_cols = _split_dot_left(seen.astype(BF16), lf)
        b_rows = b_cols.T
        a_rows = a.T
        last = L - 1 if d == 0 else 0
        outs = []
        for h in range(ML_HEADS):
            ci = d * 2 * ML_HEADS + h
            cf = ci + ML_HEADS
            st = d * ML_HEADS + h
            b_col = b_cols[:, cf:cf + 1]
            b_row = b_rows[cf:cf + 1, :]
            li_col = a[:, ci:ci + 1]
            li_row = a_rows[ci:ci + 1, :]
            b_end = b_cols[last:last + 1, cf:cf + 1]
            m_st = m_sc[st:st + 1, 0:1]
            q = qkv_ref[0, :, h * dh:(h + 1) * dh]
            k = qkv_ref[0, :, BRANCH_W + h * dh:BRANCH_W + (h + 1) * dh]
            v = qkv_ref[0, :, 2 * BRANCH_W + h * dh:2 * BRANCH_W + (h + 1) * dh]
            c_aug = c_sc[st]

            d_log = jnp.where(seen, b_col - b_row + li_row, -jnp.inf)
            inter = b_col + m_st
            m_t = jnp.maximum(inter, jnp.max(d_log, axis=1, keepdims=True))
            a_inter = jnp.exp(inter - m_t)
            s_qk = lax.dot_general(q, k, NT_DIMS, preferred_element_type=F32)
            w_ts = jnp.exp(d_log - m_t) * s_qk
            qc = _dot(q, c_aug.astype(BF16))
            num = a_inter * qc[:, 0:dh] + _dot(w_ts.astype(BF16), v)
            den = a_inter * qc[:, dh:dh + 1] + jnp.sum(w_ts, axis=1, keepdims=True)
            outs.append(num / jnp.maximum(jnp.abs(den), jnp.exp(-m_t)))

            w_log = b_end - b_col + li_col
            m_new = jnp.maximum(b_end + m_st, jnp.max(w_log, axis=0, keepdims=True))
            decay = jnp.exp(b_end + m_st - m_new)
            kw = (k.astype(F32) * jnp.exp(w_log - m_new)).astype(BF16)
            v_aug = jnp.concatenate([v, ones_col], axis=1)
            c_sc[st] = decay * c_aug + lax.dot_general(kw, v_aug, TN_DIMS, preferred_element_type=F32)
            m_sc[st:st + 1, :] = jnp.broadcast_to(m_new, (1, LANES))
        out_ref[0] = jnp.concatenate(outs, axis=1)


def _mlstm(qkv, g, gate_bias, n_lat_chunks):
    bsz, s, _ = qkv.shape
    nc = s // ML_CHUNK

    def fwd_map(b, j):
        return (b, (j + n_lat_chunks) % nc, 0)

    def bwd_map(b, j):
        return (b, nc - 1 - j, 0)

    def spec(width, imap):
        return pl.BlockSpec((1, ML_CHUNK, width), imap)

    return pl.pallas_call(
        _mlstm_kernel,
        grid=(bsz, nc),
        in_specs=[spec(3 * BRANCH_W, fwd_map), spec(3 * BRANCH_W, bwd_map),
                  spec(LANES, fwd_map), spec(LANES, bwd_map),
                  pl.BlockSpec((1, LANES), lambda b, j: (0, 0))],
        out_specs=[spec(BRANCH_W, fwd_map), spec(BRANCH_W, bwd_map)],
        out_shape=[jax.ShapeDtypeStruct((bsz, s, BRANCH_W), F32)] * 2,
        scratch_shapes=[pltpu.VMEM((2 * ML_HEADS, ML_DH, LANES), F32),
                        pltpu.VMEM((2 * ML_HEADS, LANES), F32)],
        compiler_params=_cparams("parallel", "arbitrary"),
        name="mlstm_scan",
    )(qkv, qkv, g, g, gate_bias)


def _attn_kernel(qt_ref, k_ref, vt_ref, o_ref, m_sc, acc_sc, *, n_lat, n_ctx, tk):
    m_sc[...] = jnp.full_like(m_sc, -jnp.inf)
    acc_sc[...] = jnp.zeros_like(acc_sc)

    def scores(h, start, size):
        sl = slice(h * HEAD_SLOT, (h + 1) * HEAD_SLOT)
        return _dot(k_ref[0, pl.ds(start, size), sl], qt_ref[0, sl, :])

    def keys(start, size):
        n_sub = size // ATT_SUB
        order = [(j, h) for j in range(n_sub) for h in range(MLA_HEADS)]
        pending = [scores(h, start + j * ATT_SUB, ATT_SUB) for j, h in order[:ATT_AHEAD]]
        for i, (j, h) in enumerate(order):
            off = start + j * ATT_SUB
            s = pending.pop(0)
            if i + ATT_AHEAD < len(order):
                jn, hn = order[i + ATT_AHEAD]
                pending.append(scores(hn, start + jn * ATT_SUB, ATT_SUB))
            m_old = m_sc[h]
            m_new = jnp.maximum(m_old, jnp.max(s, axis=0, keepdims=True))
            p = jnp.exp2(s - m_new)
            vt = vt_ref[0, h * V_SLOT:(h + 1) * V_SLOT, pl.ds(off, ATT_SUB)]
            acc_sc[h] = jnp.exp2(m_old - m_new) * acc_sc[h] + _dot(vt, p.astype(BF16))
            m_sc[h] = m_new

    keys(n_lat, n_ctx)

    def chunk(c, carry):
        keys(pl.multiple_of(c * tk, tk), tk)
        return carry

    is_ctx_tile = pl.program_id(1) * ATT_TQ >= n_lat
    lax.fori_loop(0, jnp.where(is_ctx_tile, 0, n_lat // tk), chunk, 0)
    out_t = jnp.concatenate([acc_sc[h, 0:V_DIM, :] / acc_sc[h, V_DIM:V_DIM + 1, :] for h in range(MLA_HEADS)],
                            axis=0)
    o_ref[0] = out_t.T.astype(o_ref.dtype)


def _key_chunk(n_keys):
    return max(t for t in range(ATT_TQ, ATT_TK + 1, ATT_TQ) if n_keys % t == 0)


def _attention(qt, k, vt, n_lat):
    bsz, s, slot_w = k.shape
    tq = ATT_TQ
    return pl.pallas_call(
        functools.partial(_attn_kernel, n_lat=n_lat, n_ctx=s - n_lat, tk=_key_chunk(n_lat)),
        grid=(bsz, s // tq),
        in_specs=[pl.BlockSpec((1, slot_w, tq), lambda b, t: (b, 0, t)),
                  pl.BlockSpec((1, s, slot_w), lambda b, t: (b, 0, 0), pipeline_mode=pl.Buffered(1)),
                  pl.BlockSpec((1, MLA_HEADS * V_SLOT, s), lambda b, t: (b, 0, 0), pipeline_mode=pl.Buffered(1))],
        out_specs=pl.BlockSpec((1, tq, BRANCH_W), lambda b, t: (b, t, 0)),
        out_shape=jax.ShapeDtypeStruct((bsz, s, BRANCH_W), BF16),
        scratch_shapes=[pltpu.VMEM((MLA_HEADS, 1, tq), F32), pltpu.VMEM((MLA_HEADS, V_SLOT, tq), F32)],
        compiler_params=_cparams("parallel", "arbitrary"),
        name="mla_attention",
    )(qt, k, vt)


def _merge_kernel(hx_ref, hf_ref, hb_ref, o_ref, ya_ref, pc_ref, prev_ref, next_ref, x_ref, mod_ref,
                  hg_ref, ones64_ref, pw_ref, ps_ref, cw_ref, cb_ref, wg_ref, bg_ref, wb_ref, wo_ref,
                  gffn_ref, xo_ref, h2_ref, *, n_lat_tiles, n_tiles):
    t = pl.program_id(1)
    tm = ROW_TILE
    bw = BRANCH_W
    halo = POOL_HALO
    no_prev = jnp.logical_or(t == 0, t == n_lat_tiles)
    no_next = jnp.logical_or(t == n_lat_tiles - 1, t == n_tiles - 1)

    h = hf_ref[0] + hb_ref[0]
    ms = _split_dot(h * h, ones64_ref[...]) * (1.0 / ML_DH)
    y_ml = (h * lax.rsqrt(ms + EPS) * hg_ref[...]) * jax.nn.sigmoid(o_ref[0].astype(F32))

    prev = jnp.where(no_prev, 0.0, prev_ref[0])
    nxt = jnp.where(no_next, 0.0, next_ref[0])
    ext = jnp.concatenate([prev, pc_ref[0], nxt], axis=0)
    pe = ext[:, 0:bw]

    def rows(arr, off):
        return arr[halo + off:halo + off + tm, :]

    r = lax.broadcasted_iota(jnp.int32, (tm, 1), 0)
    lane = lax.broadcasted_iota(jnp.int32, (tm, bw), 1)
    centre = rows(pe, 0)
    acc = centre
    mean = jnp.zeros((tm, bw), F32)
    done = 0
    for gi, w in enumerate(POOL_WINDOWS):
        for off in list(range(-(w // 2), -done)) + list(range(max(done, 1), w // 2)):
            acc = acc + rows(pe, off)
        done = w // 2
        before = jnp.where(no_prev, jnp.minimum(r, w // 2), w // 2)
        after = jnp.where(no_next, jnp.minimum(tm - r, w // 2), w // 2)
        inv = 1.0 / (before + after).astype(F32)
        mean = jnp.where(lane >= gi * POOL_GROUP, acc * inv, mean)
    y_pool = _dot((mean - centre).astype(BF16), pw_ref[...]) * ps_ref[...]

    z = ext[:, 3 * bw:4 * bw] * ext[:, bw:2 * bw]
    conv = cb_ref[...] + rows(z, -1) * cw_ref[0:1, :] + rows(z, 0) * cw_ref[1:2, :] + rows(z, 1) * cw_ref[2:3, :]
    y_conv = rows(ext[:, 2 * bw:3 * bw], 0) * conv

    hxb = hx_ref[0]
    d = hxb.shape[1]
    gates = jax.nn.sigmoid(_dot(hxb, wg_ref[...]) + bg_ref[...])
    ys = (y_ml.astype(BF16), ya_ref[0], y_pool.astype(BF16), y_conv.astype(BF16))
    merged = None
    for i, y in enumerate(ys):
        term = gates[:, i * d:(i + 1) * d] * _dot(y, wb_ref[i])
        merged = term if merged is None else merged + term
    out = _dot(merged.astype(BF16), wo_ref[...])
    x_new = x_ref[0] + mod_ref[0, 2:3, :] * out
    xo_ref[0] = x_new
    h2_ref[0] = (_rms_rows(x_new) * gffn_ref[...]) * (1.0 + mod_ref[0, 4:5, :]) + mod_ref[0, 3:4, :]


def _merge(hx, hf, hb, o, y_mla, pc, xs, mod, lw, n_lat_tiles):
    bsz, s, d = xs.shape
    tm = ROW_TILE
    nt = s // tm
    halo = POOL_HALO
    per = tm // halo

    def row_spec(width):
        return pl.BlockSpec((1, tm, width), lambda b, t: (b, t, 0))

    return pl.pallas_call(
        functools.partial(_merge_kernel, n_lat_tiles=n_lat_tiles, n_tiles=nt),
        grid=(bsz, nt),
        in_specs=[
            row_spec(d), row_spec(BRANCH_W), row_spec(BRANCH_W), row_spec(BRANCH_W), row_spec(BRANCH_W),
            row_spec(1024),
            pl.BlockSpec((1, halo, 1024), lambda b, t: (b, jnp.maximum(t * per - 1, 0), 0)),
            pl.BlockSpec((1, halo, 1024), lambda b, t: (b, jnp.minimum((t + 1) * per, s // halo - 1), 0)),
            row_spec(d),
            pl.BlockSpec((1, 6, d), lambda b, t: (jnp.where(t >= n_lat_tiles, bsz, b), 0, 0)),
            _const_spec((1, BRANCH_W)), _const_spec((BRANCH_W, BRANCH_W)), _const_spec((BRANCH_W, BRANCH_W)),
            _const_spec((1, BRANCH_W)), _const_spec((3, BRANCH_W)), _const_spec((1, BRANCH_W)),
            _const_spec((d, 4 * d)), _const_spec((1, 4 * d)), _const_spec((4, BRANCH_W, d)),
            _const_spec((d, d)), _const_spec((1, d)),
        ],
        out_specs=[row_spec(d), row_spec(d)],
        out_shape=[jax.ShapeDtypeStruct((bsz, s, d), F32)] * 2,
        compiler_params=_cparams("parallel", "parallel"),
        name="mixer_merge",
    )(hx, hf, hb, o, y_mla, pc, pc, pc, xs, mod, lw["head_gain"], lw["ones64"], lw["pool_w"], lw["pool_scale"],
      lw["conv_w"], lw["conv_b"], lw["w_gate"], lw["b_gate"], lw["w_branch"], lw["w_out"], lw["g_ffn"])


def _router_kernel(h_ref, wr_ref, br_ref, idx_ref, rank_ref, wt_ref, cnt_ref, cnt_sc):
    @pl.when(pl.program_id(0) == 0)
    def _():
        cnt_sc[...] = jnp.zeros_like(cnt_sc)

    tm = h_ref.shape[0]
    scores = jax.nn.sigmoid(_dot(h_ref[...].astype(BF16), wr_ref[...]))
    sel = scores + br_ref[...]
    lane = lax.broadcasted_iota(jnp.int32, (tm, LANES), 1)
    member = jnp.zeros((tm, LANES), F32)
    picks = []
    for _ in range(TOP_K):
        mx = jnp.max(sel, axis=1, keepdims=True)
        ix = jnp.min(jnp.where(sel == mx, lane, LANES), axis=1, keepdims=True)
        hit = lane == ix
        picks.append((ix, hit, jnp.sum(jnp.where(hit, scores, 0.0), axis=1, keepdims=True)))
        member = jnp.where(hit, 1.0, member)
        sel = jnp.where(hit, -jnp.inf, sel)
    total = sum(p[2] for p in picks)
    row = lax.broadcasted_iota(jnp.int32, (tm, tm), 0)
    col = lax.broadcasted_iota(jnp.int32, (tm, tm), 1)
    earlier = (col < row).astype(BF16)
    ranks = _dot(earlier, member.astype(BF16)) + cnt_sc[...]
    idx_o = jnp.zeros((tm, LANES), jnp.int32)
    rank_o = jnp.zeros((tm, LANES), jnp.int32)
    wt_o = jnp.zeros((tm, LANES), F32)
    for kk, (ix, hit, sc) in enumerate(picks):
        rk = jnp.sum(jnp.where(hit, ranks, 0.0), axis=1, keepdims=True)
        idx_o = jnp.where(lane == kk, ix, idx_o)
        rank_o = jnp.where(lane == kk, rk.astype(jnp.int32), rank_o)
        wt_o = jnp.where(lane == kk, sc / total * ROUTE_SCALE, wt_o)
    idx_ref[...] = idx_o
    rank_ref[...] = rank_o
    wt_ref[...] = wt_o
    cnt_sc[...] = cnt_sc[...] + jnp.sum(member, axis=0, keepdims=True)
    cnt_ref[...] = cnt_sc[...]


def _router(h2, lw):
    n, d = h2.shape
    tm = ROW_TILE

    def spec():
        return pl.BlockSpec((tm, LANES), lambda i: (i, 0))

    return pl.pallas_call(
        _router_kernel,
        grid=(n // tm,),
        in_specs=[pl.BlockSpec((tm, d), lambda i: (i, 0)), _const_spec((d, LANES)), _const_spec((1, LANES))],
        out_specs=[spec(), spec(), spec(), pl.BlockSpec((1, LANES), lambda i: (0, 0))],
        out_shape=[jax.ShapeDtypeStruct((n, LANES), jnp.int32), jax.ShapeDtypeStruct((n, LANES), jnp.int32),
                   jax.ShapeDtypeStruct((n, LANES), F32), jax.ShapeDtypeStruct((1, LANES), F32)],
        scratch_shapes=[pltpu.VMEM((1, LANES), F32)],
        compiler_params=_cparams("arbitrary"),
        name="moe_router",
    )(h2, lw["w_router"], lw["b_router"])


def _dispatch_kernel(dest_ref, h_ref, zero_ref, xs_ref, sem):
    del zero_ref
    tm = h_ref.shape[0]

    def row_copy(r, kk):
        return pltpu.make_async_copy(h_ref.at[pl.ds(r, 1)],
                                     xs_ref.at[pl.ds(dest_ref[0, 0, r * TOP_K + kk], 1)], sem)

    def issue(r, carry):
        for kk in range(TOP_K):
            row_copy(r, kk).start()
        return carry

    def drain(r, carry):
        for kk in range(TOP_K):
            row_copy(r, kk).wait()
        return carry

    lax.fori_loop(0, tm, issue, 0)
    lax.fori_loop(0, tm, drain, 0)


def _dispatch(h2, dest, n_rows):
    n, d = h2.shape
    tm = ROW_TILE
    dest3 = dest.reshape(n // tm, 1, tm * TOP_K)
    return pl.pallas_call(
        _dispatch_kernel,
        grid=(n // tm,),
        in_specs=[pl.BlockSpec((1, 1, tm * TOP_K), lambda i: (i, 0, 0), memory_space=pltpu.SMEM),
                  pl.BlockSpec((tm, d), lambda i: (i, 0)),
                  pl.BlockSpec(memory_space=pl.ANY)],
        out_specs=pl.BlockSpec(memory_space=pl.ANY),
        out_shape=jax.ShapeDtypeStruct((n_rows, d), F32),
        scratch_shapes=[pltpu.SemaphoreType.DMA(())],
        input_output_aliases={2: 0},
        compiler_params=_cparams("arbitrary"),
        name="moe_dispatch",
    )(dest3, h2, jnp.zeros((n_rows, d), F32))


def _expert_kernel(blk_e_ref, n_act_ref, x_ref, w13_ref, w2_ref, y_ref):
    del blk_e_ref

    @pl.when(pl.program_id(0) < n_act_ref[0])
    def _():
        xb = x_ref[...].astype(BF16)
        up = _dot(xb, w13_ref[0])
        a = up[:, 0:EXPERT_FF]
        act = (a * jax.nn.sigmoid(a)) * up[:, EXPERT_FF:2 * EXPERT_FF]
        y_ref[...] = _dot(act.astype(BF16), w2_ref[0])


def _experts(xs_sorted, blk_e, n_act, lw):
    n_rows, d = xs_sorted.shape
    bm = EXPERT_ROWS

    def row_map(i, blk_e_ref, n_act_ref):
        return (jnp.minimum(i, n_act_ref[0] - 1), 0)

    def w_map(i, blk_e_ref, n_act_ref):
        return (blk_e_ref[i], 0, 0)

    return pl.pallas_call(
        _expert_kernel,
        grid_spec=pltpu.PrefetchScalarGridSpec(
            num_scalar_prefetch=2,
            grid=(n_rows // bm,),
            in_specs=[pl.BlockSpec((bm, d), row_map),
                      pl.BlockSpec((1, d, 2 * EXPERT_FF), w_map),
                      pl.BlockSpec((1, EXPERT_FF, d), w_map)],
            out_specs=pl.BlockSpec((bm, d), row_map),
        ),
        out_shape=jax.ShapeDtypeStruct((n_rows, d), F32),
        compiler_params=_cparams("arbitrary"),
        name="moe_experts",
    )(blk_e, n_act, xs_sorted, lw["w13"], lw["w2"])


def _combine_kernel(dest_ref, wt_ref, h_ref, x_ref, mod_ref, ws13_ref, ws2_ref, ys_ref, xo_ref, gbuf, sem):
    tm = h_ref.shape[0]

    def row_copy(r, kk):
        return pltpu.make_async_copy(ys_ref.at[pl.ds(dest_ref[0, 0, r * TOP_K + kk], 1)],
                                     gbuf.at[kk, pl.ds(r, 1)], sem)

    def issue(r, carry):
        for kk in range(TOP_K):
            row_copy(r, kk).start()
        return carry

    def drain(r, carry):
        for kk in range(TOP_K):
            row_copy(r, kk).wait()
        return carry

    lax.fori_loop(0, tm, issue, 0)
    hb = h_ref[...].astype(BF16)
    up = _dot(hb, ws13_ref[...])
    a = up[:, 0:EXPERT_FF]
    act = (a * jax.nn.sigmoid(a)) * up[:, EXPERT_FF:2 * EXPERT_FF]
    f = _dot(act.astype(BF16), ws2_ref[...])
    lax.fori_loop(0, tm, drain, 0)
    wt = wt_ref[...]
    for kk in range(TOP_K):
        f = f + wt[:, kk:kk + 1] * gbuf[kk]
    xo_ref[...] = x_ref[...] + mod_ref[0, 5:6, :] * f


def _combine(dest, wts, h2, xs, mod, ys_sorted, lw, tiles_per_sample, n_lat_tiles, bsz):
    n, d = h2.shape
    tm = ROW_TILE
    dest3 = dest.reshape(n // tm, 1, tm * TOP_K)

    def row_spec(width):
        return pl.BlockSpec((tm, width), lambda i: (i, 0))

    def mod_map(i):
        return (jnp.where(i % tiles_per_sample >= n_lat_tiles, bsz, i // tiles_per_sample), 0, 0)

    return pl.pallas_call(
        _combine_kernel,
        grid=(n // tm,),
        in_specs=[pl.BlockSpec((1, 1, tm * TOP_K), lambda i: (i, 0, 0), memory_space=pltpu.SMEM),
                  row_spec(LANES), row_spec(d), row_spec(d),
                  pl.BlockSpec((1, 6, d), mod_map),
                  _const_spec((d, 2 * EXPERT_FF)), _const_spec((EXPERT_FF, d)),
                  pl.BlockSpec(memory_space=pl.ANY)],
        out_specs=row_spec(d),
        out_shape=jax.ShapeDtypeStruct((n, d), F32),
        scratch_shapes=[pltpu.VMEM((TOP_K, tm, d), F32), pltpu.SemaphoreType.DMA(())],
        compiler_params=_cparams("arbitrary"),
        name="moe_combine",
    )(dest3, wts, h2, xs, mod, lw["ws13"], lw["ws2"], ys_sorted)


def _moe(h2, xs, mod, lw, tiles_per_sample, n_lat_tiles, bsz):
    n, d = h2.shape
    bm = EXPERT_ROWS
    idx, rank, wts, counts = _router(h2, lw)
    counts = counts[0, :N_EXPERTS].astype(jnp.int32)
    padded = (counts + bm - 1) // bm * bm
    pad_end = jnp.cumsum(padded)
    pad_start = pad_end - padded
    dest = pad_start[idx[:, :TOP_K]] + rank[:, :TOP_K]
    n_blocks = -(-(n * TOP_K) // bm) + N_EXPERTS
    n_act = (pad_end[-1] // bm).astype(jnp.int32)
    blk = jnp.minimum(jnp.arange(n_blocks, dtype=jnp.int32), n_act - 1)
    blk_e = jnp.minimum(jnp.sum(pad_end[None, :] <= (blk * bm)[:, None], axis=1), N_EXPERTS - 1).astype(jnp.int32)
    xs_sorted = _dispatch(h2, dest, n_blocks * bm)
    ys_sorted = _experts(xs_sorted, blk_e, n_act.reshape(1), lw)
    return _combine(dest, wts, h2, xs, mod, ys_sorted, lw, tiles_per_sample, n_lat_tiles, bsz)


def _slots(w, head_w, real_w):
    rows = w.shape[0]
    w = w.reshape(rows, -1, head_w)[:, :, :real_w]
    return jnp.pad(w, ((0, 0), (0, 0), (0, HEAD_SLOT - real_w))).reshape(rows, -1)


def _block_diag(blocks):
    n, r, c = blocks.shape
    out = jnp.zeros((n * r, n * c), blocks.dtype)
    for i in range(n):
        out = out.at[i * r:(i + 1) * r, i * c:(i + 1) * c].set(blocks[i])
    return out


def _layer_weights(l, p):
    d = p["w_in"].shape[1]
    w_in = p["w_in"][l]
    ml = 4 * BRANCH_W
    o_mla = ml + 4 * ML_HEADS
    o_pool = o_mla + Q_LORA + KV_LORA + QK_ROPE
    kr = w_in[:, o_mla + Q_LORA + KV_LORA:o_pool]
    kr_slot = jnp.pad(kr, ((0, 0), (QK_NOPE, HEAD_SLOT - QK_DIM)))
    w_big = jnp.concatenate([
        w_in[:, :ml],
        w_in[:, o_pool:],
        w_in[:, o_mla:o_mla + Q_LORA + KV_LORA], jnp.zeros((d, LANES), F32),
        jnp.tile(kr_slot, (1, MLA_HEADS)),
        jnp.pad(w_in[:, ml:o_mla], ((0, 0), (0, LANES - 4 * ML_HEADS))),
    ], axis=1).astype(BF16)
    w_ukv = p["mla_w_ukv"][l].reshape(KV_LORA, MLA_HEADS, QK_NOPE + V_DIM)

    def gain_slots(g):
        return jnp.tile(jnp.pad(g, (0, HEAD_SLOT - QK_DIM)), MLA_HEADS)[None, :]

    return {
        "g_mix": p["g_mix"][l][None, :], "g_ffn": p["g_ffn"][l][None, :],
        "w_big": w_big,
        "g_cq": p["mla_g_cq"][l][None, :], "g_ckv": p["mla_g_ckv"][l][None, :],
        "w_uq": _slots(p["mla_w_uq"][l], QK_DIM, QK_DIM).astype(BF16),
        "w_uk": _slots(w_ukv[:, :, :QK_NOPE].reshape(KV_LORA, -1), QK_NOPE, QK_NOPE).astype(BF16),
        "w_uv": w_ukv[:, :, QK_NOPE:].reshape(KV_LORA, BRANCH_W).astype(BF16),
        "g_qn": gain_slots(p["mla_g_qn"][l]), "g_kn": gain_slots(p["mla_g_kn"][l]),
        "gate_bias": jnp.pad(p["ml_gate_bias"][l], (0, LANES - 4 * ML_HEADS))[None, :],
        "head_gain": p["ml_head_gain"][l][None, :],
        "ones64": _block_diag(jnp.ones((ML_HEADS, ML_DH, ML_DH), BF16)),
        "pool_w": _block_diag(p["pool_w"][l]).astype(BF16),
        "pool_scale": p["pool_scale"][l][None, :],
        "conv_w": p["conv_w"][l], "conv_b": p["conv_b"][l][None, :],
        "w_gate": jnp.concatenate(list(p["w_gate"][l]), axis=1).astype(BF16),
        "b_gate": p["b_gate"][l].reshape(1, -1),
        "w_branch": p["w_branch"][l].astype(BF16),
        "w_out": p["w_out"][l].astype(BF16),
        "w_router": jnp.pad(p["moe_w_router"][l], ((0, 0), (0, LANES - N_EXPERTS))).astype(BF16),
        "b_router": jnp.pad(p["moe_b_router"][l], (0, LANES - N_EXPERTS), constant_values=-1e30)[None, :],
        "w13": jnp.concatenate([p["moe_w1"][l], p["moe_w3"][l]], axis=2).astype(BF16),
        "w2": p["moe_w2"][l].astype(BF16),
        "ws13": jnp.concatenate([p["moe_ws1"][l], p["moe_ws3"][l]], axis=1).astype(BF16),
        "ws2": p["moe_ws2"][l].astype(BF16),
    }


def _rope_tables(n_lat, n_ctx):
    t = jnp.arange(n_lat)
    n_freq = QK_ROPE // 4
    inv = ROPE_THETA ** (-jnp.arange(n_freq, dtype=F32) / n_freq)
    ang_r = (t // GRID_W).astype(F32)[:, None] * inv
    ang_c = (t % GRID_W).astype(F32)[:, None] * inv
    cos4 = jnp.concatenate([jnp.cos(ang_r)] * 2 + [jnp.cos(ang_c)] * 2, axis=1)
    zero = jnp.zeros_like(ang_r)
    sin_a = jnp.concatenate([-jnp.sin(ang_r), zero, -jnp.sin(ang_c), zero], axis=1)
    sin_b = jnp.concatenate([zero, jnp.sin(ang_r), zero, jnp.sin(ang_c)], axis=1)

    def slot(a, fill):
        a = jnp.pad(a, ((0, 0), (QK_NOPE, 0)), constant_values=fill)
        a = jnp.pad(a, ((0, 0), (0, HEAD_SLOT - QK_DIM)), constant_values=fill)
        return jnp.pad(a, ((0, n_ctx), (0, 0)), constant_values=fill)

    return {"cos": slot(cos4, 1.0), "sin_a": slot(sin_a, 0.0), "sin_b": slot(sin_b, 0.0),
            "ones_slot": _block_diag(jnp.ones((MLA_HEADS, HEAD_SLOT, HEAD_SLOT), BF16))}


def kernel(x, c, ctx, c_ctx, w_mod, b_mod, g_mix, g_ffn, w_in, ml_gate_bias, ml_head_gain, mla_g_cq, mla_g_ckv,
           mla_w_uq, mla_w_ukv, mla_g_qn, mla_g_kn, pool_w, pool_scale, conv_w, conv_b, w_gate, b_gate, w_branch,
           w_out, moe_w_router, moe_b_router, moe_w1, moe_w3, moe_w2, moe_ws1, moe_ws3, moe_ws2):
    p = dict(g_mix=g_mix, g_ffn=g_ffn, w_in=w_in, ml_gate_bias=ml_gate_bias, ml_head_gain=ml_head_gain,
             mla_g_cq=mla_g_cq, mla_g_ckv=mla_g_ckv, mla_w_uq=mla_w_uq, mla_w_ukv=mla_w_ukv, mla_g_qn=mla_g_qn,
             mla_g_kn=mla_g_kn, pool_w=pool_w, pool_scale=pool_scale, conv_w=conv_w, conv_b=conv_b,
             w_gate=w_gate, b_gate=b_gate, w_branch=w_branch, w_out=w_out, moe_w_router=moe_w_router,
             moe_b_router=moe_b_router, moe_w1=moe_w1, moe_w3=moe_w3, moe_w2=moe_w2, moe_ws1=moe_ws1,
             moe_ws3=moe_ws3, moe_ws2=moe_ws2)
    bsz, n_lat, d = x.shape
    n_ctx = ctx.shape[1]
    depth = w_mod.shape[0]
    s = n_lat + n_ctx
    assert n_ctx == ROW_TILE == ATT_TQ and n_lat % ROW_TILE == 0 and n_lat % GRID_W == 0
    n_lat_tiles = n_lat // ROW_TILE
    tiles_per_sample = s // ROW_TILE

    mod_rows = -(-(bsz + 1) // SUBLANES) * SUBLANES
    cc = jnp.concatenate([c, c_ctx[None, :], jnp.zeros((mod_rows - bsz - 1, d), F32)], axis=0)
    mods = _modulation(cc, w_mod, b_mod).reshape(depth, mod_rows, 6, d)
    rope = _rope_tables(n_lat, n_ctx)
    xs = jnp.concatenate([x, ctx], axis=1)

    for l in range(depth):
        lw = _layer_weights(l, p)
        mod = mods[l]
        hx, qkv, o, g, pc, q, k, v = _in_proj(xs, mod, lw, rope, n_lat_tiles)
        hf, hb = _mlstm(qkv, g, lw["gate_bias"], n_lat // ML_CHUNK)
        y_mla = _attention(q, k, v, n_lat)
        xs, h2 = _merge(hx, hf, hb, o, y_mla, pc, xs, mod, lw, n_lat_tiles)
        xs = _moe(h2.reshape(bsz * s, d), xs.reshape(bsz * s, d), mod, lw, tiles_per_sample, n_lat_tiles,
                  bsz).reshape(bsz, s, d)
    return xs[:, :n_lat]
```

```python
import functools
import math

import jax
import jax.numpy as jnp
from jax import lax
from jax.experimental import pallas as pl
from jax.experimental.pallas import tpu as pltpu

GRID_W = 64
BRANCH_W = 256
EPS = 1e-6
ML_HEADS = 4
ML_DH = BRANCH_W // ML_HEADS
ML_CHUNK = 128
MLA_HEADS = 4
Q_LORA = 256
KV_LORA = 128
QK_NOPE = 64
QK_ROPE = 32
QK_DIM = QK_NOPE + QK_ROPE
V_DIM = BRANCH_W // MLA_HEADS
V_SLOT = V_DIM + 16
ROPE_THETA = 10000.0
POOL_WINDOWS = (2, 4, 8, 16)
POOL_GROUP = BRANCH_W // len(POOL_WINDOWS)
POOL_HALO = max(POOL_WINDOWS) // 2
N_EXPERTS = 64
TOP_K = 6
EXPERT_FF = 256
ROUTE_SCALE = 2.5

LANES = 128
SUBLANES = 8
HEAD_SLOT = LANES
ROW_TILE = 256
ATT_TQ = 256
ATT_TK = 1024
ATT_SUB = 128
ATT_AHEAD = 8
EXPERT_ROWS = 512
MOE_TILE = 768
RUN_ROWS = 16
MOE_CHUNK = 512
VMEM_LIMIT = 56 * 1024 * 1024

F32 = jnp.float32
BF16 = jnp.bfloat16
NT_DIMS = (((1,), (1,)), ((), ()))
TN_DIMS = (((0,), (0,)), ((), ()))


def _cparams(*sem):
    return pltpu.CompilerParams(dimension_semantics=sem, vmem_limit_bytes=VMEM_LIMIT)


def _const_spec(shape):
    nd = len(shape)
    return pl.BlockSpec(shape, lambda *_: (0,) * nd, pipeline_mode=pl.Buffered(1))


def _dot(a, b):
    return jnp.dot(a, b, preferred_element_type=F32)


def _split_dot(a_f32, ones_bf16):
    hi = a_f32.astype(BF16)
    r1 = a_f32 - hi.astype(F32)
    mid = r1.astype(BF16)
    lo = (r1 - mid.astype(F32)).astype(BF16)
    return _dot(hi, ones_bf16) + _dot(mid, ones_bf16) + _dot(lo, ones_bf16)


def _split_dot_left(ones_bf16, a_f32):
    hi = a_f32.astype(BF16)
    r1 = a_f32 - hi.astype(F32)
    mid = r1.astype(BF16)
    lo = (r1 - mid.astype(F32)).astype(BF16)
    return _dot(ones_bf16, hi) + _dot(ones_bf16, mid) + _dot(ones_bf16, lo)


def _rms_rows(x):
    return x * lax.rsqrt(jnp.mean(x * x, axis=-1, keepdims=True) + EPS)


def _mod_kernel(c_ref, w_ref, b_ref, o_ref):
    c = c_ref[...]
    a = (c * jax.nn.sigmoid(c)).astype(BF16)
    o_ref[0] = _dot(a, w_ref[0].astype(BF16)) + b_ref[0]


def _modulation(cc, w_mod, b_mod):
    depth, d, d6 = w_mod.shape
    rows = cc.shape[0]
    tn = 1536
    return pl.pallas_call(
        _mod_kernel,
        grid=(depth, d6 // tn),
        in_specs=[
            pl.BlockSpec((rows, d), lambda l, n: (0, 0)),
            pl.BlockSpec((1, d, tn), lambda l, n: (l, 0, n)),
            pl.BlockSpec((1, 1, tn), lambda l, n: (l, 0, n)),
        ],
        out_specs=pl.BlockSpec((1, rows, tn), lambda l, n: (l, 0, n)),
        out_shape=jax.ShapeDtypeStruct((depth, rows, d6), F32),
        compiler_params=_cparams("parallel", "parallel"),
        name="modulation",
    )(cc, w_mod, b_mod.reshape(depth, 1, d6))


def _group_mean_sq(x, ones_bd, width):
    return _split_dot(x * x, ones_bd) * (1.0 / width)


def _in_kernel(x_ref, mod_ref, gmix_ref, w_ref, gcq_ref, gckv_ref, wuq_ref, wuk_ref, wuv_ref,
               gq_ref, gk_ref, ones_ref, cos_ref, sa_ref, sb_ref,
               hx_ref, qkv_ref, o_ref, g_ref, pc_ref, q_ref, k_ref, v_ref):
    x = x_ref[0]
    shift = mod_ref[0, 0:1, :]
    scale = mod_ref[0, 1:2, :]
    hx = (_rms_rows(x) * gmix_ref[...]) * (1.0 + scale) + shift
    hxb = hx.astype(BF16)
    hx_ref[0] = hxb
    p = _dot(hxb, w_ref[...])
    bw = BRANCH_W
    qkv_ref[0, :, 0:bw] = p[:, 0:bw].astype(BF16)
    qkv_ref[0, :, bw:2 * bw] = (p[:, bw:2 * bw] * (ML_DH ** -0.5)).astype(BF16)
    qkv_ref[0, :, 2 * bw:3 * bw] = p[:, 2 * bw:3 * bw].astype(BF16)
    o_ref[0] = p[:, 3 * bw:4 * bw].astype(BF16)
    pc_ref[0] = p[:, 1024:2048]
    cq = p[:, 2048:2048 + Q_LORA]
    ckv = p[:, 2304:2304 + KV_LORA]
    kr4 = p[:, 2560:3072]
    g_ref[0] = p[:, 3072:3200]

    cqn = (_rms_rows(cq) * gcq_ref[...]).astype(BF16)
    ckvn = (_rms_rows(ckv) * gckv_ref[...]).astype(BF16)
    q_pre = _dot(cqn, wuq_ref[...])
    k_pre = _dot(ckvn, wuk_ref[...]) + kr4
    v_t = _dot(ckvn, wuv_ref[...]).T
    tm = v_t.shape[1]
    extra = V_SLOT - V_DIM
    one_row = jnp.where(lax.broadcasted_iota(jnp.int32, (extra, tm), 0) == 0, 1.0, 0.0)
    v_ref[0] = jnp.concatenate(
        sum([[v_t[h * V_DIM:(h + 1) * V_DIM], one_row] for h in range(MLA_HEADS)], []), axis=0).astype(BF16)

    ones_bd = ones_ref[...]
    cos = jnp.concatenate([cos_ref[...]] * MLA_HEADS, axis=1)
    sa = jnp.concatenate([sa_ref[...]] * MLA_HEADS, axis=1)
    sb = jnp.concatenate([sb_ref[...]] * MLA_HEADS, axis=1)
    width = MLA_HEADS * HEAD_SLOT
    half = QK_ROPE // 4

    def norm_rope(t, gain):
        t = t * lax.rsqrt(_group_mean_sq(t, ones_bd, QK_DIM) + EPS) * gain
        return t * cos + pltpu.roll(t, width - half, 1) * sa + pltpu.roll(t, half, 1) * sb

    q = norm_rope(q_pre, gq_ref[...]) * (QK_DIM ** -0.5 * math.log2(math.e))
    q_ref[0] = q.T.astype(BF16)
    k_ref[0] = norm_rope(k_pre, gk_ref[...]).astype(BF16)


def _in_proj(xs, mod, lw, rope, n_lat_tiles):
    bsz, s, d = xs.shape
    tm = ROW_TILE
    nt = s // tm
    wcols = lw["w_big"].shape[1]
    slot_w = MLA_HEADS * HEAD_SLOT

    def row_spec(width):
        return pl.BlockSpec((1, tm, width), lambda t, b: (b, t, 0))

    def tab_spec():
        return pl.BlockSpec((tm, HEAD_SLOT), lambda t, b: (t, 0))

    def col_spec(height):
        return pl.BlockSpec((1, height, tm), lambda t, b: (b, 0, t))

    outs = [(d, BF16), (3 * BRANCH_W, BF16), (BRANCH_W, BF16), (LANES, F32), (1024, F32)]
    v_rows = MLA_HEADS * V_SLOT
    out_specs = [row_spec(w) for w, _ in outs] + [col_spec(slot_w), row_spec(slot_w), col_spec(v_rows)]
    out_shape = [jax.ShapeDtypeStruct((bsz, s, w), dt) for w, dt in outs] + [
        jax.ShapeDtypeStruct((bsz, slot_w, s), BF16), jax.ShapeDtypeStruct((bsz, s, slot_w), BF16),
        jax.ShapeDtypeStruct((bsz, v_rows, s), BF16)]
    return pl.pallas_call(
        _in_kernel,
        grid=(nt, bsz),
        in_specs=[
            row_spec(d),
            pl.BlockSpec((1, 6, d), lambda t, b: (jnp.where(t >= n_lat_tiles, bsz, b), 0, 0)),
            _const_spec((1, d)),
            _const_spec((d, wcols)),
            _const_spec((1, Q_LORA)),
            _const_spec((1, KV_LORA)),
            _const_spec((Q_LORA, slot_w)),
            _const_spec((KV_LORA, slot_w)),
            _const_spec((KV_LORA, BRANCH_W)),
            _const_spec((1, slot_w)),
            _const_spec((1, slot_w)),
            _const_spec((slot_w, slot_w)),
            tab_spec(), tab_spec(), tab_spec(),
        ],
        out_specs=out_specs,
        out_shape=out_shape,
        compiler_params=_cparams("parallel", "parallel"),
        name="norm_in_proj",
    )(xs, mod, lw["g_mix"], lw["w_big"], lw["g_cq"], lw["g_ckv"], lw["w_uq"], lw["w_uk"], lw["w_uv"],
      lw["g_qn"], lw["g_kn"], rope["ones_slot"], rope["cos"], rope["sin_a"], rope["sin_b"])


def _log_sigmoid(x):
    return jnp.minimum(x, 0.0) - jnp.log1p(jnp.exp(-jnp.abs(x)))


def _mlstm_kernel(qkv_f, qkv_b, g_f, g_b, bias_ref, hf_ref, hb_ref, c_sc, m_sc):
    @pl.when(pl.program_id(1) == 0)
    def _():
        c_sc[...] = jnp.zeros_like(c_sc)
        m_sc[...] = jnp.zeros_like(m_sc)

    L = ML_CHUNK
    dh = ML_DH
    row = lax.broadcasted_iota(jnp.int32, (L, L), 0)
    col = lax.broadcasted_iota(jnp.int32, (L, L), 1)
    lane = lax.broadcasted_iota(jnp.int32, (L, dh), 1)
    ones_col = jnp.where(lane == 0, 1.0, 0.0).astype(BF16)

    for d, (qkv_ref, g_ref, out_ref) in enumerate(((qkv_f, g_f, hf_ref), (qkv_b, g_b, hb_ref))):
        a = g_ref[0] + bias_ref[...]
        lf = _log_sigmoid(a)
        seen = (col <= row) if d == 0 else (col >= row)
        b_cols = _split_dot_left(seen.astype(BF16), lf)
        b_rows = b_cols.T
        a_rows = a.T
        last = L - 1 if d == 0 else 0
        outs = []
        for h in range(ML_HEADS):
            ci = d * 2 * ML_HEADS + h
            cf = ci + ML_HEADS
            st = d * ML_HEADS + h
            b_col = b_cols[:, cf:cf + 1]
            b_row = b_rows[cf:cf + 1, :]
            li_col = a[:, ci:ci + 1]
            li_row = a_rows[ci:ci + 1, :]
            b_end = b_cols[last:last + 1, cf:cf + 1]
            m_st = m_sc[st:st + 1, 0:1]
            q = qkv_ref[0, :, h * dh:(h + 1) * dh]
            k = qkv_ref[0, :, BRANCH_W + h * dh:BRANCH_W + (h + 1) * dh]
            v = qkv_ref[0, :, 2 * BRANCH_W + h * dh:2 * BRANCH_W + (h + 1) * dh]
            c_aug = c_sc[st]

            d_log = jnp.where(seen, b_col - b_row + li_row, -jnp.inf)
            inter = b_col + m_st
            m_t = jnp.maximum(inter, jnp.max(d_log, axis=1, keepdims=True))
            a_inter = jnp.exp(inter - m_t)
            s_qk = lax.dot_general(q, k, NT_DIMS, preferred_element_type=F32)
            w_ts = jnp.exp(d_log - m_t) * s_qk
            qc = _dot(q, c_aug.astype(BF16))
            num = a_inter * qc[:, 0:dh] + _dot(w_ts.astype(BF16), v)
            den = a_inter * qc[:, dh:dh + 1] + jnp.sum(w_ts, axis=1, keepdims=True)
            outs.append(num / jnp.maximum(jnp.abs(den), jnp.exp(-m_t)))

            w_log = b_end - b_col + li_col
            m_new = jnp.maximum(b_end + m_st, jnp.max(w_log, axis=0, keepdims=True))
            decay = jnp.exp(b_end + m_st - m_new)
            kw = (k.astype(F32) * jnp.exp(w_log - m_new)).astype(BF16)
            v_aug = jnp.concatenate([v, ones_col], axis=1)
            c_sc[st] = decay * c_aug + lax.dot_general(kw, v_aug, TN_DIMS, preferred_element_type=F32)
            m_sc[st:st + 1, :] = jnp.broadcast_to(m_new, (1, LANES))
        out_ref[0] = jnp.concatenate(outs, axis=1)


def _mlstm(qkv, g, gate_bias, n_lat_chunks):
    bsz, s, _ = qkv.shape
    nc = s // ML_CHUNK

    def fwd_map(b, j):
        return (b, (j + n_lat_chunks) % nc, 0)

    def bwd_map(b, j):
        return (b, nc - 1 - j, 0)

    def spec(width, imap):
        return pl.BlockSpec((1, ML_CHUNK, width), imap)

    return pl.pallas_call(
        _mlstm_kernel,
        grid=(bsz, nc),
        in_specs=[spec(3 * BRANCH_W, fwd_map), spec(3 * BRANCH_W, bwd_map),
                  spec(LANES, fwd_map), spec(LANES, bwd_map),
                  pl.BlockSpec((1, LANES), lambda b, j: (0, 0))],
        out_specs=[spec(BRANCH_W, fwd_map), spec(BRANCH_W, bwd_map)],
        out_shape=[jax.ShapeDtypeStruct((bsz, s, BRANCH_W), F32)] * 2,
        scratch_shapes=[pltpu.VMEM((2 * ML_HEADS, ML_DH, LANES), F32),
                        pltpu.VMEM((2 * ML_HEADS, LANES), F32)],
        compiler_params=_cparams("parallel", "arbitrary"),
        name="mlstm_scan",
    )(qkv, qkv, g, g, gate_bias)


def _attn_kernel(qt_ref, k_ref, vt_ref, o_ref, m_sc, acc_sc, *, n_lat, n_ctx, tk):
    m_sc[...] = jnp.full_like(m_sc, -jnp.inf)
    acc_sc[...] = jnp.zeros_like(acc_sc)

    def scores(h, start, size):
        sl = slice(h * HEAD_SLOT, (h + 1) * HEAD_SLOT)
        return _dot(k_ref[0, pl.ds(start, size), sl], qt_ref[0, sl, :])

    def keys(start, size):
        n_sub = size // ATT_SUB
        order = [(j, h) for j in range(n_sub) for h in range(MLA_HEADS)]
        pending = [scores(h, start + j * ATT_SUB, ATT_SUB) for j, h in order[:ATT_AHEAD]]
        for i, (j, h) in enumerate(order):
            off = start + j * ATT_SUB
            s = pending.pop(0)
            if i + ATT_AHEAD < len(order):
                jn, hn = order[i + ATT_AHEAD]
                pending.append(scores(hn, start + jn * ATT_SUB, ATT_SUB))
            m_old = m_sc[h]
            m_new = jnp.maximum(m_old, jnp.max(s, axis=0, keepdims=True))
            p = jnp.exp2(s - m_new)
            vt = vt_ref[0, h * V_SLOT:(h + 1) * V_SLOT, pl.ds(off, ATT_SUB)]
            acc_sc[h] = jnp.exp2(m_old - m_new) * acc_sc[h] + _dot(vt, p.astype(BF16))
            m_sc[h] = m_new

    keys(n_lat, n_ctx)

    def chunk(c, carry):
        keys(pl.multiple_of(c * tk, tk), tk)
        return carry

    is_ctx_tile = pl.program_id(1) * ATT_TQ >= n_lat
    lax.fori_loop(0, jnp.where(is_ctx_tile, 0, n_lat // tk), chunk, 0)
    out_t = jnp.concatenate([acc_sc[h, 0:V_DIM, :] / acc_sc[h, V_DIM:V_DIM + 1, :] for h in range(MLA_HEADS)],
                            axis=0)
    o_ref[0] = out_t.T.astype(o_ref.dtype)


def _key_chunk(n_keys):
    return max(t for t in range(ATT_TQ, ATT_TK + 1, ATT_TQ) if n_keys % t == 0)


def _attention(qt, k, vt, n_lat):
    bsz, s, slot_w = k.shape
    tq = ATT_TQ
    return pl.pallas_call(
        functools.partial(_attn_kernel, n_lat=n_lat, n_ctx=s - n_lat, tk=_key_chunk(n_lat)),
        grid=(bsz, s // tq),
        in_specs=[pl.BlockSpec((1, slot_w, tq), lambda b, t: (b, 0, t)),
                  pl.BlockSpec((1, s, slot_w), lambda b, t: (b, 0, 0), pipeline_mode=pl.Buffered(1)),
                  pl.BlockSpec((1, MLA_HEADS * V_SLOT, s), lambda b, t: (b, 0, 0), pipeline_mode=pl.Buffered(1))],
        out_specs=pl.BlockSpec((1, tq, BRANCH_W), lambda b, t: (b, t, 0)),
        out_shape=jax.ShapeDtypeStruct((bsz, s, BRANCH_W), BF16),
        scratch_shapes=[pltpu.VMEM((MLA_HEADS, 1, tq), F32), pltpu.VMEM((MLA_HEADS, V_SLOT, tq), F32)],
        compiler_params=_cparams("parallel", "arbitrary"),
        name="mla_attention",
    )(qt, k, vt)


def _merge_kernel(hx_ref, hf_ref, hb_ref, o_ref, ya_ref, pc_ref, prev_ref, next_ref, x_ref, mod_ref,
                  hg_ref, ones64_ref, pw_ref, ps_ref, cw_ref, cb_ref, wg_ref, bg_ref, wb_ref, wo_ref,
                  gffn_ref, xo_ref, h2_ref, *, n_lat_tiles, n_tiles):
    t = pl.program_id(1)
    tm = ROW_TILE
    bw = BRANCH_W
    halo = POOL_HALO
    no_prev = jnp.logical_or(t == 0, t == n_lat_tiles)
    no_next = jnp.logical_or(t == n_lat_tiles - 1, t == n_tiles - 1)

    h = hf_ref[0] + hb_ref[0]
    ms = _split_dot(h * h, ones64_ref[...]) * (1.0 / ML_DH)
    y_ml = (h * lax.rsqrt(ms + EPS) * hg_ref[...]) * jax.nn.sigmoid(o_ref[0].astype(F32))

    prev = jnp.where(no_prev, 0.0, prev_ref[0])
    nxt = jnp.where(no_next, 0.0, next_ref[0])
    ext = jnp.concatenate([prev, pc_ref[0], nxt], axis=0)
    pe = ext[:, 0:bw]

    def rows(arr, off):
        return arr[halo + off:halo + off + tm, :]

    r = lax.broadcasted_iota(jnp.int32, (tm, 1), 0)
    lane = lax.broadcasted_iota(jnp.int32, (tm, bw), 1)
    centre = rows(pe, 0)
    acc = centre
    mean = jnp.zeros((tm, bw), F32)
    done = 0
    for gi, w in enumerate(POOL_WINDOWS):
        for off in list(range(-(w // 2), -done)) + list(range(max(done, 1), w // 2)):
            acc = acc + rows(pe, off)
        done = w // 2
        before = jnp.where(no_prev, jnp.minimum(r, w // 2), w // 2)
        after = jnp.where(no_next, jnp.minimum(tm - r, w // 2), w // 2)
        inv = 1.0 / (before + after).astype(F32)
        mean = jnp.where(lane >= gi * POOL_GROUP, acc * inv, mean)
    y_pool = _dot((mean - centre).astype(BF16), pw_ref[...]) * ps_ref[...]

    z = ext[:, 3 * bw:4 * bw] * ext[:, bw:2 * bw]
    conv = cb_ref[...] + rows(z, -1) * cw_ref[0:1, :] + rows(z, 0) * cw_ref[1:2, :] + rows(z, 1) * cw_ref[2:3, :]
    y_conv = rows(ext[:, 2 * bw:3 * bw], 0) * conv

    hxb = hx_ref[0]
    d = hxb.shape[1]
    gates = jax.nn.sigmoid(_dot(hxb, wg_ref[...]) + bg_ref[...])
    ys = (y_ml.astype(BF16), ya_ref[0], y_pool.astype(BF16), y_conv.astype(BF16))
    merged = None
    for i, y in enumerate(ys):
        term = gates[:, i * d:(i + 1) * d] * _dot(y, wb_ref[i])
        merged = term if merged is None else merged + term
    out = _dot(merged.astype(BF16), wo_ref[...])
    x_new = x_ref[0] + mod_ref[0, 2:3, :] * out
    xo_ref[0] = x_new
    h2_ref[0] = (_rms_rows(x_new) * gffn_ref[...]) * (1.0 + mod_ref[0, 4:5, :]) + mod_ref[0, 3:4, :]


def _merge(hx, hf, hb, o, y_mla, pc, xs, mod, lw, n_lat_tiles):
    bsz, s, d = xs.shape
    tm = ROW_TILE
    nt = s // tm
    halo = POOL_HALO
    per = tm // halo

    def row_spec(width):
        return pl.BlockSpec((1, tm, width), lambda b, t: (b, t, 0))

    return pl.pallas_call(
        functools.partial(_merge_kernel, n_lat_tiles=n_lat_tiles, n_tiles=nt),
        grid=(bsz, nt),
        in_specs=[
            row_spec(d), row_spec(BRANCH_W), row_spec(BRANCH_W), row_spec(BRANCH_W), row_spec(BRANCH_W),
            row_spec(1024),
            pl.BlockSpec((1, halo, 1024), lambda b, t: (b, jnp.maximum(t * per - 1, 0), 0)),
            pl.BlockSpec((1, halo, 1024), lambda b, t: (b, jnp.minimum((t + 1) * per, s // halo - 1), 0)),
            row_spec(d),
            pl.BlockSpec((1, 6, d), lambda b, t: (jnp.where(t >= n_lat_tiles, bsz, b), 0, 0)),
            _const_spec((1, BRANCH_W)), _const_spec((BRANCH_W, BRANCH_W)), _const_spec((BRANCH_W, BRANCH_W)),
            _const_spec((1, BRANCH_W)), _const_spec((3, BRANCH_W)), _const_spec((1, BRANCH_W)),
            _const_spec((d, 4 * d)), _const_spec((1, 4 * d)), _const_spec((4, BRANCH_W, d)),
            _const_spec((d, d)), _const_spec((1, d)),
        ],
        out_specs=[row_spec(d), row_spec(d)],
        out_shape=[jax.ShapeDtypeStruct((bsz, s, d), F32)] * 2,
        compiler_params=_cparams("parallel", "parallel"),
        name="mixer_merge",
    )(hx, hf, hb, o, y_mla, pc, pc, pc, xs, mod, lw["head_gain"], lw["ones64"], lw["pool_w"], lw["pool_scale"],
      lw["conv_w"], lw["conv_b"], lw["w_gate"], lw["b_gate"], lw["w_branch"], lw["w_out"], lw["g_ffn"])


def _router_kernel(h_ref, wr_ref, br_ref, pe_ref, we_ref, c16_ref):
    tm = h_ref.shape[0]
    scores = jax.nn.sigmoid(_dot(h_ref[...].astype(BF16), wr_ref[...]))
    sel = scores + br_ref[...]
    lane = lax.broadcasted_iota(jnp.int32, (tm, LANES), 1)
    member = jnp.zeros((tm, LANES), F32)
    for _ in range(TOP_K):
        mx = jnp.max(sel, axis=1, keepdims=True)
        ix = jnp.min(jnp.where(sel == mx, lane, LANES), axis=1, keepdims=True)
        hit = lane == ix
        member = jnp.where(hit, 1.0, member)
        sel = jnp.where(hit, -jnp.inf, sel)
    picked = member * scores
    weights = picked / jnp.sum(picked, axis=1, keepdims=True) * ROUTE_SCALE
    counts = jnp.sum(member, axis=0, keepdims=True)
    c16 = jnp.floor((counts + (RUN_ROWS - 1)) * (1.0 / RUN_ROWS))
    e_row = lax.broadcasted_iota(jnp.int32, (LANES, LANES), 0)
    e_col = lax.broadcasted_iota(jnp.int32, (LANES, LANES), 1)
    lower = (e_row < e_col).astype(BF16)
    o16 = _dot(jnp.broadcast_to(c16, (SUBLANES, LANES)).astype(BF16), lower)[0:1]
    row = lax.broadcasted_iota(jnp.int32, (tm, tm), 0)
    col = lax.broadcasted_iota(jnp.int32, (tm, tm), 1)
    rank = _dot((col < row).astype(BF16), member.astype(BF16))
    pos = jnp.where(member > 0.0, RUN_ROWS * o16 + rank, -1.0)
    pe_ref[0] = pos.T[0:N_EXPERTS]
    we_ref[0] = weights.T[0:N_EXPERTS]
    c16_ref[0] = c16.astype(jnp.int32)


def _router(h2, lw):
    n, d = h2.shape
    tm = MOE_TILE
    nt = n // tm

    def t_spec():
        return pl.BlockSpec((1, N_EXPERTS, tm), lambda i: (i, 0, 0))

    return pl.pallas_call(
        _router_kernel,
        grid=(nt,),
        in_specs=[pl.BlockSpec((tm, d), lambda i: (i, 0)), _const_spec((d, LANES)), _const_spec((1, LANES))],
        out_specs=[t_spec(), t_spec(), pl.BlockSpec((1, 1, LANES), lambda i: (i, 0, 0))],
        out_shape=[jax.ShapeDtypeStruct((nt, N_EXPERTS, tm), F32), jax.ShapeDtypeStruct((nt, N_EXPERTS, tm), F32),
                   jax.ShapeDtypeStruct((nt, 1, LANES), jnp.int32)],
        compiler_params=_cparams("parallel"),
        name="moe_router",
    )(h2, lw["w_router"], lw["b_router"])


def _tile_row_bound(tm):
    worst = tm * TOP_K + N_EXPERTS * (RUN_ROWS - 1)
    return -(-worst // MOE_CHUNK) * MOE_CHUNK


def _selection_rows(g, grp_e_ref, pe_ref, tm):
    prow = pe_ref[0, pl.ds(grp_e_ref[0, 0, g], 1), :]
    rows = (lax.broadcasted_iota(jnp.int32, (RUN_ROWS, tm), 0) + g * RUN_ROWS).astype(F32)
    return prow, rows


def _dispatch_kernel(r16_ref, tail_ref, n_act_ref, grp_e_ref, gdst_ref, h_ref, pe_ref, xs_ref,
                     sel_sc, xb_sc, xp_sc, zero_sc, sem, *, n_groups):
    i = pl.program_id(0)
    tm = h_ref.shape[0]
    r16 = r16_ref[i]

    def build(g, carry):
        prow, rows = _selection_rows(g, grp_e_ref, pe_ref, tm)
        sel_sc[pl.ds(pl.multiple_of(g * RUN_ROWS, RUN_ROWS), RUN_ROWS), :] = jnp.where(
            prow == rows, 1.0, 0.0).astype(BF16)
        return carry

    lax.fori_loop(0, n_groups, build, 0)
    xb_sc[...] = h_ref[...].astype(BF16)

    def permute(ch, carry):
        r0 = pl.multiple_of(ch * MOE_CHUNK, MOE_CHUNK)
        xp_sc[pl.ds(r0, MOE_CHUNK), :] = _dot(sel_sc[pl.ds(r0, MOE_CHUNK), :], xb_sc[...]).astype(BF16)
        return carry

    lax.fori_loop(0, (r16 * RUN_ROWS + MOE_CHUNK - 1) // MOE_CHUNK, permute, 0)

    def run_copy(g):
        return pltpu.make_async_copy(
            xp_sc.at[pl.ds(pl.multiple_of(g * RUN_ROWS, RUN_ROWS), RUN_ROWS)],
            xs_ref.at[pl.ds(pl.multiple_of(gdst_ref[0, 0, g] * RUN_ROWS, RUN_ROWS), RUN_ROWS)], sem)

    def start(g, carry):
        run_copy(g).start()
        return carry

    def wait(g, carry):
        run_copy(g).wait()
        return carry

    lax.fori_loop(0, r16, start, 0)
    lax.fori_loop(0, r16, wait, 0)

    @pl.when(i == pl.num_programs(0) - 1)
    def _():
        zero_sc[...] = jnp.zeros_like(zero_sc)

        def fill_copy(q):
            return pltpu.make_async_copy(
                zero_sc.at[pl.ds(0, RUN_ROWS)],
                xs_ref.at[pl.ds(pl.multiple_of(tail_ref[q] * RUN_ROWS, RUN_ROWS), RUN_ROWS)], sem)

        def block_copy(b):
            return pltpu.make_async_copy(
                zero_sc, xs_ref.at[pl.ds(pl.multiple_of(b * EXPERT_ROWS, EXPERT_ROWS), EXPERT_ROWS)], sem)

        def block_start(b, carry):
            block_copy(b).start()
            return carry

        def block_wait(b, carry):
            block_copy(b).wait()
            return carry

        n_blocks = xs_ref.shape[0] // EXPERT_ROWS
        lax.fori_loop(n_act_ref[0], n_blocks, block_start, 0)
        lax.fori_loop(n_act_ref[0], n_blocks, block_wait, 0)

        def fill_start(q, carry):
            @pl.when(tail_ref[q] >= 0)
            def _():
                fill_copy(q).start()
            return carry

        def fill_wait(q, carry):
            @pl.when(tail_ref[q] >= 0)
            def _():
                fill_copy(q).wait()
            return carry

        lax.fori_loop(0, tail_ref.shape[0], fill_start, 0)
        lax.fori_loop(0, tail_ref.shape[0], fill_wait, 0)


def _dispatch(h2, pe, plan, n_rows):
    n, d = h2.shape
    tm = MOE_TILE
    nt = n // tm
    rb = _tile_row_bound(tm)
    n_groups = rb // RUN_ROWS

    def smem_spec():
        return pl.BlockSpec((1, 1, n_groups), lambda i, *_: (i, 0, 0), memory_space=pltpu.SMEM)

    return pl.pallas_call(
        functools.partial(_dispatch_kernel, n_groups=n_groups),
        grid_spec=pltpu.PrefetchScalarGridSpec(
            num_scalar_prefetch=3,
            grid=(nt,),
            in_specs=[smem_spec(), smem_spec(),
                      pl.BlockSpec((tm, d), lambda i, *_: (i, 0)),
                      pl.BlockSpec((1, N_EXPERTS, tm), lambda i, *_: (i, 0, 0))],
            out_specs=pl.BlockSpec(memory_space=pl.ANY),
            scratch_shapes=[pltpu.VMEM((rb, tm), BF16), pltpu.VMEM((tm, d), BF16), pltpu.VMEM((rb, d), BF16),
                            pltpu.VMEM((EXPERT_ROWS, d), BF16), pltpu.SemaphoreType.DMA(())],
        ),
        out_shape=jax.ShapeDtypeStruct((n_rows, d), BF16),
        compiler_params=_cparams("arbitrary"),
        name="moe_dispatch",
    )(plan["r16"], plan["tail"], plan["n_act"].reshape(1), plan["grp_e"], plan["gdst"], h2, pe)


def _expert_kernel(blk_e_ref, n_act_ref, x_ref, w13_ref, w2_ref, y_ref):
    del blk_e_ref

    @pl.when(pl.program_id(0) < n_act_ref[0])
    def _():
        up = _dot(x_ref[...], w13_ref[0])
        a = up[:, 0:EXPERT_FF]
        act = (a * jax.nn.sigmoid(a)) * up[:, EXPERT_FF:2 * EXPERT_FF]
        y_ref[...] = _dot(act.astype(BF16), w2_ref[0]).astype(BF16)

    @pl.when(pl.program_id(0) >= n_act_ref[0])
    def _():
        y_ref[...] = jnp.zeros_like(y_ref)


def _experts(xs_sorted, blk_e, n_act, lw):
    n_rows, d = xs_sorted.shape
    bm = EXPERT_ROWS

    def row_map(i, blk_e_ref, n_act_ref):
        return (jnp.minimum(i, n_act_ref[0] - 1), 0)

    def w_map(i, blk_e_ref, n_act_ref):
        return (blk_e_ref[i], 0, 0)

    return pl.pallas_call(
        _expert_kernel,
        grid_spec=pltpu.PrefetchScalarGridSpec(
            num_scalar_prefetch=2,
            grid=(n_rows // bm,),
            in_specs=[pl.BlockSpec((bm, d), row_map),
                      pl.BlockSpec((1, d, 2 * EXPERT_FF), w_map),
                      pl.BlockSpec((1, EXPERT_FF, d), w_map)],
            out_specs=pl.BlockSpec((bm, d), lambda i, *_: (i, 0)),
        ),
        out_shape=jax.ShapeDtypeStruct((n_rows, d), BF16),
        compiler_params=_cparams("arbitrary"),
        name="moe_experts",
    )(blk_e, n_act, xs_sorted, lw["w13"], lw["w2"])


def _combine_kernel(r16_ref, grp_e_ref, gdst_ref, h_ref, x_ref, pe_ref, we_ref, mod_ref, modc_ref, ws13_ref,
                    ws2_ref, ys_ref, xo_ref, sel_sc, yp_sc, acc_sc, sem, *, n_groups, tiles_per_sample, n_ctx):
    i = pl.program_id(0)
    tm = h_ref.shape[0]
    r16 = r16_ref[i]
    n_ch = (r16 * RUN_ROWS + MOE_CHUNK - 1) // MOE_CHUNK

    def run_copy(g):
        return pltpu.make_async_copy(
            ys_ref.at[pl.ds(pl.multiple_of(gdst_ref[0, 0, g] * RUN_ROWS, RUN_ROWS), RUN_ROWS)],
            yp_sc.at[pl.ds(pl.multiple_of(g * RUN_ROWS, RUN_ROWS), RUN_ROWS)], sem)

    def start(g, carry):
        run_copy(g).start()
        return carry

    def wait(g, carry):
        run_copy(g).wait()
        return carry

    lax.fori_loop(0, r16, start, 0)

    def build(g, carry):
        prow, rows = _selection_rows(g, grp_e_ref, pe_ref, tm)
        wrow = we_ref[0, pl.ds(grp_e_ref[0, 0, g], 1), :]
        sel_sc[pl.ds(pl.multiple_of(g * RUN_ROWS, RUN_ROWS), RUN_ROWS), :] = jnp.where(
            prow == rows, wrow, 0.0).astype(BF16)
        return carry

    lax.fori_loop(0, n_groups, build, 0)

    def clear(g, carry):
        yp_sc[pl.ds(pl.multiple_of(g * RUN_ROWS, RUN_ROWS), RUN_ROWS), :] = jnp.zeros(
            (RUN_ROWS, yp_sc.shape[1]), BF16)
        return carry

    lax.fori_loop(r16, n_ch * (MOE_CHUNK // RUN_ROWS), clear, 0)

    up = _dot(h_ref[...].astype(BF16), ws13_ref[...])
    a = up[:, 0:EXPERT_FF]
    act = (a * jax.nn.sigmoid(a)) * up[:, EXPERT_FF:2 * EXPERT_FF]
    acc_sc[...] = _dot(act.astype(BF16), ws2_ref[...])
    lax.fori_loop(0, r16, wait, 0)

    def gather_sum(ch, carry):
        r0 = pl.multiple_of(ch * MOE_CHUNK, MOE_CHUNK)
        acc_sc[...] += lax.dot_general(sel_sc[pl.ds(r0, MOE_CHUNK), :], yp_sc[pl.ds(r0, MOE_CHUNK), :], TN_DIMS,
                                       preferred_element_type=F32)
        return carry

    lax.fori_loop(0, n_ch, gather_sum, 0)
    r = lax.broadcasted_iota(jnp.int32, (tm, 1), 0)
    is_ctx = jnp.logical_and(i % tiles_per_sample == tiles_per_sample - 1, r >= tm - n_ctx)
    gate = jnp.where(is_ctx, modc_ref[0, 5:6, :], mod_ref[0, 5:6, :])
    xo_ref[...] = x_ref[...] + gate * acc_sc[...]


def _combine(h2, xs, pe, we, plan, mod, ys_sorted, lw, tiles_per_sample, n_ctx, bsz):
    n, d = h2.shape
    tm = MOE_TILE
    nt = n // tm
    rb = _tile_row_bound(tm)
    n_groups = rb // RUN_ROWS

    def smem_spec():
        return pl.BlockSpec((1, 1, n_groups), lambda i, *_: (i, 0, 0), memory_space=pltpu.SMEM)

    def row_spec():
        return pl.BlockSpec((tm, d), lambda i, *_: (i, 0))

    def t_spec():
        return pl.BlockSpec((1, N_EXPERTS, tm), lambda i, *_: (i, 0, 0))

    return pl.pallas_call(
        functools.partial(_combine_kernel, n_groups=n_groups, tiles_per_sample=tiles_per_sample, n_ctx=n_ctx),
        grid_spec=pltpu.PrefetchScalarGridSpec(
            num_scalar_prefetch=1,
            grid=(nt,),
            in_specs=[smem_spec(), smem_spec(), row_spec(), row_spec(), t_spec(), t_spec(),
                      pl.BlockSpec((1, 6, d), lambda i, *_: (i // tiles_per_sample, 0, 0)),
                      pl.BlockSpec((1, 6, d), lambda i, *_: (bsz, 0, 0)),
                      pl.BlockSpec((d, 2 * EXPERT_FF), lambda i, *_: (0, 0), pipeline_mode=pl.Buffered(1)),
                      pl.BlockSpec((EXPERT_FF, d), lambda i, *_: (0, 0), pipeline_mode=pl.Buffered(1)),
                      pl.BlockSpec(memory_space=pl.ANY)],
            out_specs=row_spec(),
            scratch_shapes=[pltpu.VMEM((rb, tm), BF16), pltpu.VMEM((rb, d), BF16), pltpu.VMEM((tm, d), F32),
                            pltpu.SemaphoreType.DMA(())],
        ),
        out_shape=jax.ShapeDtypeStruct((n, d), F32),
        compiler_params=_cparams("arbitrary"),
        name="moe_combine",
    )(plan["r16"], plan["grp_e"], plan["gdst"], h2, xs, pe, we, mod, mod, lw["ws13"], lw["ws2"], ys_sorted)


def _moe_plan(c16, n_groups):
    nt = c16.shape[0]
    per_blk = EXPERT_ROWS // RUN_ROWS
    o16 = jnp.cumsum(c16, axis=1) - c16
    before16 = jnp.cumsum(c16, axis=0) - c16
    gtot16 = jnp.sum(c16, axis=0)
    gblk = (gtot16 + per_blk - 1) // per_blk
    gend_blk = jnp.cumsum(gblk)
    gstart16 = per_blk * (gend_blk - gblk)
    g = jnp.arange(n_groups, dtype=jnp.int32)
    grp_e = jnp.minimum(jnp.sum((o16 + c16)[:, None, :] <= g[None, :, None], axis=2), N_EXPERTS - 1)
    run0 = gstart16[None, :] + before16 - o16
    gdst = jnp.take_along_axis(run0, grp_e, axis=1) + g[None, :]
    q = jnp.arange(per_blk, dtype=jnp.int32)
    tail = jnp.where(q[None, :] < (per_blk * gblk - gtot16)[:, None],
                     (gstart16 + gtot16)[:, None] + q[None, :], -1)
    return {
        "r16": jnp.sum(c16, axis=1).astype(jnp.int32),
        "grp_e": grp_e.astype(jnp.int32).reshape(nt, 1, n_groups),
        "gdst": gdst.astype(jnp.int32).reshape(nt, 1, n_groups),
        "tail": tail.astype(jnp.int32).reshape(-1),
        "gend_blk": gend_blk, "n_act": gend_blk[-1].astype(jnp.int32),
    }


def _moe(h2, xs, mod, lw, tiles_per_sample, n_ctx, bsz):
    n, d = h2.shape
    nt = n // MOE_TILE
    n_groups = _tile_row_bound(MOE_TILE) // RUN_ROWS
    pe, we, c16 = _router(h2, lw)
    plan = _moe_plan(c16[:, 0, :N_EXPERTS], n_groups)
    worst_rows = n * TOP_K + nt * N_EXPERTS * (RUN_ROWS - 1) + N_EXPERTS * (EXPERT_ROWS - 1)
    n_blocks = -(-worst_rows // EXPERT_ROWS)
    blk = jnp.minimum(jnp.arange(n_blocks, dtype=jnp.int32), plan["n_act"] - 1)
    blk_e = jnp.minimum(jnp.sum(plan["gend_blk"][None, :] <= blk[:, None], axis=1), N_EXPERTS - 1).astype(jnp.int32)
    xs_sorted = _dispatch(h2, pe, plan, n_blocks * EXPERT_ROWS)
    ys_sorted = _experts(xs_sorted, blk_e, plan["n_act"].reshape(1), lw)
    return _combine(h2, xs, pe, we, plan, mod, ys_sorted, lw, tiles_per_sample, n_ctx, bsz)


def _slots(w, head_w, real_w):
    rows = w.shape[0]
    w = w.reshape(rows, -1, head_w)[:, :, :real_w]
    return jnp.pad(w, ((0, 0), (0, 0), (0, HEAD_SLOT - real_w))).reshape(rows, -1)


def _block_diag(blocks):
    n, r, c = blocks.shape
    out = jnp.zeros((n * r, n * c), blocks.dtype)
    for i in range(n):
        out = out.at[i * r:(i + 1) * r, i * c:(i + 1) * c].set(blocks[i])
    return out


def _layer_weights(l, p):
    d = p["w_in"].shape[1]
    w_in = p["w_in"][l]
    ml = 4 * BRANCH_W
    o_mla = ml + 4 * ML_HEADS
    o_pool = o_mla + Q_LORA + KV_LORA + QK_ROPE
    kr = w_in[:, o_mla + Q_LORA + KV_LORA:o_pool]
    kr_slot = jnp.pad(kr, ((0, 0), (QK_NOPE, HEAD_SLOT - QK_DIM)))
    w_big = jnp.concatenate([
        w_in[:, :ml],
        w_in[:, o_pool:],
        w_in[:, o_mla:o_mla + Q_LORA + KV_LORA], jnp.zeros((d, LANES), F32),
        jnp.tile(kr_slot, (1, MLA_HEADS)),
        jnp.pad(w_in[:, ml:o_mla], ((0, 0), (0, LANES - 4 * ML_HEADS))),
    ], axis=1).astype(BF16)
    w_ukv = p["mla_w_ukv"][l].reshape(KV_LORA, MLA_HEADS, QK_NOPE + V_DIM)

    def gain_slots(g):
        return jnp.tile(jnp.pad(g, (0, HEAD_SLOT - QK_DIM)), MLA_HEADS)[None, :]

    return {
        "g_mix": p["g_mix"][l][None, :], "g_ffn": p["g_ffn"][l][None, :],
        "w_big": w_big,
        "g_cq": p["mla_g_cq"][l][None, :], "g_ckv": p["mla_g_ckv"][l][None, :],
        "w_uq": _slots(p["mla_w_uq"][l], QK_DIM, QK_DIM).astype(BF16),
        "w_uk": _slots(w_ukv[:, :, :QK_NOPE].reshape(KV_LORA, -1), QK_NOPE, QK_NOPE).astype(BF16),
        "w_uv": w_ukv[:, :, QK_NOPE:].reshape(KV_LORA, BRANCH_W).astype(BF16),
        "g_qn": gain_slots(p["mla_g_qn"][l]), "g_kn": gain_slots(p["mla_g_kn"][l]),
        "gate_bias": jnp.pad(p["ml_gate_bias"][l], (0, LANES - 4 * ML_HEADS))[None, :],
        "head_gain": p["ml_head_gain"][l][None, :],
        "ones64": _block_diag(jnp.ones((ML_HEADS, ML_DH, ML_DH), BF16)),
        "pool_w": _block_diag(p["pool_w"][l]).astype(BF16),
        "pool_scale": p["pool_scale"][l][None, :],
        "conv_w": p["conv_w"][l], "conv_b": p["conv_b"][l][None, :],
        "w_gate": jnp.concatenate(list(p["w_gate"][l]), axis=1).astype(BF16),
        "b_gate": p["b_gate"][l].reshape(1, -1),
        "w_branch": p["w_branch"][l].astype(BF16),
        "w_out": p["w_out"][l].astype(BF16),
        "w_router": jnp.pad(p["moe_w_router"][l], ((0, 0), (0, LANES - N_EXPERTS))).astype(BF16),
        "b_router": jnp.pad(p["moe_b_router"][l], (0, LANES - N_EXPERTS), constant_values=-1e30)[None, :],
        "w13": jnp.concatenate([p["moe_w1"][l], p["moe_w3"][l]], axis=2).astype(BF16),
        "w2": p["moe_w2"][l].astype(BF16),
        "ws13": jnp.concatenate([p["moe_ws1"][l], p["moe_ws3"][l]], axis=1).astype(BF16),
        "ws2": p["moe_ws2"][l].astype(BF16),
    }


def _rope_tables(n_lat, n_ctx):
    t = jnp.arange(n_lat)
    n_freq = QK_ROPE // 4
    inv = ROPE_THETA ** (-jnp.arange(n_freq, dtype=F32) / n_freq)
    ang_r = (t // GRID_W).astype(F32)[:, None] * inv
    ang_c = (t % GRID_W).astype(F32)[:, None] * inv
    cos4 = jnp.concatenate([jnp.cos(ang_r)] * 2 + [jnp.cos(ang_c)] * 2, axis=1)
    zero = jnp.zeros_like(ang_r)
    sin_a = jnp.concatenate([-jnp.sin(ang_r), zero, -jnp.sin(ang_c), zero], axis=1)
    sin_b = jnp.concatenate([zero, jnp.sin(ang_r), zero, jnp.sin(ang_c)], axis=1)

    def slot(a, fill):
        a = jnp.pad(a, ((0, 0), (QK_NOPE, 0)), constant_values=fill)
        a = jnp.pad(a, ((0, 0), (0, HEAD_SLOT - QK_DIM)), constant_values=fill)
        return jnp.pad(a, ((0, n_ctx), (0, 0)), constant_values=fill)

    return {"cos": slot(cos4, 1.0), "sin_a": slot(sin_a, 0.0), "sin_b": slot(sin_b, 0.0),
            "ones_slot": _block_diag(jnp.ones((MLA_HEADS, HEAD_SLOT, HEAD_SLOT), BF16))}


def kernel(x, c, ctx, c_ctx, w_mod, b_mod, g_mix, g_ffn, w_in, ml_gate_bias, ml_head_gain, mla_g_cq, mla_g_ckv,
           mla_w_uq, mla_w_ukv, mla_g_qn, mla_g_kn, pool_w, pool_scale, conv_w, conv_b, w_gate, b_gate, w_branch,
           w_out, moe_w_router, moe_b_router, moe_w1, moe_w3, moe_w2, moe_ws1, moe_ws3, moe_ws2):
    p = dict(g_mix=g_mix, g_ffn=g_ffn, w_in=w_in, ml_gate_bias=ml_gate_bias, ml_head_gain=ml_head_gain,
             mla_g_cq=mla_g_cq, mla_g_ckv=mla_g_ckv, mla_w_uq=mla_w_uq, mla_w_ukv=mla_w_ukv, mla_g_qn=mla_g_qn,
             mla_g_kn=mla_g_kn, pool_w=pool_w, pool_scale=pool_scale, conv_w=conv_w, conv_b=conv_b,
             w_gate=w_gate, b_gate=b_gate, w_branch=w_branch, w_out=w_out, moe_w_router=moe_w_router,
             moe_b_router=moe_b_router, moe_w1=moe_w1, moe_w3=moe_w3, moe_w2=moe_w2, moe_ws1=moe_ws1,
             moe_ws3=moe_ws3, moe_ws2=moe_ws2)
    bsz, n_lat, d = x.shape
    n_ctx = ctx.shape[1]
    depth = w_mod.shape[0]
    s = n_lat + n_ctx
    assert n_ctx == ROW_TILE == ATT_TQ and n_lat % ROW_TILE == 0 and n_lat % GRID_W == 0
    assert s % MOE_TILE == 0 and n_ctx <= MOE_TILE
    n_lat_tiles = n_lat // ROW_TILE

    mod_rows = -(-(bsz + 1) // SUBLANES) * SUBLANES
    cc = jnp.concatenate([c, c_ctx[None, :], jnp.zeros((mod_rows - bsz - 1, d), F32)], axis=0)
    mods = _modulation(cc, w_mod, b_mod).reshape(depth, mod_rows, 6, d)
    rope = _rope_tables(n_lat, n_ctx)
    xs = jnp.concatenate([x, ctx], axis=1)

    for l in range(depth):
        lw = _layer_weights(l, p)
        mod = mods[l]
        hx, qkv, o, g, pc, q, k, v = _in_proj(xs, mod, lw, rope, n_lat_tiles)
        hf, hb = _mlstm(qkv, g, lw["gate_bias"], n_lat // ML_CHUNK)
        y_mla = _attention(q, k, v, n_lat)
        xs, h2 = _merge(hx, hf, hb, o, y_mla, pc, xs, mod, lw, n_lat_tiles)
        xs = _moe(h2.reshape(bsz * s, d), xs.reshape(bsz * s, d), mod, lw, s // MOE_TILE, n_ctx,
                  bsz).reshape(bsz, s, d)
    return xs[:, :n_lat]
```

```python
import functools
import math

import jax
import jax.numpy as jnp
from jax import lax
from jax.experimental import pallas as pl
from jax.experimental.pallas import tpu as pltpu

GRID_W = 64
BRANCH_W = 256
EPS = 1e-6
ML_HEADS = 4
ML_DH = BRANCH_W // ML_HEADS
ML_CHUNK = 128
MLA_HEADS = 4
Q_LORA = 256
KV_LORA = 128
QK_NOPE = 64
QK_ROPE = 32
QK_DIM = QK_NOPE + QK_ROPE
V_DIM = BRANCH_W // MLA_HEADS
V_SLOT = V_DIM + 16
ROPE_THETA = 10000.0
POOL_WINDOWS = (2, 4, 8, 16)
POOL_GROUP = BRANCH_W // len(POOL_WINDOWS)
POOL_HALO = max(POOL_WINDOWS) // 2
N_EXPERTS = 64
TOP_K = 6
EXPERT_FF = 256
ROUTE_SCALE = 2.5

LANES = 128
SUBLANES = 8
HEAD_SLOT = LANES
ROW_TILE = 256
ATT_TQ = 256
ATT_TK = 2048
ATT_SUB = 128
ATT_AHEAD = 8
EXPERT_ROWS = 512
MOE_TILE = 768
RUN_ROWS = 16
MOE_CHUNK = 512
VMEM_LIMIT = 56 * 1024 * 1024

F32 = jnp.float32
BF16 = jnp.bfloat16
NT_DIMS = (((1,), (1,)), ((), ()))
TN_DIMS = (((0,), (0,)), ((), ()))


def _cparams(*sem):
    return pltpu.CompilerParams(dimension_semantics=sem, vmem_limit_bytes=VMEM_LIMIT)


def _const_spec(shape):
    nd = len(shape)
    return pl.BlockSpec(shape, lambda *_: (0,) * nd, pipeline_mode=pl.Buffered(1))


def _dot(a, b):
    return jnp.dot(a, b, preferred_element_type=F32)


def _split_dot(a_f32, ones_bf16):
    hi = a_f32.astype(BF16)
    r1 = a_f32 - hi.astype(F32)
    mid = r1.astype(BF16)
    lo = (r1 - mid.astype(F32)).astype(BF16)
    return _dot(hi, ones_bf16) + _dot(mid, ones_bf16) + _dot(lo, ones_bf16)


def _split_dot_left(ones_bf16, a_f32):
    hi = a_f32.astype(BF16)
    r1 = a_f32 - hi.astype(F32)
    mid = r1.astype(BF16)
    lo = (r1 - mid.astype(F32)).astype(BF16)
    return _dot(ones_bf16, hi) + _dot(ones_bf16, mid) + _dot(ones_bf16, lo)


def _rms_rows(x):
    return x * lax.rsqrt(jnp.mean(x * x, axis=-1, keepdims=True) + EPS)


def _mod_kernel(c_ref, w_ref, b_ref, o_ref):
    c = c_ref[...]
    a = (c * jax.nn.sigmoid(c)).astype(BF16)
    o_ref[0] = _dot(a, w_ref[0].astype(BF16)) + b_ref[0]


def _modulation(cc, w_mod, b_mod):
    depth, d, d6 = w_mod.shape
    rows = cc.shape[0]
    tn = 1536
    return pl.pallas_call(
        _mod_kernel,
        grid=(depth, d6 // tn),
        in_specs=[
            pl.BlockSpec((rows, d), lambda l, n: (0, 0)),
            pl.BlockSpec((1, d, tn), lambda l, n: (l, 0, n)),
            pl.BlockSpec((1, 1, tn), lambda l, n: (l, 0, n)),
        ],
        out_specs=pl.BlockSpec((1, rows, tn), lambda l, n: (l, 0, n)),
        out_shape=jax.ShapeDtypeStruct((depth, rows, d6), F32),
        compiler_params=_cparams("parallel", "parallel"),
        name="modulation",
    )(cc, w_mod, b_mod.reshape(depth, 1, d6))


def _group_mean_sq(x, ones_bd, width):
    return _split_dot(x * x, ones_bd) * (1.0 / width)


def _in_kernel(x_ref, mod_ref, gmix_ref, w_ref, gb_ref, gcq_ref, gckv_ref, wuq_ref, wuk_ref, wuv_ref,
               gq_ref, gk_ref, ones_ref, cos_ref, sa_ref, sb_ref,
               hx_ref, mq_ref, mk_ref, mv_ref, mo_ref, g_ref, pc_ref, q_ref, k_ref, v_ref):
    x = x_ref[0]
    shift = mod_ref[0, 0:1, :]
    scale = mod_ref[0, 1:2, :]
    hx = (_rms_rows(x) * gmix_ref[...]) * (1.0 + scale) + shift
    hxb = hx.astype(BF16)
    hx_ref[0] = hxb
    p = _dot(hxb, w_ref[...])
    bw = BRANCH_W
    mq_ref[0] = p[:, 0:bw].T.astype(BF16)
    mk_ref[0] = (p[:, bw:2 * bw] * (ML_DH ** -0.5)).astype(BF16)
    mv_ref[0] = p[:, 2 * bw:3 * bw].T.astype(BF16)
    mo_ref[0] = p[:, 3 * bw:4 * bw].T.astype(BF16)
    g_ref[0] = (p[:, 3072:3200] + gb_ref[...]).T[0:4 * ML_HEADS]
    pc_ref[0] = p[:, 1024:2048]
    cq = p[:, 2048:2048 + Q_LORA]
    ckv = p[:, 2304:2304 + KV_LORA]
    kr4 = p[:, 2560:3072]

    cqn = (_rms_rows(cq) * gcq_ref[...]).astype(BF16)
    ckvn = (_rms_rows(ckv) * gckv_ref[...]).astype(BF16)
    q_pre = _dot(cqn, wuq_ref[...])
    k_pre = _dot(ckvn, wuk_ref[...]) + kr4
    v_t = _dot(ckvn, wuv_ref[...]).T
    tm = v_t.shape[1]
    extra = V_SLOT - V_DIM
    one_row = jnp.where(lax.broadcasted_iota(jnp.int32, (extra, tm), 0) == 0, 1.0, 0.0)
    v_ref[0] = jnp.concatenate(
        sum([[v_t[h * V_DIM:(h + 1) * V_DIM], one_row] for h in range(MLA_HEADS)], []), axis=0).astype(BF16)

    ones_bd = ones_ref[...]
    cos = jnp.concatenate([cos_ref[...]] * MLA_HEADS, axis=1)
    sa = jnp.concatenate([sa_ref[...]] * MLA_HEADS, axis=1)
    sb = jnp.concatenate([sb_ref[...]] * MLA_HEADS, axis=1)
    width = MLA_HEADS * HEAD_SLOT
    half = QK_ROPE // 4

    def norm_rope(t, gain):
        t = t * lax.rsqrt(_group_mean_sq(t, ones_bd, QK_DIM) + EPS) * gain
        return t * cos + pltpu.roll(t, width - half, 1) * sa + pltpu.roll(t, half, 1) * sb

    q = norm_rope(q_pre, gq_ref[...]) * (QK_DIM ** -0.5 * math.log2(math.e))
    q_ref[0] = q.T.astype(BF16)
    k_ref[0] = norm_rope(k_pre, gk_ref[...]).astype(BF16)


def _in_proj(xs, mod, lw, rope, n_lat_tiles):
    bsz, s, d = xs.shape
    tm = ROW_TILE
    nt = s // tm
    wcols = lw["w_big"].shape[1]
    slot_w = MLA_HEADS * HEAD_SLOT

    def row_spec(width):
        return pl.BlockSpec((1, tm, width), lambda t, b: (b, t, 0))

    def tab_spec():
        return pl.BlockSpec((tm, HEAD_SLOT), lambda t, b: (t, 0))

    def col_spec(height):
        return pl.BlockSpec((1, height, tm), lambda t, b: (b, 0, t))

    v_rows = MLA_HEADS * V_SLOT
    outs = [("row", d, BF16), ("col", BRANCH_W, BF16), ("row", BRANCH_W, BF16), ("col", BRANCH_W, BF16),
            ("col", BRANCH_W, BF16), ("col", 4 * ML_HEADS, F32), ("row", 1024, F32),
            ("col", slot_w, BF16), ("row", slot_w, BF16), ("col", v_rows, BF16)]
    out_specs = [row_spec(w) if kind == "row" else col_spec(w) for kind, w, _ in outs]
    out_shape = [jax.ShapeDtypeStruct((bsz, s, w) if kind == "row" else (bsz, w, s), dt) for kind, w, dt in outs]
    return pl.pallas_call(
        _in_kernel,
        grid=(nt, bsz),
        in_specs=[
            row_spec(d),
            pl.BlockSpec((1, 6, d), lambda t, b: (jnp.where(t >= n_lat_tiles, bsz, b), 0, 0)),
            _const_spec((1, d)),
            _const_spec((d, wcols)),
            _const_spec((1, LANES)),
            _const_spec((1, Q_LORA)),
            _const_spec((1, KV_LORA)),
            _const_spec((Q_LORA, slot_w)),
            _const_spec((KV_LORA, slot_w)),
            _const_spec((KV_LORA, BRANCH_W)),
            _const_spec((1, slot_w)),
            _const_spec((1, slot_w)),
            _const_spec((slot_w, slot_w)),
            tab_spec(), tab_spec(), tab_spec(),
        ],
        out_specs=out_specs,
        out_shape=out_shape,
        compiler_params=_cparams("parallel", "parallel"),
        name="norm_in_proj",
    )(xs, mod, lw["g_mix"], lw["w_big"], lw["gate_bias"], lw["g_cq"], lw["g_ckv"], lw["w_uq"], lw["w_uk"], lw["w_uv"],
      lw["g_qn"], lw["g_kn"], rope["ones_slot"], rope["cos"], rope["sin_a"], rope["sin_b"])


def _log_sigmoid(x):
    return jnp.minimum(x, 0.0) - jnp.log1p(jnp.exp(-jnp.abs(x)))


def _mlstm_kernel(qt_f, k_f, vt_f, g_f, qt_b, k_b, vt_b, g_b, hf_ref, hb_ref, c_sc, m_sc):
    @pl.when(pl.program_id(1) == 0)
    def _():
        c_sc[...] = jnp.zeros_like(c_sc)
        m_sc[...] = jnp.zeros_like(m_sc)

    L = ML_CHUNK
    dh = ML_DH
    row = lax.broadcasted_iota(jnp.int32, (L, L), 0)
    col = lax.broadcasted_iota(jnp.int32, (L, L), 1)
    diag = row == col
    ones_ll = jnp.ones((L, L), BF16)
    one_rows = jnp.where(lax.broadcasted_iota(jnp.int32, (LANES - dh, L), 0) == 0, 1.0, 0.0).astype(BF16)

    dirs = ((qt_f, k_f, vt_f, g_f), (qt_b, k_b, vt_b, g_b))
    units = []
    for d, (qt_ref, k_ref, vt_ref, g_ref) in enumerate(dirs):
        a = g_ref[0]
        lf = _log_sigmoid(a)
        valid = (row <= col) if d == 0 else (row >= col)
        b_rows = _split_dot(lf, valid.astype(BF16))
        last = L - 1 if d == 0 else 0
        for h in range(ML_HEADS):
            ci = d * 2 * ML_HEADS + h
            cf = ci + ML_HEADS
            st = d * ML_HEADS + h
            u = {"st": st, "valid": valid}
            bt = b_rows[cf:cf + 1, :]
            li = a[ci:ci + 1, :]
            b_end = bt[:, last:last + 1]
            m_st = m_sc[st:st + 1, 0:1]
            qt = qt_ref[0, h * dh:(h + 1) * dh, :]
            k = k_ref[0, :, h * dh:(h + 1) * dh]
            u["vt"] = vt_ref[0, h * dh:(h + 1) * dh, :]
            u["bt"] = bt
            u["inter"] = bt + m_st
            src = jnp.where(diag, bt - li, 0.0)
            hi = src.astype(BF16)
            lo = (src - hi.astype(F32)).astype(BF16)
            u["src"] = _dot(hi, ones_ll) + _dot(lo, ones_ll)
            w_log = b_end - bt + li
            u["m_new"] = jnp.maximum(b_end + m_st, jnp.max(w_log, axis=1, keepdims=True))
            u["decay"] = jnp.exp(b_end + m_st - u["m_new"])
            v_aug = jnp.concatenate([u["vt"], one_rows], axis=0)
            vw = (v_aug.astype(F32) * jnp.exp(w_log - u["m_new"])).astype(BF16)
            u["s_kq"] = _dot(k, qt)
            u["qc"] = _dot(c_sc[st].astype(BF16), qt)
            u["upd"] = _dot(vw, k)
            units.append(u)

    for u in units:
        d_log = jnp.where(u["valid"], u["bt"] - u["src"], -jnp.inf)
        u["m_t"] = jnp.maximum(u["inter"], jnp.max(d_log, axis=0, keepdims=True))
        w_st = jnp.exp(d_log - u["m_t"]) * u["s_kq"]
        u["w_sum"] = jnp.sum(w_st, axis=0, keepdims=True)
        u["pv"] = _dot(u["vt"], w_st.astype(BF16))

    outs = []
    for u in units:
        a_inter = jnp.exp(u["inter"] - u["m_t"])
        num = a_inter * u["qc"][0:dh, :] + u["pv"]
        den = a_inter * u["qc"][dh:dh + 1, :] + u["w_sum"]
        outs.append(num / jnp.maximum(jnp.abs(den), jnp.exp(-u["m_t"])))
        st = u["st"]
        c_sc[st] = u["decay"] * c_sc[st] + u["upd"]
        m_sc[st:st + 1, :] = jnp.broadcast_to(u["m_new"], (1, LANES))
    hf_ref[0] = jnp.concatenate(outs[0:ML_HEADS], axis=0)
    hb_ref[0] = jnp.concatenate(outs[ML_HEADS:], axis=0)


def _mlstm(qt, k, vt, gt, n_lat_chunks):
    bsz, s, _ = k.shape
    nc = s // ML_CHUNK

    def fwd_chunk(j):
        return (j + n_lat_chunks) % nc

    def bwd_chunk(j):
        return nc - 1 - j

    def specs(chunk):
        def col(height):
            return pl.BlockSpec((1, height, ML_CHUNK), lambda b, j: (b, 0, chunk(j)))
        return [col(BRANCH_W), pl.BlockSpec((1, ML_CHUNK, BRANCH_W), lambda b, j: (b, chunk(j), 0)),
                col(BRANCH_W), col(4 * ML_HEADS)], col(BRANCH_W)

    in_f, out_f = specs(fwd_chunk)
    in_b, out_b = specs(bwd_chunk)
    return pl.pallas_call(
        _mlstm_kernel,
        grid=(bsz, nc),
        in_specs=in_f + in_b,
        out_specs=[out_f, out_b],
        out_shape=[jax.ShapeDtypeStruct((bsz, BRANCH_W, s), F32)] * 2,
        scratch_shapes=[pltpu.VMEM((2 * ML_HEADS, LANES, ML_DH), F32),
                        pltpu.VMEM((2 * ML_HEADS, LANES), F32)],
        compiler_params=_cparams("parallel", "arbitrary"),
        name="mlstm_scan",
    )(qt, k, vt, gt, qt, k, vt, gt)


def _attn_kernel(qt_ref, k_ref, vt_ref, o_ref, m_sc, acc_sc, *, n_lat, n_ctx, tk):
    m_sc[...] = jnp.full_like(m_sc, -jnp.inf)
    acc_sc[...] = jnp.zeros_like(acc_sc)

    def scores(h, start, size):
        sl = slice(h * HEAD_SLOT, (h + 1) * HEAD_SLOT)
        return _dot(k_ref[0, pl.ds(start, size), sl], qt_ref[0, sl, :])

    def keys(start, size):
        n_sub = size // ATT_SUB
        order = [(j, h) for j in range(n_sub) for h in range(MLA_HEADS)]
        pending = [scores(h, start + j * ATT_SUB, ATT_SUB) for j, h in order[:ATT_AHEAD]]
        for i, (j, h) in enumerate(order):
            off = start + j * ATT_SUB
            s = pending.pop(0)
            if i + ATT_AHEAD < len(order):
                jn, hn = order[i + ATT_AHEAD]
                pending.append(scores(hn, start + jn * ATT_SUB, ATT_SUB))
            m_old = m_sc[h]
            m_new = jnp.maximum(m_old, jnp.max(s, axis=0, keepdims=True))
            p = jnp.exp2(s - m_new)
            vt = vt_ref[0, h * V_SLOT:(h + 1) * V_SLOT, pl.ds(off, ATT_SUB)]
            acc_sc[h] = jnp.exp2(m_old - m_new) * acc_sc[h] + _dot(vt, p.astype(BF16))
            m_sc[h] = m_new

    keys(n_lat, n_ctx)

    def chunk(c, carry):
        keys(pl.multiple_of(c * tk, tk), tk)
        return carry

    is_ctx_tile = pl.program_id(1) * ATT_TQ >= n_lat
    lax.fori_loop(0, jnp.where(is_ctx_tile, 0, n_lat // tk), chunk, 0)
    out_t = jnp.concatenate([acc_sc[h, 0:V_DIM, :] / acc_sc[h, V_DIM:V_DIM + 1, :] for h in range(MLA_HEADS)],
                            axis=0)
    o_ref[0] = out_t.T.astype(o_ref.dtype)


def _key_chunk(n_keys):
    return max(t for t in range(ATT_TQ, ATT_TK + 1, ATT_TQ) if n_keys % t == 0)


def _attention(qt, k, vt, n_lat):
    bsz, s, slot_w = k.shape
    tq = ATT_TQ
    return pl.pallas_call(
        functools.partial(_attn_kernel, n_lat=n_lat, n_ctx=s - n_lat, tk=_key_chunk(n_lat)),
        grid=(bsz, s // tq),
        in_specs=[pl.BlockSpec((1, slot_w, tq), lambda b, t: (b, 0, t)),
                  pl.BlockSpec((1, s, slot_w), lambda b, t: (b, 0, 0), pipeline_mode=pl.Buffered(1)),
                  pl.BlockSpec((1, MLA_HEADS * V_SLOT, s), lambda b, t: (b, 0, 0), pipeline_mode=pl.Buffered(1))],
        out_specs=pl.BlockSpec((1, tq, BRANCH_W), lambda b, t: (b, t, 0)),
        out_shape=jax.ShapeDtypeStruct((bsz, s, BRANCH_W), BF16),
        scratch_shapes=[pltpu.VMEM((MLA_HEADS, 1, tq), F32), pltpu.VMEM((MLA_HEADS, V_SLOT, tq), F32)],
        compiler_params=_cparams("parallel", "arbitrary"),
        name="mla_attention",
    )(qt, k, vt)


def _merge_kernel(hx_ref, hf_ref, hb_ref, o_ref, ya_ref, pc_ref, prev_ref, next_ref, x_ref, mod_ref,
                  hg_ref, pw_ref, ps_ref, cw_ref, cb_ref, wg_ref, bg_ref, wb_ref, wo_ref,
                  gffn_ref, xo_ref, h2_ref, *, n_lat_tiles, n_tiles):
    t = pl.program_id(1)
    tm = ROW_TILE
    bw = BRANCH_W
    halo = POOL_HALO
    no_prev = jnp.logical_or(t == 0, t == n_lat_tiles)
    no_next = jnp.logical_or(t == n_lat_tiles - 1, t == n_tiles - 1)

    h_t = hf_ref[0] + hb_ref[0]
    normed = []
    for hd in range(ML_HEADS):
        hh = h_t[hd * ML_DH:(hd + 1) * ML_DH]
        normed.append(hh * lax.rsqrt(jnp.mean(hh * hh, axis=0, keepdims=True) + EPS))
    y_ml = (jnp.concatenate(normed, axis=0) * hg_ref[...] * jax.nn.sigmoid(o_ref[0].astype(F32))).T

    prev = jnp.where(no_prev, 0.0, prev_ref[0])
    nxt = jnp.where(no_next, 0.0, next_ref[0])
    ext = jnp.concatenate([prev, pc_ref[0], nxt], axis=0)
    pe = ext[:, 0:bw]

    def rows(arr, off):
        return arr[halo + off:halo + off + tm, :]

    r = lax.broadcasted_iota(jnp.int32, (tm, 1), 0)
    lane = lax.broadcasted_iota(jnp.int32, (tm, bw), 1)
    centre = rows(pe, 0)
    acc = centre
    mean = jnp.zeros((tm, bw), F32)
    done = 0
    for gi, w in enumerate(POOL_WINDOWS):
        for off in list(range(-(w // 2), -done)) + list(range(max(done, 1), w // 2)):
            acc = acc + rows(pe, off)
        done = w // 2
        before = jnp.where(no_prev, jnp.minimum(r, w // 2), w // 2)
        after = jnp.where(no_next, jnp.minimum(tm - r, w // 2), w // 2)
        inv = 1.0 / (before + after).astype(F32)
        mean = jnp.where(lane >= gi * POOL_GROUP, acc * inv, mean)
    y_pool = _dot((mean - centre).astype(BF16), pw_ref[...]) * ps_ref[...]

    z = ext[:, 3 * bw:4 * bw] * ext[:, bw:2 * bw]
    conv = cb_ref[...] + rows(z, -1) * cw_ref[0:1, :] + rows(z, 0) * cw_ref[1:2, :] + rows(z, 1) * cw_ref[2:3, :]
    y_conv = rows(ext[:, 2 * bw:3 * bw], 0) * conv

    hxb = hx_ref[0]
    d = hxb.shape[1]
    gates = jax.nn.sigmoid(_dot(hxb, wg_ref[...]) + bg_ref[...])
    ys = (y_ml.astype(BF16), ya_ref[0], y_pool.astype(BF16), y_conv.astype(BF16))
    merged = None
    for i, y in enumerate(ys):
        term = gates[:, i * d:(i + 1) * d] * _dot(y, wb_ref[i])
        merged = term if merged is None else merged + term
    out = _dot(merged.astype(BF16), wo_ref[...])
    x_new = x_ref[0] + mod_ref[0, 2:3, :] * out
    xo_ref[0] = x_new
    h2_ref[0] = (_rms_rows(x_new) * gffn_ref[...]) * (1.0 + mod_ref[0, 4:5, :]) + mod_ref[0, 3:4, :]


def _merge(hx, hf, hb, o, y_mla, pc, xs, mod, lw, n_lat_tiles):
    bsz, s, d = xs.shape
    tm = ROW_TILE
    nt = s // tm
    halo = POOL_HALO
    per = tm // halo

    def row_spec(width):
        return pl.BlockSpec((1, tm, width), lambda b, t: (b, t, 0))

    def col_spec():
        return pl.BlockSpec((1, BRANCH_W, tm), lambda b, t: (b, 0, t))

    return pl.pallas_call(
        functools.partial(_merge_kernel, n_lat_tiles=n_lat_tiles, n_tiles=nt),
        grid=(bsz, nt),
        in_specs=[
            row_spec(d), col_spec(), col_spec(), col_spec(), row_spec(BRANCH_W),
            row_spec(1024),
            pl.BlockSpec((1, halo, 1024), lambda b, t: (b, jnp.maximum(t * per - 1, 0), 0)),
            pl.BlockSpec((1, halo, 1024), lambda b, t: (b, jnp.minimum((t + 1) * per, s // halo - 1), 0)),
            row_spec(d),
            pl.BlockSpec((1, 6, d), lambda b, t: (jnp.where(t >= n_lat_tiles, bsz, b), 0, 0)),
            _const_spec((BRANCH_W, tm)), _const_spec((BRANCH_W, BRANCH_W)),
            _const_spec((1, BRANCH_W)), _const_spec((3, BRANCH_W)), _const_spec((1, BRANCH_W)),
            _const_spec((d, 4 * d)), _const_spec((1, 4 * d)), _const_spec((4, BRANCH_W, d)),
            _const_spec((d, d)), _const_spec((1, d)),
        ],
        out_specs=[row_spec(d), row_spec(d)],
        out_shape=[jax.ShapeDtypeStruct((bsz, s, d), F32)] * 2,
        compiler_params=_cparams("parallel", "parallel"),
        name="mixer_merge",
    )(hx, hf, hb, o, y_mla, pc, pc, pc, xs, mod, lw["head_gain"], lw["pool_w"], lw["pool_scale"],
      lw["conv_w"], lw["conv_b"], lw["w_gate"], lw["b_gate"], lw["w_branch"], lw["w_out"], lw["g_ffn"])


def _router_kernel(h_ref, wr_ref, br_ref, pe_ref, we_ref, c16_ref):
    tm = h_ref.shape[0]
    scores = jax.nn.sigmoid(_dot(h_ref[...].astype(BF16), wr_ref[...]))
    sel = scores + br_ref[...]
    lane = lax.broadcasted_iota(jnp.int32, (tm, LANES), 1)
    member = jnp.zeros((tm, LANES), F32)
    for _ in range(TOP_K):
        mx = jnp.max(sel, axis=1, keepdims=True)
        ix = jnp.min(jnp.where(sel == mx, lane, LANES), axis=1, keepdims=True)
        hit = lane == ix
        member = jnp.where(hit, 1.0, member)
        sel = jnp.where(hit, -jnp.inf, sel)
    picked = member * scores
    weights = picked / jnp.sum(picked, axis=1, keepdims=True) * ROUTE_SCALE
    counts = jnp.sum(member, axis=0, keepdims=True)
    c16 = jnp.floor((counts + (RUN_ROWS - 1)) * (1.0 / RUN_ROWS))
    e_row = lax.broadcasted_iota(jnp.int32, (LANES, LANES), 0)
    e_col = lax.broadcasted_iota(jnp.int32, (LANES, LANES), 1)
    lower = (e_row < e_col).astype(BF16)
    o16 = _dot(jnp.broadcast_to(c16, (SUBLANES, LANES)).astype(BF16), lower)[0:1]
    row = lax.broadcasted_iota(jnp.int32, (tm, tm), 0)
    col = lax.broadcasted_iota(jnp.int32, (tm, tm), 1)
    rank = _dot((col < row).astype(BF16), member.astype(BF16))
    pos = jnp.where(member > 0.0, RUN_ROWS * o16 + rank, -1.0)
    pe_ref[0] = pos.T[0:N_EXPERTS]
    we_ref[0] = weights.T[0:N_EXPERTS]
    c16_ref[0] = c16.astype(jnp.int32)


def _router(h2, lw):
    n, d = h2.shape
    tm = MOE_TILE
    nt = n // tm

    def t_spec():
        return pl.BlockSpec((1, N_EXPERTS, tm), lambda i: (i, 0, 0))

    return pl.pallas_call(
        _router_kernel,
        grid=(nt,),
        in_specs=[pl.BlockSpec((tm, d), lambda i: (i, 0)), _const_spec((d, LANES)), _const_spec((1, LANES))],
        out_specs=[t_spec(), t_spec(), pl.BlockSpec((1, 1, LANES), lambda i: (i, 0, 0))],
        out_shape=[jax.ShapeDtypeStruct((nt, N_EXPERTS, tm), F32), jax.ShapeDtypeStruct((nt, N_EXPERTS, tm), F32),
                   jax.ShapeDtypeStruct((nt, 1, LANES), jnp.int32)],
        compiler_params=_cparams("parallel"),
        name="moe_router",
    )(h2, lw["w_router"], lw["b_router"])


def _tile_row_bound(tm):
    worst = tm * TOP_K + N_EXPERTS * (RUN_ROWS - 1)
    return -(-worst // MOE_CHUNK) * MOE_CHUNK


def _selection_rows(g, grp_e_ref, pe_ref, tm):
    prow = pe_ref[0, pl.ds(grp_e_ref[0, 0, g], 1), :]
    rows = (lax.broadcasted_iota(jnp.int32, (RUN_ROWS, tm), 0) + g * RUN_ROWS).astype(F32)
    return prow, rows


def _dispatch_kernel(r16_ref, tail_ref, n_act_ref, grp_e_ref, gdst_ref, h_ref, pe_ref, xs_ref,
                     sel_sc, xb_sc, xp_sc, zero_sc, sem, *, n_groups):
    i = pl.program_id(0)
    tm = h_ref.shape[0]
    r16 = r16_ref[i]

    def build(g, carry):
        prow, rows = _selection_rows(g, grp_e_ref, pe_ref, tm)
        sel_sc[pl.ds(pl.multiple_of(g * RUN_ROWS, RUN_ROWS), RUN_ROWS), :] = jnp.where(
            prow == rows, 1.0, 0.0).astype(BF16)
        return carry

    lax.fori_loop(0, n_groups, build, 0, unroll=4)
    xb_sc[...] = h_ref[...].astype(BF16)

    def permute(ch, carry):
        r0 = pl.multiple_of(ch * MOE_CHUNK, MOE_CHUNK)
        xp_sc[pl.ds(r0, MOE_CHUNK), :] = _dot(sel_sc[pl.ds(r0, MOE_CHUNK), :], xb_sc[...]).astype(BF16)
        return carry

    lax.fori_loop(0, (r16 * RUN_ROWS + MOE_CHUNK - 1) // MOE_CHUNK, permute, 0)

    def run_copy(g):
        return pltpu.make_async_copy(
            xp_sc.at[pl.ds(pl.multiple_of(g * RUN_ROWS, RUN_ROWS), RUN_ROWS)],
            xs_ref.at[pl.ds(pl.multiple_of(gdst_ref[0, 0, g] * RUN_ROWS, RUN_ROWS), RUN_ROWS)], sem)

    def start(g, carry):
        run_copy(g).start()
        return carry

    def wait(g, carry):
        run_copy(g).wait()
        return carry

    lax.fori_loop(0, r16, start, 0)
    lax.fori_loop(0, r16, wait, 0)

    @pl.when(i == pl.num_programs(0) - 1)
    def _():
        zero_sc[...] = jnp.zeros_like(zero_sc)

        def fill_copy(q):
            return pltpu.make_async_copy(
                zero_sc.at[pl.ds(0, RUN_ROWS)],
                xs_ref.at[pl.ds(pl.multiple_of(tail_ref[q] * RUN_ROWS, RUN_ROWS), RUN_ROWS)], sem)

        def block_copy(b):
            return pltpu.make_async_copy(
                zero_sc, xs_ref.at[pl.ds(pl.multiple_of(b * EXPERT_ROWS, EXPERT_ROWS), EXPERT_ROWS)], sem)

        def block_start(b, carry):
            block_copy(b).start()
            return carry

        def block_wait(b, carry):
            block_copy(b).wait()
            return carry

        n_blocks = xs_ref.shape[0] // EXPERT_ROWS
        lax.fori_loop(n_act_ref[0], n_blocks, block_start, 0)
        lax.fori_loop(n_act_ref[0], n_blocks, block_wait, 0)

        def fill_start(q, carry):
            @pl.when(tail_ref[q] >= 0)
            def _():
                fill_copy(q).start()
            return carry

        def fill_wait(q, carry):
            @pl.when(tail_ref[q] >= 0)
            def _():
                fill_copy(q).wait()
            return carry

        lax.fori_loop(0, tail_ref.shape[0], fill_start, 0)
        lax.fori_loop(0, tail_ref.shape[0], fill_wait, 0)


def _dispatch(h2, pe, plan, n_rows):
    n, d = h2.shape
    tm = MOE_TILE
    nt = n // tm
    rb = _tile_row_bound(tm)
    n_groups = rb // RUN_ROWS

    def smem_spec():
        return pl.BlockSpec((1, 1, n_groups), lambda i, *_: (i, 0, 0), memory_space=pltpu.SMEM)

    return pl.pallas_call(
        functools.partial(_dispatch_kernel, n_groups=n_groups),
        grid_spec=pltpu.PrefetchScalarGridSpec(
            num_scalar_prefetch=3,
            grid=(nt,),
            in_specs=[smem_spec(), smem_spec(),
                      pl.BlockSpec((tm, d), lambda i, *_: (i, 0)),
                      pl.BlockSpec((1, N_EXPERTS, tm), lambda i, *_: (i, 0, 0))],
            out_specs=pl.BlockSpec(memory_space=pl.ANY),
            scratch_shapes=[pltpu.VMEM((rb, tm), BF16), pltpu.VMEM((tm, d), BF16), pltpu.VMEM((rb, d), BF16),
                            pltpu.VMEM((EXPERT_ROWS, d), BF16), pltpu.SemaphoreType.DMA(())],
        ),
        out_shape=jax.ShapeDtypeStruct((n_rows, d), BF16),
        compiler_params=_cparams("arbitrary"),
        name="moe_dispatch",
    )(plan["r16"], plan["tail"], plan["n_act"].reshape(1), plan["grp_e"], plan["gdst"], h2, pe)


def _expert_kernel(blk_e_ref, n_act_ref, x_ref, w13_ref, w2_ref, y_ref):
    del blk_e_ref

    @pl.when(pl.program_id(0) < n_act_ref[0])
    def _():
        up = _dot(x_ref[...], w13_ref[0])
        a = up[:, 0:EXPERT_FF]
        act = (a * jax.nn.sigmoid(a)) * up[:, EXPERT_FF:2 * EXPERT_FF]
        y_ref[...] = _dot(act.astype(BF16), w2_ref[0]).astype(BF16)

    @pl.when(pl.program_id(0) >= n_act_ref[0])
    def _():
        y_ref[...] = jnp.zeros_like(y_ref)


def _experts(xs_sorted, blk_e, n_act, lw):
    n_rows, d = xs_sorted.shape
    bm = EXPERT_ROWS

    def row_map(i, blk_e_ref, n_act_ref):
        return (jnp.minimum(i, n_act_ref[0] - 1), 0)

    def w_map(i, blk_e_ref, n_act_ref):
        return (blk_e_ref[i], 0, 0)

    return pl.pallas_call(
        _expert_kernel,
        grid_spec=pltpu.PrefetchScalarGridSpec(
            num_scalar_prefetch=2,
            grid=(n_rows // bm,),
            in_specs=[pl.BlockSpec((bm, d), row_map),
                      pl.BlockSpec((1, d, 2 * EXPERT_FF), w_map),
                      pl.BlockSpec((1, EXPERT_FF, d), w_map)],
            out_specs=pl.BlockSpec((bm, d), lambda i, *_: (i, 0)),
        ),
        out_shape=jax.ShapeDtypeStruct((n_rows, d), BF16),
        compiler_params=_cparams("arbitrary"),
        name="moe_experts",
    )(blk_e, n_act, xs_sorted, lw["w13"], lw["w2"])


def _combine_kernel(r16_ref, grp_e_ref, gdst_ref, h_ref, x_ref, pe_ref, we_ref, mod_ref, modc_ref, ws13_ref,
                    ws2_ref, ys_ref, xo_ref, sel_sc, yp_sc, acc_sc, sem, *, n_groups, tiles_per_sample, n_ctx):
    i = pl.program_id(0)
    tm = h_ref.shape[0]
    r16 = r16_ref[i]
    n_ch = (r16 * RUN_ROWS + MOE_CHUNK - 1) // MOE_CHUNK

    def run_copy(g):
        return pltpu.make_async_copy(
            ys_ref.at[pl.ds(pl.multiple_of(gdst_ref[0, 0, g] * RUN_ROWS, RUN_ROWS), RUN_ROWS)],
            yp_sc.at[pl.ds(pl.multiple_of(g * RUN_ROWS, RUN_ROWS), RUN_ROWS)], sem)

    def start(g, carry):
        run_copy(g).start()
        return carry

    def wait(g, carry):
        run_copy(g).wait()
        return carry

    lax.fori_loop(0, r16, start, 0)

    def build(g, carry):
        prow, rows = _selection_rows(g, grp_e_ref, pe_ref, tm)
        wrow = we_ref[0, pl.ds(grp_e_ref[0, 0, g], 1), :]
        sel_sc[pl.ds(pl.multiple_of(g * RUN_ROWS, RUN_ROWS), RUN_ROWS), :] = jnp.where(
            prow == rows, wrow, 0.0).astype(BF16)
        return carry

    lax.fori_loop(0, n_groups, build, 0, unroll=4)

    def clear(g, carry):
        yp_sc[pl.ds(pl.multiple_of(g * RUN_ROWS, RUN_ROWS), RUN_ROWS), :] = jnp.zeros(
            (RUN_ROWS, yp_sc.shape[1]), BF16)
        return carry

    lax.fori_loop(r16, n_ch * (MOE_CHUNK // RUN_ROWS), clear, 0)

    up = _dot(h_ref[...].astype(BF16), ws13_ref[...])
    a = up[:, 0:EXPERT_FF]
    act = (a * jax.nn.sigmoid(a)) * up[:, EXPERT_FF:2 * EXPERT_FF]
    acc_sc[...] = _dot(act.astype(BF16), ws2_ref[...])
    lax.fori_loop(0, r16, wait, 0)

    def gather_sum(ch, carry):
        r0 = pl.multiple_of(ch * MOE_CHUNK, MOE_CHUNK)
        acc_sc[...] += lax.dot_general(sel_sc[pl.ds(r0, MOE_CHUNK), :], yp_sc[pl.ds(r0, MOE_CHUNK), :], TN_DIMS,
                                       preferred_element_type=F32)
        return carry

    lax.fori_loop(0, n_ch, gather_sum, 0)
    r = lax.broadcasted_iota(jnp.int32, (tm, 1), 0)
    is_ctx = jnp.logical_and(i % tiles_per_sample == tiles_per_sample - 1, r >= tm - n_ctx)
    gate = jnp.where(is_ctx, modc_ref[0, 5:6, :], mod_ref[0, 5:6, :])
    xo_ref[...] = x_ref[...] + gate * acc_sc[...]


def _combine(h2, xs, pe, we, plan, mod, ys_sorted, lw, tiles_per_sample, n_ctx, bsz):
    n, d = h2.shape
    tm = MOE_TILE
    nt = n // tm
    rb = _tile_row_bound(tm)
    n_groups = rb // RUN_ROWS

    def smem_spec():
        return pl.BlockSpec((1, 1, n_groups), lambda i, *_: (i, 0, 0), memory_space=pltpu.SMEM)

    def row_spec():
        return pl.BlockSpec((tm, d), lambda i, *_: (i, 0))

    def t_spec():
        return pl.BlockSpec((1, N_EXPERTS, tm), lambda i, *_: (i, 0, 0))

    return pl.pallas_call(
        functools.partial(_combine_kernel, n_groups=n_groups, tiles_per_sample=tiles_per_sample, n_ctx=n_ctx),
        grid_spec=pltpu.PrefetchScalarGridSpec(
            num_scalar_prefetch=1,
            grid=(nt,),
            in_specs=[smem_spec(), smem_spec(), row_spec(), row_spec(), t_spec(), t_spec(),
                      pl.BlockSpec((1, 6, d), lambda i, *_: (i // tiles_per_sample, 0, 0)),
                      pl.BlockSpec((1, 6, d), lambda i, *_: (bsz, 0, 0)),
                      pl.BlockSpec((d, 2 * EXPERT_FF), lambda i, *_: (0, 0), pipeline_mode=pl.Buffered(1)),
                      pl.BlockSpec((EXPERT_FF, d), lambda i, *_: (0, 0), pipeline_mode=pl.Buffered(1)),
                      pl.BlockSpec(memory_space=pl.ANY)],
            out_specs=row_spec(),
            scratch_shapes=[pltpu.VMEM((rb, tm), BF16), pltpu.VMEM((rb, d), BF16), pltpu.VMEM((tm, d), F32),
                            pltpu.SemaphoreType.DMA(())],
        ),
        out_shape=jax.ShapeDtypeStruct((n, d), F32),
        compiler_params=_cparams("arbitrary"),
        name="moe_combine",
    )(plan["r16"], plan["grp_e"], plan["gdst"], h2, xs, pe, we, mod, mod, lw["ws13"], lw["ws2"], ys_sorted)


def _moe_plan(c16, n_groups):
    nt = c16.shape[0]
    per_blk = EXPERT_ROWS // RUN_ROWS
    o16 = jnp.cumsum(c16, axis=1) - c16
    before16 = jnp.cumsum(c16, axis=0) - c16
    gtot16 = jnp.sum(c16, axis=0)
    gblk = (gtot16 + per_blk - 1) // per_blk
    gend_blk = jnp.cumsum(gblk)
    gstart16 = per_blk * (gend_blk - gblk)
    g = jnp.arange(n_groups, dtype=jnp.int32)
    grp_e = jnp.minimum(jnp.sum((o16 + c16)[:, None, :] <= g[None, :, None], axis=2), N_EXPERTS - 1)
    run0 = gstart16[None, :] + before16 - o16
    experts = jnp.arange(N_EXPERTS, dtype=jnp.int32)
    gdst = jnp.sum(jnp.where(grp_e[:, :, None] == experts[None, None, :], run0[:, None, :], 0), axis=2) + g[None, :]
    q = jnp.arange(per_blk, dtype=jnp.int32)
    tail = jnp.where(q[None, :] < (per_blk * gblk - gtot16)[:, None],
                     (gstart16 + gtot16)[:, None] + q[None, :], -1)
    return {
        "r16": jnp.sum(c16, axis=1).astype(jnp.int32),
        "grp_e": grp_e.astype(jnp.int32).reshape(nt, 1, n_groups),
        "gdst": gdst.astype(jnp.int32).reshape(nt, 1, n_groups),
        "tail": tail.astype(jnp.int32).reshape(-1),
        "gend_blk": gend_blk, "n_act": gend_blk[-1].astype(jnp.int32),
    }


def _moe(h2, xs, mod, lw, tiles_per_sample, n_ctx, bsz):
    n, d = h2.shape
    nt = n // MOE_TILE
    n_groups = _tile_row_bound(MOE_TILE) // RUN_ROWS
    pe, we, c16 = _router(h2, lw)
    plan = _moe_plan(c16[:, 0, :N_EXPERTS], n_groups)
    worst_rows = n * TOP_K + nt * N_EXPERTS * (RUN_ROWS - 1) + N_EXPERTS * (EXPERT_ROWS - 1)
    n_blocks = -(-worst_rows // EXPERT_ROWS)
    blk = jnp.minimum(jnp.arange(n_blocks, dtype=jnp.int32), plan["n_act"] - 1)
    blk_e = jnp.minimum(jnp.sum(plan["gend_blk"][None, :] <= blk[:, None], axis=1), N_EXPERTS - 1).astype(jnp.int32)
    xs_sorted = _dispatch(h2, pe, plan, n_blocks * EXPERT_ROWS)
    ys_sorted = _experts(xs_sorted, blk_e, plan["n_act"].reshape(1), lw)
    return _combine(h2, xs, pe, we, plan, mod, ys_sorted, lw, tiles_per_sample, n_ctx, bsz)


def _slots(w, head_w, real_w):
    rows = w.shape[0]
    w = w.reshape(rows, -1, head_w)[:, :, :real_w]
    return jnp.pad(w, ((0, 0), (0, 0), (0, HEAD_SLOT - real_w))).reshape(rows, -1)


def _block_diag(blocks):
    n, r, c = blocks.shape
    out = jnp.zeros((n * r, n * c), blocks.dtype)
    for i in range(n):
        out = out.at[i * r:(i + 1) * r, i * c:(i + 1) * c].set(blocks[i])
    return out


def _layer_weights(l, p):
    d = p["w_in"].shape[1]
    w_in = p["w_in"][l]
    ml = 4 * BRANCH_W
    o_mla = ml + 4 * ML_HEADS
    o_pool = o_mla + Q_LORA + KV_LORA + QK_ROPE
    kr = w_in[:, o_mla + Q_LORA + KV_LORA:o_pool]
    kr_slot = jnp.pad(kr, ((0, 0), (QK_NOPE, HEAD_SLOT - QK_DIM)))
    w_big = jnp.concatenate([
        w_in[:, :ml],
        w_in[:, o_pool:],
        w_in[:, o_mla:o_mla + Q_LORA + KV_LORA], jnp.zeros((d, LANES), F32),
        jnp.tile(kr_slot, (1, MLA_HEADS)),
        jnp.pad(w_in[:, ml:o_mla], ((0, 0), (0, LANES - 4 * ML_HEADS))),
    ], axis=1).astype(BF16)
    w_ukv = p["mla_w_ukv"][l].reshape(KV_LORA, MLA_HEADS, QK_NOPE + V_DIM)

    def gain_slots(g):
        return jnp.tile(jnp.pad(g, (0, HEAD_SLOT - QK_DIM)), MLA_HEADS)[None, :]

    return {
        "g_mix": p["g_mix"][l][None, :], "g_ffn": p["g_ffn"][l][None, :],
        "w_big": w_big,
        "g_cq": p["mla_g_cq"][l][None, :], "g_ckv": p["mla_g_ckv"][l][None, :],
        "w_uq": _slots(p["mla_w_uq"][l], QK_DIM, QK_DIM).astype(BF16),
        "w_uk": _slots(w_ukv[:, :, :QK_NOPE].reshape(KV_LORA, -1), QK_NOPE, QK_NOPE).astype(BF16),
        "w_uv": w_ukv[:, :, QK_NOPE:].reshape(KV_LORA, BRANCH_W).astype(BF16),
        "g_qn": gain_slots(p["mla_g_qn"][l]), "g_kn": gain_slots(p["mla_g_kn"][l]),
        "gate_bias": jnp.pad(p["ml_gate_bias"][l], (0, LANES - 4 * ML_HEADS))[None, :],
        "head_gain": jnp.broadcast_to(p["ml_head_gain"][l][:, None], (BRANCH_W, ROW_TILE)),
        "pool_w": _block_diag(p["pool_w"][l]).astype(BF16),
        "pool_scale": p["pool_scale"][l][None, :],
        "conv_w": p["conv_w"][l], "conv_b": p["conv_b"][l][None, :],
        "w_gate": jnp.concatenate(list(p["w_gate"][l]), axis=1).astype(BF16),
        "b_gate": p["b_gate"][l].reshape(1, -1),
        "w_branch": p["w_branch"][l].astype(BF16),
        "w_out": p["w_out"][l].astype(BF16),
        "w_router": jnp.pad(p["moe_w_router"][l], ((0, 0), (0, LANES - N_EXPERTS))).astype(BF16),
        "b_router": jnp.pad(p["moe_b_router"][l], (0, LANES - N_EXPERTS), constant_values=-1e30)[None, :],
        "w13": jnp.concatenate([p["moe_w1"][l], p["moe_w3"][l]], axis=2).astype(BF16),
        "w2": p["moe_w2"][l].astype(BF16),
        "ws13": jnp.concatenate([p["moe_ws1"][l], p["moe_ws3"][l]], axis=1).astype(BF16),
        "ws2": p["moe_ws2"][l].astype(BF16),
    }


def _rope_tables(n_lat, n_ctx):
    t = jnp.arange(n_lat)
    n_freq = QK_ROPE // 4
    inv = ROPE_THETA ** (-jnp.arange(n_freq, dtype=F32) / n_freq)
    ang_r = (t // GRID_W).astype(F32)[:, None] * inv
    ang_c = (t % GRID_W).astype(F32)[:, None] * inv
    cos4 = jnp.concatenate([jnp.cos(ang_r)] * 2 + [jnp.cos(ang_c)] * 2, axis=1)
    zero = jnp.zeros_like(ang_r)
    sin_a = jnp.concatenate([-jnp.sin(ang_r), zero, -jnp.sin(ang_c), zero], axis=1)
    sin_b = jnp.concatenate([zero, jnp.sin(ang_r), zero, jnp.sin(ang_c)], axis=1)

    def slot(a, fill):
        a = jnp.pad(a, ((0, 0), (QK_NOPE, 0)), constant_values=fill)
        a = jnp.pad(a, ((0, 0), (0, HEAD_SLOT - QK_DIM)), constant_values=fill)
        return jnp.pad(a, ((0, n_ctx), (0, 0)), constant_values=fill)

    return {"cos": slot(cos4, 1.0), "sin_a": slot(sin_a, 0.0), "sin_b": slot(sin_b, 0.0),
            "ones_slot": _block_diag(jnp.ones((MLA_HEADS, HEAD_SLOT, HEAD_SLOT), BF16))}


def kernel(x, c, ctx, c_ctx, w_mod, b_mod, g_mix, g_ffn, w_in, ml_gate_bias, ml_head_gain, mla_g_cq, mla_g_ckv,
           mla_w_uq, mla_w_ukv, mla_g_qn, mla_g_kn, pool_w, pool_scale, conv_w, conv_b, w_gate, b_gate, w_branch,
           w_out, moe_w_router, moe_b_router, moe_w1, moe_w3, moe_w2, moe_ws1, moe_ws3, moe_ws2):
    p = dict(g_mix=g_mix, g_ffn=g_ffn, w_in=w_in, ml_gate_bias=ml_gate_bias, ml_head_gain=ml_head_gain,
             mla_g_cq=mla_g_cq, mla_g_ckv=mla_g_ckv, mla_w_uq=mla_w_uq, mla_w_ukv=mla_w_ukv, mla_g_qn=mla_g_qn,
             mla_g_kn=mla_g_kn, pool_w=pool_w, pool_scale=pool_scale, conv_w=conv_w, conv_b=conv_b,
             w_gate=w_gate, b_gate=b_gate, w_branch=w_branch, w_out=w_out, moe_w_router=moe_w_router,
             moe_b_router=moe_b_router, moe_w1=moe_w1, moe_w3=moe_w3, moe_w2=moe_w2, moe_ws1=moe_ws1,
             moe_ws3=moe_ws3, moe_ws2=moe_ws2)
    bsz, n_lat, d = x.shape
    n_ctx = ctx.shape[1]
    depth = w_mod.shape[0]
    s = n_lat + n_ctx
    assert n_ctx == ROW_TILE == ATT_TQ and n_lat % ROW_TILE == 0 and n_lat % GRID_W == 0
    assert s % MOE_TILE == 0 and n_ctx <= MOE_TILE
    n_lat_tiles = n_lat // ROW_TILE

    mod_rows = -(-(bsz + 1) // SUBLANES) * SUBLANES
    cc = jnp.concatenate([c, c_ctx[None, :], jnp.zeros((mod_rows - bsz - 1, d), F32)], axis=0)
    mods = _modulation(cc, w_mod, b_mod).reshape(depth, mod_rows, 6, d)
    rope = _rope_tables(n_lat, n_ctx)
    xs = jnp.concatenate([x, ctx], axis=1)

    for l in range(depth):
        lw = _layer_weights(l, p)
        mod = mods[l]
        hx, mq, mk, mv, o, g, pc, q, k, v = _in_proj(xs, mod, lw, rope, n_lat_tiles)
        hf, hb = _mlstm(mq, mk, mv, g, n_lat // ML_CHUNK)
        y_mla = _attention(q, k, v, n_lat)
        xs, h2 = _merge(hx, hf, hb, o, y_mla, pc, xs, mod, lw, n_lat_tiles)
        xs = _moe(h2.reshape(bsz * s, d), xs.reshape(bsz * s, d), mod, lw, s // MOE_TILE, n_ctx,
                  bsz).reshape(bsz, s, d)
    return xs[:, :n_lat]
```

```python
import functools
import math

import jax
import jax.numpy as jnp
from jax import lax
from jax.experimental import pallas as pl
from jax.experimental.pallas import tpu as pltpu

GRID_W = 64
BRANCH_W = 256
EPS = 1e-6
ML_HEADS = 4
ML_DH = BRANCH_W // ML_HEADS
ML_CHUNK = 128
MLA_HEADS = 4
Q_LORA = 256
KV_LORA = 128
QK_NOPE = 64
QK_ROPE = 32
QK_DIM = QK_NOPE + QK_ROPE
V_DIM = BRANCH_W // MLA_HEADS
V_SLOT = V_DIM + 16
ROPE_THETA = 10000.0
POOL_WINDOWS = (2, 4, 8, 16)
POOL_GROUP = BRANCH_W // len(POOL_WINDOWS)
POOL_HALO = max(POOL_WINDOWS) // 2
N_EXPERTS = 64
TOP_K = 6
EXPERT_FF = 256
ROUTE_SCALE = 2.5

LANES = 128
SUBLANES = 8
HEAD_SLOT = LANES
ROW_TILE = 256
ATT_TQ = 256
ATT_TK = 2048
ATT_SUB = 128
ATT_AHEAD = 8
EXPERT_ROWS = 512
MOE_TILE = 768
RUN_ROWS = 16
MOE_CHUNK = 512
VMEM_LIMIT = 56 * 1024 * 1024

F32 = jnp.float32
BF16 = jnp.bfloat16
NT_DIMS = (((1,), (1,)), ((), ()))
TN_DIMS = (((0,), (0,)), ((), ()))


def _cparams(*sem):
    return pltpu.CompilerParams(dimension_semantics=sem, vmem_limit_bytes=VMEM_LIMIT)


def _const_spec(shape):
    nd = len(shape)
    return pl.BlockSpec(shape, lambda *_: (0,) * nd, pipeline_mode=pl.Buffered(1))


def _dot(a, b):
    return jnp.dot(a, b, preferred_element_type=F32)


def _split_dot(a_f32, ones_bf16):
    hi = a_f32.astype(BF16)
    r1 = a_f32 - hi.astype(F32)
    mid = r1.astype(BF16)
    lo = (r1 - mid.astype(F32)).astype(BF16)
    return _dot(hi, ones_bf16) + _dot(mid, ones_bf16) + _dot(lo, ones_bf16)


def _split_dot_left(ones_bf16, a_f32):
    hi = a_f32.astype(BF16)
    r1 = a_f32 - hi.astype(F32)
    mid = r1.astype(BF16)
    lo = (r1 - mid.astype(F32)).astype(BF16)
    return _dot(ones_bf16, hi) + _dot(ones_bf16, mid) + _dot(ones_bf16, lo)


def _rms_rows(x):
    return x * lax.rsqrt(jnp.mean(x * x, axis=-1, keepdims=True) + EPS)


def _mod_kernel(c_ref, w_ref, b_ref, o_ref):
    c = c_ref[...]
    a = (c * jax.nn.sigmoid(c)).astype(BF16)
    o_ref[0] = _dot(a, w_ref[0].astype(BF16)) + b_ref[0]


def _modulation(cc, w_mod, b_mod):
    depth, d, d6 = w_mod.shape
    rows = cc.shape[0]
    tn = 1536
    return pl.pallas_call(
        _mod_kernel,
        grid=(depth, d6 // tn),
        in_specs=[
            pl.BlockSpec((rows, d), lambda l, n: (0, 0)),
            pl.BlockSpec((1, d, tn), lambda l, n: (l, 0, n)),
            pl.BlockSpec((1, 1, tn), lambda l, n: (l, 0, n)),
        ],
        out_specs=pl.BlockSpec((1, rows, tn), lambda l, n: (l, 0, n)),
        out_shape=jax.ShapeDtypeStruct((depth, rows, d6), F32),
        compiler_params=_cparams("parallel", "parallel"),
        name="modulation",
    )(cc, w_mod, b_mod.reshape(depth, 1, d6))


def _group_mean_sq(x, ones_bd, width):
    return _split_dot(x * x, ones_bd) * (1.0 / width)


def _in_kernel(x_ref, mod_ref, gmix_ref, w_ref, gb_ref, gcq_ref, gckv_ref, wuq_ref, wuk_ref, wuv_ref,
               gq_ref, gk_ref, ones_ref, cos_ref, sa_ref, sb_ref,
               hx_ref, mq_ref, mk_ref, mv_ref, mo_ref, g_ref, pc_ref, q_ref, k_ref, v_ref):
    x = x_ref[0]
    shift = mod_ref[0, 0:1, :]
    scale = mod_ref[0, 1:2, :]
    hx = (_rms_rows(x) * gmix_ref[...]) * (1.0 + scale) + shift
    hxb = hx.astype(BF16)
    hx_ref[0] = hxb
    p = _dot(hxb, w_ref[...])
    bw = BRANCH_W
    mq_ref[0] = p[:, 0:bw].T.astype(BF16)
    mk_ref[0] = (p[:, bw:2 * bw] * (ML_DH ** -0.5)).astype(BF16)
    mv_ref[0] = p[:, 2 * bw:3 * bw].T.astype(BF16)
    mo_ref[0] = p[:, 3 * bw:4 * bw].T.astype(BF16)
    g_ref[0] = (p[:, 3072:3200] + gb_ref[...]).T[0:4 * ML_HEADS]
    pc_ref[0] = p[:, 1024:2048]
    cq = p[:, 2048:2048 + Q_LORA]
    ckv = p[:, 2304:2304 + KV_LORA]
    kr4 = p[:, 2560:3072]

    cqn = (_rms_rows(cq) * gcq_ref[...]).astype(BF16)
    ckvn = (_rms_rows(ckv) * gckv_ref[...]).astype(BF16)
    q_pre = _dot(cqn, wuq_ref[...])
    k_pre = _dot(ckvn, wuk_ref[...]) + kr4
    v_t = _dot(ckvn, wuv_ref[...]).T
    tm = v_t.shape[1]
    extra = V_SLOT - V_DIM
    one_row = jnp.where(lax.broadcasted_iota(jnp.int32, (extra, tm), 0) == 0, 1.0, 0.0)
    v_ref[0] = jnp.concatenate(
        sum([[v_t[h * V_DIM:(h + 1) * V_DIM], one_row] for h in range(MLA_HEADS)], []), axis=0).astype(BF16)

    ones_bd = ones_ref[...]
    cos = jnp.concatenate([cos_ref[...]] * MLA_HEADS, axis=1)
    sa = jnp.concatenate([sa_ref[...]] * MLA_HEADS, axis=1)
    sb = jnp.concatenate([sb_ref[...]] * MLA_HEADS, axis=1)
    width = MLA_HEADS * HEAD_SLOT
    half = QK_ROPE // 4

    def norm_rope(t, gain):
        t = t * lax.rsqrt(_group_mean_sq(t, ones_bd, QK_DIM) + EPS) * gain
        return t * cos + pltpu.roll(t, width - half, 1) * sa + pltpu.roll(t, half, 1) * sb

    q = norm_rope(q_pre, gq_ref[...]) * (QK_DIM ** -0.5 * math.log2(math.e))
    q_ref[0] = q.T.astype(BF16)
    k_ref[0] = norm_rope(k_pre, gk_ref[...]).astype(BF16)


def _in_proj(xs, mod, lw, rope, n_lat_tiles):
    bsz, s, d = xs.shape
    tm = ROW_TILE
    nt = s // tm
    wcols = lw["w_big"].shape[1]
    slot_w = MLA_HEADS * HEAD_SLOT

    def row_spec(width):
        return pl.BlockSpec((1, tm, width), lambda t, b: (b, t, 0))

    def tab_spec():
        return pl.BlockSpec((tm, HEAD_SLOT), lambda t, b: (t, 0))

    def col_spec(height):
        return pl.BlockSpec((1, height, tm), lambda t, b: (b, 0, t))

    v_rows = MLA_HEADS * V_SLOT
    outs = [("row", d, BF16), ("col", BRANCH_W, BF16), ("row", BRANCH_W, BF16), ("col", BRANCH_W, BF16),
            ("col", BRANCH_W, BF16), ("col", 4 * ML_HEADS, F32), ("row", 1024, F32),
            ("col", slot_w, BF16), ("row", slot_w, BF16), ("col", v_rows, BF16)]
    out_specs = [row_spec(w) if kind == "row" else col_spec(w) for kind, w, _ in outs]
    out_shape = [jax.ShapeDtypeStruct((bsz, s, w) if kind == "row" else (bsz, w, s), dt) for kind, w, dt in outs]
    return pl.pallas_call(
        _in_kernel,
        grid=(nt, bsz),
        in_specs=[
            row_spec(d),
            pl.BlockSpec((1, 6, d), lambda t, b: (jnp.where(t >= n_lat_tiles, bsz, b), 0, 0)),
            _const_spec((1, d)),
            _const_spec((d, wcols)),
            _const_spec((1, LANES)),
            _const_spec((1, Q_LORA)),
            _const_spec((1, KV_LORA)),
            _const_spec((Q_LORA, slot_w)),
            _const_spec((KV_LORA, slot_w)),
            _const_spec((KV_LORA, BRANCH_W)),
            _const_spec((1, slot_w)),
            _const_spec((1, slot_w)),
            _const_spec((slot_w, slot_w)),
            tab_spec(), tab_spec(), tab_spec(),
        ],
        out_specs=out_specs,
        out_shape=out_shape,
        compiler_params=_cparams("parallel", "parallel"),
        name="norm_in_proj",
    )(xs, mod, lw["g_mix"], lw["w_big"], lw["gate_bias"], lw["g_cq"], lw["g_ckv"], lw["w_uq"], lw["w_uk"], lw["w_uv"],
      lw["g_qn"], lw["g_kn"], rope["ones_slot"], rope["cos"], rope["sin_a"], rope["sin_b"])


def _log_sigmoid(x):
    return jnp.minimum(x, 0.0) - jnp.log1p(jnp.exp(-jnp.abs(x)))


def _mlstm_kernel(qt_f, k_f, vt_f, g_f, qt_b, k_b, vt_b, g_b, hf_ref, hb_ref, c_sc, m_sc):
    @pl.when(pl.program_id(1) == 0)
    def _():
        c_sc[...] = jnp.zeros_like(c_sc)
        m_sc[...] = jnp.zeros_like(m_sc)

    L = ML_CHUNK
    dh = ML_DH
    row = lax.broadcasted_iota(jnp.int32, (L, L), 0)
    col = lax.broadcasted_iota(jnp.int32, (L, L), 1)
    diag = row == col
    ones_ll = jnp.ones((L, L), BF16)
    one_rows = jnp.where(lax.broadcasted_iota(jnp.int32, (LANES - dh, L), 0) == 0, 1.0, 0.0).astype(BF16)

    dirs = ((qt_f, k_f, vt_f, g_f), (qt_b, k_b, vt_b, g_b))
    units = []
    for d, (qt_ref, k_ref, vt_ref, g_ref) in enumerate(dirs):
        a = g_ref[0]
        lf = _log_sigmoid(a)
        valid = (row <= col) if d == 0 else (row >= col)
        b_rows = _split_dot(lf, valid.astype(BF16))
        last = L - 1 if d == 0 else 0
        for h in range(ML_HEADS):
            ci = d * 2 * ML_HEADS + h
            cf = ci + ML_HEADS
            st = d * ML_HEADS + h
            u = {"st": st, "valid": valid}
            bt = b_rows[cf:cf + 1, :]
            li = a[ci:ci + 1, :]
            b_end = bt[:, last:last + 1]
            m_st = m_sc[st:st + 1, 0:1]
            qt = qt_ref[0, h * dh:(h + 1) * dh, :]
            k = k_ref[0, :, h * dh:(h + 1) * dh]
            u["vt"] = vt_ref[0, h * dh:(h + 1) * dh, :]
            u["bt"] = bt
            u["inter"] = bt + m_st
            src = jnp.where(diag, bt - li, 0.0)
            hi = src.astype(BF16)
            lo = (src - hi.astype(F32)).astype(BF16)
            u["src"] = _dot(hi, ones_ll) + _dot(lo, ones_ll)
            w_log = b_end - bt + li
            u["m_new"] = jnp.maximum(b_end + m_st, jnp.max(w_log, axis=1, keepdims=True))
            u["decay"] = jnp.exp(b_end + m_st - u["m_new"])
            v_aug = jnp.concatenate([u["vt"], one_rows], axis=0)
            vw = (v_aug.astype(F32) * jnp.exp(w_log - u["m_new"])).astype(BF16)
            u["s_kq"] = _dot(k, qt)
            u["qc"] = _dot(c_sc[st].astype(BF16), qt)
            u["upd"] = _dot(vw, k)
            units.append(u)

    for u in units:
        d_log = jnp.where(u["valid"], u["bt"] - u["src"], -jnp.inf)
        u["m_t"] = jnp.maximum(u["inter"], jnp.max(d_log, axis=0, keepdims=True))
        w_st = jnp.exp(d_log - u["m_t"]) * u["s_kq"]
        u["w_sum"] = jnp.sum(w_st, axis=0, keepdims=True)
        u["pv"] = _dot(u["vt"], w_st.astype(BF16))

    outs = []
    for u in units:
        a_inter = jnp.exp(u["inter"] - u["m_t"])
        num = a_inter * u["qc"][0:dh, :] + u["pv"]
        den = a_inter * u["qc"][dh:dh + 1, :] + u["w_sum"]
        outs.append(num / jnp.maximum(jnp.abs(den), jnp.exp(-u["m_t"])))
        st = u["st"]
        c_sc[st] = u["decay"] * c_sc[st] + u["upd"]
        m_sc[st:st + 1, :] = jnp.broadcast_to(u["m_new"], (1, LANES))
    hf_ref[0] = jnp.concatenate(outs[0:ML_HEADS], axis=0)
    hb_ref[0] = jnp.concatenate(outs[ML_HEADS:], axis=0)


def _mlstm(qt, k, vt, gt, n_lat_chunks):
    bsz, s, _ = k.shape
    nc = s // ML_CHUNK

    def fwd_chunk(j):
        return (j + n_lat_chunks) % nc

    def bwd_chunk(j):
        return nc - 1 - j

    def specs(chunk):
        def col(height):
            return pl.BlockSpec((1, height, ML_CHUNK), lambda b, j: (b, 0, chunk(j)))
        return [col(BRANCH_W), pl.BlockSpec((1, ML_CHUNK, BRANCH_W), lambda b, j: (b, chunk(j), 0)),
                col(BRANCH_W), col(4 * ML_HEADS)], col(BRANCH_W)

    in_f, out_f = specs(fwd_chunk)
    in_b, out_b = specs(bwd_chunk)
    return pl.pallas_call(
        _mlstm_kernel,
        grid=(bsz, nc),
        in_specs=in_f + in_b,
        out_specs=[out_f, out_b],
        out_shape=[jax.ShapeDtypeStruct((bsz, BRANCH_W, s), F32)] * 2,
        scratch_shapes=[pltpu.VMEM((2 * ML_HEADS, LANES, ML_DH), F32),
                        pltpu.VMEM((2 * ML_HEADS, LANES), F32)],
        compiler_params=_cparams("parallel", "arbitrary"),
        name="mlstm_scan",
    )(qt, k, vt, gt, qt, k, vt, gt)


def _attn_kernel(qt_ref, k_ref, vt_ref, o_ref, m_sc, acc_sc, *, n_lat, n_ctx, tk):
    m_sc[...] = jnp.full_like(m_sc, -jnp.inf)
    acc_sc[...] = jnp.zeros_like(acc_sc)

    def scores(h, start, size):
        sl = slice(h * HEAD_SLOT, (h + 1) * HEAD_SLOT)
        return _dot(k_ref[0, pl.ds(start, size), sl], qt_ref[0, sl, :])

    def keys(start, size):
        n_sub = size // ATT_SUB
        order = [(j, h) for j in range(n_sub) for h in range(MLA_HEADS)]
        pending = [scores(h, start + j * ATT_SUB, ATT_SUB) for j, h in order[:ATT_AHEAD]]
        for i, (j, h) in enumerate(order):
            off = start + j * ATT_SUB
            s = pending.pop(0)
            if i + ATT_AHEAD < len(order):
                jn, hn = order[i + ATT_AHEAD]
                pending.append(scores(hn, start + jn * ATT_SUB, ATT_SUB))
            m_old = m_sc[h]
            m_new = jnp.maximum(m_old, jnp.max(s, axis=0, keepdims=True))
            p = jnp.exp2(s - m_new)
            vt = vt_ref[0, h * V_SLOT:(h + 1) * V_SLOT, pl.ds(off, ATT_SUB)]
            acc_sc[h] = jnp.exp2(m_old - m_new) * acc_sc[h] + _dot(vt, p.astype(BF16))
            m_sc[h] = m_new

    keys(n_lat, n_ctx)

    def chunk(c, carry):
        keys(pl.multiple_of(c * tk, tk), tk)
        return carry

    is_ctx_tile = pl.program_id(1) * ATT_TQ >= n_lat
    lax.fori_loop(0, jnp.where(is_ctx_tile, 0, n_lat // tk), chunk, 0)
    out_t = jnp.concatenate([acc_sc[h, 0:V_DIM, :] / acc_sc[h, V_DIM:V_DIM + 1, :] for h in range(MLA_HEADS)],
                            axis=0)
    o_ref[0] = out_t.T.astype(o_ref.dtype)


def _key_chunk(n_keys):
    return max(t for t in range(ATT_TQ, ATT_TK + 1, ATT_TQ) if n_keys % t == 0)


def _attention(qt, k, vt, n_lat):
    bsz, s, slot_w = k.shape
    tq = ATT_TQ
    return pl.pallas_call(
        functools.partial(_attn_kernel, n_lat=n_lat, n_ctx=s - n_lat, tk=_key_chunk(n_lat)),
        grid=(bsz, s // tq),
        in_specs=[pl.BlockSpec((1, slot_w, tq), lambda b, t: (b, 0, t)),
                  pl.BlockSpec((1, s, slot_w), lambda b, t: (b, 0, 0), pipeline_mode=pl.Buffered(1)),
                  pl.BlockSpec((1, MLA_HEADS * V_SLOT, s), lambda b, t: (b, 0, 0), pipeline_mode=pl.Buffered(1))],
        out_specs=pl.BlockSpec((1, tq, BRANCH_W), lambda b, t: (b, t, 0)),
        out_shape=jax.ShapeDtypeStruct((bsz, s, BRANCH_W), BF16),
        scratch_shapes=[pltpu.VMEM((MLA_HEADS, 1, tq), F32), pltpu.VMEM((MLA_HEADS, V_SLOT, tq), F32)],
        compiler_params=_cparams("parallel", "arbitrary"),
        name="mla_attention",
    )(qt, k, vt)


def _merge_kernel(hx_ref, hf_ref, hb_ref, o_ref, ya_ref, pc_ref, prev_ref, next_ref, x_ref, mod_ref,
                  hg_ref, pw_ref, ps_ref, cw_ref, cb_ref, wg_ref, bg_ref, wb_ref, wo_ref,
                  gffn_ref, xo_ref, h2_ref, *, n_lat_tiles, n_tiles):
    t = pl.program_id(1)
    tm = ROW_TILE
    bw = BRANCH_W
    halo = POOL_HALO
    no_prev = jnp.logical_or(t == 0, t == n_lat_tiles)
    no_next = jnp.logical_or(t == n_lat_tiles - 1, t == n_tiles - 1)

    h_t = hf_ref[0] + hb_ref[0]
    normed = []
    for hd in range(ML_HEADS):
        hh = h_t[hd * ML_DH:(hd + 1) * ML_DH]
        normed.append(hh * lax.rsqrt(jnp.mean(hh * hh, axis=0, keepdims=True) + EPS))
    y_ml = (jnp.concatenate(normed, axis=0) * hg_ref[...] * jax.nn.sigmoid(o_ref[0].astype(F32))).T

    prev = jnp.where(no_prev, 0.0, prev_ref[0])
    nxt = jnp.where(no_next, 0.0, next_ref[0])
    ext = jnp.concatenate([prev, pc_ref[0], nxt], axis=0)
    pe = ext[:, 0:bw]

    def rows(arr, off):
        return arr[halo + off:halo + off + tm, :]

    r = lax.broadcasted_iota(jnp.int32, (tm, 1), 0)
    lane = lax.broadcasted_iota(jnp.int32, (tm, bw), 1)
    centre = rows(pe, 0)
    acc = centre
    mean = jnp.zeros((tm, bw), F32)
    done = 0
    for gi, w in enumerate(POOL_WINDOWS):
        for off in list(range(-(w // 2), -done)) + list(range(max(done, 1), w // 2)):
            acc = acc + rows(pe, off)
        done = w // 2
        before = jnp.where(no_prev, jnp.minimum(r, w // 2), w // 2)
        after = jnp.where(no_next, jnp.minimum(tm - r, w // 2), w // 2)
        inv = 1.0 / (before + after).astype(F32)
        mean = jnp.where(lane >= gi * POOL_GROUP, acc * inv, mean)
    y_pool = _dot((mean - centre).astype(BF16), pw_ref[...]) * ps_ref[...]

    z = ext[:, 3 * bw:4 * bw] * ext[:, bw:2 * bw]
    conv = cb_ref[...] + rows(z, -1) * cw_ref[0:1, :] + rows(z, 0) * cw_ref[1:2, :] + rows(z, 1) * cw_ref[2:3, :]
    y_conv = rows(ext[:, 2 * bw:3 * bw], 0) * conv

    hxb = hx_ref[0]
    d = hxb.shape[1]
    gates = jax.nn.sigmoid(_dot(hxb, wg_ref[...]) + bg_ref[...])
    ys = (y_ml.astype(BF16), ya_ref[0], y_pool.astype(BF16), y_conv.astype(BF16))
    merged = None
    for i, y in enumerate(ys):
        term = gates[:, i * d:(i + 1) * d] * _dot(y, wb_ref[i])
        merged = term if merged is None else merged + term
    out = _dot(merged.astype(BF16), wo_ref[...])
    x_new = x_ref[0] + mod_ref[0, 2:3, :] * out
    xo_ref[0] = x_new
    h2_ref[0] = (_rms_rows(x_new) * gffn_ref[...]) * (1.0 + mod_ref[0, 4:5, :]) + mod_ref[0, 3:4, :]


def _merge(hx, hf, hb, o, y_mla, pc, xs, mod, lw, n_lat_tiles):
    bsz, s, d = xs.shape
    tm = ROW_TILE
    nt = s // tm
    halo = POOL_HALO
    per = tm // halo

    def row_spec(width):
        return pl.BlockSpec((1, tm, width), lambda b, t: (b, t, 0))

    def col_spec():
        return pl.BlockSpec((1, BRANCH_W, tm), lambda b, t: (b, 0, t))

    return pl.pallas_call(
        functools.partial(_merge_kernel, n_lat_tiles=n_lat_tiles, n_tiles=nt),
        grid=(bsz, nt),
        in_specs=[
            row_spec(d), col_spec(), col_spec(), col_spec(), row_spec(BRANCH_W),
            row_spec(1024),
            pl.BlockSpec((1, halo, 1024), lambda b, t: (b, jnp.maximum(t * per - 1, 0), 0)),
            pl.BlockSpec((1, halo, 1024), lambda b, t: (b, jnp.minimum((t + 1) * per, s // halo - 1), 0)),
            row_spec(d),
            pl.BlockSpec((1, 6, d), lambda b, t: (jnp.where(t >= n_lat_tiles, bsz, b), 0, 0)),
            _const_spec((BRANCH_W, tm)), _const_spec((BRANCH_W, BRANCH_W)),
            _const_spec((1, BRANCH_W)), _const_spec((3, BRANCH_W)), _const_spec((1, BRANCH_W)),
            _const_spec((d, 4 * d)), _const_spec((1, 4 * d)), _const_spec((4, BRANCH_W, d)),
            _const_spec((d, d)), _const_spec((1, d)),
        ],
        out_specs=[row_spec(d), row_spec(d)],
        out_shape=[jax.ShapeDtypeStruct((bsz, s, d), F32)] * 2,
        compiler_params=_cparams("parallel", "parallel"),
        name="mixer_merge",
    )(hx, hf, hb, o, y_mla, pc, pc, pc, xs, mod, lw["head_gain"], lw["pool_w"], lw["pool_scale"],
      lw["conv_w"], lw["conv_b"], lw["w_gate"], lw["b_gate"], lw["w_branch"], lw["w_out"], lw["g_ffn"])


def _router_kernel(h_ref, wr_ref, br_ref, pe_ref, we_ref, c16_ref):
    tm = h_ref.shape[0]
    scores = jax.nn.sigmoid(_dot(h_ref[...].astype(BF16), wr_ref[...]))
    sel = scores + br_ref[...]
    lane = lax.broadcasted_iota(jnp.int32, (tm, LANES), 1)
    member = jnp.zeros((tm, LANES), F32)
    for _ in range(TOP_K):
        mx = jnp.max(sel, axis=1, keepdims=True)
        ix = jnp.min(jnp.where(sel == mx, lane, LANES), axis=1, keepdims=True)
        hit = lane == ix
        member = jnp.where(hit, 1.0, member)
        sel = jnp.where(hit, -jnp.inf, sel)
    picked = member * scores
    weights = picked / jnp.sum(picked, axis=1, keepdims=True) * ROUTE_SCALE
    counts = jnp.sum(member, axis=0, keepdims=True)
    c16 = jnp.floor((counts + (RUN_ROWS - 1)) * (1.0 / RUN_ROWS))
    e_row = lax.broadcasted_iota(jnp.int32, (LANES, LANES), 0)
    e_col = lax.broadcasted_iota(jnp.int32, (LANES, LANES), 1)
    lower = (e_row < e_col).astype(BF16)
    o16 = _dot(jnp.broadcast_to(c16, (SUBLANES, LANES)).astype(BF16), lower)[0:1]
    row = lax.broadcasted_iota(jnp.int32, (tm, tm), 0)
    col = lax.broadcasted_iota(jnp.int32, (tm, tm), 1)
    rank = _dot((col < row).astype(BF16), member.astype(BF16))
    pos = jnp.where(member > 0.0, RUN_ROWS * o16 + rank, -1.0)
    pe_ref[0] = pos.T[0:N_EXPERTS]
    we_ref[0] = weights.T[0:N_EXPERTS]
    c16_ref[0] = c16.astype(jnp.int32)


def _router(h2, lw):
    n, d = h2.shape
    tm = MOE_TILE
    nt = n // tm

    def t_spec():
        return pl.BlockSpec((1, N_EXPERTS, tm), lambda i: (i, 0, 0))

    return pl.pallas_call(
        _router_kernel,
        grid=(nt,),
        in_specs=[pl.BlockSpec((tm, d), lambda i: (i, 0)), _const_spec((d, LANES)), _const_spec((1, LANES))],
        out_specs=[t_spec(), t_spec(), pl.BlockSpec((1, 1, LANES), lambda i: (i, 0, 0))],
        out_shape=[jax.ShapeDtypeStruct((nt, N_EXPERTS, tm), F32), jax.ShapeDtypeStruct((nt, N_EXPERTS, tm), F32),
                   jax.ShapeDtypeStruct((nt, 1, LANES), jnp.int32)],
        compiler_params=_cparams("parallel"),
        name="moe_router",
    )(h2, lw["w_router"], lw["b_router"])


def _tile_row_bound(tm):
    worst = tm * TOP_K + N_EXPERTS * (RUN_ROWS - 1)
    return -(-worst // MOE_CHUNK) * MOE_CHUNK


def _selection_rows(g, grp_e_ref, pe_ref, tm):
    prow = pe_ref[0, pl.ds(grp_e_ref[0, 0, g], 1), :]
    rows = (lax.broadcasted_iota(jnp.int32, (RUN_ROWS, tm), 0) + g * RUN_ROWS).astype(F32)
    return prow, rows


def _dispatch_kernel(tail_ref, n_act_ref, grp_e_ref, gdst_ref, h_ref, pe_ref, xs_ref,
                     sel_sc, xb_sc, xp_sc, zero_sc, sem, *, n_groups):
    i = pl.program_id(0)
    tm = h_ref.shape[0]
    n_chunks = sel_sc.shape[0] // MOE_CHUNK
    per_chunk = MOE_CHUNK // RUN_ROWS

    @pl.when(i == pl.num_programs(0) - 1)
    def _():
        zero_sc[...] = jnp.zeros_like(zero_sc)

        def fill_copy(q):
            return pltpu.make_async_copy(
                zero_sc.at[pl.ds(0, RUN_ROWS)],
                xs_ref.at[pl.ds(pl.multiple_of(tail_ref[q] * RUN_ROWS, RUN_ROWS), RUN_ROWS)], sem)

        def block_copy(b):
            return pltpu.make_async_copy(
                zero_sc, xs_ref.at[pl.ds(pl.multiple_of(b * EXPERT_ROWS, EXPERT_ROWS), EXPERT_ROWS)], sem)

        def block_start(b, carry):
            block_copy(b).start()
            return carry

        def block_wait(b, carry):
            block_copy(b).wait()
            return carry

        n_blocks = xs_ref.shape[0] // EXPERT_ROWS
        lax.fori_loop(n_act_ref[0], n_blocks, block_start, 0)
        lax.fori_loop(n_act_ref[0], n_blocks, block_wait, 0)

        def fill_start(q, carry):
            @pl.when(tail_ref[q] >= 0)
            def _():
                fill_copy(q).start()
            return carry

        def fill_wait(q, carry):
            @pl.when(tail_ref[q] >= 0)
            def _():
                fill_copy(q).wait()
            return carry

        lax.fori_loop(0, tail_ref.shape[0], fill_start, 0)
        lax.fori_loop(0, tail_ref.shape[0], fill_wait, 0)

    def build(g, carry):
        prow, rows = _selection_rows(g, grp_e_ref, pe_ref, tm)
        sel_sc[pl.ds(pl.multiple_of(g * RUN_ROWS, RUN_ROWS), RUN_ROWS), :] = jnp.where(
            prow == rows, 1.0, 0.0).astype(BF16)
        return carry

    lax.fori_loop(0, n_groups, build, 0, unroll=4)
    xb_sc[...] = h_ref[...].astype(BF16)

    def run_copy(g):
        return pltpu.make_async_copy(
            xp_sc.at[pl.ds(pl.multiple_of(g * RUN_ROWS, RUN_ROWS), RUN_ROWS)],
            xs_ref.at[pl.ds(pl.multiple_of(gdst_ref[0, 0, g] * RUN_ROWS, RUN_ROWS), RUN_ROWS)], sem)

    def permute(ch):
        r0 = pl.multiple_of(ch * MOE_CHUNK, MOE_CHUNK)
        xp_sc[pl.ds(r0, MOE_CHUNK), :] = _dot(sel_sc[pl.ds(r0, MOE_CHUNK), :], xb_sc[...]).astype(BF16)

    def start_chunk(ch):
        for j in range(per_chunk):
            run_copy(ch * per_chunk + j).start()

    permute(0)

    def step(ch, carry):
        start_chunk(ch - 1)
        permute(ch)
        return carry

    lax.fori_loop(1, n_chunks, step, 0)
    start_chunk(n_chunks - 1)
    pltpu.make_async_copy(xp_sc, xs_ref.at[pl.ds(0, xp_sc.shape[0])], sem).wait()


def _dispatch(h2, pe, plan, n_rows):
    n, d = h2.shape
    tm = MOE_TILE
    nt = n // tm
    rb = _tile_row_bound(tm)
    n_groups = rb // RUN_ROWS

    def smem_spec():
        return pl.BlockSpec((1, 1, n_groups), lambda i, *_: (i, 0, 0), memory_space=pltpu.SMEM)

    return pl.pallas_call(
        functools.partial(_dispatch_kernel, n_groups=n_groups),
        grid_spec=pltpu.PrefetchScalarGridSpec(
            num_scalar_prefetch=2,
            grid=(nt,),
            in_specs=[smem_spec(), smem_spec(),
                      pl.BlockSpec((tm, d), lambda i, *_: (i, 0)),
                      pl.BlockSpec((1, N_EXPERTS, tm), lambda i, *_: (i, 0, 0))],
            out_specs=pl.BlockSpec(memory_space=pl.ANY),
            scratch_shapes=[pltpu.VMEM((rb, tm), BF16), pltpu.VMEM((tm, d), BF16), pltpu.VMEM((rb, d), BF16),
                            pltpu.VMEM((EXPERT_ROWS, d), BF16), pltpu.SemaphoreType.DMA(())],
        ),
        out_shape=jax.ShapeDtypeStruct((n_rows, d), BF16),
        compiler_params=_cparams("arbitrary"),
        name="moe_dispatch",
    )(plan["tail"], plan["n_act"].reshape(1), plan["grp_e"], plan["gdst"], h2, pe)


def _expert_kernel(blk_e_ref, n_act_ref, x_ref, w13_ref, w2_ref, y_ref):
    del blk_e_ref

    @pl.when(pl.program_id(0) < n_act_ref[0])
    def _():
        up = _dot(x_ref[...], w13_ref[0])
        a = up[:, 0:EXPERT_FF]
        act = (a * jax.nn.sigmoid(a)) * up[:, EXPERT_FF:2 * EXPERT_FF]
        y_ref[...] = _dot(act.astype(BF16), w2_ref[0]).astype(BF16)

    @pl.when(pl.program_id(0) >= n_act_ref[0])
    def _():
        y_ref[...] = jnp.zeros_like(y_ref)


def _experts(xs_sorted, blk_e, n_act, lw):
    n_rows, d = xs_sorted.shape
    bm = EXPERT_ROWS

    def row_map(i, blk_e_ref, n_act_ref):
        return (jnp.minimum(i, n_act_ref[0] - 1), 0)

    def w_map(i, blk_e_ref, n_act_ref):
        return (blk_e_ref[i], 0, 0)

    return pl.pallas_call(
        _expert_kernel,
        grid_spec=pltpu.PrefetchScalarGridSpec(
            num_scalar_prefetch=2,
            grid=(n_rows // bm,),
            in_specs=[pl.BlockSpec((bm, d), row_map),
                      pl.BlockSpec((1, d, 2 * EXPERT_FF), w_map),
                      pl.BlockSpec((1, EXPERT_FF, d), w_map)],
            out_specs=pl.BlockSpec((bm, d), lambda i, *_: (i, 0)),
        ),
        out_shape=jax.ShapeDtypeStruct((n_rows, d), BF16),
        compiler_params=_cparams("arbitrary"),
        name="moe_experts",
    )(blk_e, n_act, xs_sorted, lw["w13"], lw["w2"])


def _combine_kernel(grp_e_ref, gsrc_ref, h_ref, x_ref, pe_ref, we_ref, mod_ref, modc_ref, ws13_ref,
                    ws2_ref, ys_ref, xo_ref, sel_sc, yp_sc, acc_sc, sem, *, n_groups, tiles_per_sample, n_ctx):
    i = pl.program_id(0)
    tm = h_ref.shape[0]

    def build(g, carry):
        pltpu.make_async_copy(
            ys_ref.at[pl.ds(pl.multiple_of(gsrc_ref[0, 0, g] * RUN_ROWS, RUN_ROWS), RUN_ROWS)],
            yp_sc.at[pl.ds(pl.multiple_of(g * RUN_ROWS, RUN_ROWS), RUN_ROWS)], sem).start()
        prow, rows = _selection_rows(g, grp_e_ref, pe_ref, tm)
        wrow = we_ref[0, pl.ds(grp_e_ref[0, 0, g], 1), :]
        sel_sc[pl.ds(pl.multiple_of(g * RUN_ROWS, RUN_ROWS), RUN_ROWS), :] = jnp.where(
            prow == rows, wrow, 0.0).astype(BF16)
        return carry

    lax.fori_loop(0, n_groups, build, 0, unroll=4)

    up = _dot(h_ref[...].astype(BF16), ws13_ref[...])
    a = up[:, 0:EXPERT_FF]
    act = (a * jax.nn.sigmoid(a)) * up[:, EXPERT_FF:2 * EXPERT_FF]
    acc_sc[...] = _dot(act.astype(BF16), ws2_ref[...])
    pltpu.make_async_copy(ys_ref.at[pl.ds(0, yp_sc.shape[0])], yp_sc, sem).wait()

    def gather_sum(ch, carry):
        r0 = pl.multiple_of(ch * MOE_CHUNK, MOE_CHUNK)
        acc_sc[...] += lax.dot_general(sel_sc[pl.ds(r0, MOE_CHUNK), :], yp_sc[pl.ds(r0, MOE_CHUNK), :], TN_DIMS,
                                       preferred_element_type=F32)
        return carry

    lax.fori_loop(0, sel_sc.shape[0] // MOE_CHUNK, gather_sum, 0)
    r = lax.broadcasted_iota(jnp.int32, (tm, 1), 0)
    is_ctx = jnp.logical_and(i % tiles_per_sample == tiles_per_sample - 1, r >= tm - n_ctx)
    gate = jnp.where(is_ctx, modc_ref[0, 5:6, :], mod_ref[0, 5:6, :])
    xo_ref[...] = x_ref[...] + gate * acc_sc[...]


def _combine(h2, xs, pe, we, plan, mod, ys_sorted, lw, tiles_per_sample, n_ctx, bsz):
    n, d = h2.shape
    tm = MOE_TILE
    nt = n // tm
    rb = _tile_row_bound(tm)
    n_groups = rb // RUN_ROWS

    def smem_spec():
        return pl.BlockSpec((1, 1, n_groups), lambda i, *_: (i, 0, 0), memory_space=pltpu.SMEM)

    def row_spec():
        return pl.BlockSpec((tm, d), lambda i, *_: (i, 0))

    def t_spec():
        return pl.BlockSpec((1, N_EXPERTS, tm), lambda i, *_: (i, 0, 0))

    return pl.pallas_call(
        functools.partial(_combine_kernel, n_groups=n_groups, tiles_per_sample=tiles_per_sample, n_ctx=n_ctx),
        grid_spec=pltpu.PrefetchScalarGridSpec(
            num_scalar_prefetch=0,
            grid=(nt,),
            in_specs=[smem_spec(), smem_spec(), row_spec(), row_spec(), t_spec(), t_spec(),
                      pl.BlockSpec((1, 6, d), lambda i, *_: (i // tiles_per_sample, 0, 0)),
                      pl.BlockSpec((1, 6, d), lambda i, *_: (bsz, 0, 0)),
                      pl.BlockSpec((d, 2 * EXPERT_FF), lambda i, *_: (0, 0), pipeline_mode=pl.Buffered(1)),
                      pl.BlockSpec((EXPERT_FF, d), lambda i, *_: (0, 0), pipeline_mode=pl.Buffered(1)),
                      pl.BlockSpec(memory_space=pl.ANY)],
            out_specs=row_spec(),
            scratch_shapes=[pltpu.VMEM((rb, tm), BF16), pltpu.VMEM((rb, d), BF16), pltpu.VMEM((tm, d), F32),
                            pltpu.SemaphoreType.DMA(())],
        ),
        out_shape=jax.ShapeDtypeStruct((n, d), F32),
        compiler_params=_cparams("arbitrary"),
        name="moe_combine",
    )(plan["grp_e"], plan["gsrc"], h2, xs, pe, we, mod, mod, lw["ws13"], lw["ws2"], ys_sorted)


def _moe_plan(c16, n_groups, spare16):
    nt = c16.shape[0]
    per_blk = EXPERT_ROWS // RUN_ROWS
    o16 = jnp.cumsum(c16, axis=1) - c16
    before16 = jnp.cumsum(c16, axis=0) - c16
    gtot16 = jnp.sum(c16, axis=0)
    gblk = (gtot16 + per_blk - 1) // per_blk
    gend_blk = jnp.cumsum(gblk)
    gstart16 = per_blk * (gend_blk - gblk)
    g = jnp.arange(n_groups, dtype=jnp.int32)
    grp_e = jnp.minimum(jnp.sum((o16 + c16)[:, None, :] <= g[None, :, None], axis=2), N_EXPERTS - 1)
    run0 = gstart16[None, :] + before16 - o16
    experts = jnp.arange(N_EXPERTS, dtype=jnp.int32)
    gdst = jnp.sum(jnp.where(grp_e[:, :, None] == experts[None, None, :], run0[:, None, :], 0), axis=2) + g[None, :]
    q = jnp.arange(per_blk, dtype=jnp.int32)
    tail = jnp.where(q[None, :] < (per_blk * gblk - gtot16)[:, None],
                     (gstart16 + gtot16)[:, None] + q[None, :], -1)
    used = g[None, :] < jnp.sum(c16, axis=1)[:, None]
    return {
        "grp_e": grp_e.astype(jnp.int32).reshape(nt, 1, n_groups),
        "gdst": jnp.where(used, gdst, spare16 + g[None, :]).astype(jnp.int32).reshape(nt, 1, n_groups),
        "gsrc": jnp.where(used, gdst, 0).astype(jnp.int32).reshape(nt, 1, n_groups),
        "tail": tail.astype(jnp.int32).reshape(-1),
        "gend_blk": gend_blk, "n_act": gend_blk[-1].astype(jnp.int32),
    }


def _moe(h2, xs, mod, lw, tiles_per_sample, n_ctx, bsz):
    n, d = h2.shape
    nt = n // MOE_TILE
    n_groups = _tile_row_bound(MOE_TILE) // RUN_ROWS
    pe, we, c16 = _router(h2, lw)
    worst_rows = n * TOP_K + nt * N_EXPERTS * (RUN_ROWS - 1) + N_EXPERTS * (EXPERT_ROWS - 1)
    run_blocks = -(-worst_rows // EXPERT_ROWS)
    n_blocks = run_blocks + -(-n_groups * RUN_ROWS // EXPERT_ROWS)
    plan = _moe_plan(c16[:, 0, :N_EXPERTS], n_groups, run_blocks * (EXPERT_ROWS // RUN_ROWS))
    blk = jnp.minimum(jnp.arange(n_blocks, dtype=jnp.int32), plan["n_act"] - 1)
    blk_e = jnp.minimum(jnp.sum(plan["gend_blk"][None, :] <= blk[:, None], axis=1), N_EXPERTS - 1).astype(jnp.int32)
    xs_sorted = _dispatch(h2, pe, plan, n_blocks * EXPERT_ROWS)
    ys_sorted = _experts(xs_sorted, blk_e, plan["n_act"].reshape(1), lw)
    return _combine(h2, xs, pe, we, plan, mod, ys_sorted, lw, tiles_per_sample, n_ctx, bsz)


def _slots(w, head_w, real_w):
    rows = w.shape[0]
    w = w.reshape(rows, -1, head_w)[:, :, :real_w]
    return jnp.pad(w, ((0, 0), (0, 0), (0, HEAD_SLOT - real_w))).reshape(rows, -1)


def _block_diag(blocks):
    n, r, c = blocks.shape
    out = jnp.zeros((n * r, n * c), blocks.dtype)
    for i in range(n):
        out = out.at[i * r:(i + 1) * r, i * c:(i + 1) * c].set(blocks[i])
    return out


def _layer_weights(l, p):
    d = p["w_in"].shape[1]
    w_in = p["w_in"][l]
    ml = 4 * BRANCH_W
    o_mla = ml + 4 * ML_HEADS
    o_pool = o_mla + Q_LORA + KV_LORA + QK_ROPE
    kr = w_in[:, o_mla + Q_LORA + KV_LORA:o_pool]
    kr_slot = jnp.pad(kr, ((0, 0), (QK_NOPE, HEAD_SLOT - QK_DIM)))
    w_big = jnp.concatenate([
        w_in[:, :ml],
        w_in[:, o_pool:],
        w_in[:, o_mla:o_mla + Q_LORA + KV_LORA], jnp.zeros((d, LANES), F32),
        jnp.tile(kr_slot, (1, MLA_HEADS)),
        jnp.pad(w_in[:, ml:o_mla], ((0, 0), (0, LANES - 4 * ML_HEADS))),
    ], axis=1).astype(BF16)
    w_ukv = p["mla_w_ukv"][l].reshape(KV_LORA, MLA_HEADS, QK_NOPE + V_DIM)

    def gain_slots(g):
        return jnp.tile(jnp.pad(g, (0, HEAD_SLOT - QK_DIM)), MLA_HEADS)[None, :]

    return {
        "g_mix": p["g_mix"][l][None, :], "g_ffn": p["g_ffn"][l][None, :],
        "w_big": w_big,
        "g_cq": p["mla_g_cq"][l][None, :], "g_ckv": p["mla_g_ckv"][l][None, :],
        "w_uq": _slots(p["mla_w_uq"][l], QK_DIM, QK_DIM).astype(BF16),
        "w_uk": _slots(w_ukv[:, :, :QK_NOPE].reshape(KV_LORA, -1), QK_NOPE, QK_NOPE).astype(BF16),
        "w_uv": w_ukv[:, :, QK_NOPE:].reshape(KV_LORA, BRANCH_W).astype(BF16),
        "g_qn": gain_slots(p["mla_g_qn"][l]), "g_kn": gain_slots(p["mla_g_kn"][l]),
        "gate_bias": jnp.pad(p["ml_gate_bias"][l], (0, LANES - 4 * ML_HEADS))[None, :],
        "head_gain": jnp.broadcast_to(p["ml_head_gain"][l][:, None], (BRANCH_W, ROW_TILE)),
        "pool_w": _block_diag(p["pool_w"][l]).astype(BF16),
        "pool_scale": p["pool_scale"][l][None, :],
        "conv_w": p["conv_w"][l], "conv_b": p["conv_b"][l][None, :],
        "w_gate": jnp.concatenate(list(p["w_gate"][l]), axis=1).astype(BF16),
        "b_gate": p["b_gate"][l].reshape(1, -1),
        "w_branch": p["w_branch"][l].astype(BF16),
        "w_out": p["w_out"][l].astype(BF16),
        "w_router": jnp.pad(p["moe_w_router"][l], ((0, 0), (0, LANES - N_EXPERTS))).astype(BF16),
        "b_router": jnp.pad(p["moe_b_router"][l], (0, LANES - N_EXPERTS), constant_values=-1e30)[None, :],
        "w13": jnp.concatenate([p["moe_w1"][l], p["moe_w3"][l]], axis=2).astype(BF16),
        "w2": p["moe_w2"][l].astype(BF16),
        "ws13": jnp.concatenate([p["moe_ws1"][l], p["moe_ws3"][l]], axis=1).astype(BF16),
        "ws2": p["moe_ws2"][l].astype(BF16),
    }


def _rope_tables(n_lat, n_ctx):
    t = jnp.arange(n_lat)
    n_freq = QK_ROPE // 4
    inv = ROPE_THETA ** (-jnp.arange(n_freq, dtype=F32) / n_freq)
    ang_r = (t // GRID_W).astype(F32)[:, None] * inv
    ang_c = (t % GRID_W).astype(F32)[:, None] * inv
    cos4 = jnp.concatenate([jnp.cos(ang_r)] * 2 + [jnp.cos(ang_c)] * 2, axis=1)
    zero = jnp.zeros_like(ang_r)
    sin_a = jnp.concatenate([-jnp.sin(ang_r), zero, -jnp.sin(ang_c), zero], axis=1)
    sin_b = jnp.concatenate([zero, jnp.sin(ang_r), zero, jnp.sin(ang_c)], axis=1)

    def slot(a, fill):
        a = jnp.pad(a, ((0, 0), (QK_NOPE, 0)), constant_values=fill)
        a = jnp.pad(a, ((0, 0), (0, HEAD_SLOT - QK_DIM)), constant_values=fill)
        return jnp.pad(a, ((0, n_ctx), (0, 0)), constant_values=fill)

    return {"cos": slot(cos4, 1.0), "sin_a": slot(sin_a, 0.0), "sin_b": slot(sin_b, 0.0),
            "ones_slot": _block_diag(jnp.ones((MLA_HEADS, HEAD_SLOT, HEAD_SLOT), BF16))}


def kernel(x, c, ctx, c_ctx, w_mod, b_mod, g_mix, g_ffn, w_in, ml_gate_bias, ml_head_gain, mla_g_cq, mla_g_ckv,
           mla_w_uq, mla_w_ukv, mla_g_qn, mla_g_kn, pool_w, pool_scale, conv_w, conv_b, w_gate, b_gate, w_branch,
           w_out, moe_w_router, moe_b_router, moe_w1, moe_w3, moe_w2, moe_ws1, moe_ws3, moe_ws2):
    p = dict(g_mix=g_mix, g_ffn=g_ffn, w_in=w_in, ml_gate_bias=ml_gate_bias, ml_head_gain=ml_head_gain,
             mla_g_cq=mla_g_cq, mla_g_ckv=mla_g_ckv, mla_w_uq=mla_w_uq, mla_w_ukv=mla_w_ukv, mla_g_qn=mla_g_qn,
             mla_g_kn=mla_g_kn, pool_w=pool_w, pool_scale=pool_scale, conv_w=conv_w, conv_b=conv_b,
             w_gate=w_gate, b_gate=b_gate, w_branch=w_branch, w_out=w_out, moe_w_router=moe_w_router,
             moe_b_router=moe_b_router, moe_w1=moe_w1, moe_w3=moe_w3, moe_w2=moe_w2, moe_ws1=moe_ws1,
             moe_ws3=moe_ws3, moe_ws2=moe_ws2)
    bsz, n_lat, d = x.shape
    n_ctx = ctx.shape[1]
    depth = w_mod.shape[0]
    s = n_lat + n_ctx
    assert n_ctx == ROW_TILE == ATT_TQ and n_lat % ROW_TILE == 0 and n_lat % GRID_W == 0
    assert s % MOE_TILE == 0 and n_ctx <= MOE_TILE
    n_lat_tiles = n_lat // ROW_TILE

    mod_rows = -(-(bsz + 1) // SUBLANES) * SUBLANES
    cc = jnp.concatenate([c, c_ctx[None, :], jnp.zeros((mod_rows - bsz - 1, d), F32)], axis=0)
    mods = _modulation(cc, w_mod, b_mod).reshape(depth, mod_rows, 6, d)
    rope = _rope_tables(n_lat, n_ctx)
    xs = jnp.concatenate([x, ctx], axis=1)

    for l in range(depth):
        lw = _layer_weights(l, p)
        mod = mods[l]
        hx, mq, mk, mv, o, g, pc, q, k, v = _in_proj(xs, mod, lw, rope, n_lat_tiles)
        hf, hb = _mlstm(mq, mk, mv, g, n_lat // ML_CHUNK)
        y_mla = _attention(q, k, v, n_lat)
        xs, h2 = _merge(hx, hf, hb, o, y_mla, pc, xs, mod, lw, n_lat_tiles)
        xs = _moe(h2.reshape(bsz * s, d), xs.reshape(bsz * s, d), mod, lw, s // MOE_TILE, n_ctx,
                  bsz).reshape(bsz, s, d)
    return xs[:, :n_lat]
```

```python
import functools
import math

import jax
import jax.numpy as jnp
from jax import lax
from jax.experimental import pallas as pl
from jax.experimental.pallas import tpu as pltpu

GRID_W = 64
BRANCH_W = 256
EPS = 1e-6
ML_HEADS = 4
ML_DH = BRANCH_W // ML_HEADS
ML_CHUNK = 128
MLA_HEADS = 4
Q_LORA = 256
KV_LORA = 128
QK_NOPE = 64
QK_ROPE = 32
QK_DIM = QK_NOPE + QK_ROPE
V_DIM = BRANCH_W // MLA_HEADS
V_SLOT = V_DIM + 16
ROPE_THETA = 10000.0
POOL_WINDOWS = (2, 4, 8, 16)
POOL_GROUP = BRANCH_W // len(POOL_WINDOWS)
POOL_HALO = max(POOL_WINDOWS) // 2
N_EXPERTS = 64
TOP_K = 6
EXPERT_FF = 256
ROUTE_SCALE = 2.5

LANES = 128
SUBLANES = 8
HEAD_SLOT = LANES
ROW_TILE = 256
ATT_TQ = 256
ATT_TK = 2048
ATT_SUB = 256
ATT_AHEAD = 8
EXPERT_ROWS = 1024
MOE_TILE = 768
RUN_ROWS = 16
MOE_CHUNK = 512
VMEM_LIMIT = 56 * 1024 * 1024

F32 = jnp.float32
BF16 = jnp.bfloat16
NT_DIMS = (((1,), (1,)), ((), ()))
TN_DIMS = (((0,), (0,)), ((), ()))


def _cparams(*sem):
    return pltpu.CompilerParams(dimension_semantics=sem, vmem_limit_bytes=VMEM_LIMIT)


def _const_spec(shape):
    nd = len(shape)
    return pl.BlockSpec(shape, lambda *_: (0,) * nd, pipeline_mode=pl.Buffered(1))


def _dot(a, b):
    return jnp.dot(a, b, preferred_element_type=F32)


def _split_dot(a_f32, ones_bf16):
    hi = a_f32.astype(BF16)
    r1 = a_f32 - hi.astype(F32)
    mid = r1.astype(BF16)
    lo = (r1 - mid.astype(F32)).astype(BF16)
    return _dot(hi, ones_bf16) + _dot(mid, ones_bf16) + _dot(lo, ones_bf16)


def _split_dot_left(ones_bf16, a_f32):
    hi = a_f32.astype(BF16)
    r1 = a_f32 - hi.astype(F32)
    mid = r1.astype(BF16)
    lo = (r1 - mid.astype(F32)).astype(BF16)
    return _dot(ones_bf16, hi) + _dot(ones_bf16, mid) + _dot(ones_bf16, lo)


def _rms_rows(x):
    return x * lax.rsqrt(jnp.mean(x * x, axis=-1, keepdims=True) + EPS)


def _mod_kernel(c_ref, w_ref, b_ref, o_ref):
    c = c_ref[...]
    a = (c * jax.nn.sigmoid(c)).astype(BF16)
    o_ref[0] = _dot(a, w_ref[0].astype(BF16)) + b_ref[0]


def _modulation(cc, w_mod, b_mod):
    depth, d, d6 = w_mod.shape
    rows = cc.shape[0]
    tn = 1536
    return pl.pallas_call(
        _mod_kernel,
        grid=(depth, d6 // tn),
        in_specs=[
            pl.BlockSpec((rows, d), lambda l, n: (0, 0)),
            pl.BlockSpec((1, d, tn), lambda l, n: (l, 0, n)),
            pl.BlockSpec((1, 1, tn), lambda l, n: (l, 0, n)),
        ],
        out_specs=pl.BlockSpec((1, rows, tn), lambda l, n: (l, 0, n)),
        out_shape=jax.ShapeDtypeStruct((depth, rows, d6), F32),
        compiler_params=_cparams("parallel", "parallel"),
        name="modulation",
    )(cc, w_mod, b_mod.reshape(depth, 1, d6))


def _group_mean_sq(x, ones_bd, width):
    return _split_dot(x * x, ones_bd) * (1.0 / width)


def _in_kernel(x_ref, mod_ref, gmix_ref, w_ref, gb_ref, gcq_ref, gckv_ref, wuq_ref, wuk_ref, wuv_ref,
               gq_ref, gk_ref, ones_ref, cos_ref, sa_ref, sb_ref,
               hx_ref, mq_ref, mk_ref, mv_ref, mo_ref, g_ref, pc_ref, q_ref, k_ref, v_ref):
    x = x_ref[0]
    shift = mod_ref[0, 0:1, :]
    scale = mod_ref[0, 1:2, :]
    hx = (_rms_rows(x) * gmix_ref[...]) * (1.0 + scale) + shift
    hxb = hx.astype(BF16)
    hx_ref[0] = hxb
    p = _dot(hxb, w_ref[...])
    bw = BRANCH_W
    mq_ref[0] = p[:, 0:bw].T.astype(BF16)
    mk_ref[0] = (p[:, bw:2 * bw] * (ML_DH ** -0.5)).astype(BF16)
    mv_ref[0] = p[:, 2 * bw:3 * bw].T.astype(BF16)
    mo_ref[0] = p[:, 3 * bw:4 * bw].T.astype(BF16)
    g_ref[0] = (p[:, 3072:3200] + gb_ref[...]).T[0:4 * ML_HEADS]
    pc_ref[0] = p[:, 1024:2048]
    cq = p[:, 2048:2048 + Q_LORA]
    ckv = p[:, 2304:2304 + KV_LORA]
    kr4 = p[:, 2560:3072]

    cqn = (_rms_rows(cq) * gcq_ref[...]).astype(BF16)
    ckvn = (_rms_rows(ckv) * gckv_ref[...]).astype(BF16)
    q_pre = _dot(cqn, wuq_ref[...])
    k_pre = _dot(ckvn, wuk_ref[...]) + kr4
    v_t = _dot(ckvn, wuv_ref[...]).T
    tm = v_t.shape[1]
    extra = V_SLOT - V_DIM
    one_row = jnp.where(lax.broadcasted_iota(jnp.int32, (extra, tm), 0) == 0, 1.0, 0.0)
    v_ref[0] = jnp.concatenate(
        sum([[v_t[h * V_DIM:(h + 1) * V_DIM], one_row] for h in range(MLA_HEADS)], []), axis=0).astype(BF16)

    ones_bd = ones_ref[...]
    cos = jnp.concatenate([cos_ref[...]] * MLA_HEADS, axis=1)
    sa = jnp.concatenate([sa_ref[...]] * MLA_HEADS, axis=1)
    sb = jnp.concatenate([sb_ref[...]] * MLA_HEADS, axis=1)
    width = MLA_HEADS * HEAD_SLOT
    half = QK_ROPE // 4

    def norm_rope(t, gain):
        t = t * lax.rsqrt(_group_mean_sq(t, ones_bd, QK_DIM) + EPS) * gain
        return t * cos + pltpu.roll(t, width - half, 1) * sa + pltpu.roll(t, half, 1) * sb

    q = norm_rope(q_pre, gq_ref[...]) * (QK_DIM ** -0.5 * math.log2(math.e))
    q_ref[0] = q.T.astype(BF16)
    k_ref[0] = norm_rope(k_pre, gk_ref[...]).astype(BF16)


def _in_proj(xs, mod, lw, rope, n_lat_tiles):
    bsz, s, d = xs.shape
    tm = ROW_TILE
    nt = s // tm
    wcols = lw["w_big"].shape[1]
    slot_w = MLA_HEADS * HEAD_SLOT

    def row_spec(width):
        return pl.BlockSpec((1, tm, width), lambda t, b: (b, t, 0))

    def tab_spec():
        return pl.BlockSpec((tm, HEAD_SLOT), lambda t, b: (t, 0))

    def col_spec(height):
        return pl.BlockSpec((1, height, tm), lambda t, b: (b, 0, t))

    v_rows = MLA_HEADS * V_SLOT
    outs = [("row", d, BF16), ("col", BRANCH_W, BF16), ("row", BRANCH_W, BF16), ("col", BRANCH_W, BF16),
            ("col", BRANCH_W, BF16), ("col", 4 * ML_HEADS, F32), ("row", 1024, F32),
            ("col", slot_w, BF16), ("row", slot_w, BF16), ("col", v_rows, BF16)]
    out_specs = [row_spec(w) if kind == "row" else col_spec(w) for kind, w, _ in outs]
    out_shape = [jax.ShapeDtypeStruct((bsz, s, w) if kind == "row" else (bsz, w, s), dt) for kind, w, dt in outs]
    return pl.pallas_call(
        _in_kernel,
        grid=(nt, bsz),
        in_specs=[
            row_spec(d),
            pl.BlockSpec((1, 6, d), lambda t, b: (jnp.where(t >= n_lat_tiles, bsz, b), 0, 0)),
            _const_spec((1, d)),
            _const_spec((d, wcols)),
            _const_spec((1, LANES)),
            _const_spec((1, Q_LORA)),
            _const_spec((1, KV_LORA)),
            _const_spec((Q_LORA, slot_w)),
            _const_spec((KV_LORA, slot_w)),
            _const_spec((KV_LORA, BRANCH_W)),
            _const_spec((1, slot_w)),
            _const_spec((1, slot_w)),
            _const_spec((slot_w, slot_w)),
            tab_spec(), tab_spec(), tab_spec(),
        ],
        out_specs=out_specs,
        out_shape=out_shape,
        compiler_params=_cparams("parallel", "parallel"),
        name="norm_in_proj",
    )(xs, mod, lw["g_mix"], lw["w_big"], lw["gate_bias"], lw["g_cq"], lw["g_ckv"], lw["w_uq"], lw["w_uk"], lw["w_uv"],
      lw["g_qn"], lw["g_kn"], rope["ones_slot"], rope["cos"], rope["sin_a"], rope["sin_b"])


def _log_sigmoid(x):
    return jnp.minimum(x, 0.0) - jnp.log1p(jnp.exp(-jnp.abs(x)))


def _mlstm_kernel(qt_f, k_f, vt_f, g_f, qt_b, k_b, vt_b, g_b, hf_ref, hb_ref, c_sc, m_sc):
    @pl.when(pl.program_id(1) == 0)
    def _():
        c_sc[...] = jnp.zeros_like(c_sc)
        m_sc[...] = jnp.zeros_like(m_sc)

    L = ML_CHUNK
    dh = ML_DH
    row = lax.broadcasted_iota(jnp.int32, (L, L), 0)
    col = lax.broadcasted_iota(jnp.int32, (L, L), 1)
    diag = row == col
    ones_ll = jnp.ones((L, L), BF16)
    one_rows = jnp.where(lax.broadcasted_iota(jnp.int32, (LANES - dh, L), 0) == 0, 1.0, 0.0).astype(BF16)

    dirs = ((qt_f, k_f, vt_f, g_f), (qt_b, k_b, vt_b, g_b))
    units = []
    for d, (qt_ref, k_ref, vt_ref, g_ref) in enumerate(dirs):
        a = g_ref[0]
        lf = _log_sigmoid(a)
        valid = (row <= col) if d == 0 else (row >= col)
        b_rows = _split_dot(lf, valid.astype(BF16))
        last = L - 1 if d == 0 else 0
        for h in range(ML_HEADS):
            ci = d * 2 * ML_HEADS + h
            cf = ci + ML_HEADS
            st = d * ML_HEADS + h
            u = {"st": st, "valid": valid}
            bt = b_rows[cf:cf + 1, :]
            li = a[ci:ci + 1, :]
            b_end = bt[:, last:last + 1]
            m_st = m_sc[st:st + 1, 0:1]
            qt = qt_ref[0, h * dh:(h + 1) * dh, :]
            k = k_ref[0, :, h * dh:(h + 1) * dh]
            u["vt"] = vt_ref[0, h * dh:(h + 1) * dh, :]
            u["bt"] = bt
            u["inter"] = bt + m_st
            src = jnp.where(diag, bt - li, 0.0)
            hi = src.astype(BF16)
            lo = (src - hi.astype(F32)).astype(BF16)
            u["src"] = _dot(hi, ones_ll) + _dot(lo, ones_ll)
            w_log = b_end - bt + li
            u["m_new"] = jnp.maximum(b_end + m_st, jnp.max(w_log, axis=1, keepdims=True))
            u["decay"] = jnp.exp(b_end + m_st - u["m_new"])
            v_aug = jnp.concatenate([u["vt"], one_rows], axis=0)
            vw = (v_aug.astype(F32) * jnp.exp(w_log - u["m_new"])).astype(BF16)
            u["s_kq"] = _dot(k, qt)
            u["qc"] = _dot(c_sc[st].astype(BF16), qt)
            u["upd"] = _dot(vw, k)
            units.append(u)

    for u in units:
        d_log = jnp.where(u["valid"], u["bt"] - u["src"], -jnp.inf)
        u["m_t"] = jnp.maximum(u["inter"], jnp.max(d_log, axis=0, keepdims=True))
        w_st = jnp.exp(d_log - u["m_t"]) * u["s_kq"]
        u["w_sum"] = jnp.sum(w_st, axis=0, keepdims=True)
        u["pv"] = _dot(u["vt"], w_st.astype(BF16))

    outs = []
    for u in units:
        a_inter = jnp.exp(u["inter"] - u["m_t"])
        num = a_inter * u["qc"][0:dh, :] + u["pv"]
        den = a_inter * u["qc"][dh:dh + 1, :] + u["w_sum"]
        outs.append(num / jnp.maximum(jnp.abs(den), jnp.exp(-u["m_t"])))
        st = u["st"]
        c_sc[st] = u["decay"] * c_sc[st] + u["upd"]
        m_sc[st:st + 1, :] = jnp.broadcast_to(u["m_new"], (1, LANES))
    hf_ref[0] = jnp.concatenate(outs[0:ML_HEADS], axis=0)
    hb_ref[0] = jnp.concatenate(outs[ML_HEADS:], axis=0)


def _mlstm(qt, k, vt, gt, n_lat_chunks):
    bsz, s, _ = k.shape
    nc = s // ML_CHUNK

    def fwd_chunk(j):
        return (j + n_lat_chunks) % nc

    def bwd_chunk(j):
        return nc - 1 - j

    def specs(chunk):
        def col(height):
            return pl.BlockSpec((1, height, ML_CHUNK), lambda b, j: (b, 0, chunk(j)))
        return [col(BRANCH_W), pl.BlockSpec((1, ML_CHUNK, BRANCH_W), lambda b, j: (b, chunk(j), 0)),
                col(BRANCH_W), col(4 * ML_HEADS)], col(BRANCH_W)

    in_f, out_f = specs(fwd_chunk)
    in_b, out_b = specs(bwd_chunk)
    return pl.pallas_call(
        _mlstm_kernel,
        grid=(bsz, nc),
        in_specs=in_f + in_b,
        out_specs=[out_f, out_b],
        out_shape=[jax.ShapeDtypeStruct((bsz, BRANCH_W, s), F32)] * 2,
        scratch_shapes=[pltpu.VMEM((2 * ML_HEADS, LANES, ML_DH), F32),
                        pltpu.VMEM((2 * ML_HEADS, LANES), F32)],
        compiler_params=_cparams("parallel", "arbitrary"),
        name="mlstm_scan",
    )(qt, k, vt, gt, qt, k, vt, gt)


def _attn_kernel(qt_ref, k_ref, vt_ref, o_ref, m_sc, acc_sc, *, n_lat, n_ctx, tk):
    m_sc[...] = jnp.full_like(m_sc, -jnp.inf)
    acc_sc[...] = jnp.zeros_like(acc_sc)

    def scores(h, start, size):
        sl = slice(h * HEAD_SLOT, (h + 1) * HEAD_SLOT)
        return _dot(k_ref[0, pl.ds(start, size), sl], qt_ref[0, sl, :])

    def keys(start, size):
        n_sub = size // ATT_SUB
        order = [(j, h) for j in range(n_sub) for h in range(MLA_HEADS)]
        pending = [scores(h, start + j * ATT_SUB, ATT_SUB) for j, h in order[:ATT_AHEAD]]
        for i, (j, h) in enumerate(order):
            off = start + j * ATT_SUB
            s = pending.pop(0)
            if i + ATT_AHEAD < len(order):
                jn, hn = order[i + ATT_AHEAD]
                pending.append(scores(hn, start + jn * ATT_SUB, ATT_SUB))
            m_old = m_sc[h]
            m_new = jnp.maximum(m_old, jnp.max(s, axis=0, keepdims=True))
            p = jnp.exp2(s - m_new)
            vt = vt_ref[0, h * V_SLOT:(h + 1) * V_SLOT, pl.ds(off, ATT_SUB)]
            acc_sc[h] = jnp.exp2(m_old - m_new) * acc_sc[h] + _dot(vt, p.astype(BF16))
            m_sc[h] = m_new

    keys(n_lat, n_ctx)

    def chunk(c, carry):
        keys(pl.multiple_of(c * tk, tk), tk)
        return carry

    is_ctx_tile = pl.program_id(1) * ATT_TQ >= n_lat
    lax.fori_loop(0, jnp.where(is_ctx_tile, 0, n_lat // tk), chunk, 0)
    out_t = jnp.concatenate([acc_sc[h, 0:V_DIM, :] / acc_sc[h, V_DIM:V_DIM + 1, :] for h in range(MLA_HEADS)],
                            axis=0)
    o_ref[0] = out_t.T.astype(o_ref.dtype)


def _key_chunk(n_keys):
    return max(t for t in range(ATT_TQ, ATT_TK + 1, ATT_TQ) if n_keys % t == 0)


def _attention(qt, k, vt, n_lat):
    bsz, s, slot_w = k.shape
    tq = ATT_TQ
    return pl.pallas_call(
        functools.partial(_attn_kernel, n_lat=n_lat, n_ctx=s - n_lat, tk=_key_chunk(n_lat)),
        grid=(bsz, s // tq),
        in_specs=[pl.BlockSpec((1, slot_w, tq), lambda b, t: (b, 0, t)),
                  pl.BlockSpec((1, s, slot_w), lambda b, t: (b, 0, 0), pipeline_mode=pl.Buffered(1)),
                  pl.BlockSpec((1, MLA_HEADS * V_SLOT, s), lambda b, t: (b, 0, 0), pipeline_mode=pl.Buffered(1))],
        out_specs=pl.BlockSpec((1, tq, BRANCH_W), lambda b, t: (b, t, 0)),
        out_shape=jax.ShapeDtypeStruct((bsz, s, BRANCH_W), BF16),
        scratch_shapes=[pltpu.VMEM((MLA_HEADS, 1, tq), F32), pltpu.VMEM((MLA_HEADS, V_SLOT, tq), F32)],
        compiler_params=_cparams("parallel", "arbitrary"),
        name="mla_attention",
    )(qt, k, vt)


def _merge_kernel(hx_ref, hf_ref, hb_ref, o_ref, ya_ref, pc_ref, prev_ref, next_ref, x_ref, mod_ref,
                  hg_ref, pw_ref, ps_ref, cw_ref, cb_ref, wg_ref, bg_ref, wb_ref, wo_ref,
                  gffn_ref, xo_ref, h2_ref, *, n_lat_tiles, n_tiles):
    t = pl.program_id(1)
    tm = ROW_TILE
    bw = BRANCH_W
    halo = POOL_HALO
    no_prev = jnp.logical_or(t == 0, t == n_lat_tiles)
    no_next = jnp.logical_or(t == n_lat_tiles - 1, t == n_tiles - 1)

    h_t = hf_ref[0] + hb_ref[0]
    normed = []
    for hd in range(ML_HEADS):
        hh = h_t[hd * ML_DH:(hd + 1) * ML_DH]
        normed.append(hh * lax.rsqrt(jnp.mean(hh * hh, axis=0, keepdims=True) + EPS))
    y_ml = (jnp.concatenate(normed, axis=0) * hg_ref[...] * jax.nn.sigmoid(o_ref[0].astype(F32))).T

    prev = jnp.where(no_prev, 0.0, prev_ref[0])
    nxt = jnp.where(no_next, 0.0, next_ref[0])
    ext = jnp.concatenate([prev, pc_ref[0], nxt], axis=0)
    pe = ext[:, 0:bw]

    def rows(arr, off):
        return arr[halo + off:halo + off + tm, :]

    r = lax.broadcasted_iota(jnp.int32, (tm, 1), 0)
    lane = lax.broadcasted_iota(jnp.int32, (tm, bw), 1)
    centre = rows(pe, 0)
    acc = centre
    mean = jnp.zeros((tm, bw), F32)
    done = 0
    for gi, w in enumerate(POOL_WINDOWS):
        for off in list(range(-(w // 2), -done)) + list(range(max(done, 1), w // 2)):
            acc = acc + rows(pe, off)
        done = w // 2
        before = jnp.where(no_prev, jnp.minimum(r, w // 2), w // 2)
        after = jnp.where(no_next, jnp.minimum(tm - r, w // 2), w // 2)
        inv = 1.0 / (before + after).astype(F32)
        mean = jnp.where(lane >= gi * POOL_GROUP, acc * inv, mean)
    y_pool = _dot((mean - centre).astype(BF16), pw_ref[...]) * ps_ref[...]

    z = ext[:, 3 * bw:4 * bw] * ext[:, bw:2 * bw]
    conv = cb_ref[...] + rows(z, -1) * cw_ref[0:1, :] + rows(z, 0) * cw_ref[1:2, :] + rows(z, 1) * cw_ref[2:3, :]
    y_conv = rows(ext[:, 2 * bw:3 * bw], 0) * conv

    hxb = hx_ref[0]
    d = hxb.shape[1]
    gates = jax.nn.sigmoid(_dot(hxb, wg_ref[...]) + bg_ref[...])
    ys = (y_ml.astype(BF16), ya_ref[0], y_pool.astype(BF16), y_conv.astype(BF16))
    merged = None
    for i, y in enumerate(ys):
        term = gates[:, i * d:(i + 1) * d] * _dot(y, wb_ref[i])
        merged = term if merged is None else merged + term
    out = _dot(merged.astype(BF16), wo_ref[...])
    x_new = x_ref[0] + mod_ref[0, 2:3, :] * out
    xo_ref[0] = x_new
    h2_ref[0] = (_rms_rows(x_new) * gffn_ref[...]) * (1.0 + mod_ref[0, 4:5, :]) + mod_ref[0, 3:4, :]


def _merge(hx, hf, hb, o, y_mla, pc, xs, mod, lw, n_lat_tiles):
    bsz, s, d = xs.shape
    tm = ROW_TILE
    nt = s // tm
    halo = POOL_HALO
    per = tm // halo

    def row_spec(width):
        return pl.BlockSpec((1, tm, width), lambda b, t: (b, t, 0))

    def col_spec():
        return pl.BlockSpec((1, BRANCH_W, tm), lambda b, t: (b, 0, t))

    return pl.pallas_call(
        functools.partial(_merge_kernel, n_lat_tiles=n_lat_tiles, n_tiles=nt),
        grid=(bsz, nt),
        in_specs=[
            row_spec(d), col_spec(), col_spec(), col_spec(), row_spec(BRANCH_W),
            row_spec(1024),
            pl.BlockSpec((1, halo, 1024), lambda b, t: (b, jnp.maximum(t * per - 1, 0), 0)),
            pl.BlockSpec((1, halo, 1024), lambda b, t: (b, jnp.minimum((t + 1) * per, s // halo - 1), 0)),
            row_spec(d),
            pl.BlockSpec((1, 6, d), lambda b, t: (jnp.where(t >= n_lat_tiles, bsz, b), 0, 0)),
            _const_spec((BRANCH_W, tm)), _const_spec((BRANCH_W, BRANCH_W)),
            _const_spec((1, BRANCH_W)), _const_spec((3, BRANCH_W)), _const_spec((1, BRANCH_W)),
            _const_spec((d, 4 * d)), _const_spec((1, 4 * d)), _const_spec((4, BRANCH_W, d)),
            _const_spec((d, d)), _const_spec((1, d)),
        ],
        out_specs=[row_spec(d), row_spec(d)],
        out_shape=[jax.ShapeDtypeStruct((bsz, s, d), F32)] * 2,
        compiler_params=_cparams("parallel", "parallel"),
        name="mixer_merge",
    )(hx, hf, hb, o, y_mla, pc, pc, pc, xs, mod, lw["head_gain"], lw["pool_w"], lw["pool_scale"],
      lw["conv_w"], lw["conv_b"], lw["w_gate"], lw["b_gate"], lw["w_branch"], lw["w_out"], lw["g_ffn"])


def _router_kernel(h_ref, wr_ref, br_ref, pe_ref, we_ref, c16_ref):
    tm = h_ref.shape[0]
    scores = jax.nn.sigmoid(_dot(h_ref[...].astype(BF16), wr_ref[...]))
    sel = scores + br_ref[...]
    lane = lax.broadcasted_iota(jnp.int32, (tm, LANES), 1)
    member = jnp.zeros((tm, LANES), F32)
    for _ in range(TOP_K):
        mx = jnp.max(sel, axis=1, keepdims=True)
        ix = jnp.min(jnp.where(sel == mx, lane, LANES), axis=1, keepdims=True)
        hit = lane == ix
        member = jnp.where(hit, 1.0, member)
        sel = jnp.where(hit, -jnp.inf, sel)
    picked = member * scores
    weights = picked / jnp.sum(picked, axis=1, keepdims=True) * ROUTE_SCALE
    counts = jnp.sum(member, axis=0, keepdims=True)
    c16 = jnp.floor((counts + (RUN_ROWS - 1)) * (1.0 / RUN_ROWS))
    e_row = lax.broadcasted_iota(jnp.int32, (LANES, LANES), 0)
    e_col = lax.broadcasted_iota(jnp.int32, (LANES, LANES), 1)
    lower = (e_row < e_col).astype(BF16)
    o16 = _dot(jnp.broadcast_to(c16, (SUBLANES, LANES)).astype(BF16), lower)[0:1]
    row = lax.broadcasted_iota(jnp.int32, (tm, tm), 0)
    col = lax.broadcasted_iota(jnp.int32, (tm, tm), 1)
    rank = _dot((col < row).astype(BF16), member.astype(BF16))
    pos = jnp.where(member > 0.0, RUN_ROWS * o16 + rank, -1.0)
    pe_ref[0] = pos.T[0:N_EXPERTS]
    we_ref[0] = weights.T[0:N_EXPERTS]
    c16_ref[0] = c16.astype(jnp.int32)


def _router(h2, lw):
    n, d = h2.shape
    tm = MOE_TILE
    nt = n // tm

    def t_spec():
        return pl.BlockSpec((1, N_EXPERTS, tm), lambda i: (i, 0, 0))

    return pl.pallas_call(
        _router_kernel,
        grid=(nt,),
        in_specs=[pl.BlockSpec((tm, d), lambda i: (i, 0)), _const_spec((d, LANES)), _const_spec((1, LANES))],
        out_specs=[t_spec(), t_spec(), pl.BlockSpec((1, 1, LANES), lambda i: (i, 0, 0))],
        out_shape=[jax.ShapeDtypeStruct((nt, N_EXPERTS, tm), F32), jax.ShapeDtypeStruct((nt, N_EXPERTS, tm), F32),
                   jax.ShapeDtypeStruct((nt, 1, LANES), jnp.int32)],
        compiler_params=_cparams("parallel"),
        name="moe_router",
    )(h2, lw["w_router"], lw["b_router"])


def _tile_row_bound(tm):
    worst = tm * TOP_K + N_EXPERTS * (RUN_ROWS - 1)
    return -(-worst // MOE_CHUNK) * MOE_CHUNK


def _selection_rows(g, grp_e_ref, pe_ref, tm):
    prow = pe_ref[0, pl.ds(grp_e_ref[0, 0, g], 1), :]
    rows = (lax.broadcasted_iota(jnp.int32, (RUN_ROWS, tm), 0) + g * RUN_ROWS).astype(F32)
    return prow, rows


def _dispatch_kernel(tail_ref, n_act_ref, grp_e_ref, gdst_ref, h_ref, pe_ref, xs_ref,
                     sel_sc, xb_sc, xp_sc, zero_sc, sem, *, n_groups):
    i = pl.program_id(0)
    tm = h_ref.shape[0]
    n_chunks = sel_sc.shape[0] // MOE_CHUNK
    per_chunk = MOE_CHUNK // RUN_ROWS

    @pl.when(i == pl.num_programs(0) - 1)
    def _():
        zero_sc[...] = jnp.zeros_like(zero_sc)

        def fill_copy(q):
            return pltpu.make_async_copy(
                zero_sc.at[pl.ds(0, RUN_ROWS)],
                xs_ref.at[pl.ds(pl.multiple_of(tail_ref[q] * RUN_ROWS, RUN_ROWS), RUN_ROWS)], sem)

        def block_copy(b):
            return pltpu.make_async_copy(
                zero_sc, xs_ref.at[pl.ds(pl.multiple_of(b * EXPERT_ROWS, EXPERT_ROWS), EXPERT_ROWS)], sem)

        def block_start(b, carry):
            block_copy(b).start()
            return carry

        def block_wait(b, carry):
            block_copy(b).wait()
            return carry

        n_blocks = xs_ref.shape[0] // EXPERT_ROWS
        lax.fori_loop(n_act_ref[0], n_blocks, block_start, 0)
        lax.fori_loop(n_act_ref[0], n_blocks, block_wait, 0)

        def fill_start(q, carry):
            @pl.when(tail_ref[q] >= 0)
            def _():
                fill_copy(q).start()
            return carry

        def fill_wait(q, carry):
            @pl.when(tail_ref[q] >= 0)
            def _():
                fill_copy(q).wait()
            return carry

        lax.fori_loop(0, tail_ref.shape[0], fill_start, 0)
        lax.fori_loop(0, tail_ref.shape[0], fill_wait, 0)

    def build(g, carry):
        prow, rows = _selection_rows(g, grp_e_ref, pe_ref, tm)
        sel_sc[pl.ds(pl.multiple_of(g * RUN_ROWS, RUN_ROWS), RUN_ROWS), :] = jnp.where(
            prow == rows, 1.0, 0.0).astype(BF16)
        return carry

    lax.fori_loop(0, n_groups, build, 0, unroll=4)
    xb_sc[...] = h_ref[...].astype(BF16)

    def run_copy(g):
        return pltpu.make_async_copy(
            xp_sc.at[pl.ds(pl.multiple_of(g * RUN_ROWS, RUN_ROWS), RUN_ROWS)],
            xs_ref.at[pl.ds(pl.multiple_of(gdst_ref[0, 0, g] * RUN_ROWS, RUN_ROWS), RUN_ROWS)], sem)

    def permute(ch):
        r0 = pl.multiple_of(ch * MOE_CHUNK, MOE_CHUNK)
        xp_sc[pl.ds(r0, MOE_CHUNK), :] = _dot(sel_sc[pl.ds(r0, MOE_CHUNK), :], xb_sc[...]).astype(BF16)

    def start_chunk(ch):
        for j in range(per_chunk):
            run_copy(ch * per_chunk + j).start()

    permute(0)

    def step(ch, carry):
        start_chunk(ch - 1)
        permute(ch)
        return carry

    lax.fori_loop(1, n_chunks, step, 0)
    start_chunk(n_chunks - 1)
    pltpu.make_async_copy(xp_sc, xs_ref.at[pl.ds(0, xp_sc.shape[0])], sem).wait()


def _dispatch(h2, pe, plan, n_rows):
    n, d = h2.shape
    tm = MOE_TILE
    nt = n // tm
    rb = _tile_row_bound(tm)
    n_groups = rb // RUN_ROWS

    def smem_spec():
        return pl.BlockSpec((1, 1, n_groups), lambda i, *_: (i, 0, 0), memory_space=pltpu.SMEM)

    return pl.pallas_call(
        functools.partial(_dispatch_kernel, n_groups=n_groups),
        grid_spec=pltpu.PrefetchScalarGridSpec(
            num_scalar_prefetch=2,
            grid=(nt,),
            in_specs=[smem_spec(), smem_spec(),
                      pl.BlockSpec((tm, d), lambda i, *_: (i, 0)),
                      pl.BlockSpec((1, N_EXPERTS, tm), lambda i, *_: (i, 0, 0))],
            out_specs=pl.BlockSpec(memory_space=pl.ANY),
            scratch_shapes=[pltpu.VMEM((rb, tm), BF16), pltpu.VMEM((tm, d), BF16), pltpu.VMEM((rb, d), BF16),
                            pltpu.VMEM((EXPERT_ROWS, d), BF16), pltpu.SemaphoreType.DMA(())],
        ),
        out_shape=jax.ShapeDtypeStruct((n_rows, d), BF16),
        compiler_params=_cparams("arbitrary"),
        name="moe_dispatch",
    )(plan["tail"], plan["n_act"].reshape(1), plan["grp_e"], plan["gdst"], h2, pe)


def _expert_kernel(blk_e_ref, n_act_ref, x_ref, w13_ref, w2_ref, y_ref):
    del blk_e_ref

    @pl.when(pl.program_id(0) < n_act_ref[0])
    def _():
        up = _dot(x_ref[...], w13_ref[0])
        a = up[:, 0:EXPERT_FF]
        act = (a * jax.nn.sigmoid(a)) * up[:, EXPERT_FF:2 * EXPERT_FF]
        y_ref[...] = _dot(act.astype(BF16), w2_ref[0]).astype(BF16)

    @pl.when(pl.program_id(0) >= n_act_ref[0])
    def _():
        y_ref[...] = jnp.zeros_like(y_ref)


def _experts(xs_sorted, blk_e, n_act, lw):
    n_rows, d = xs_sorted.shape
    bm = EXPERT_ROWS

    def row_map(i, blk_e_ref, n_act_ref):
        return (jnp.minimum(i, n_act_ref[0] - 1), 0)

    def w_map(i, blk_e_ref, n_act_ref):
        return (blk_e_ref[i], 0, 0)

    return pl.pallas_call(
        _expert_kernel,
        grid_spec=pltpu.PrefetchScalarGridSpec(
            num_scalar_prefetch=2,
            grid=(n_rows // bm,),
            in_specs=[pl.BlockSpec((bm, d), row_map),
                      pl.BlockSpec((1, d, 2 * EXPERT_FF), w_map),
                      pl.BlockSpec((1, EXPERT_FF, d), w_map)],
            out_specs=pl.BlockSpec((bm, d), lambda i, *_: (i, 0)),
        ),
        out_shape=jax.ShapeDtypeStruct((n_rows, d), BF16),
        compiler_params=_cparams("arbitrary"),
        name="moe_experts",
    )(blk_e, n_act, xs_sorted, lw["w13"], lw["w2"])


def _combine_kernel(grp_e_ref, gsrc_ref, h_ref, x_ref, pe_ref, we_ref, mod_ref, modc_ref, ws13_ref,
                    ws2_ref, ys_ref, xo_ref, sel_sc, yp_sc, acc_sc, sem, *, n_groups, tiles_per_sample, n_ctx):
    i = pl.program_id(0)
    tm = h_ref.shape[0]

    def build(g, carry):
        pltpu.make_async_copy(
            ys_ref.at[pl.ds(pl.multiple_of(gsrc_ref[0, 0, g] * RUN_ROWS, RUN_ROWS), RUN_ROWS)],
            yp_sc.at[pl.ds(pl.multiple_of(g * RUN_ROWS, RUN_ROWS), RUN_ROWS)], sem).start()
        prow, rows = _selection_rows(g, grp_e_ref, pe_ref, tm)
        wrow = we_ref[0, pl.ds(grp_e_ref[0, 0, g], 1), :]
        sel_sc[pl.ds(pl.multiple_of(g * RUN_ROWS, RUN_ROWS), RUN_ROWS), :] = jnp.where(
            prow == rows, wrow, 0.0).astype(BF16)
        return carry

    lax.fori_loop(0, n_groups, build, 0, unroll=4)

    up = _dot(h_ref[...].astype(BF16), ws13_ref[...])
    a = up[:, 0:EXPERT_FF]
    act = (a * jax.nn.sigmoid(a)) * up[:, EXPERT_FF:2 * EXPERT_FF]
    acc_sc[...] = _dot(act.astype(BF16), ws2_ref[...])
    pltpu.make_async_copy(ys_ref.at[pl.ds(0, yp_sc.shape[0])], yp_sc, sem).wait()

    def gather_sum(ch, carry):
        r0 = pl.multiple_of(ch * MOE_CHUNK, MOE_CHUNK)
        acc_sc[...] += lax.dot_general(sel_sc[pl.ds(r0, MOE_CHUNK), :], yp_sc[pl.ds(r0, MOE_CHUNK), :], TN_DIMS,
                                       preferred_element_type=F32)
        return carry

    lax.fori_loop(0, sel_sc.shape[0] // MOE_CHUNK, gather_sum, 0)
    r = lax.broadcasted_iota(jnp.int32, (tm, 1), 0)
    is_ctx = jnp.logical_and(i % tiles_per_sample == tiles_per_sample - 1, r >= tm - n_ctx)
    gate = jnp.where(is_ctx, modc_ref[0, 5:6, :], mod_ref[0, 5:6, :])
    xo_ref[...] = x_ref[...] + gate * acc_sc[...]


def _combine(h2, xs, pe, we, plan, mod, ys_sorted, lw, tiles_per_sample, n_ctx, bsz):
    n, d = h2.shape
    tm = MOE_TILE
    nt = n // tm
    rb = _tile_row_bound(tm)
    n_groups = rb // RUN_ROWS

    def smem_spec():
        return pl.BlockSpec((1, 1, n_groups), lambda i, *_: (i, 0, 0), memory_space=pltpu.SMEM)

    def row_spec():
        return pl.BlockSpec((tm, d), lambda i, *_: (i, 0))

    def t_spec():
        return pl.BlockSpec((1, N_EXPERTS, tm), lambda i, *_: (i, 0, 0))

    return pl.pallas_call(
        functools.partial(_combine_kernel, n_groups=n_groups, tiles_per_sample=tiles_per_sample, n_ctx=n_ctx),
        grid_spec=pltpu.PrefetchScalarGridSpec(
            num_scalar_prefetch=0,
            grid=(nt,),
            in_specs=[smem_spec(), smem_spec(), row_spec(), row_spec(), t_spec(), t_spec(),
                      pl.BlockSpec((1, 6, d), lambda i, *_: (i // tiles_per_sample, 0, 0)),
                      pl.BlockSpec((1, 6, d), lambda i, *_: (bsz, 0, 0)),
                      pl.BlockSpec((d, 2 * EXPERT_FF), lambda i, *_: (0, 0), pipeline_mode=pl.Buffered(1)),
                      pl.BlockSpec((EXPERT_FF, d), lambda i, *_: (0, 0), pipeline_mode=pl.Buffered(1)),
                      pl.BlockSpec(memory_space=pl.ANY)],
            out_specs=row_spec(),
            scratch_shapes=[pltpu.VMEM((rb, tm), BF16), pltpu.VMEM((rb, d), BF16), pltpu.VMEM((tm, d), F32),
                            pltpu.SemaphoreType.DMA(())],
        ),
        out_shape=jax.ShapeDtypeStruct((n, d), F32),
        compiler_params=_cparams("arbitrary"),
        name="moe_combine",
    )(plan["grp_e"], plan["gsrc"], h2, xs, pe, we, mod, mod, lw["ws13"], lw["ws2"], ys_sorted)


def _moe_plan(c16, n_groups, spare16):
    nt = c16.shape[0]
    per_blk = EXPERT_ROWS // RUN_ROWS
    o16 = jnp.cumsum(c16, axis=1) - c16
    before16 = jnp.cumsum(c16, axis=0) - c16
    gtot16 = jnp.sum(c16, axis=0)
    gblk = (gtot16 + per_blk - 1) // per_blk
    gend_blk = jnp.cumsum(gblk)
    gstart16 = per_blk * (gend_blk - gblk)
    g = jnp.arange(n_groups, dtype=jnp.int32)
    grp_e = jnp.minimum(jnp.sum((o16 + c16)[:, None, :] <= g[None, :, None], axis=2), N_EXPERTS - 1)
    run0 = gstart16[None, :] + before16 - o16
    experts = jnp.arange(N_EXPERTS, dtype=jnp.int32)
    gdst = jnp.sum(jnp.where(grp_e[:, :, None] == experts[None, None, :], run0[:, None, :], 0), axis=2) + g[None, :]
    q = jnp.arange(per_blk, dtype=jnp.int32)
    tail = jnp.where(q[None, :] < (per_blk * gblk - gtot16)[:, None],
                     (gstart16 + gtot16)[:, None] + q[None, :], -1)
    used = g[None, :] < jnp.sum(c16, axis=1)[:, None]
    return {
        "grp_e": grp_e.astype(jnp.int32).reshape(nt, 1, n_groups),
        "gdst": jnp.where(used, gdst, spare16 + g[None, :]).astype(jnp.int32).reshape(nt, 1, n_groups),
        "gsrc": jnp.where(used, gdst, 0).astype(jnp.int32).reshape(nt, 1, n_groups),
        "tail": tail.astype(jnp.int32).reshape(-1),
        "gend_blk": gend_blk, "n_act": gend_blk[-1].astype(jnp.int32),
    }


def _moe(h2, xs, mod, lw, tiles_per_sample, n_ctx, bsz):
    n, d = h2.shape
    nt = n // MOE_TILE
    n_groups = _tile_row_bound(MOE_TILE) // RUN_ROWS
    pe, we, c16 = _router(h2, lw)
    worst_rows = n * TOP_K + nt * N_EXPERTS * (RUN_ROWS - 1) + N_EXPERTS * (EXPERT_ROWS - 1)
    run_blocks = -(-worst_rows // EXPERT_ROWS)
    n_blocks = run_blocks + -(-n_groups * RUN_ROWS // EXPERT_ROWS)
    plan = _moe_plan(c16[:, 0, :N_EXPERTS], n_groups, run_blocks * (EXPERT_ROWS // RUN_ROWS))
    blk = jnp.minimum(jnp.arange(n_blocks, dtype=jnp.int32), plan["n_act"] - 1)
    blk_e = jnp.minimum(jnp.sum(plan["gend_blk"][None, :] <= blk[:, None], axis=1), N_EXPERTS - 1).astype(jnp.int32)
    xs_sorted = _dispatch(h2, pe, plan, n_blocks * EXPERT_ROWS)
    ys_sorted = _experts(xs_sorted, blk_e, plan["n_act"].reshape(1), lw)
    return _combine(h2, xs, pe, we, plan, mod, ys_sorted, lw, tiles_per_sample, n_ctx, bsz)


def _slots(w, head_w, real_w):
    rows = w.shape[0]
    w = w.reshape(rows, -1, head_w)[:, :, :real_w]
    return jnp.pad(w, ((0, 0), (0, 0), (0, HEAD_SLOT - real_w))).reshape(rows, -1)


def _block_diag(blocks):
    n, r, c = blocks.shape
    out = jnp.zeros((n * r, n * c), blocks.dtype)
    for i in range(n):
        out = out.at[i * r:(i + 1) * r, i * c:(i + 1) * c].set(blocks[i])
    return out


def _layer_weights(l, p):
    d = p["w_in"].shape[1]
    w_in = p["w_in"][l]
    ml = 4 * BRANCH_W
    o_mla = ml + 4 * ML_HEADS
    o_pool = o_mla + Q_LORA + KV_LORA + QK_ROPE
    kr = w_in[:, o_mla + Q_LORA + KV_LORA:o_pool]
    kr_slot = jnp.pad(kr, ((0, 0), (QK_NOPE, HEAD_SLOT - QK_DIM)))
    w_big = jnp.concatenate([
        w_in[:, :ml],
        w_in[:, o_pool:],
        w_in[:, o_mla:o_mla + Q_LORA + KV_LORA], jnp.zeros((d, LANES), F32),
        jnp.tile(kr_slot, (1, MLA_HEADS)),
        jnp.pad(w_in[:, ml:o_mla], ((0, 0), (0, LANES - 4 * ML_HEADS))),
    ], axis=1).astype(BF16)
    w_ukv = p["mla_w_ukv"][l].reshape(KV_LORA, MLA_HEADS, QK_NOPE + V_DIM)

    def gain_slots(g):
        return jnp.tile(jnp.pad(g, (0, HEAD_SLOT - QK_DIM)), MLA_HEADS)[None, :]

    return {
        "g_mix": p["g_mix"][l][None, :], "g_ffn": p["g_ffn"][l][None, :],
        "w_big": w_big,
        "g_cq": p["mla_g_cq"][l][None, :], "g_ckv": p["mla_g_ckv"][l][None, :],
        "w_uq": _slots(p["mla_w_uq"][l], QK_DIM, QK_DIM).astype(BF16),
        "w_uk": _slots(w_ukv[:, :, :QK_NOPE].reshape(KV_LORA, -1), QK_NOPE, QK_NOPE).astype(BF16),
        "w_uv": w_ukv[:, :, QK_NOPE:].reshape(KV_LORA, BRANCH_W).astype(BF16),
        "g_qn": gain_slots(p["mla_g_qn"][l]), "g_kn": gain_slots(p["mla_g_kn"][l]),
        "gate_bias": jnp.pad(p["ml_gate_bias"][l], (0, LANES - 4 * ML_HEADS))[None, :],
        "head_gain": jnp.broadcast_to(p["ml_head_gain"][l][:, None], (BRANCH_W, ROW_TILE)),
        "pool_w": _block_diag(p["pool_w"][l]).astype(BF16),
        "pool_scale": p["pool_scale"][l][None, :],
        "conv_w": p["conv_w"][l], "conv_b": p["conv_b"][l][None, :],
        "w_gate": jnp.concatenate(list(p["w_gate"][l]), axis=1).astype(BF16),
        "b_gate": p["b_gate"][l].reshape(1, -1),
        "w_branch": p["w_branch"][l].astype(BF16),
        "w_out": p["w_out"][l].astype(BF16),
        "w_router": jnp.pad(p["moe_w_router"][l], ((0, 0), (0, LANES - N_EXPERTS))).astype(BF16),
        "b_router": jnp.pad(p["moe_b_router"][l], (0, LANES - N_EXPERTS), constant_values=-1e30)[None, :],
        "w13": jnp.concatenate([p["moe_w1"][l], p["moe_w3"][l]], axis=2).astype(BF16),
        "w2": p["moe_w2"][l].astype(BF16),
        "ws13": jnp.concatenate([p["moe_ws1"][l], p["moe_ws3"][l]], axis=1).astype(BF16),
        "ws2": p["moe_ws2"][l].astype(BF16),
    }


def _rope_tables(n_lat, n_ctx):
    t = jnp.arange(n_lat)
    n_freq = QK_ROPE // 4
    inv = ROPE_THETA ** (-jnp.arange(n_freq, dtype=F32) / n_freq)
    ang_r = (t // GRID_W).astype(F32)[:, None] * inv
    ang_c = (t % GRID_W).astype(F32)[:, None] * inv
    cos4 = jnp.concatenate([jnp.cos(ang_r)] * 2 + [jnp.cos(ang_c)] * 2, axis=1)
    zero = jnp.zeros_like(ang_r)
    sin_a = jnp.concatenate([-jnp.sin(ang_r), zero, -jnp.sin(ang_c), zero], axis=1)
    sin_b = jnp.concatenate([zero, jnp.sin(ang_r), zero, jnp.sin(ang_c)], axis=1)

    def slot(a, fill):
        a = jnp.pad(a, ((0, 0), (QK_NOPE, 0)), constant_values=fill)
        a = jnp.pad(a, ((0, 0), (0, HEAD_SLOT - QK_DIM)), constant_values=fill)
        return jnp.pad(a, ((0, n_ctx), (0, 0)), constant_values=fill)

    return {"cos": slot(cos4, 1.0), "sin_a": slot(sin_a, 0.0), "sin_b": slot(sin_b, 0.0),
            "ones_slot": _block_diag(jnp.ones((MLA_HEADS, HEAD_SLOT, HEAD_SLOT), BF16))}


def kernel(x, c, ctx, c_ctx, w_mod, b_mod, g_mix, g_ffn, w_in, ml_gate_bias, ml_head_gain, mla_g_cq, mla_g_ckv,
           mla_w_uq, mla_w_ukv, mla_g_qn, mla_g_kn, pool_w, pool_scale, conv_w, conv_b, w_gate, b_gate, w_branch,
           w_out, moe_w_router, moe_b_router, moe_w1, moe_w3, moe_w2, moe_ws1, moe_ws3, moe_ws2):
    p = dict(g_mix=g_mix, g_ffn=g_ffn, w_in=w_in, ml_gate_bias=ml_gate_bias, ml_head_gain=ml_head_gain,
             mla_g_cq=mla_g_cq, mla_g_ckv=mla_g_ckv, mla_w_uq=mla_w_uq, mla_w_ukv=mla_w_ukv, mla_g_qn=mla_g_qn,
             mla_g_kn=mla_g_kn, pool_w=pool_w, pool_scale=pool_scale, conv_w=conv_w, conv_b=conv_b,
             w_gate=w_gate, b_gate=b_gate, w_branch=w_branch, w_out=w_out, moe_w_router=moe_w_router,
             moe_b_router=moe_b_router, moe_w1=moe_w1, moe_w3=moe_w3, moe_w2=moe_w2, moe_ws1=moe_ws1,
             moe_ws3=moe_ws3, moe_ws2=moe_ws2)
    bsz, n_lat, d = x.shape
    n_ctx = ctx.shape[1]
    depth = w_mod.shape[0]
    s = n_lat + n_ctx
    assert n_ctx == ROW_TILE == ATT_TQ and n_lat % ROW_TILE == 0 and n_lat % GRID_W == 0
    assert s % MOE_TILE == 0 and n_ctx <= MOE_TILE
    n_lat_tiles = n_lat // ROW_TILE

    mod_rows = -(-(bsz + 1) // SUBLANES) * SUBLANES
    cc = jnp.concatenate([c, c_ctx[None, :], jnp.zeros((mod_rows - bsz - 1, d), F32)], axis=0)
    mods = _modulation(cc, w_mod, b_mod).reshape(depth, mod_rows, 6, d)
    rope = _rope_tables(n_lat, n_ctx)
    xs = jnp.concatenate([x, ctx], axis=1)

    for l in range(depth):
        lw = _layer_weights(l, p)
        mod = mods[l]
        hx, mq, mk, mv, o, g, pc, q, k, v = _in_proj(xs, mod, lw, rope, n_lat_tiles)
        hf, hb = _mlstm(mq, mk, mv, g, n_lat // ML_CHUNK)
        y_mla = _attention(q, k, v, n_lat)
        xs, h2 = _merge(hx, hf, hb, o, y_mla, pc, xs, mod, lw, n_lat_tiles)
        xs = _moe(h2.reshape(bsz * s, d), xs.reshape(bsz * s, d), mod, lw, s // MOE_TILE, n_ctx,
                  bsz).reshape(bsz, s, d)
    return xs[:, :n_lat]
```

```python
import functools
import math

import jax
import jax.numpy as jnp
from jax import lax
from jax.experimental import pallas as pl
from jax.experimental.pallas import tpu as pltpu

GRID_W = 64
BRANCH_W = 256
EPS = 1e-6
ML_HEADS = 4
ML_DH = BRANCH_W // ML_HEADS
ML_CHUNK = 128
MLA_HEADS = 4
Q_LORA = 256
KV_LORA = 128
QK_NOPE = 64
QK_ROPE = 32
QK_DIM = QK_NOPE + QK_ROPE
V_DIM = BRANCH_W // MLA_HEADS
V_SLOT = V_DIM + 16
ROPE_THETA = 10000.0
POOL_WINDOWS = (2, 4, 8, 16)
POOL_GROUP = BRANCH_W // len(POOL_WINDOWS)
POOL_HALO = max(POOL_WINDOWS) // 2
N_EXPERTS = 64
TOP_K = 6
EXPERT_FF = 256
ROUTE_SCALE = 2.5

LANES = 128
SUBLANES = 8
HEAD_SLOT = LANES
ROW_TILE = 256
ATT_TQ = 256
ATT_TK = 2048
ATT_SUB = 128
ATT_AHEAD = 8
EXPERT_ROWS = 1024
MOE_TILE = 256
RUN_ROWS = 16
MOE_CHUNK = 512
VMEM_LIMIT = 56 * 1024 * 1024

F32 = jnp.float32
BF16 = jnp.bfloat16
NT_DIMS = (((1,), (1,)), ((), ()))
TN_DIMS = (((0,), (0,)), ((), ()))


def _cparams(*sem):
    return pltpu.CompilerParams(dimension_semantics=sem, vmem_limit_bytes=VMEM_LIMIT)


def _const_spec(shape):
    nd = len(shape)
    return pl.BlockSpec(shape, lambda *_: (0,) * nd, pipeline_mode=pl.Buffered(1))


def _dot(a, b):
    return jnp.dot(a, b, preferred_element_type=F32)


def _split_dot(a_f32, ones_bf16):
    hi = a_f32.astype(BF16)
    r1 = a_f32 - hi.astype(F32)
    mid = r1.astype(BF16)
    lo = (r1 - mid.astype(F32)).astype(BF16)
    return _dot(hi, ones_bf16) + _dot(mid, ones_bf16) + _dot(lo, ones_bf16)


def _split_dot_left(ones_bf16, a_f32):
    hi = a_f32.astype(BF16)
    r1 = a_f32 - hi.astype(F32)
    mid = r1.astype(BF16)
    lo = (r1 - mid.astype(F32)).astype(BF16)
    return _dot(ones_bf16, hi) + _dot(ones_bf16, mid) + _dot(ones_bf16, lo)


def _rms_rows(x):
    return x * lax.rsqrt(jnp.mean(x * x, axis=-1, keepdims=True) + EPS)


def _mod_kernel(c_ref, w_ref, b_ref, o_ref):
    c = c_ref[...]
    a = (c * jax.nn.sigmoid(c)).astype(BF16)
    o_ref[0] = _dot(a, w_ref[0].astype(BF16)) + b_ref[0]


def _modulation(cc, w_mod, b_mod):
    depth, d, d6 = w_mod.shape
    rows = cc.shape[0]
    tn = 1536
    return pl.pallas_call(
        _mod_kernel,
        grid=(depth, d6 // tn),
        in_specs=[
            pl.BlockSpec((rows, d), lambda l, n: (0, 0)),
            pl.BlockSpec((1, d, tn), lambda l, n: (l, 0, n)),
            pl.BlockSpec((1, 1, tn), lambda l, n: (l, 0, n)),
        ],
        out_specs=pl.BlockSpec((1, rows, tn), lambda l, n: (l, 0, n)),
        out_shape=jax.ShapeDtypeStruct((depth, rows, d6), F32),
        compiler_params=_cparams("parallel", "parallel"),
        name="modulation",
    )(cc, w_mod, b_mod.reshape(depth, 1, d6))


def _group_mean_sq(x, ones_bd, width):
    return _split_dot(x * x, ones_bd) * (1.0 / width)


def _in_kernel(x_ref, mod_ref, gmix_ref, w_ref, gb_ref, gcq_ref, gckv_ref, wuq_ref, wuk_ref, wuv_ref,
               gq_ref, gk_ref, ones_ref, cos_ref, sa_ref, sb_ref,
               hx_ref, mq_ref, mk_ref, mv_ref, mo_ref, g_ref, pc_ref, q_ref, k_ref, v_ref):
    x = x_ref[0]
    shift = mod_ref[0, 0:1, :]
    scale = mod_ref[0, 1:2, :]
    hx = (_rms_rows(x) * gmix_ref[...]) * (1.0 + scale) + shift
    hxb = hx.astype(BF16)
    hx_ref[0] = hxb
    p = _dot(hxb, w_ref[...])
    bw = BRANCH_W
    mq_ref[0] = p[:, 0:bw].T.astype(BF16)
    mk_ref[0] = (p[:, bw:2 * bw] * (ML_DH ** -0.5)).astype(BF16)
    mv_ref[0] = p[:, 2 * bw:3 * bw].T.astype(BF16)
    mo_ref[0] = p[:, 3 * bw:4 * bw].T.astype(BF16)
    g_ref[0] = (p[:, 3072:3200] + gb_ref[...]).T[0:4 * ML_HEADS]
    pc_ref[0] = p[:, 1024:2048]
    cq = p[:, 2048:2048 + Q_LORA]
    ckv = p[:, 2304:2304 + KV_LORA]
    kr4 = p[:, 2560:3072]

    cqn = (_rms_rows(cq) * gcq_ref[...]).astype(BF16)
    ckvn = (_rms_rows(ckv) * gckv_ref[...]).astype(BF16)
    q_pre = _dot(cqn, wuq_ref[...])
    k_pre = _dot(ckvn, wuk_ref[...]) + kr4
    v_t = _dot(ckvn, wuv_ref[...]).T
    tm = v_t.shape[1]
    extra = V_SLOT - V_DIM
    one_row = jnp.where(lax.broadcasted_iota(jnp.int32, (extra, tm), 0) == 0, 1.0, 0.0)
    v_ref[0] = jnp.concatenate(
        sum([[v_t[h * V_DIM:(h + 1) * V_DIM], one_row] for h in range(MLA_HEADS)], []), axis=0).astype(BF16)

    ones_bd = ones_ref[...]
    cos = jnp.concatenate([cos_ref[...]] * MLA_HEADS, axis=1)
    sa = jnp.concatenate([sa_ref[...]] * MLA_HEADS, axis=1)
    sb = jnp.concatenate([sb_ref[...]] * MLA_HEADS, axis=1)
    width = MLA_HEADS * HEAD_SLOT
    half = QK_ROPE // 4

    def norm_rope(t, gain):
        t = t * lax.rsqrt(_group_mean_sq(t, ones_bd, QK_DIM) + EPS) * gain
        return t * cos + pltpu.roll(t, width - half, 1) * sa + pltpu.roll(t, half, 1) * sb

    q = norm_rope(q_pre, gq_ref[...]) * (QK_DIM ** -0.5 * math.log2(math.e))
    q_ref[0] = q.T.astype(BF16)
    k_ref[0] = norm_rope(k_pre, gk_ref[...]).astype(BF16)


def _in_proj(xs, mod, lw, rope, n_lat_tiles):
    bsz, s, d = xs.shape
    tm = ROW_TILE
    nt = s // tm
    wcols = lw["w_big"].shape[1]
    slot_w = MLA_HEADS * HEAD_SLOT

    def row_spec(width):
        return pl.BlockSpec((1, tm, width), lambda t, b: (b, t, 0))

    def tab_spec():
        return pl.BlockSpec((tm, HEAD_SLOT), lambda t, b: (t, 0))

    def col_spec(height):
        return pl.BlockSpec((1, height, tm), lambda t, b: (b, 0, t))

    v_rows = MLA_HEADS * V_SLOT
    outs = [("row", d, BF16), ("col", BRANCH_W, BF16), ("row", BRANCH_W, BF16), ("col", BRANCH_W, BF16),
            ("col", BRANCH_W, BF16), ("col", 4 * ML_HEADS, F32), ("row", 1024, F32),
            ("col", slot_w, BF16), ("row", slot_w, BF16), ("col", v_rows, BF16)]
    out_specs = [row_spec(w) if kind == "row" else col_spec(w) for kind, w, _ in outs]
    out_shape = [jax.ShapeDtypeStruct((bsz, s, w) if kind == "row" else (bsz, w, s), dt) for kind, w, dt in outs]
    return pl.pallas_call(
        _in_kernel,
        grid=(nt, bsz),
        in_specs=[
            row_spec(d),
            pl.BlockSpec((1, 6, d), lambda t, b: (jnp.where(t >= n_lat_tiles, bsz, b), 0, 0)),
            _const_spec((1, d)),
            _const_spec((d, wcols)),
            _const_spec((1, LANES)),
            _const_spec((1, Q_LORA)),
            _const_spec((1, KV_LORA)),
            _const_spec((Q_LORA, slot_w)),
            _const_spec((KV_LORA, slot_w)),
            _const_spec((KV_LORA, BRANCH_W)),
            _const_spec((1, slot_w)),
            _const_spec((1, slot_w)),
            _const_spec((slot_w, slot_w)),
            tab_spec(), tab_spec(), tab_spec(),
        ],
        out_specs=out_specs,
        out_shape=out_shape,
        compiler_params=_cparams("parallel", "parallel"),
        name="norm_in_proj",
    )(xs, mod, lw["g_mix"], lw["w_big"], lw["gate_bias"], lw["g_cq"], lw["g_ckv"], lw["w_uq"], lw["w_uk"], lw["w_uv"],
      lw["g_qn"], lw["g_kn"], rope["ones_slot"], rope["cos"], rope["sin_a"], rope["sin_b"])


def _log_sigmoid(x):
    return jnp.minimum(x, 0.0) - jnp.log1p(jnp.exp(-jnp.abs(x)))


def _mlstm_kernel(qt_f, k_f, vt_f, g_f, qt_b, k_b, vt_b, g_b, hf_ref, hb_ref, c_sc, m_sc):
    @pl.when(pl.program_id(1) == 0)
    def _():
        c_sc[...] = jnp.zeros_like(c_sc)
        m_sc[...] = jnp.zeros_like(m_sc)

    L = ML_CHUNK
    dh = ML_DH
    row = lax.broadcasted_iota(jnp.int32, (L, L), 0)
    col = lax.broadcasted_iota(jnp.int32, (L, L), 1)
    diag = row == col
    ones_ll = jnp.ones((L, L), BF16)
    one_rows = jnp.where(lax.broadcasted_iota(jnp.int32, (LANES - dh, L), 0) == 0, 1.0, 0.0).astype(BF16)

    dirs = ((qt_f, k_f, vt_f, g_f), (qt_b, k_b, vt_b, g_b))
    units = []
    for d, (qt_ref, k_ref, vt_ref, g_ref) in enumerate(dirs):
        a = g_ref[0]
        lf = _log_sigmoid(a)
        valid = (row <= col) if d == 0 else (row >= col)
        b_rows = _split_dot(lf, valid.astype(BF16))
        last = L - 1 if d == 0 else 0
        for h in range(ML_HEADS):
            ci = d * 2 * ML_HEADS + h
            cf = ci + ML_HEADS
            st = d * ML_HEADS + h
            u = {"st": st, "valid": valid}
            bt = b_rows[cf:cf + 1, :]
            li = a[ci:ci + 1, :]
            b_end = bt[:, last:last + 1]
            m_st = m_sc[st:st + 1, 0:1]
            qt = qt_ref[0, h * dh:(h + 1) * dh, :]
            k = k_ref[0, :, h * dh:(h + 1) * dh]
            u["vt"] = vt_ref[0, h * dh:(h + 1) * dh, :]
            u["bt"] = bt
            u["inter"] = bt + m_st
            src = jnp.where(diag, bt - li, 0.0)
            hi = src.astype(BF16)
            lo = (src - hi.astype(F32)).astype(BF16)
            u["src"] = _dot(hi, ones_ll) + _dot(lo, ones_ll)
            w_log = b_end - bt + li
            u["m_new"] = jnp.maximum(b_end + m_st, jnp.max(w_log, axis=1, keepdims=True))
            u["decay"] = jnp.exp(b_end + m_st - u["m_new"])
            v_aug = jnp.concatenate([u["vt"], one_rows], axis=0)
            vw = (v_aug.astype(F32) * jnp.exp(w_log - u["m_new"])).astype(BF16)
            u["s_kq"] = _dot(k, qt)
            u["qc"] = _dot(c_sc[st].astype(BF16), qt)
            u["upd"] = _dot(vw, k)
            units.append(u)

    for u in units:
        d_log = jnp.where(u["valid"], u["bt"] - u["src"], -jnp.inf)
        u["m_t"] = jnp.maximum(u["inter"], jnp.max(d_log, axis=0, keepdims=True))
        w_st = jnp.exp(d_log - u["m_t"]) * u["s_kq"]
        u["w_sum"] = jnp.sum(w_st, axis=0, keepdims=True)
        u["pv"] = _dot(u["vt"], w_st.astype(BF16))

    outs = []
    for u in units:
        a_inter = jnp.exp(u["inter"] - u["m_t"])
        num = a_inter * u["qc"][0:dh, :] + u["pv"]
        den = a_inter * u["qc"][dh:dh + 1, :] + u["w_sum"]
        outs.append(num / jnp.maximum(jnp.abs(den), jnp.exp(-u["m_t"])))
        st = u["st"]
        c_sc[st] = u["decay"] * c_sc[st] + u["upd"]
        m_sc[st:st + 1, :] = jnp.broadcast_to(u["m_new"], (1, LANES))
    hf_ref[0] = jnp.concatenate(outs[0:ML_HEADS], axis=0)
    hb_ref[0] = jnp.concatenate(outs[ML_HEADS:], axis=0)


def _mlstm(qt, k, vt, gt, n_lat_chunks):
    bsz, s, _ = k.shape
    nc = s // ML_CHUNK

    def fwd_chunk(j):
        return (j + n_lat_chunks) % nc

    def bwd_chunk(j):
        return nc - 1 - j

    def specs(chunk):
        def col(height):
            return pl.BlockSpec((1, height, ML_CHUNK), lambda b, j: (b, 0, chunk(j)))
        return [col(BRANCH_W), pl.BlockSpec((1, ML_CHUNK, BRANCH_W), lambda b, j: (b, chunk(j), 0)),
                col(BRANCH_W), col(4 * ML_HEADS)], col(BRANCH_W)

    in_f, out_f = specs(fwd_chunk)
    in_b, out_b = specs(bwd_chunk)
    return pl.pallas_call(
        _mlstm_kernel,
        grid=(bsz, nc),
        in_specs=in_f + in_b,
        out_specs=[out_f, out_b],
        out_shape=[jax.ShapeDtypeStruct((bsz, BRANCH_W, s), F32)] * 2,
        scratch_shapes=[pltpu.VMEM((2 * ML_HEADS, LANES, ML_DH), F32),
                        pltpu.VMEM((2 * ML_HEADS, LANES), F32)],
        compiler_params=_cparams("parallel", "arbitrary"),
        name="mlstm_scan",
    )(qt, k, vt, gt, qt, k, vt, gt)


def _attn_kernel(qt_ref, k_ref, vt_ref, o_ref, m_sc, acc_sc, *, n_lat, n_ctx, tk):
    m_sc[...] = jnp.full_like(m_sc, -jnp.inf)
    acc_sc[...] = jnp.zeros_like(acc_sc)

    def scores(h, start, size):
        sl = slice(h * HEAD_SLOT, (h + 1) * HEAD_SLOT)
        return _dot(k_ref[0, pl.ds(start, size), sl], qt_ref[0, sl, :])

    def keys(start, size):
        n_sub = size // ATT_SUB
        order = [(j, h) for j in range(n_sub) for h in range(MLA_HEADS)]
        pending = [scores(h, start + j * ATT_SUB, ATT_SUB) for j, h in order[:ATT_AHEAD]]
        for i, (j, h) in enumerate(order):
            off = start + j * ATT_SUB
            s = pending.pop(0)
            if i + ATT_AHEAD < len(order):
                jn, hn = order[i + ATT_AHEAD]
                pending.append(scores(hn, start + jn * ATT_SUB, ATT_SUB))
            m_old = m_sc[h]
            m_new = jnp.maximum(m_old, jnp.max(s, axis=0, keepdims=True))
            p = jnp.exp2(s - m_new)
            vt = vt_ref[0, h * V_SLOT:(h + 1) * V_SLOT, pl.ds(off, ATT_SUB)]
            acc_sc[h] = jnp.exp2(m_old - m_new) * acc_sc[h] + _dot(vt, p.astype(BF16))
            m_sc[h] = m_new

    keys(n_lat, n_ctx)

    def chunk(c, carry):
        keys(pl.multiple_of(c * tk, tk), tk)
        return carry

    is_ctx_tile = pl.program_id(1) * ATT_TQ >= n_lat
    lax.fori_loop(0, jnp.where(is_ctx_tile, 0, n_lat // tk), chunk, 0)
    out_t = jnp.concatenate([acc_sc[h, 0:V_DIM, :] / acc_sc[h, V_DIM:V_DIM + 1, :] for h in range(MLA_HEADS)],
                            axis=0)
    o_ref[0] = out_t.T.astype(o_ref.dtype)


def _key_chunk(n_keys):
    return max(t for t in range(ATT_TQ, ATT_TK + 1, ATT_TQ) if n_keys % t == 0)


def _attention(qt, k, vt, n_lat):
    bsz, s, slot_w = k.shape
    tq = ATT_TQ
    return pl.pallas_call(
        functools.partial(_attn_kernel, n_lat=n_lat, n_ctx=s - n_lat, tk=_key_chunk(n_lat)),
        grid=(bsz, s // tq),
        in_specs=[pl.BlockSpec((1, slot_w, tq), lambda b, t: (b, 0, t)),
                  pl.BlockSpec((1, s, slot_w), lambda b, t: (b, 0, 0), pipeline_mode=pl.Buffered(1)),
                  pl.BlockSpec((1, MLA_HEADS * V_SLOT, s), lambda b, t: (b, 0, 0), pipeline_mode=pl.Buffered(1))],
        out_specs=pl.BlockSpec((1, tq, BRANCH_W), lambda b, t: (b, t, 0)),
        out_shape=jax.ShapeDtypeStruct((bsz, s, BRANCH_W), BF16),
        scratch_shapes=[pltpu.VMEM((MLA_HEADS, 1, tq), F32), pltpu.VMEM((MLA_HEADS, V_SLOT, tq), F32)],
        compiler_params=_cparams("parallel", "arbitrary"),
        name="mla_attention",
    )(qt, k, vt)


def _merge_kernel(hx_ref, hf_ref, hb_ref, o_ref, ya_ref, pc_ref, prev_ref, next_ref, x_ref, mod_ref,
                  hg_ref, pw_ref, ps_ref, cw_ref, cb_ref, wg_ref, bg_ref, wb_ref, wo_ref,
                  gffn_ref, xo_ref, h2_ref, *, n_lat_tiles, n_tiles):
    t = pl.program_id(1)
    tm = ROW_TILE
    bw = BRANCH_W
    halo = POOL_HALO
    no_prev = jnp.logical_or(t == 0, t == n_lat_tiles)
    no_next = jnp.logical_or(t == n_lat_tiles - 1, t == n_tiles - 1)

    h_t = hf_ref[0] + hb_ref[0]
    normed = []
    for hd in range(ML_HEADS):
        hh = h_t[hd * ML_DH:(hd + 1) * ML_DH]
        normed.append(hh * lax.rsqrt(jnp.mean(hh * hh, axis=0, keepdims=True) + EPS))
    y_ml = (jnp.concatenate(normed, axis=0) * hg_ref[...] * jax.nn.sigmoid(o_ref[0].astype(F32))).T

    prev = jnp.where(no_prev, 0.0, prev_ref[0])
    nxt = jnp.where(no_next, 0.0, next_ref[0])
    ext = jnp.concatenate([prev, pc_ref[0], nxt], axis=0)
    pe = ext[:, 0:bw]

    def rows(arr, off):
        return arr[halo + off:halo + off + tm, :]

    r = lax.broadcasted_iota(jnp.int32, (tm, 1), 0)
    lane = lax.broadcasted_iota(jnp.int32, (tm, bw), 1)
    centre = rows(pe, 0)
    acc = centre
    mean = jnp.zeros((tm, bw), F32)
    done = 0
    for gi, w in enumerate(POOL_WINDOWS):
        for off in list(range(-(w // 2), -done)) + list(range(max(done, 1), w // 2)):
            acc = acc + rows(pe, off)
        done = w // 2
        before = jnp.where(no_prev, jnp.minimum(r, w // 2), w // 2)
        after = jnp.where(no_next, jnp.minimum(tm - r, w // 2), w // 2)
        inv = 1.0 / (before + after).astype(F32)
        mean = jnp.where(lane >= gi * POOL_GROUP, acc * inv, mean)
    y_pool = _dot((mean - centre).astype(BF16), pw_ref[...]) * ps_ref[...]

    z = ext[:, 3 * bw:4 * bw] * ext[:, bw:2 * bw]
    conv = cb_ref[...] + rows(z, -1) * cw_ref[0:1, :] + rows(z, 0) * cw_ref[1:2, :] + rows(z, 1) * cw_ref[2:3, :]
    y_conv = rows(ext[:, 2 * bw:3 * bw], 0) * conv

    hxb = hx_ref[0]
    d = hxb.shape[1]
    gates = jax.nn.sigmoid(_dot(hxb, wg_ref[...]) + bg_ref[...])
    ys = (y_ml.astype(BF16), ya_ref[0], y_pool.astype(BF16), y_conv.astype(BF16))
    merged = None
    for i, y in enumerate(ys):
        term = gates[:, i * d:(i + 1) * d] * _dot(y, wb_ref[i])
        merged = term if merged is None else merged + term
    out = _dot(merged.astype(BF16), wo_ref[...])
    x_new = x_ref[0] + mod_ref[0, 2:3, :] * out
    xo_ref[0] = x_new
    h2_ref[0] = (_rms_rows(x_new) * gffn_ref[...]) * (1.0 + mod_ref[0, 4:5, :]) + mod_ref[0, 3:4, :]


def _merge(hx, hf, hb, o, y_mla, pc, xs, mod, lw, n_lat_tiles):
    bsz, s, d = xs.shape
    tm = ROW_TILE
    nt = s // tm
    halo = POOL_HALO
    per = tm // halo

    def row_spec(width):
        return pl.BlockSpec((1, tm, width), lambda b, t: (b, t, 0))

    def col_spec():
        return pl.BlockSpec((1, BRANCH_W, tm), lambda b, t: (b, 0, t))

    return pl.pallas_call(
        functools.partial(_merge_kernel, n_lat_tiles=n_lat_tiles, n_tiles=nt),
        grid=(bsz, nt),
        in_specs=[
            row_spec(d), col_spec(), col_spec(), col_spec(), row_spec(BRANCH_W),
            row_spec(1024),
            pl.BlockSpec((1, halo, 1024), lambda b, t: (b, jnp.maximum(t * per - 1, 0), 0)),
            pl.BlockSpec((1, halo, 1024), lambda b, t: (b, jnp.minimum((t + 1) * per, s // halo - 1), 0)),
            row_spec(d),
            pl.BlockSpec((1, 6, d), lambda b, t: (jnp.where(t >= n_lat_tiles, bsz, b), 0, 0)),
            _const_spec((BRANCH_W, tm)), _const_spec((BRANCH_W, BRANCH_W)),
            _const_spec((1, BRANCH_W)), _const_spec((3, BRANCH_W)), _const_spec((1, BRANCH_W)),
            _const_spec((d, 4 * d)), _const_spec((1, 4 * d)), _const_spec((4, BRANCH_W, d)),
            _const_spec((d, d)), _const_spec((1, d)),
        ],
        out_specs=[row_spec(d), row_spec(d)],
        out_shape=[jax.ShapeDtypeStruct((bsz, s, d), F32)] * 2,
        compiler_params=_cparams("parallel", "parallel"),
        name="mixer_merge",
    )(hx, hf, hb, o, y_mla, pc, pc, pc, xs, mod, lw["head_gain"], lw["pool_w"], lw["pool_scale"],
      lw["conv_w"], lw["conv_b"], lw["w_gate"], lw["b_gate"], lw["w_branch"], lw["w_out"], lw["g_ffn"])


def _router_kernel(h_ref, wr_ref, br_ref, pe_ref, we_ref, c16_ref):
    tm = h_ref.shape[0]
    scores = jax.nn.sigmoid(_dot(h_ref[...].astype(BF16), wr_ref[...]))
    sel = scores + br_ref[...]
    lane = lax.broadcasted_iota(jnp.int32, (tm, LANES), 1)
    member = jnp.zeros((tm, LANES), F32)
    for _ in range(TOP_K):
        mx = jnp.max(sel, axis=1, keepdims=True)
        ix = jnp.min(jnp.where(sel == mx, lane, LANES), axis=1, keepdims=True)
        hit = lane == ix
        member = jnp.where(hit, 1.0, member)
        sel = jnp.where(hit, -jnp.inf, sel)
    picked = member * scores
    weights = picked / jnp.sum(picked, axis=1, keepdims=True) * ROUTE_SCALE
    counts = jnp.sum(member, axis=0, keepdims=True)
    c16 = jnp.floor((counts + (RUN_ROWS - 1)) * (1.0 / RUN_ROWS))
    e_row = lax.broadcasted_iota(jnp.int32, (LANES, LANES), 0)
    e_col = lax.broadcasted_iota(jnp.int32, (LANES, LANES), 1)
    lower = (e_row < e_col).astype(BF16)
    o16 = _dot(jnp.broadcast_to(c16, (SUBLANES, LANES)).astype(BF16), lower)[0:1]
    row = lax.broadcasted_iota(jnp.int32, (tm, tm), 0)
    col = lax.broadcasted_iota(jnp.int32, (tm, tm), 1)
    rank = _dot((col < row).astype(BF16), member.astype(BF16))
    pos = jnp.where(member > 0.0, RUN_ROWS * o16 + rank, -1.0)
    pe_ref[0] = pos.T[0:N_EXPERTS]
    we_ref[0] = weights.T[0:N_EXPERTS]
    c16_ref[0] = c16.astype(jnp.int32)


def _router(h2, lw):
    n, d = h2.shape
    tm = MOE_TILE
    nt = n // tm

    def t_spec():
        return pl.BlockSpec((1, N_EXPERTS, tm), lambda i: (i, 0, 0))

    return pl.pallas_call(
        _router_kernel,
        grid=(nt,),
        in_specs=[pl.BlockSpec((tm, d), lambda i: (i, 0)), _const_spec((d, LANES)), _const_spec((1, LANES))],
        out_specs=[t_spec(), t_spec(), pl.BlockSpec((1, 1, LANES), lambda i: (i, 0, 0))],
        out_shape=[jax.ShapeDtypeStruct((nt, N_EXPERTS, tm), F32), jax.ShapeDtypeStruct((nt, N_EXPERTS, tm), F32),
                   jax.ShapeDtypeStruct((nt, 1, LANES), jnp.int32)],
        compiler_params=_cparams("parallel"),
        name="moe_router",
    )(h2, lw["w_router"], lw["b_router"])


def _tile_row_bound(tm):
    worst = tm * TOP_K + N_EXPERTS * (RUN_ROWS - 1)
    return -(-worst // MOE_CHUNK) * MOE_CHUNK


def _selection_rows(g, grp_e_ref, pe_ref, tm):
    prow = pe_ref[0, pl.ds(grp_e_ref[0, 0, g], 1), :]
    rows = (lax.broadcasted_iota(jnp.int32, (RUN_ROWS, tm), 0) + g * RUN_ROWS).astype(F32)
    return prow, rows


def _dispatch_kernel(tail_ref, n_act_ref, grp_e_ref, gdst_ref, h_ref, pe_ref, xs_ref,
                     sel_sc, xb_sc, xp_sc, zero_sc, sem, *, n_groups):
    i = pl.program_id(0)
    tm = h_ref.shape[0]
    n_chunks = sel_sc.shape[0] // MOE_CHUNK
    per_chunk = MOE_CHUNK // RUN_ROWS

    @pl.when(i == pl.num_programs(0) - 1)
    def _():
        zero_sc[...] = jnp.zeros_like(zero_sc)

        def fill_copy(q):
            return pltpu.make_async_copy(
                zero_sc.at[pl.ds(0, RUN_ROWS)],
                xs_ref.at[pl.ds(pl.multiple_of(tail_ref[q] * RUN_ROWS, RUN_ROWS), RUN_ROWS)], sem)

        def block_copy(b):
            return pltpu.make_async_copy(
                zero_sc, xs_ref.at[pl.ds(pl.multiple_of(b * EXPERT_ROWS, EXPERT_ROWS), EXPERT_ROWS)], sem)

        def block_start(b, carry):
            block_copy(b).start()
            return carry

        def block_wait(b, carry):
            block_copy(b).wait()
            return carry

        n_blocks = xs_ref.shape[0] // EXPERT_ROWS
        lax.fori_loop(n_act_ref[0], n_blocks, block_start, 0)
        lax.fori_loop(n_act_ref[0], n_blocks, block_wait, 0)

        def fill_start(q, carry):
            @pl.when(tail_ref[q] >= 0)
            def _():
                fill_copy(q).start()
            return carry

        def fill_wait(q, carry):
            @pl.when(tail_ref[q] >= 0)
            def _():
                fill_copy(q).wait()
            return carry

        lax.fori_loop(0, tail_ref.shape[0], fill_start, 0)
        lax.fori_loop(0, tail_ref.shape[0], fill_wait, 0)

    def build(g, carry):
        prow, rows = _selection_rows(g, grp_e_ref, pe_ref, tm)
        sel_sc[pl.ds(pl.multiple_of(g * RUN_ROWS, RUN_ROWS), RUN_ROWS), :] = jnp.where(
            prow == rows, 1.0, 0.0).astype(BF16)
        return carry

    lax.fori_loop(0, n_groups, build, 0, unroll=4)
    xb_sc[...] = h_ref[...].astype(BF16)

    def run_copy(g):
        return pltpu.make_async_copy(
            xp_sc.at[pl.ds(pl.multiple_of(g * RUN_ROWS, RUN_ROWS), RUN_ROWS)],
            xs_ref.at[pl.ds(pl.multiple_of(gdst_ref[0, 0, g] * RUN_ROWS, RUN_ROWS), RUN_ROWS)], sem)

    def permute(ch):
        r0 = pl.multiple_of(ch * MOE_CHUNK, MOE_CHUNK)
        xp_sc[pl.ds(r0, MOE_CHUNK), :] = _dot(sel_sc[pl.ds(r0, MOE_CHUNK), :], xb_sc[...]).astype(BF16)

    def start_chunk(ch):
        for j in range(per_chunk):
            run_copy(ch * per_chunk + j).start()

    permute(0)

    def step(ch, carry):
        start_chunk(ch - 1)
        permute(ch)
        return carry

    lax.fori_loop(1, n_chunks, step, 0)
    start_chunk(n_chunks - 1)
    pltpu.make_async_copy(xp_sc, xs_ref.at[pl.ds(0, xp_sc.shape[0])], sem).wait()


def _dispatch(h2, pe, plan, n_rows):
    n, d = h2.shape
    tm = MOE_TILE
    nt = n // tm
    rb = _tile_row_bound(tm)
    n_groups = rb // RUN_ROWS

    def smem_spec():
        return pl.BlockSpec((1, 1, n_groups), lambda i, *_: (i, 0, 0), memory_space=pltpu.SMEM)

    return pl.pallas_call(
        functools.partial(_dispatch_kernel, n_groups=n_groups),
        grid_spec=pltpu.PrefetchScalarGridSpec(
            num_scalar_prefetch=2,
            grid=(nt,),
            in_specs=[smem_spec(), smem_spec(),
                      pl.BlockSpec((tm, d), lambda i, *_: (i, 0)),
                      pl.BlockSpec((1, N_EXPERTS, tm), lambda i, *_: (i, 0, 0))],
            out_specs=pl.BlockSpec(memory_space=pl.ANY),
            scratch_shapes=[pltpu.VMEM((rb, tm), BF16), pltpu.VMEM((tm, d), BF16), pltpu.VMEM((rb, d), BF16),
                            pltpu.VMEM((EXPERT_ROWS, d), BF16), pltpu.SemaphoreType.DMA(())],
        ),
        out_shape=jax.ShapeDtypeStruct((n_rows, d), BF16),
        compiler_params=_cparams("arbitrary"),
        name="moe_dispatch",
    )(plan["tail"], plan["n_act"].reshape(1), plan["grp_e"], plan["gdst"], h2, pe)


def _expert_kernel(blk_e_ref, n_act_ref, x_ref, w1_ref, w3_ref, w2_ref, y_ref):
    del blk_e_ref

    @pl.when(pl.program_id(0) < n_act_ref[0])
    def _():
        x = x_ref[...]
        a = _dot(x, w1_ref[0].astype(BF16))
        act = (a * jax.nn.sigmoid(a)) * _dot(x, w3_ref[0].astype(BF16))
        y_ref[...] = _dot(act.astype(BF16), w2_ref[0].astype(BF16)).astype(BF16)

    @pl.when(pl.program_id(0) >= n_act_ref[0])
    def _():
        y_ref[...] = jnp.zeros_like(y_ref)


def _experts(xs_sorted, blk_e, n_act, lw):
    n_rows, d = xs_sorted.shape
    bm = EXPERT_ROWS

    def row_map(i, blk_e_ref, n_act_ref):
        return (jnp.minimum(i, n_act_ref[0] - 1), 0)

    def w_map(i, blk_e_ref, n_act_ref):
        return (blk_e_ref[i], 0, 0)

    return pl.pallas_call(
        _expert_kernel,
        grid_spec=pltpu.PrefetchScalarGridSpec(
            num_scalar_prefetch=2,
            grid=(n_rows // bm,),
            in_specs=[pl.BlockSpec((bm, d), row_map),
                      pl.BlockSpec((1, d, EXPERT_FF), w_map),
                      pl.BlockSpec((1, d, EXPERT_FF), w_map),
                      pl.BlockSpec((1, EXPERT_FF, d), w_map)],
            out_specs=pl.BlockSpec((bm, d), lambda i, *_: (i, 0)),
        ),
        out_shape=jax.ShapeDtypeStruct((n_rows, d), BF16),
        compiler_params=_cparams("arbitrary"),
        name="moe_experts",
    )(blk_e, n_act, xs_sorted, lw["w1"], lw["w3"], lw["w2"])


def _combine_kernel(grp_e_ref, gsrc_ref, h_ref, x_ref, pe_ref, we_ref, mod_ref, modc_ref, ws13_ref,
                    ws2_ref, ys_ref, xo_ref, sel_sc, yp_sc, acc_sc, sem, *, n_groups, tiles_per_sample, n_ctx):
    i = pl.program_id(0)
    tm = h_ref.shape[0]

    def build(g, carry):
        pltpu.make_async_copy(
            ys_ref.at[pl.ds(pl.multiple_of(gsrc_ref[0, 0, g] * RUN_ROWS, RUN_ROWS), RUN_ROWS)],
            yp_sc.at[pl.ds(pl.multiple_of(g * RUN_ROWS, RUN_ROWS), RUN_ROWS)], sem).start()
        prow, rows = _selection_rows(g, grp_e_ref, pe_ref, tm)
        wrow = we_ref[0, pl.ds(grp_e_ref[0, 0, g], 1), :]
        sel_sc[pl.ds(pl.multiple_of(g * RUN_ROWS, RUN_ROWS), RUN_ROWS), :] = jnp.where(
            prow == rows, wrow, 0.0).astype(BF16)
        return carry

    lax.fori_loop(0, n_groups, build, 0, unroll=4)

    up = _dot(h_ref[...].astype(BF16), ws13_ref[...])
    a = up[:, 0:EXPERT_FF]
    act = (a * jax.nn.sigmoid(a)) * up[:, EXPERT_FF:2 * EXPERT_FF]
    acc_sc[...] = _dot(act.astype(BF16), ws2_ref[...])
    pltpu.make_async_copy(ys_ref.at[pl.ds(0, yp_sc.shape[0])], yp_sc, sem).wait()

    def gather_sum(ch, carry):
        r0 = pl.multiple_of(ch * MOE_CHUNK, MOE_CHUNK)
        acc_sc[...] += lax.dot_general(sel_sc[pl.ds(r0, MOE_CHUNK), :], yp_sc[pl.ds(r0, MOE_CHUNK), :], TN_DIMS,
                                       preferred_element_type=F32)
        return carry

    lax.fori_loop(0, sel_sc.shape[0] // MOE_CHUNK, gather_sum, 0)
    r = lax.broadcasted_iota(jnp.int32, (tm, 1), 0)
    is_ctx = jnp.logical_and(i % tiles_per_sample == tiles_per_sample - 1, r >= tm - n_ctx)
    gate = jnp.where(is_ctx, modc_ref[0, 5:6, :], mod_ref[0, 5:6, :])
    xo_ref[...] = x_ref[...] + gate * acc_sc[...]


def _combine(h2, xs, pe, we, plan, mod, ys_sorted, lw, tiles_per_sample, n_ctx, bsz):
    n, d = h2.shape
    tm = MOE_TILE
    nt = n // tm
    rb = _tile_row_bound(tm)
    n_groups = rb // RUN_ROWS

    def smem_spec():
        return pl.BlockSpec((1, 1, n_groups), lambda i, *_: (i, 0, 0), memory_space=pltpu.SMEM)

    def row_spec():
        return pl.BlockSpec((tm, d), lambda i, *_: (i, 0))

    def t_spec():
        return pl.BlockSpec((1, N_EXPERTS, tm), lambda i, *_: (i, 0, 0))

    return pl.pallas_call(
        functools.partial(_combine_kernel, n_groups=n_groups, tiles_per_sample=tiles_per_sample, n_ctx=n_ctx),
        grid_spec=pltpu.PrefetchScalarGridSpec(
            num_scalar_prefetch=0,
            grid=(nt,),
            in_specs=[smem_spec(), smem_spec(), row_spec(), row_spec(), t_spec(), t_spec(),
                      pl.BlockSpec((1, 6, d), lambda i, *_: (i // tiles_per_sample, 0, 0)),
                      pl.BlockSpec((1, 6, d), lambda i, *_: (bsz, 0, 0)),
                      pl.BlockSpec((d, 2 * EXPERT_FF), lambda i, *_: (0, 0), pipeline_mode=pl.Buffered(1)),
                      pl.BlockSpec((EXPERT_FF, d), lambda i, *_: (0, 0), pipeline_mode=pl.Buffered(1)),
                      pl.BlockSpec(memory_space=pl.ANY)],
            out_specs=row_spec(),
            scratch_shapes=[pltpu.VMEM((rb, tm), BF16), pltpu.VMEM((rb, d), BF16), pltpu.VMEM((tm, d), F32),
                            pltpu.SemaphoreType.DMA(())],
        ),
        out_shape=jax.ShapeDtypeStruct((n, d), F32),
        compiler_params=_cparams("arbitrary"),
        name="moe_combine",
    )(plan["grp_e"], plan["gsrc"], h2, xs, pe, we, mod, mod, lw["ws13"], lw["ws2"], ys_sorted)


def _moe_plan(c16, n_groups, spare16):
    nt = c16.shape[0]
    per_blk = EXPERT_ROWS // RUN_ROWS
    o16 = jnp.cumsum(c16, axis=1) - c16
    before16 = jnp.cumsum(c16, axis=0) - c16
    gtot16 = jnp.sum(c16, axis=0)
    gblk = (gtot16 + per_blk - 1) // per_blk
    gend_blk = jnp.cumsum(gblk)
    gstart16 = per_blk * (gend_blk - gblk)
    g = jnp.arange(n_groups, dtype=jnp.int32)
    grp_e = jnp.minimum(jnp.sum((o16 + c16)[:, None, :] <= g[None, :, None], axis=2), N_EXPERTS - 1)
    run0 = gstart16[None, :] + before16 - o16
    experts = jnp.arange(N_EXPERTS, dtype=jnp.int32)
    gdst = jnp.sum(jnp.where(grp_e[:, :, None] == experts[None, None, :], run0[:, None, :], 0), axis=2) + g[None, :]
    q = jnp.arange(per_blk, dtype=jnp.int32)
    tail = jnp.where(q[None, :] < (per_blk * gblk - gtot16)[:, None],
                     (gstart16 + gtot16)[:, None] + q[None, :], -1)
    used = g[None, :] < jnp.sum(c16, axis=1)[:, None]
    return {
        "grp_e": grp_e.astype(jnp.int32).reshape(nt, 1, n_groups),
        "gdst": jnp.where(used, gdst, spare16 + g[None, :]).astype(jnp.int32).reshape(nt, 1, n_groups),
        "gsrc": jnp.where(used, gdst, 0).astype(jnp.int32).reshape(nt, 1, n_groups),
        "tail": tail.astype(jnp.int32).reshape(-1),
        "gend_blk": gend_blk, "n_act": gend_blk[-1].astype(jnp.int32),
    }


def _moe(h2, xs, mod, lw, tiles_per_sample, n_ctx, bsz):
    n, d = h2.shape
    nt = n // MOE_TILE
    n_groups = _tile_row_bound(MOE_TILE) // RUN_ROWS
    pe, we, c16 = _router(h2, lw)
    worst_rows = n * TOP_K + nt * N_EXPERTS * (RUN_ROWS - 1) + N_EXPERTS * (EXPERT_ROWS - 1)
    run_blocks = -(-worst_rows // EXPERT_ROWS)
    n_blocks = run_blocks + -(-n_groups * RUN_ROWS // EXPERT_ROWS)
    plan = _moe_plan(c16[:, 0, :N_EXPERTS], n_groups, run_blocks * (EXPERT_ROWS // RUN_ROWS))
    blk = jnp.minimum(jnp.arange(n_blocks, dtype=jnp.int32), plan["n_act"] - 1)
    blk_e = jnp.minimum(jnp.sum(plan["gend_blk"][None, :] <= blk[:, None], axis=1), N_EXPERTS - 1).astype(jnp.int32)
    xs_sorted = _dispatch(h2, pe, plan, n_blocks * EXPERT_ROWS)
    ys_sorted = _experts(xs_sorted, blk_e, plan["n_act"].reshape(1), lw)
    return _combine(h2, xs, pe, we, plan, mod, ys_sorted, lw, tiles_per_sample, n_ctx, bsz)


def _slots(w, head_w, real_w):
    rows = w.shape[0]
    w = w.reshape(rows, -1, head_w)[:, :, :real_w]
    return jnp.pad(w, ((0, 0), (0, 0), (0, HEAD_SLOT - real_w))).reshape(rows, -1)


def _block_diag(blocks):
    n, r, c = blocks.shape
    out = jnp.zeros((n * r, n * c), blocks.dtype)
    for i in range(n):
        out = out.at[i * r:(i + 1) * r, i * c:(i + 1) * c].set(blocks[i])
    return out


def _layer_weights(l, p):
    d = p["w_in"].shape[1]
    w_in = p["w_in"][l]
    ml = 4 * BRANCH_W
    o_mla = ml + 4 * ML_HEADS
    o_pool = o_mla + Q_LORA + KV_LORA + QK_ROPE
    kr = w_in[:, o_mla + Q_LORA + KV_LORA:o_pool]
    kr_slot = jnp.pad(kr, ((0, 0), (QK_NOPE, HEAD_SLOT - QK_DIM)))
    w_big = jnp.concatenate([
        w_in[:, :ml],
        w_in[:, o_pool:],
        w_in[:, o_mla:o_mla + Q_LORA + KV_LORA], jnp.zeros((d, LANES), F32),
        jnp.tile(kr_slot, (1, MLA_HEADS)),
        jnp.pad(w_in[:, ml:o_mla], ((0, 0), (0, LANES - 4 * ML_HEADS))),
    ], axis=1).astype(BF16)
    w_ukv = p["mla_w_ukv"][l].reshape(KV_LORA, MLA_HEADS, QK_NOPE + V_DIM)

    def gain_slots(g):
        return jnp.tile(jnp.pad(g, (0, HEAD_SLOT - QK_DIM)), MLA_HEADS)[None, :]

    return {
        "g_mix": p["g_mix"][l][None, :], "g_ffn": p["g_ffn"][l][None, :],
        "w_big": w_big,
        "g_cq": p["mla_g_cq"][l][None, :], "g_ckv": p["mla_g_ckv"][l][None, :],
        "w_uq": _slots(p["mla_w_uq"][l], QK_DIM, QK_DIM).astype(BF16),
        "w_uk": _slots(w_ukv[:, :, :QK_NOPE].reshape(KV_LORA, -1), QK_NOPE, QK_NOPE).astype(BF16),
        "w_uv": w_ukv[:, :, QK_NOPE:].reshape(KV_LORA, BRANCH_W).astype(BF16),
        "g_qn": gain_slots(p["mla_g_qn"][l]), "g_kn": gain_slots(p["mla_g_kn"][l]),
        "gate_bias": jnp.pad(p["ml_gate_bias"][l], (0, LANES - 4 * ML_HEADS))[None, :],
        "head_gain": jnp.broadcast_to(p["ml_head_gain"][l][:, None], (BRANCH_W, ROW_TILE)),
        "pool_w": _block_diag(p["pool_w"][l]).astype(BF16),
        "pool_scale": p["pool_scale"][l][None, :],
        "conv_w": p["conv_w"][l], "conv_b": p["conv_b"][l][None, :],
        "w_gate": jnp.concatenate(list(p["w_gate"][l]), axis=1).astype(BF16),
        "b_gate": p["b_gate"][l].reshape(1, -1),
        "w_branch": p["w_branch"][l].astype(BF16),
        "w_out": p["w_out"][l].astype(BF16),
        "w_router": jnp.pad(p["moe_w_router"][l], ((0, 0), (0, LANES - N_EXPERTS))).astype(BF16),
        "b_router": jnp.pad(p["moe_b_router"][l], (0, LANES - N_EXPERTS), constant_values=-1e30)[None, :],
        "w1": p["moe_w1"][l], "w3": p["moe_w3"][l], "w2": p["moe_w2"][l],
        "ws13": jnp.concatenate([p["moe_ws1"][l], p["moe_ws3"][l]], axis=1).astype(BF16),
        "ws2": p["moe_ws2"][l].astype(BF16),
    }


def _rope_tables(n_lat, n_ctx):
    t = jnp.arange(n_lat)
    n_freq = QK_ROPE // 4
    inv = ROPE_THETA ** (-jnp.arange(n_freq, dtype=F32) / n_freq)
    ang_r = (t // GRID_W).astype(F32)[:, None] * inv
    ang_c = (t % GRID_W).astype(F32)[:, None] * inv
    cos4 = jnp.concatenate([jnp.cos(ang_r)] * 2 + [jnp.cos(ang_c)] * 2, axis=1)
    zero = jnp.zeros_like(ang_r)
    sin_a = jnp.concatenate([-jnp.sin(ang_r), zero, -jnp.sin(ang_c), zero], axis=1)
    sin_b = jnp.concatenate([zero, jnp.sin(ang_r), zero, jnp.sin(ang_c)], axis=1)

    def slot(a, fill):
        a = jnp.pad(a, ((0, 0), (QK_NOPE, 0)), constant_values=fill)
        a = jnp.pad(a, ((0, 0), (0, HEAD_SLOT - QK_DIM)), constant_values=fill)
        return jnp.pad(a, ((0, n_ctx), (0, 0)), constant_values=fill)

    return {"cos": slot(cos4, 1.0), "sin_a": slot(sin_a, 0.0), "sin_b": slot(sin_b, 0.0),
            "ones_slot": _block_diag(jnp.ones((MLA_HEADS, HEAD_SLOT, HEAD_SLOT), BF16))}


def kernel(x, c, ctx, c_ctx, w_mod, b_mod, g_mix, g_ffn, w_in, ml_gate_bias, ml_head_gain, mla_g_cq, mla_g_ckv,
           mla_w_uq, mla_w_ukv, mla_g_qn, mla_g_kn, pool_w, pool_scale, conv_w, conv_b, w_gate, b_gate, w_branch,
           w_out, moe_w_router, moe_b_router, moe_w1, moe_w3, moe_w2, moe_ws1, moe_ws3, moe_ws2):
    p = dict(g_mix=g_mix, g_ffn=g_ffn, w_in=w_in, ml_gate_bias=ml_gate_bias, ml_head_gain=ml_head_gain,
             mla_g_cq=mla_g_cq, mla_g_ckv=mla_g_ckv, mla_w_uq=mla_w_uq, mla_w_ukv=mla_w_ukv, mla_g_qn=mla_g_qn,
             mla_g_kn=mla_g_kn, pool_w=pool_w, pool_scale=pool_scale, conv_w=conv_w, conv_b=conv_b,
             w_gate=w_gate, b_gate=b_gate, w_branch=w_branch, w_out=w_out, moe_w_router=moe_w_router,
             moe_b_router=moe_b_router, moe_w1=moe_w1, moe_w3=moe_w3, moe_w2=moe_w2, moe_ws1=moe_ws1,
             moe_ws3=moe_ws3, moe_ws2=moe_ws2)
    bsz, n_lat, d = x.shape
    n_ctx = ctx.shape[1]
    depth = w_mod.shape[0]
    s = n_lat + n_ctx
    assert n_ctx == ROW_TILE == ATT_TQ and n_lat % ROW_TILE == 0 and n_lat % GRID_W == 0
    assert s % MOE_TILE == 0 and n_ctx <= MOE_TILE
    n_lat_tiles = n_lat // ROW_TILE

    mod_rows = -(-(bsz + 1) // SUBLANES) * SUBLANES
    cc = jnp.concatenate([c, c_ctx[None, :], jnp.zeros((mod_rows - bsz - 1, d), F32)], axis=0)
    mods = _modulation(cc, w_mod, b_mod).reshape(depth, mod_rows, 6, d)
    rope = _rope_tables(n_lat, n_ctx)
    xs = jnp.concatenate([x, ctx], axis=1)

    for l in range(depth):
        lw = _layer_weights(l, p)
        mod = mods[l]
        hx, mq, mk, mv, o, g, pc, q, k, v = _in_proj(xs, mod, lw, rope, n_lat_tiles)
        hf, hb = _mlstm(mq, mk, mv, g, n_lat // ML_CHUNK)
        y_mla = _attention(q, k, v, n_lat)
        xs, h2 = _merge(hx, hf, hb, o, y_mla, pc, xs, mod, lw, n_lat_tiles)
        xs = _moe(h2.reshape(bsz * s, d), xs.reshape(bsz * s, d), mod, lw, s // MOE_TILE, n_ctx,
                  bsz).reshape(bsz, s, d)
    return xs[:, :n_lat]
```

```python
import functools
import math

import jax
import jax.numpy as jnp
from jax import lax
from jax.experimental import pallas as pl
from jax.experimental.pallas import tpu as pltpu

GRID_W = 64
BRANCH_W = 256
EPS = 1e-6
ML_HEADS = 4
ML_DH = BRANCH_W // ML_HEADS
ML_CHUNK = 128
MLA_HEADS = 4
Q_LORA = 256
KV_LORA = 128
QK_NOPE = 64
QK_ROPE = 32
QK_DIM = QK_NOPE + QK_ROPE
V_DIM = BRANCH_W // MLA_HEADS
V_SLOT = V_DIM + 16
ROPE_THETA = 10000.0
POOL_WINDOWS = (2, 4, 8, 16)
POOL_GROUP = BRANCH_W // len(POOL_WINDOWS)
POOL_HALO = max(POOL_WINDOWS) // 2
N_EXPERTS = 64
TOP_K = 6
EXPERT_FF = 256
ROUTE_SCALE = 2.5

LANES = 128
SUBLANES = 8
HEAD_SLOT = LANES
ROW_TILE = 256
ATT_TQ = 256
ATT_TK = 2048
ATT_SUB = 128
ATT_AHEAD = 8
EXPERT_ROWS = 1024
MOE_TILE = 768
RUN_ROWS = 16
MOE_CHUNK = 512
VMEM_LIMIT = 56 * 1024 * 1024

F32 = jnp.float32
BF16 = jnp.bfloat16
NT_DIMS = (((1,), (1,)), ((), ()))
TN_DIMS = (((0,), (0,)), ((), ()))


def _cparams(*sem):
    return pltpu.CompilerParams(dimension_semantics=sem, vmem_limit_bytes=VMEM_LIMIT)


def _const_spec(shape):
    nd = len(shape)
    return pl.BlockSpec(shape, lambda *_: (0,) * nd, pipeline_mode=pl.Buffered(1))


def _dot(a, b):
    return jnp.dot(a, b, preferred_element_type=F32)


def _split_dot(a_f32, ones_bf16):
    hi = a_f32.astype(BF16)
    r1 = a_f32 - hi.astype(F32)
    mid = r1.astype(BF16)
    lo = (r1 - mid.astype(F32)).astype(BF16)
    return _dot(hi, ones_bf16) + _dot(mid, ones_bf16) + _dot(lo, ones_bf16)


def _split_dot_left(ones_bf16, a_f32):
    hi = a_f32.astype(BF16)
    r1 = a_f32 - hi.astype(F32)
    mid = r1.astype(BF16)
    lo = (r1 - mid.astype(F32)).astype(BF16)
    return _dot(ones_bf16, hi) + _dot(ones_bf16, mid) + _dot(ones_bf16, lo)


def _rms_rows(x):
    return x * lax.rsqrt(jnp.mean(x * x, axis=-1, keepdims=True) + EPS)


def _mod_kernel(c_ref, w_ref, b_ref, o_ref):
    c = c_ref[...]
    a = (c * jax.nn.sigmoid(c)).astype(BF16)
    o_ref[0] = _dot(a, w_ref[0].astype(BF16)) + b_ref[0]


def _modulation(cc, w_mod, b_mod):
    depth, d, d6 = w_mod.shape
    rows = cc.shape[0]
    tn = 1536
    return pl.pallas_call(
        _mod_kernel,
        grid=(depth, d6 // tn),
        in_specs=[
            pl.BlockSpec((rows, d), lambda l, n: (0, 0)),
            pl.BlockSpec((1, d, tn), lambda l, n: (l, 0, n)),
            pl.BlockSpec((1, 1, tn), lambda l, n: (l, 0, n)),
        ],
        out_specs=pl.BlockSpec((1, rows, tn), lambda l, n: (l, 0, n)),
        out_shape=jax.ShapeDtypeStruct((depth, rows, d6), F32),
        compiler_params=_cparams("parallel", "parallel"),
        name="modulation",
    )(cc, w_mod, b_mod.reshape(depth, 1, d6))


def _group_mean_sq(x, ones_bd, width):
    return _split_dot(x * x, ones_bd) * (1.0 / width)


def _in_kernel(x_ref, mod_ref, gmix_ref, w_ref, gb_ref, gcq_ref, gckv_ref, wuq_ref, wuk_ref, wuv_ref,
               gq_ref, gk_ref, ones_ref, cos_ref, sa_ref, sb_ref,
               hx_ref, mq_ref, mk_ref, mv_ref, mo_ref, g_ref, pc_ref, q_ref, k_ref, v_ref):
    x = x_ref[0]
    shift = mod_ref[0, 0:1, :]
    scale = mod_ref[0, 1:2, :]
    hx = (_rms_rows(x) * gmix_ref[...]) * (1.0 + scale) + shift
    hxb = hx.astype(BF16)
    hx_ref[0] = hxb
    p = _dot(hxb, w_ref[...])
    bw = BRANCH_W
    mq_ref[0] = p[:, 0:bw].T.astype(BF16)
    mk_ref[0] = (p[:, bw:2 * bw] * (ML_DH ** -0.5)).astype(BF16)
    mv_ref[0] = p[:, 2 * bw:3 * bw].T.astype(BF16)
    mo_ref[0] = p[:, 3 * bw:4 * bw].T.astype(BF16)
    g_ref[0] = (p[:, 3072:3200] + gb_ref[...]).T[0:4 * ML_HEADS]
    pc_ref[0] = p[:, 1024:2048]
    cq = p[:, 2048:2048 + Q_LORA]
    ckv = p[:, 2304:2304 + KV_LORA]
    kr4 = p[:, 2560:3072]

    cqn = (_rms_rows(cq) * gcq_ref[...]).astype(BF16)
    ckvn = (_rms_rows(ckv) * gckv_ref[...]).astype(BF16)
    q_pre = _dot(cqn, wuq_ref[...])
    k_pre = _dot(ckvn, wuk_ref[...]) + kr4
    v_t = _dot(ckvn, wuv_ref[...]).T
    tm = v_t.shape[1]
    extra = V_SLOT - V_DIM
    one_row = jnp.where(lax.broadcasted_iota(jnp.int32, (extra, tm), 0) == 0, 1.0, 0.0)
    v_ref[0] = jnp.concatenate(
        sum([[v_t[h * V_DIM:(h + 1) * V_DIM], one_row] for h in range(MLA_HEADS)], []), axis=0).astype(BF16)

    ones_bd = ones_ref[...]
    cos = jnp.concatenate([cos_ref[...]] * MLA_HEADS, axis=1)
    sa = jnp.concatenate([sa_ref[...]] * MLA_HEADS, axis=1)
    sb = jnp.concatenate([sb_ref[...]] * MLA_HEADS, axis=1)
    width = MLA_HEADS * HEAD_SLOT
    half = QK_ROPE // 4

    def norm_rope(t, gain):
        t = t * lax.rsqrt(_group_mean_sq(t, ones_bd, QK_DIM) + EPS) * gain
        return t * cos + pltpu.roll(t, width - half, 1) * sa + pltpu.roll(t, half, 1) * sb

    q = norm_rope(q_pre, gq_ref[...]) * (QK_DIM ** -0.5 * math.log2(math.e))
    q_ref[0] = q.T.astype(BF16)
    k_ref[0] = norm_rope(k_pre, gk_ref[...]).astype(BF16)


def _in_proj(xs, mod, lw, rope, n_lat_tiles):
    bsz, s, d = xs.shape
    tm = ROW_TILE
    nt = s // tm
    wcols = lw["w_big"].shape[1]
    slot_w = MLA_HEADS * HEAD_SLOT

    def row_spec(width):
        return pl.BlockSpec((1, tm, width), lambda t, b: (b, t, 0))

    def tab_spec():
        return pl.BlockSpec((tm, HEAD_SLOT), lambda t, b: (t, 0))

    def col_spec(height):
        return pl.BlockSpec((1, height, tm), lambda t, b: (b, 0, t))

    v_rows = MLA_HEADS * V_SLOT
    outs = [("row", d, BF16), ("col", BRANCH_W, BF16), ("row", BRANCH_W, BF16), ("col", BRANCH_W, BF16),
            ("col", BRANCH_W, BF16), ("col", 4 * ML_HEADS, F32), ("row", 1024, F32),
            ("col", slot_w, BF16), ("row", slot_w, BF16), ("col", v_rows, BF16)]
    out_specs = [row_spec(w) if kind == "row" else col_spec(w) for kind, w, _ in outs]
    out_shape = [jax.ShapeDtypeStruct((bsz, s, w) if kind == "row" else (bsz, w, s), dt) for kind, w, dt in outs]
    return pl.pallas_call(
        _in_kernel,
        grid=(nt, bsz),
        in_specs=[
            row_spec(d),
            pl.BlockSpec((1, 6, d), lambda t, b: (jnp.where(t >= n_lat_tiles, bsz, b), 0, 0)),
            _const_spec((1, d)),
            _const_spec((d, wcols)),
            _const_spec((1, LANES)),
            _const_spec((1, Q_LORA)),
            _const_spec((1, KV_LORA)),
            _const_spec((Q_LORA, slot_w)),
            _const_spec((KV_LORA, slot_w)),
            _const_spec((KV_LORA, BRANCH_W)),
            _const_spec((1, slot_w)),
            _const_spec((1, slot_w)),
            _const_spec((slot_w, slot_w)),
            tab_spec(), tab_spec(), tab_spec(),
        ],
        out_specs=out_specs,
        out_shape=out_shape,
        compiler_params=_cparams("parallel", "parallel"),
        name="norm_in_proj",
    )(xs, mod, lw["g_mix"], lw["w_big"], lw["gate_bias"], lw["g_cq"], lw["g_ckv"], lw["w_uq"], lw["w_uk"], lw["w_uv"],
      lw["g_qn"], lw["g_kn"], rope["ones_slot"], rope["cos"], rope["sin_a"], rope["sin_b"])


def _log_sigmoid(x):
    return jnp.minimum(x, 0.0) - jnp.log1p(jnp.exp(-jnp.abs(x)))


def _mlstm_kernel(qt_f, k_f, vt_f, g_f, qt_b, k_b, vt_b, g_b, hf_ref, hb_ref, c_sc, m_sc):
    @pl.when(pl.program_id(1) == 0)
    def _():
        c_sc[...] = jnp.zeros_like(c_sc)
        m_sc[...] = jnp.zeros_like(m_sc)

    L = ML_CHUNK
    dh = ML_DH
    row = lax.broadcasted_iota(jnp.int32, (L, L), 0)
    col = lax.broadcasted_iota(jnp.int32, (L, L), 1)
    diag = row == col
    ones_ll = jnp.ones((L, L), BF16)
    one_rows = jnp.where(lax.broadcasted_iota(jnp.int32, (LANES - dh, L), 0) == 0, 1.0, 0.0).astype(BF16)

    dirs = ((qt_f, k_f, vt_f, g_f), (qt_b, k_b, vt_b, g_b))
    units = []
    for d, (qt_ref, k_ref, vt_ref, g_ref) in enumerate(dirs):
        a = g_ref[0]
        lf = _log_sigmoid(a)
        valid = (row <= col) if d == 0 else (row >= col)
        b_rows = _split_dot(lf, valid.astype(BF16))
        last = L - 1 if d == 0 else 0
        for h in range(ML_HEADS):
            ci = d * 2 * ML_HEADS + h
            cf = ci + ML_HEADS
            st = d * ML_HEADS + h
            u = {"st": st, "valid": valid}
            bt = b_rows[cf:cf + 1, :]
            li = a[ci:ci + 1, :]
            b_end = bt[:, last:last + 1]
            m_st = m_sc[st:st + 1, 0:1]
            qt = qt_ref[0, h * dh:(h + 1) * dh, :]
            k = k_ref[0, :, h * dh:(h + 1) * dh]
            u["vt"] = vt_ref[0, h * dh:(h + 1) * dh, :]
            u["bt"] = bt
            u["inter"] = bt + m_st
            src = jnp.where(diag, bt - li, 0.0)
            hi = src.astype(BF16)
            lo = (src - hi.astype(F32)).astype(BF16)
            u["src"] = _dot(hi, ones_ll) + _dot(lo, ones_ll)
            w_log = b_end - bt + li
            u["m_new"] = jnp.maximum(b_end + m_st, jnp.max(w_log, axis=1, keepdims=True))
            u["decay"] = jnp.exp(b_end + m_st - u["m_new"])
            v_aug = jnp.concatenate([u["vt"], one_rows], axis=0)
            vw = (v_aug.astype(F32) * jnp.exp(w_log - u["m_new"])).astype(BF16)
            u["s_kq"] = _dot(k, qt)
            u["qc"] = _dot(c_sc[st].astype(BF16), qt)
            u["upd"] = _dot(vw, k)
            units.append(u)

    for u in units:
        d_log = jnp.where(u["valid"], u["bt"] - u["src"], -jnp.inf)
        u["m_t"] = jnp.maximum(u["inter"], jnp.max(d_log, axis=0, keepdims=True))
        w_st = jnp.exp(d_log - u["m_t"]) * u["s_kq"]
        u["w_sum"] = jnp.sum(w_st, axis=0, keepdims=True)
        u["pv"] = _dot(u["vt"], w_st.astype(BF16))

    outs = []
    for u in units:
        a_inter = jnp.exp(u["inter"] - u["m_t"])
        num = a_inter * u["qc"][0:dh, :] + u["pv"]
        den = a_inter * u["qc"][dh:dh + 1, :] + u["w_sum"]
        outs.append(num / jnp.maximum(jnp.abs(den), jnp.exp(-u["m_t"])))
        st = u["st"]
        c_sc[st] = u["decay"] * c_sc[st] + u["upd"]
        m_sc[st:st + 1, :] = jnp.broadcast_to(u["m_new"], (1, LANES))
    hf_ref[0] = jnp.concatenate(outs[0:ML_HEADS], axis=0)
    hb_ref[0] = jnp.concatenate(outs[ML_HEADS:], axis=0)


def _mlstm(qt, k, vt, gt, n_lat_chunks):
    bsz, s, _ = k.shape
    nc = s // ML_CHUNK

    def fwd_chunk(j):
        return (j + n_lat_chunks) % nc

    def bwd_chunk(j):
        return nc - 1 - j

    def specs(chunk):
        def col(height):
            return pl.BlockSpec((1, height, ML_CHUNK), lambda b, j: (b, 0, chunk(j)))
        return [col(BRANCH_W), pl.BlockSpec((1, ML_CHUNK, BRANCH_W), lambda b, j: (b, chunk(j), 0)),
                col(BRANCH_W), col(4 * ML_HEADS)], col(BRANCH_W)

    in_f, out_f = specs(fwd_chunk)
    in_b, out_b = specs(bwd_chunk)
    return pl.pallas_call(
        _mlstm_kernel,
        grid=(bsz, nc),
        in_specs=in_f + in_b,
        out_specs=[out_f, out_b],
        out_shape=[jax.ShapeDtypeStruct((bsz, BRANCH_W, s), F32)] * 2,
        scratch_shapes=[pltpu.VMEM((2 * ML_HEADS, LANES, ML_DH), F32),
                        pltpu.VMEM((2 * ML_HEADS, LANES), F32)],
        compiler_params=_cparams("parallel", "arbitrary"),
        name="mlstm_scan",
    )(qt, k, vt, gt, qt, k, vt, gt)


def _attn_kernel(qt_ref, k_ref, vt_ref, o_ref, m_sc, acc_sc, *, n_lat, n_ctx, tk):
    m_sc[...] = jnp.full_like(m_sc, -jnp.inf)
    acc_sc[...] = jnp.zeros_like(acc_sc)

    def scores(h, start, size):
        sl = slice(h * HEAD_SLOT, (h + 1) * HEAD_SLOT)
        return _dot(k_ref[0, pl.ds(start, size), sl], qt_ref[0, sl, :])

    def keys(start, size):
        n_sub = size // ATT_SUB
        order = [(j, h) for j in range(n_sub) for h in range(MLA_HEADS)]
        pending = [scores(h, start + j * ATT_SUB, ATT_SUB) for j, h in order[:ATT_AHEAD]]
        for i, (j, h) in enumerate(order):
            off = start + j * ATT_SUB
            s = pending.pop(0)
            if i + ATT_AHEAD < len(order):
                jn, hn = order[i + ATT_AHEAD]
                pending.append(scores(hn, start + jn * ATT_SUB, ATT_SUB))
            m_old = m_sc[h]
            m_new = jnp.maximum(m_old, jnp.max(s, axis=0, keepdims=True))
            p = jnp.exp2(s - m_new)
            vt = vt_ref[0, h * V_SLOT:(h + 1) * V_SLOT, pl.ds(off, ATT_SUB)]
            acc_sc[h] = jnp.exp2(m_old - m_new) * acc_sc[h] + _dot(vt, p.astype(BF16))
            m_sc[h] = m_new

    keys(n_lat, n_ctx)

    def chunk(c, carry):
        keys(pl.multiple_of(c * tk, tk), tk)
        return carry

    is_ctx_tile = pl.program_id(1) * ATT_TQ >= n_lat
    lax.fori_loop(0, jnp.where(is_ctx_tile, 0, n_lat // tk), chunk, 0)
    out_t = jnp.concatenate([acc_sc[h, 0:V_DIM, :] / acc_sc[h, V_DIM:V_DIM + 1, :] for h in range(MLA_HEADS)],
                            axis=0)
    o_ref[0] = out_t.T.astype(o_ref.dtype)


def _key_chunk(n_keys):
    return max(t for t in range(ATT_TQ, ATT_TK + 1, ATT_TQ) if n_keys % t == 0)


def _attention(qt, k, vt, n_lat):
    bsz, s, slot_w = k.shape
    tq = ATT_TQ
    return pl.pallas_call(
        functools.partial(_attn_kernel, n_lat=n_lat, n_ctx=s - n_lat, tk=_key_chunk(n_lat)),
        grid=(bsz, s // tq),
        in_specs=[pl.BlockSpec((1, slot_w, tq), lambda b, t: (b, 0, t)),
                  pl.BlockSpec((1, s, slot_w), lambda b, t: (b, 0, 0), pipeline_mode=pl.Buffered(1)),
                  pl.BlockSpec((1, MLA_HEADS * V_SLOT, s), lambda b, t: (b, 0, 0), pipeline_mode=pl.Buffered(1))],
        out_specs=pl.BlockSpec((1, tq, BRANCH_W), lambda b, t: (b, t, 0)),
        out_shape=jax.ShapeDtypeStruct((bsz, s, BRANCH_W), BF16),
        scratch_shapes=[pltpu.VMEM((MLA_HEADS, 1, tq), F32), pltpu.VMEM((MLA_HEADS, V_SLOT, tq), F32)],
        compiler_params=_cparams("parallel", "arbitrary"),
        name="mla_attention",
    )(qt, k, vt)


def _merge_kernel(hx_ref, hf_ref, hb_ref, o_ref, ya_ref, pc_ref, prev_ref, next_ref, x_ref, mod_ref,
                  hg_ref, pw_ref, ps_ref, cw_ref, cb_ref, wg_ref, bg_ref, wb_ref, wo_ref,
                  gffn_ref, xo_ref, h2_ref, *, n_lat_tiles, n_tiles):
    t = pl.program_id(1)
    tm = ROW_TILE
    bw = BRANCH_W
    halo = POOL_HALO
    no_prev = jnp.logical_or(t == 0, t == n_lat_tiles)
    no_next = jnp.logical_or(t == n_lat_tiles - 1, t == n_tiles - 1)

    h_t = hf_ref[0] + hb_ref[0]
    normed = []
    for hd in range(ML_HEADS):
        hh = h_t[hd * ML_DH:(hd + 1) * ML_DH]
        normed.append(hh * lax.rsqrt(jnp.mean(hh * hh, axis=0, keepdims=True) + EPS))
    y_ml = (jnp.concatenate(normed, axis=0) * hg_ref[...] * jax.nn.sigmoid(o_ref[0].astype(F32))).T

    prev = jnp.where(no_prev, 0.0, prev_ref[0])
    nxt = jnp.where(no_next, 0.0, next_ref[0])
    ext = jnp.concatenate([prev, pc_ref[0], nxt], axis=0)
    pe = ext[:, 0:bw]

    def rows(arr, off):
        return arr[halo + off:halo + off + tm, :]

    r = lax.broadcasted_iota(jnp.int32, (tm, 1), 0)
    lane = lax.broadcasted_iota(jnp.int32, (tm, bw), 1)
    centre = rows(pe, 0)
    acc = centre
    mean = jnp.zeros((tm, bw), F32)
    done = 0
    for gi, w in enumerate(POOL_WINDOWS):
        for off in list(range(-(w // 2), -done)) + list(range(max(done, 1), w // 2)):
            acc = acc + rows(pe, off)
        done = w // 2
        before = jnp.where(no_prev, jnp.minimum(r, w // 2), w // 2)
        after = jnp.where(no_next, jnp.minimum(tm - r, w // 2), w // 2)
        inv = 1.0 / (before + after).astype(F32)
        mean = jnp.where(lane >= gi * POOL_GROUP, acc * inv, mean)
    y_pool = _dot((mean - centre).astype(BF16), pw_ref[...]) * ps_ref[...]

    z = ext[:, 3 * bw:4 * bw] * ext[:, bw:2 * bw]
    conv = cb_ref[...] + rows(z, -1) * cw_ref[0:1, :] + rows(z, 0) * cw_ref[1:2, :] + rows(z, 1) * cw_ref[2:3, :]
    y_conv = rows(ext[:, 2 * bw:3 * bw], 0) * conv

    hxb = hx_ref[0]
    d = hxb.shape[1]
    gates = jax.nn.sigmoid(_dot(hxb, wg_ref[...]) + bg_ref[...])
    ys = (y_ml.astype(BF16), ya_ref[0], y_pool.astype(BF16), y_conv.astype(BF16))
    merged = None
    for i, y in enumerate(ys):
        term = gates[:, i * d:(i + 1) * d] * _dot(y, wb_ref[i])
        merged = term if merged is None else merged + term
    out = _dot(merged.astype(BF16), wo_ref[...])
    x_new = x_ref[0] + mod_ref[0, 2:3, :] * out
    xo_ref[0] = x_new
    h2_ref[0] = (_rms_rows(x_new) * gffn_ref[...]) * (1.0 + mod_ref[0, 4:5, :]) + mod_ref[0, 3:4, :]


def _merge(hx, hf, hb, o, y_mla, pc, xs, mod, lw, n_lat_tiles):
    bsz, s, d = xs.shape
    tm = ROW_TILE
    nt = s // tm
    halo = POOL_HALO
    per = tm // halo

    def row_spec(width):
        return pl.BlockSpec((1, tm, width), lambda b, t: (b, t, 0))

    def col_spec():
        return pl.BlockSpec((1, BRANCH_W, tm), lambda b, t: (b, 0, t))

    return pl.pallas_call(
        functools.partial(_merge_kernel, n_lat_tiles=n_lat_tiles, n_tiles=nt),
        grid=(bsz, nt),
        in_specs=[
            row_spec(d), col_spec(), col_spec(), col_spec(), row_spec(BRANCH_W),
            row_spec(1024),
            pl.BlockSpec((1, halo, 1024), lambda b, t: (b, jnp.maximum(t * per - 1, 0), 0)),
            pl.BlockSpec((1, halo, 1024), lambda b, t: (b, jnp.minimum((t + 1) * per, s // halo - 1), 0)),
            row_spec(d),
            pl.BlockSpec((1, 6, d), lambda b, t: (jnp.where(t >= n_lat_tiles, bsz, b), 0, 0)),
            _const_spec((BRANCH_W, tm)), _const_spec((BRANCH_W, BRANCH_W)),
            _const_spec((1, BRANCH_W)), _const_spec((3, BRANCH_W)), _const_spec((1, BRANCH_W)),
            _const_spec((d, 4 * d)), _const_spec((1, 4 * d)), _const_spec((4, BRANCH_W, d)),
            _const_spec((d, d)), _const_spec((1, d)),
        ],
        out_specs=[row_spec(d), row_spec(d)],
        out_shape=[jax.ShapeDtypeStruct((bsz, s, d), F32)] * 2,
        compiler_params=_cparams("parallel", "parallel"),
        name="mixer_merge",
    )(hx, hf, hb, o, y_mla, pc, pc, pc, xs, mod, lw["head_gain"], lw["pool_w"], lw["pool_scale"],
      lw["conv_w"], lw["conv_b"], lw["w_gate"], lw["b_gate"], lw["w_branch"], lw["w_out"], lw["g_ffn"])


def _router_kernel(h_ref, wr_ref, br_ref, pe_ref, we_ref, c16_ref):
    tm = h_ref.shape[0]
    scores = jax.nn.sigmoid(_dot(h_ref[...].astype(BF16), wr_ref[...]))
    sel = scores + br_ref[...]
    lane = lax.broadcasted_iota(jnp.int32, (tm, LANES), 1)
    member = jnp.zeros((tm, LANES), F32)
    for _ in range(TOP_K):
        mx = jnp.max(sel, axis=1, keepdims=True)
        ix = jnp.min(jnp.where(sel == mx, lane, LANES), axis=1, keepdims=True)
        hit = lane == ix
        member = jnp.where(hit, 1.0, member)
        sel = jnp.where(hit, -jnp.inf, sel)
    picked = member * scores
    weights = picked / jnp.sum(picked, axis=1, keepdims=True) * ROUTE_SCALE
    counts = jnp.sum(member, axis=0, keepdims=True)
    c16 = jnp.floor((counts + (RUN_ROWS - 1)) * (1.0 / RUN_ROWS))
    e_row = lax.broadcasted_iota(jnp.int32, (LANES, LANES), 0)
    e_col = lax.broadcasted_iota(jnp.int32, (LANES, LANES), 1)
    lower = (e_row < e_col).astype(BF16)
    o16 = _dot(jnp.broadcast_to(c16, (SUBLANES, LANES)).astype(BF16), lower)[0:1]
    row = lax.broadcasted_iota(jnp.int32, (tm, tm), 0)
    col = lax.broadcasted_iota(jnp.int32, (tm, tm), 1)
    rank = _dot((col < row).astype(BF16), member.astype(BF16))
    pos = jnp.where(member > 0.0, RUN_ROWS * o16 + rank, -1.0)
    pe_ref[0] = pos.T[0:N_EXPERTS]
    we_ref[0] = weights.T[0:N_EXPERTS]
    c16_ref[0] = c16.astype(jnp.int32)


def _router(h2, lw):
    n, d = h2.shape
    tm = MOE_TILE
    nt = n // tm

    def t_spec():
        return pl.BlockSpec((1, N_EXPERTS, tm), lambda i: (i, 0, 0))

    return pl.pallas_call(
        _router_kernel,
        grid=(nt,),
        in_specs=[pl.BlockSpec((tm, d), lambda i: (i, 0)), _const_spec((d, LANES)), _const_spec((1, LANES))],
        out_specs=[t_spec(), t_spec(), pl.BlockSpec((1, 1, LANES), lambda i: (i, 0, 0))],
        out_shape=[jax.ShapeDtypeStruct((nt, N_EXPERTS, tm), F32), jax.ShapeDtypeStruct((nt, N_EXPERTS, tm), F32),
                   jax.ShapeDtypeStruct((nt, 1, LANES), jnp.int32)],
        compiler_params=_cparams("parallel"),
        name="moe_router",
    )(h2, lw["w_router"], lw["b_router"])


def _tile_row_bound(tm):
    worst = tm * TOP_K + N_EXPERTS * (RUN_ROWS - 1)
    return -(-worst // MOE_CHUNK) * MOE_CHUNK


def _selection_rows(g, grp_e_ref, pe_ref, tm):
    prow = pe_ref[0, pl.ds(grp_e_ref[0, 0, g], 1), :]
    rows = (lax.broadcasted_iota(jnp.int32, (RUN_ROWS, tm), 0) + g * RUN_ROWS).astype(F32)
    return prow, rows


def _dispatch_kernel(tail_ref, n_act_ref, grp_e_ref, gdst_ref, h_ref, pe_ref, xs_ref,
                     sel_sc, xb_sc, xp_sc, zero_sc, sem, *, n_groups):
    i = pl.program_id(0)
    tm = h_ref.shape[0]
    n_chunks = sel_sc.shape[0] // MOE_CHUNK
    per_chunk = MOE_CHUNK // RUN_ROWS

    @pl.when(i == pl.num_programs(0) - 1)
    def _():
        zero_sc[...] = jnp.zeros_like(zero_sc)

        def fill_copy(q):
            return pltpu.make_async_copy(
                zero_sc.at[pl.ds(0, RUN_ROWS)],
                xs_ref.at[pl.ds(pl.multiple_of(tail_ref[q] * RUN_ROWS, RUN_ROWS), RUN_ROWS)], sem)

        def block_copy(b):
            return pltpu.make_async_copy(
                zero_sc, xs_ref.at[pl.ds(pl.multiple_of(b * EXPERT_ROWS, EXPERT_ROWS), EXPERT_ROWS)], sem)

        def block_start(b, carry):
            block_copy(b).start()
            return carry

        def block_wait(b, carry):
            block_copy(b).wait()
            return carry

        n_blocks = xs_ref.shape[0] // EXPERT_ROWS
        lax.fori_loop(n_act_ref[0], n_blocks, block_start, 0)
        lax.fori_loop(n_act_ref[0], n_blocks, block_wait, 0)

        def fill_start(q, carry):
            @pl.when(tail_ref[q] >= 0)
            def _():
                fill_copy(q).start()
            return carry

        def fill_wait(q, carry):
            @pl.when(tail_ref[q] >= 0)
            def _():
                fill_copy(q).wait()
            return carry

        lax.fori_loop(0, tail_ref.shape[0], fill_start, 0)
        lax.fori_loop(0, tail_ref.shape[0], fill_wait, 0)

    def build(g, carry):
        prow, rows = _selection_rows(g, grp_e_ref, pe_ref, tm)
        sel_sc[pl.ds(pl.multiple_of(g * RUN_ROWS, RUN_ROWS), RUN_ROWS), :] = jnp.where(
            prow == rows, 1.0, 0.0).astype(BF16)
        return carry

    lax.fori_loop(0, n_groups, build, 0, unroll=4)
    xb_sc[...] = h_ref[...].astype(BF16)

    def run_copy(g):
        return pltpu.make_async_copy(
            xp_sc.at[pl.ds(pl.multiple_of(g * RUN_ROWS, RUN_ROWS), RUN_ROWS)],
            xs_ref.at[pl.ds(pl.multiple_of(gdst_ref[0, 0, g] * RUN_ROWS, RUN_ROWS), RUN_ROWS)], sem)

    def permute(ch):
        r0 = pl.multiple_of(ch * MOE_CHUNK, MOE_CHUNK)
        xp_sc[pl.ds(r0, MOE_CHUNK), :] = _dot(sel_sc[pl.ds(r0, MOE_CHUNK), :], xb_sc[...]).astype(BF16)

    def start_chunk(ch):
        for j in range(per_chunk):
            run_copy(ch * per_chunk + j).start()

    permute(0)

    def step(ch, carry):
        start_chunk(ch - 1)
        permute(ch)
        return carry

    lax.fori_loop(1, n_chunks, step, 0)
    start_chunk(n_chunks - 1)
    pltpu.make_async_copy(xp_sc, xs_ref.at[pl.ds(0, xp_sc.shape[0])], sem).wait()


def _dispatch(h2, pe, plan, n_rows):
    n, d = h2.shape
    tm = MOE_TILE
    nt = n // tm
    rb = _tile_row_bound(tm)
    n_groups = rb // RUN_ROWS

    def smem_spec():
        return pl.BlockSpec((1, 1, n_groups), lambda i, *_: (i, 0, 0), memory_space=pltpu.SMEM)

    return pl.pallas_call(
        functools.partial(_dispatch_kernel, n_groups=n_groups),
        grid_spec=pltpu.PrefetchScalarGridSpec(
            num_scalar_prefetch=2,
            grid=(nt,),
            in_specs=[smem_spec(), smem_spec(),
                      pl.BlockSpec((tm, d), lambda i, *_: (i, 0)),
                      pl.BlockSpec((1, N_EXPERTS, tm), lambda i, *_: (i, 0, 0))],
            out_specs=pl.BlockSpec(memory_space=pl.ANY),
            scratch_shapes=[pltpu.VMEM((rb, tm), BF16), pltpu.VMEM((tm, d), BF16), pltpu.VMEM((rb, d), BF16),
                            pltpu.VMEM((EXPERT_ROWS, d), BF16), pltpu.SemaphoreType.DMA(())],
        ),
        out_shape=jax.ShapeDtypeStruct((n_rows, d), BF16),
        compiler_params=_cparams("arbitrary"),
        name="moe_dispatch",
    )(plan["tail"], plan["n_act"].reshape(1), plan["grp_e"], plan["gdst"], h2, pe)


def _expert_kernel(blk_e_ref, n_act_ref, x_ref, w1_ref, w3_ref, w2_ref, y_ref):
    del blk_e_ref

    @pl.when(pl.program_id(0) < n_act_ref[0])
    def _():
        x = x_ref[...]
        a = _dot(x, w1_ref[0, 0].astype(BF16))
        act = (a * jax.nn.sigmoid(a)) * _dot(x, w3_ref[0, 0].astype(BF16))
        y_ref[...] = _dot(act.astype(BF16), w2_ref[0, 0].astype(BF16)).astype(BF16)

    @pl.when(pl.program_id(0) >= n_act_ref[0])
    def _():
        y_ref[...] = jnp.zeros_like(y_ref)


def _experts(xs_sorted, blk_e, n_act, lw):
    n_rows, d = xs_sorted.shape
    bm = EXPERT_ROWS

    def row_map(i, blk_e_ref, n_act_ref):
        return (jnp.minimum(i, n_act_ref[0] - 1), 0)

    layer = lw["layer"]

    def w_map(i, blk_e_ref, n_act_ref):
        return (layer, blk_e_ref[i], 0, 0)

    return pl.pallas_call(
        _expert_kernel,
        grid_spec=pltpu.PrefetchScalarGridSpec(
            num_scalar_prefetch=2,
            grid=(n_rows // bm,),
            in_specs=[pl.BlockSpec((bm, d), row_map),
                      pl.BlockSpec((1, 1, d, EXPERT_FF), w_map),
                      pl.BlockSpec((1, 1, d, EXPERT_FF), w_map),
                      pl.BlockSpec((1, 1, EXPERT_FF, d), w_map)],
            out_specs=pl.BlockSpec((bm, d), lambda i, *_: (i, 0)),
        ),
        out_shape=jax.ShapeDtypeStruct((n_rows, d), BF16),
        compiler_params=_cparams("arbitrary"),
        name="moe_experts",
    )(blk_e, n_act, xs_sorted, lw["w1"], lw["w3"], lw["w2"])


def _combine_kernel(grp_e_ref, gsrc_ref, h_ref, x_ref, pe_ref, we_ref, mod_ref, modc_ref, ws13_ref,
                    ws2_ref, ys_ref, xo_ref, sel_sc, yp_sc, acc_sc, sem, *, n_groups, tiles_per_sample, n_ctx):
    i = pl.program_id(0)
    tm = h_ref.shape[0]

    n_chunks = sel_sc.shape[0] // MOE_CHUNK
    per_chunk = MOE_CHUNK // RUN_ROWS

    def start_chunk(ch):
        for j in range(per_chunk):
            g = ch * per_chunk + j
            pltpu.make_async_copy(
                ys_ref.at[pl.ds(pl.multiple_of(gsrc_ref[0, 0, g] * RUN_ROWS, RUN_ROWS), RUN_ROWS)],
                yp_sc.at[pl.ds(pl.multiple_of(g * RUN_ROWS, RUN_ROWS), RUN_ROWS)], sem).start()

    def wait_chunk(ch):
        r0 = pl.multiple_of(ch * MOE_CHUNK, MOE_CHUNK)
        pltpu.make_async_copy(ys_ref.at[pl.ds(0, MOE_CHUNK)], yp_sc.at[pl.ds(r0, MOE_CHUNK)], sem).wait()

    start_chunk(0)

    def build(g, carry):
        prow, rows = _selection_rows(g, grp_e_ref, pe_ref, tm)
        wrow = we_ref[0, pl.ds(grp_e_ref[0, 0, g], 1), :]
        sel_sc[pl.ds(pl.multiple_of(g * RUN_ROWS, RUN_ROWS), RUN_ROWS), :] = jnp.where(
            prow == rows, wrow, 0.0).astype(BF16)
        return carry

    lax.fori_loop(0, n_groups, build, 0, unroll=4)

    up = _dot(h_ref[...].astype(BF16), ws13_ref[...])
    a = up[:, 0:EXPERT_FF]
    act = (a * jax.nn.sigmoid(a)) * up[:, EXPERT_FF:2 * EXPERT_FF]
    acc_sc[...] = _dot(act.astype(BF16), ws2_ref[...])

    def gather_sum(ch):
        r0 = pl.multiple_of(ch * MOE_CHUNK, MOE_CHUNK)
        acc_sc[...] += lax.dot_general(sel_sc[pl.ds(r0, MOE_CHUNK), :], yp_sc[pl.ds(r0, MOE_CHUNK), :], TN_DIMS,
                                       preferred_element_type=F32)

    def step(ch, carry):
        wait_chunk(ch)
        start_chunk(ch + 1)
        gather_sum(ch)
        return carry

    lax.fori_loop(0, n_chunks - 1, step, 0)
    wait_chunk(n_chunks - 1)
    gather_sum(n_chunks - 1)
    r = lax.broadcasted_iota(jnp.int32, (tm, 1), 0)
    is_ctx = jnp.logical_and(i % tiles_per_sample == tiles_per_sample - 1, r >= tm - n_ctx)
    gate = jnp.where(is_ctx, modc_ref[0, 5:6, :], mod_ref[0, 5:6, :])
    xo_ref[...] = x_ref[...] + gate * acc_sc[...]


def _combine(h2, xs, pe, we, plan, mod, ys_sorted, lw, tiles_per_sample, n_ctx, bsz):
    n, d = h2.shape
    tm = MOE_TILE
    nt = n // tm
    rb = _tile_row_bound(tm)
    n_groups = rb // RUN_ROWS

    def smem_spec():
        return pl.BlockSpec((1, 1, n_groups), lambda i, *_: (i, 0, 0), memory_space=pltpu.SMEM)

    def row_spec():
        return pl.BlockSpec((tm, d), lambda i, *_: (i, 0))

    def t_spec():
        return pl.BlockSpec((1, N_EXPERTS, tm), lambda i, *_: (i, 0, 0))

    return pl.pallas_call(
        functools.partial(_combine_kernel, n_groups=n_groups, tiles_per_sample=tiles_per_sample, n_ctx=n_ctx),
        grid_spec=pltpu.PrefetchScalarGridSpec(
            num_scalar_prefetch=0,
            grid=(nt,),
            in_specs=[smem_spec(), smem_spec(), row_spec(), row_spec(), t_spec(), t_spec(),
                      pl.BlockSpec((1, 6, d), lambda i, *_: (i // tiles_per_sample, 0, 0)),
                      pl.BlockSpec((1, 6, d), lambda i, *_: (bsz, 0, 0)),
                      pl.BlockSpec((d, 2 * EXPERT_FF), lambda i, *_: (0, 0), pipeline_mode=pl.Buffered(1)),
                      pl.BlockSpec((EXPERT_FF, d), lambda i, *_: (0, 0), pipeline_mode=pl.Buffered(1)),
                      pl.BlockSpec(memory_space=pl.ANY)],
            out_specs=row_spec(),
            scratch_shapes=[pltpu.VMEM((rb, tm), BF16), pltpu.VMEM((rb, d), BF16), pltpu.VMEM((tm, d), F32),
                            pltpu.SemaphoreType.DMA(())],
        ),
        out_shape=jax.ShapeDtypeStruct((n, d), F32),
        compiler_params=_cparams("arbitrary"),
        name="moe_combine",
    )(plan["grp_e"], plan["gsrc"], h2, xs, pe, we, mod, mod, lw["ws13"], lw["ws2"], ys_sorted)


def _moe_plan(c16, n_groups, spare16):
    nt = c16.shape[0]
    per_blk = EXPERT_ROWS // RUN_ROWS
    o16 = jnp.cumsum(c16, axis=1) - c16
    before16 = jnp.cumsum(c16, axis=0) - c16
    gtot16 = jnp.sum(c16, axis=0)
    gblk = (gtot16 + per_blk - 1) // per_blk
    gend_blk = jnp.cumsum(gblk)
    gstart16 = per_blk * (gend_blk - gblk)
    g = jnp.arange(n_groups, dtype=jnp.int32)
    grp_e = jnp.minimum(jnp.sum((o16 + c16)[:, None, :] <= g[None, :, None], axis=2), N_EXPERTS - 1)
    run0 = gstart16[None, :] + before16 - o16
    experts = jnp.arange(N_EXPERTS, dtype=jnp.int32)
    gdst = jnp.sum(jnp.where(grp_e[:, :, None] == experts[None, None, :], run0[:, None, :], 0), axis=2) + g[None, :]
    q = jnp.arange(per_blk, dtype=jnp.int32)
    tail = jnp.where(q[None, :] < (per_blk * gblk - gtot16)[:, None],
                     (gstart16 + gtot16)[:, None] + q[None, :], -1)
    used = g[None, :] < jnp.sum(c16, axis=1)[:, None]
    return {
        "grp_e": grp_e.astype(jnp.int32).reshape(nt, 1, n_groups),
        "gdst": jnp.where(used, gdst, spare16 + g[None, :]).astype(jnp.int32).reshape(nt, 1, n_groups),
        "gsrc": jnp.where(used, gdst, 0).astype(jnp.int32).reshape(nt, 1, n_groups),
        "tail": tail.astype(jnp.int32).reshape(-1),
        "gend_blk": gend_blk, "n_act": gend_blk[-1].astype(jnp.int32),
    }


def _moe(h2, xs, mod, lw, tiles_per_sample, n_ctx, bsz):
    n, d = h2.shape
    nt = n // MOE_TILE
    n_groups = _tile_row_bound(MOE_TILE) // RUN_ROWS
    pe, we, c16 = _router(h2, lw)
    worst_rows = n * TOP_K + nt * N_EXPERTS * (RUN_ROWS - 1) + N_EXPERTS * (EXPERT_ROWS - 1)
    run_blocks = -(-worst_rows // EXPERT_ROWS)
    n_blocks = run_blocks + -(-n_groups * RUN_ROWS // EXPERT_ROWS)
    plan = _moe_plan(c16[:, 0, :N_EXPERTS], n_groups, run_blocks * (EXPERT_ROWS // RUN_ROWS))
    blk = jnp.minimum(jnp.arange(n_blocks, dtype=jnp.int32), plan["n_act"] - 1)
    blk_e = jnp.minimum(jnp.sum(plan["gend_blk"][None, :] <= blk[:, None], axis=1), N_EXPERTS - 1).astype(jnp.int32)
    xs_sorted = _dispatch(h2, pe, plan, n_blocks * EXPERT_ROWS)
    ys_sorted = _experts(xs_sorted, blk_e, plan["n_act"].reshape(1), lw)
    return _combine(h2, xs, pe, we, plan, mod, ys_sorted, lw, tiles_per_sample, n_ctx, bsz)


def _slots(w, head_w, real_w):
    rows = w.shape[0]
    w = w.reshape(rows, -1, head_w)[:, :, :real_w]
    return jnp.pad(w, ((0, 0), (0, 0), (0, HEAD_SLOT - real_w))).reshape(rows, -1)


def _block_diag(blocks):
    n, r, c = blocks.shape
    out = jnp.zeros((n * r, n * c), blocks.dtype)
    for i in range(n):
        out = out.at[i * r:(i + 1) * r, i * c:(i + 1) * c].set(blocks[i])
    return out


def _layer_weights(l, p):
    d = p["w_in"].shape[1]
    w_in = p["w_in"][l]
    ml = 4 * BRANCH_W
    o_mla = ml + 4 * ML_HEADS
    o_pool = o_mla + Q_LORA + KV_LORA + QK_ROPE
    kr = w_in[:, o_mla + Q_LORA + KV_LORA:o_pool]
    kr_slot = jnp.pad(kr, ((0, 0), (QK_NOPE, HEAD_SLOT - QK_DIM)))
    w_big = jnp.concatenate([
        w_in[:, :ml],
        w_in[:, o_pool:],
        w_in[:, o_mla:o_mla + Q_LORA + KV_LORA], jnp.zeros((d, LANES), F32),
        jnp.tile(kr_slot, (1, MLA_HEADS)),
        jnp.pad(w_in[:, ml:o_mla], ((0, 0), (0, LANES - 4 * ML_HEADS))),
    ], axis=1).astype(BF16)
    w_ukv = p["mla_w_ukv"][l].reshape(KV_LORA, MLA_HEADS, QK_NOPE + V_DIM)

    def gain_slots(g):
        return jnp.tile(jnp.pad(g, (0, HEAD_SLOT - QK_DIM)), MLA_HEADS)[None, :]

    return {
        "g_mix": p["g_mix"][l][None, :], "g_ffn": p["g_ffn"][l][None, :],
        "w_big": w_big,
        "g_cq": p["mla_g_cq"][l][None, :], "g_ckv": p["mla_g_ckv"][l][None, :],
        "w_uq": _slots(p["mla_w_uq"][l], QK_DIM, QK_DIM).astype(BF16),
        "w_uk": _slots(w_ukv[:, :, :QK_NOPE].reshape(KV_LORA, -1), QK_NOPE, QK_NOPE).astype(BF16),
        "w_uv": w_ukv[:, :, QK_NOPE:].reshape(KV_LORA, BRANCH_W).astype(BF16),
        "g_qn": gain_slots(p["mla_g_qn"][l]), "g_kn": gain_slots(p["mla_g_kn"][l]),
        "gate_bias": jnp.pad(p["ml_gate_bias"][l], (0, LANES - 4 * ML_HEADS))[None, :],
        "head_gain": jnp.broadcast_to(p["ml_head_gain"][l][:, None], (BRANCH_W, ROW_TILE)),
        "pool_w": _block_diag(p["pool_w"][l]).astype(BF16),
        "pool_scale": p["pool_scale"][l][None, :],
        "conv_w": p["conv_w"][l], "conv_b": p["conv_b"][l][None, :],
        "w_gate": jnp.concatenate(list(p["w_gate"][l]), axis=1).astype(BF16),
        "b_gate": p["b_gate"][l].reshape(1, -1),
        "w_branch": p["w_branch"][l].astype(BF16),
        "w_out": p["w_out"][l].astype(BF16),
        "w_router": jnp.pad(p["moe_w_router"][l], ((0, 0), (0, LANES - N_EXPERTS))).astype(BF16),
        "b_router": jnp.pad(p["moe_b_router"][l], (0, LANES - N_EXPERTS), constant_values=-1e30)[None, :],
        "layer": l, "w1": p["moe_w1"], "w3": p["moe_w3"], "w2": p["moe_w2"],
        "ws13": jnp.concatenate([p["moe_ws1"][l], p["moe_ws3"][l]], axis=1).astype(BF16),
        "ws2": p["moe_ws2"][l].astype(BF16),
    }


def _rope_tables(n_lat, n_ctx):
    t = jnp.arange(n_lat)
    n_freq = QK_ROPE // 4
    inv = ROPE_THETA ** (-jnp.arange(n_freq, dtype=F32) / n_freq)
    ang_r = (t // GRID_W).astype(F32)[:, None] * inv
    ang_c = (t % GRID_W).astype(F32)[:, None] * inv
    cos4 = jnp.concatenate([jnp.cos(ang_r)] * 2 + [jnp.cos(ang_c)] * 2, axis=1)
    zero = jnp.zeros_like(ang_r)
    sin_a = jnp.concatenate([-jnp.sin(ang_r), zero, -jnp.sin(ang_c), zero], axis=1)
    sin_b = jnp.concatenate([zero, jnp.sin(ang_r), zero, jnp.sin(ang_c)], axis=1)

    def slot(a, fill):
        a = jnp.pad(a, ((0, 0), (QK_NOPE, 0)), constant_values=fill)
        a = jnp.pad(a, ((0, 0), (0, HEAD_SLOT - QK_DIM)), constant_values=fill)
        return jnp.pad(a, ((0, n_ctx), (0, 0)), constant_values=fill)

    return {"cos": slot(cos4, 1.0), "sin_a": slot(sin_a, 0.0), "sin_b": slot(sin_b, 0.0),
            "ones_slot": _block_diag(jnp.ones((MLA_HEADS, HEAD_SLOT, HEAD_SLOT), BF16))}


def kernel(x, c, ctx, c_ctx, w_mod, b_mod, g_mix, g_ffn, w_in, ml_gate_bias, ml_head_gain, mla_g_cq, mla_g_ckv,
           mla_w_uq, mla_w_ukv, mla_g_qn, mla_g_kn, pool_w, pool_scale, conv_w, conv_b, w_gate, b_gate, w_branch,
           w_out, moe_w_router, moe_b_router, moe_w1, moe_w3, moe_w2, moe_ws1, moe_ws3, moe_ws2):
    p = dict(g_mix=g_mix, g_ffn=g_ffn, w_in=w_in, ml_gate_bias=ml_gate_bias, ml_head_gain=ml_head_gain,
             mla_g_cq=mla_g_cq, mla_g_ckv=mla_g_ckv, mla_w_uq=mla_w_uq, mla_w_ukv=mla_w_ukv, mla_g_qn=mla_g_qn,
             mla_g_kn=mla_g_kn, pool_w=pool_w, pool_scale=pool_scale, conv_w=conv_w, conv_b=conv_b,
             w_gate=w_gate, b_gate=b_gate, w_branch=w_branch, w_out=w_out, moe_w_router=moe_w_router,
             moe_b_router=moe_b_router, moe_w1=moe_w1, moe_w3=moe_w3, moe_w2=moe_w2, moe_ws1=moe_ws1,
             moe_ws3=moe_ws3, moe_ws2=moe_ws2)
    bsz, n_lat, d = x.shape
    n_ctx = ctx.shape[1]
    depth = w_mod.shape[0]
    s = n_lat + n_ctx
    assert n_ctx == ROW_TILE == ATT_TQ and n_lat % ROW_TILE == 0 and n_lat % GRID_W == 0
    assert s % MOE_TILE == 0 and n_ctx <= MOE_TILE
    n_lat_tiles = n_lat // ROW_TILE

    mod_rows = -(-(bsz + 1) // SUBLANES) * SUBLANES
    cc = jnp.concatenate([c, c_ctx[None, :], jnp.zeros((mod_rows - bsz - 1, d), F32)], axis=0)
    mods = _modulation(cc, w_mod, b_mod).reshape(depth, mod_rows, 6, d)
    rope = _rope_tables(n_lat, n_ctx)
    xs = jnp.concatenate([x, ctx], axis=1)

    for l in range(depth):
        lw = _layer_weights(l, p)
        mod = mods[l]
        hx, mq, mk, mv, o, g, pc, q, k, v = _in_proj(xs, mod, lw, rope, n_lat_tiles)
        hf, hb = _mlstm(mq, mk, mv, g, n_lat // ML_CHUNK)
        y_mla = _attention(q, k, v, n_lat)
        xs, h2 = _merge(hx, hf, hb, o, y_mla, pc, xs, mod, lw, n_lat_tiles)
        xs = _moe(h2.reshape(bsz * s, d), xs.reshape(bsz * s, d), mod, lw, s // MOE_TILE, n_ctx,
                  bsz).reshape(bsz, s, d)
    return xs[:, :n_lat]
```

```python
import functools
import math

import jax
import jax.numpy as jnp
from jax import lax
from jax.experimental import pallas as pl
from jax.experimental.pallas import tpu as pltpu

GRID_W = 64
BRANCH_W = 256
EPS = 1e-6
ML_HEADS = 4
ML_DH = BRANCH_W // ML_HEADS
ML_CHUNK = 128
MLA_HEADS = 4
Q_LORA = 256
KV_LORA = 128
QK_NOPE = 64
QK_ROPE = 32
QK_DIM = QK_NOPE + QK_ROPE
V_DIM = BRANCH_W // MLA_HEADS
V_SLOT = V_DIM + 16
ROPE_THETA = 10000.0
POOL_WINDOWS = (2, 4, 8, 16)
POOL_GROUP = BRANCH_W // len(POOL_WINDOWS)
POOL_HALO = max(POOL_WINDOWS) // 2
N_EXPERTS = 64
TOP_K = 6
EXPERT_FF = 256
ROUTE_SCALE = 2.5

LANES = 128
SUBLANES = 8
HEAD_SLOT = LANES
ROW_TILE = 256
ATT_TQ = 256
ATT_TK = 2048
ATT_SUB = 128
ATT_AHEAD = 8
EXPERT_ROWS = 1024
MOE_TILE = 768
RUN_ROWS = 16
MOE_CHUNK = 512
VMEM_LIMIT = 56 * 1024 * 1024

F32 = jnp.float32
BF16 = jnp.bfloat16
NT_DIMS = (((1,), (1,)), ((), ()))
TN_DIMS = (((0,), (0,)), ((), ()))


def _cparams(*sem):
    return pltpu.CompilerParams(dimension_semantics=sem, vmem_limit_bytes=VMEM_LIMIT)


def _const_spec(shape):
    nd = len(shape)
    return pl.BlockSpec(shape, lambda *_: (0,) * nd, pipeline_mode=pl.Buffered(1))


def _dot(a, b):
    return jnp.dot(a, b, preferred_element_type=F32)


def _split_dot(a_f32, ones_bf16):
    hi = a_f32.astype(BF16)
    r1 = a_f32 - hi.astype(F32)
    mid = r1.astype(BF16)
    lo = (r1 - mid.astype(F32)).astype(BF16)
    return _dot(hi, ones_bf16) + _dot(mid, ones_bf16) + _dot(lo, ones_bf16)


def _rms_rows(x):
    return x * lax.rsqrt(jnp.mean(x * x, axis=-1, keepdims=True) + EPS)


def _mod_kernel(c_ref, w_ref, b_ref, o_ref):
    c = c_ref[...]
    a = (c * jax.nn.sigmoid(c)).astype(BF16)
    o_ref[0] = _dot(a, w_ref[0].astype(BF16)) + b_ref[0]


def _modulation(cc, w_mod, b_mod):
    depth, d, d6 = w_mod.shape
    rows = cc.shape[0]
    tn = 1536
    return pl.pallas_call(
        _mod_kernel,
        grid=(depth, d6 // tn),
        in_specs=[
            pl.BlockSpec((rows, d), lambda l, n: (0, 0)),
            pl.BlockSpec((1, d, tn), lambda l, n: (l, 0, n)),
            pl.BlockSpec((1, 1, tn), lambda l, n: (l, 0, n)),
        ],
        out_specs=pl.BlockSpec((1, rows, tn), lambda l, n: (l, 0, n)),
        out_shape=jax.ShapeDtypeStruct((depth, rows, d6), F32),
        compiler_params=_cparams("parallel", "parallel"),
        name="modulation",
    )(cc, w_mod, b_mod.reshape(depth, 1, d6))


def _group_mean_sq(x, ones_bd, width):
    sq = x * x
    hi = sq.astype(BF16)
    lo = (sq - hi.astype(F32)).astype(BF16)
    return (_dot(hi, ones_bd) + _dot(lo, ones_bd)) * (1.0 / width)


def _in_kernel(x_ref, mod_ref, gmix_ref, w_ref, gb_ref, gcq_ref, gckv_ref, wuq_ref, wuk_ref, wuv_ref,
               gq_ref, gk_ref, ones_ref, cos_ref, sa_ref, sb_ref,
               hx_ref, mq_ref, mk_ref, mv_ref, mo_ref, g_ref, pc_ref, q_ref, k_ref, v_ref):
    x = x_ref[0]
    shift = mod_ref[0, 0:1, :]
    scale = mod_ref[0, 1:2, :]
    hx = (_rms_rows(x) * gmix_ref[...]) * (1.0 + scale) + shift
    hxb = hx.astype(BF16)
    hx_ref[0] = hxb
    p = _dot(hxb, w_ref[...])
    bw = BRANCH_W
    mq_ref[0] = p[:, 0:bw].T.astype(BF16)
    mk_ref[0] = (p[:, bw:2 * bw] * (ML_DH ** -0.5)).astype(BF16)
    mv_ref[0] = p[:, 2 * bw:3 * bw].T.astype(BF16)
    mo_ref[0] = p[:, 3 * bw:4 * bw].T.astype(BF16)
    pc_ref[0] = p[:, 1024:2048]
    cq = p[:, 2048:2048 + Q_LORA]
    ckv = p[:, 2304:2304 + KV_LORA]
    misc = p[:, 2432:2432 + LANES]
    g_ref[0] = (misc + gb_ref[...]).T[QK_ROPE:QK_ROPE + 4 * ML_HEADS]
    lane = lax.broadcasted_iota(jnp.int32, misc.shape, 1)
    kr_slot = jnp.where(jnp.logical_and(lane >= QK_NOPE, lane < QK_DIM), pltpu.roll(misc, QK_NOPE, 1), 0.0)
    kr4 = jnp.concatenate([kr_slot] * MLA_HEADS, axis=1)

    cqn = (_rms_rows(cq) * gcq_ref[...]).astype(BF16)
    ckvn = (_rms_rows(ckv) * gckv_ref[...]).astype(BF16)
    q_pre = _dot(cqn, wuq_ref[...])
    k_pre = _dot(ckvn, wuk_ref[...]) + kr4
    v_t = _dot(ckvn, wuv_ref[...]).T
    tm = v_t.shape[1]
    extra = V_SLOT - V_DIM
    one_row = jnp.where(lax.broadcasted_iota(jnp.int32, (extra, tm), 0) == 0, 1.0, 0.0)
    v_ref[0] = jnp.concatenate(
        sum([[v_t[h * V_DIM:(h + 1) * V_DIM], one_row] for h in range(MLA_HEADS)], []), axis=0).astype(BF16)

    ones_bd = ones_ref[...]
    cos = jnp.concatenate([cos_ref[...]] * MLA_HEADS, axis=1)
    sa = jnp.concatenate([sa_ref[...]] * MLA_HEADS, axis=1)
    sb = jnp.concatenate([sb_ref[...]] * MLA_HEADS, axis=1)
    width = MLA_HEADS * HEAD_SLOT
    half = QK_ROPE // 4

    def norm_rope(t, gain):
        t = t * lax.rsqrt(_group_mean_sq(t, ones_bd, QK_DIM) + EPS) * gain
        return t * cos + pltpu.roll(t, width - half, 1) * sa + pltpu.roll(t, half, 1) * sb

    q = norm_rope(q_pre, gq_ref[...]) * (QK_DIM ** -0.5 * math.log2(math.e))
    q_ref[0] = q.T.astype(BF16)
    k_ref[0] = norm_rope(k_pre, gk_ref[...]).astype(BF16)


def _in_proj(xs, mod, lw, rope, n_lat_tiles):
    bsz, s, d = xs.shape
    tm = ROW_TILE
    nt = s // tm
    wcols = lw["w_big"].shape[1]
    slot_w = MLA_HEADS * HEAD_SLOT

    def row_spec(width):
        return pl.BlockSpec((1, tm, width), lambda t, b: (b, t, 0))

    def tab_spec():
        return pl.BlockSpec((tm, HEAD_SLOT), lambda t, b: (t, 0))

    def col_spec(height):
        return pl.BlockSpec((1, height, tm), lambda t, b: (b, 0, t))

    v_rows = MLA_HEADS * V_SLOT
    outs = [("row", d, BF16), ("col", BRANCH_W, BF16), ("row", BRANCH_W, BF16), ("col", BRANCH_W, BF16),
            ("col", BRANCH_W, BF16), ("col", 4 * ML_HEADS, F32), ("row", 1024, F32),
            ("col", slot_w, BF16), ("row", slot_w, BF16), ("col", v_rows, BF16)]
    out_specs = [row_spec(w) if kind == "row" else col_spec(w) for kind, w, _ in outs]
    out_shape = [jax.ShapeDtypeStruct((bsz, s, w) if kind == "row" else (bsz, w, s), dt) for kind, w, dt in outs]
    return pl.pallas_call(
        _in_kernel,
        grid=(nt, bsz),
        in_specs=[
            row_spec(d),
            pl.BlockSpec((1, 6, d), lambda t, b: (jnp.where(t >= n_lat_tiles, bsz, b), 0, 0)),
            _const_spec((1, d)),
            _const_spec((d, wcols)),
            _const_spec((1, LANES)),
            _const_spec((1, Q_LORA)),
            _const_spec((1, KV_LORA)),
            _const_spec((Q_LORA, slot_w)),
            _const_spec((KV_LORA, slot_w)),
            _const_spec((KV_LORA, BRANCH_W)),
            _const_spec((1, slot_w)),
            _const_spec((1, slot_w)),
            _const_spec((slot_w, slot_w)),
            tab_spec(), tab_spec(), tab_spec(),
        ],
        out_specs=out_specs,
        out_shape=out_shape,
        compiler_params=_cparams("parallel", "parallel"),
        name="norm_in_proj",
    )(xs, mod, lw["g_mix"], lw["w_big"], lw["gate_bias"], lw["g_cq"], lw["g_ckv"], lw["w_uq"], lw["w_uk"], lw["w_uv"],
      lw["g_qn"], lw["g_kn"], rope["ones_slot"], rope["cos"], rope["sin_a"], rope["sin_b"])


def _log_sigmoid(x):
    return jnp.minimum(x, 0.0) - jnp.log1p(jnp.exp(-jnp.abs(x)))


def _mlstm_kernel(qt_f, k_f, vt_f, g_f, qt_b, k_b, vt_b, g_b, hf_ref, hb_ref, c_sc, m_sc):
    @pl.when(pl.program_id(1) == 0)
    def _():
        c_sc[...] = jnp.zeros_like(c_sc)
        m_sc[...] = jnp.zeros_like(m_sc)

    L = ML_CHUNK
    dh = ML_DH
    row = lax.broadcasted_iota(jnp.int32, (L, L), 0)
    col = lax.broadcasted_iota(jnp.int32, (L, L), 1)
    diag = row == col
    ones_ll = jnp.ones((L, L), BF16)
    one_rows = jnp.where(lax.broadcasted_iota(jnp.int32, (LANES - dh, L), 0) == 0, 1.0, 0.0).astype(BF16)

    dirs = ((qt_f, k_f, vt_f, g_f), (qt_b, k_b, vt_b, g_b))
    units = []
    for d, (qt_ref, k_ref, vt_ref, g_ref) in enumerate(dirs):
        a = g_ref[0]
        lf = _log_sigmoid(a)
        valid = (row <= col) if d == 0 else (row >= col)
        b_rows = _split_dot(lf, valid.astype(BF16))
        last = L - 1 if d == 0 else 0
        for h in range(ML_HEADS):
            ci = d * 2 * ML_HEADS + h
            cf = ci + ML_HEADS
            st = d * ML_HEADS + h
            u = {"st": st, "valid": valid}
            bt = b_rows[cf:cf + 1, :]
            li = a[ci:ci + 1, :]
            b_end = bt[:, last:last + 1]
            m_st = m_sc[st:st + 1, 0:1]
            qt = qt_ref[0, h * dh:(h + 1) * dh, :]
            k = k_ref[0, :, h * dh:(h + 1) * dh]
            u["vt"] = vt_ref[0, h * dh:(h + 1) * dh, :]
            u["bt"] = bt
            u["inter"] = bt + m_st
            src = jnp.where(diag, bt - li, 0.0)
            hi = src.astype(BF16)
            lo = (src - hi.astype(F32)).astype(BF16)
            u["src"] = _dot(hi, ones_ll) + _dot(lo, ones_ll)
            w_log = b_end - bt + li
            u["m_new"] = jnp.maximum(b_end + m_st, jnp.max(w_log, axis=1, keepdims=True))
            u["decay"] = jnp.exp(b_end + m_st - u["m_new"])
            v_aug = jnp.concatenate([u["vt"], one_rows], axis=0)
            vw = (v_aug.astype(F32) * jnp.exp(w_log - u["m_new"])).astype(BF16)
            u["s_kq"] = _dot(k, qt)
            u["qc"] = _dot(c_sc[st].astype(BF16), qt)
            u["upd"] = _dot(vw, k)
            units.append(u)

    for u in units:
        d_log = jnp.where(u["valid"], u["bt"] - u["src"], -jnp.inf)
        u["m_t"] = jnp.maximum(u["inter"], jnp.max(d_log, axis=0, keepdims=True))
        w_st = jnp.exp(d_log - u["m_t"]) * u["s_kq"]
        u["w_sum"] = jnp.sum(w_st, axis=0, keepdims=True)
        u["pv"] = _dot(u["vt"], w_st.astype(BF16))

    outs = []
    for u in units:
        a_inter = jnp.exp(u["inter"] - u["m_t"])
        num = a_inter * u["qc"][0:dh, :] + u["pv"]
        den = a_inter * u["qc"][dh:dh + 1, :] + u["w_sum"]
        outs.append(num / jnp.maximum(jnp.abs(den), jnp.exp(-u["m_t"])))
        st = u["st"]
        c_sc[st] = u["decay"] * c_sc[st] + u["upd"]
        m_sc[st:st + 1, :] = jnp.broadcast_to(u["m_new"], (1, LANES))
    hf_ref[0] = jnp.concatenate(outs[0:ML_HEADS], axis=0)
    hb_ref[0] = jnp.concatenate(outs[ML_HEADS:], axis=0)


def _mlstm(qt, k, vt, gt, n_lat_chunks):
    bsz, s, _ = k.shape
    nc = s // ML_CHUNK

    def fwd_chunk(j):
        return (j + n_lat_chunks) % nc

    def bwd_chunk(j):
        return nc - 1 - j

    def specs(chunk):
        def col(height):
            return pl.BlockSpec((1, height, ML_CHUNK), lambda b, j: (b, 0, chunk(j)))
        return [col(BRANCH_W), pl.BlockSpec((1, ML_CHUNK, BRANCH_W), lambda b, j: (b, chunk(j), 0)),
                col(BRANCH_W), col(4 * ML_HEADS)], col(BRANCH_W)

    in_f, out_f = specs(fwd_chunk)
    in_b, out_b = specs(bwd_chunk)
    return pl.pallas_call(
        _mlstm_kernel,
        grid=(bsz, nc),
        in_specs=in_f + in_b,
        out_specs=[out_f, out_b],
        out_shape=[jax.ShapeDtypeStruct((bsz, BRANCH_W, s), F32)] * 2,
        scratch_shapes=[pltpu.VMEM((2 * ML_HEADS, LANES, ML_DH), F32),
                        pltpu.VMEM((2 * ML_HEADS, LANES), F32)],
        compiler_params=_cparams("parallel", "arbitrary"),
        name="mlstm_scan",
    )(qt, k, vt, gt, qt, k, vt, gt)


def _attn_kernel(qt_ref, k_ref, vt_ref, o_ref, m_sc, acc_sc, *, n_lat, n_ctx, tk):
    m_sc[...] = jnp.full_like(m_sc, -jnp.inf)
    acc_sc[...] = jnp.zeros_like(acc_sc)

    def scores(h, start, size):
        sl = slice(h * HEAD_SLOT, (h + 1) * HEAD_SLOT)
        return _dot(k_ref[0, pl.ds(start, size), sl], qt_ref[0, sl, :])

    def keys(start, size):
        n_sub = size // ATT_SUB
        order = [(j, h) for j in range(n_sub) for h in range(MLA_HEADS)]
        pending = [scores(h, start + j * ATT_SUB, ATT_SUB) for j, h in order[:ATT_AHEAD]]
        for i, (j, h) in enumerate(order):
            off = start + j * ATT_SUB
            s = pending.pop(0)
            if i + ATT_AHEAD < len(order):
                jn, hn = order[i + ATT_AHEAD]
                pending.append(scores(hn, start + jn * ATT_SUB, ATT_SUB))
            m_old = m_sc[h]
            m_new = jnp.maximum(m_old, jnp.max(s, axis=0, keepdims=True))
            p = jnp.exp2(s - m_new)
            vt = vt_ref[0, h * V_SLOT:(h + 1) * V_SLOT, pl.ds(off, ATT_SUB)]
            acc_sc[h] = jnp.exp2(m_old - m_new) * acc_sc[h] + _dot(vt, p.astype(BF16))
            m_sc[h] = m_new

    keys(n_lat, n_ctx)

    def chunk(c, carry):
        keys(pl.multiple_of(c * tk, tk), tk)
        return carry

    is_ctx_tile = pl.program_id(1) * ATT_TQ >= n_lat
    lax.fori_loop(0, jnp.where(is_ctx_tile, 0, n_lat // tk), chunk, 0)
    out_t = jnp.concatenate([acc_sc[h, 0:V_DIM, :] / acc_sc[h, V_DIM:V_DIM + 1, :] for h in range(MLA_HEADS)],
                            axis=0)
    o_ref[0] = out_t.T.astype(o_ref.dtype)


def _key_chunk(n_keys):
    return max(t for t in range(ATT_TQ, ATT_TK + 1, ATT_TQ) if n_keys % t == 0)


def _attention(qt, k, vt, n_lat):
    bsz, s, slot_w = k.shape
    tq = ATT_TQ
    return pl.pallas_call(
        functools.partial(_attn_kernel, n_lat=n_lat, n_ctx=s - n_lat, tk=_key_chunk(n_lat)),
        grid=(bsz, s // tq),
        in_specs=[pl.BlockSpec((1, slot_w, tq), lambda b, t: (b, 0, t)),
                  pl.BlockSpec((1, s, slot_w), lambda b, t: (b, 0, 0), pipeline_mode=pl.Buffered(1)),
                  pl.BlockSpec((1, MLA_HEADS * V_SLOT, s), lambda b, t: (b, 0, 0), pipeline_mode=pl.Buffered(1))],
        out_specs=pl.BlockSpec((1, tq, BRANCH_W), lambda b, t: (b, t, 0)),
        out_shape=jax.ShapeDtypeStruct((bsz, s, BRANCH_W), BF16),
        scratch_shapes=[pltpu.VMEM((MLA_HEADS, 1, tq), F32), pltpu.VMEM((MLA_HEADS, V_SLOT, tq), F32)],
        compiler_params=_cparams("parallel", "arbitrary"),
        name="mla_attention",
    )(qt, k, vt)


def _merge_kernel(hx_ref, hf_ref, hb_ref, o_ref, ya_ref, pc_ref, prev_ref, next_ref, x_ref, mod_ref,
                  hg_ref, pw_ref, ps_ref, cw_ref, cb_ref, wg_ref, bg_ref, wb_ref, wo_ref,
                  gffn_ref, xo_ref, h2_ref, *, n_lat_tiles, n_tiles):
    t = pl.program_id(1)
    tm = ROW_TILE
    bw = BRANCH_W
    halo = POOL_HALO
    no_prev = jnp.logical_or(t == 0, t == n_lat_tiles)
    no_next = jnp.logical_or(t == n_lat_tiles - 1, t == n_tiles - 1)

    h_t = hf_ref[0] + hb_ref[0]
    normed = []
    for hd in range(ML_HEADS):
        hh = h_t[hd * ML_DH:(hd + 1) * ML_DH]
        normed.append(hh * lax.rsqrt(jnp.mean(hh * hh, axis=0, keepdims=True) + EPS))
    y_ml = (jnp.concatenate(normed, axis=0) * hg_ref[...] * jax.nn.sigmoid(o_ref[0].astype(F32))).T

    prev = jnp.where(no_prev, 0.0, prev_ref[0])
    nxt = jnp.where(no_next, 0.0, next_ref[0])
    ext = jnp.concatenate([prev, pc_ref[0], nxt], axis=0)
    pe = ext[:, 0:bw]

    def rows(arr, off):
        return arr[halo + off:halo + off + tm, :]

    r = lax.broadcasted_iota(jnp.int32, (tm, 1), 0)
    lane = lax.broadcasted_iota(jnp.int32, (tm, bw), 1)
    centre = rows(pe, 0)
    acc = centre
    mean = jnp.zeros((tm, bw), F32)
    done = 0
    for gi, w in enumerate(POOL_WINDOWS):
        for off in list(range(-(w // 2), -done)) + list(range(max(done, 1), w // 2)):
            acc = acc + rows(pe, off)
        done = w // 2
        before = jnp.where(no_prev, jnp.minimum(r, w // 2), w // 2)
        after = jnp.where(no_next, jnp.minimum(tm - r, w // 2), w // 2)
        inv = 1.0 / (before + after).astype(F32)
        mean = jnp.where(lane >= gi * POOL_GROUP, acc * inv, mean)
    y_pool = _dot((mean - centre).astype(BF16), pw_ref[...]) * ps_ref[...]

    z = ext[:, 3 * bw:4 * bw] * ext[:, bw:2 * bw]
    conv = cb_ref[...] + rows(z, -1) * cw_ref[0:1, :] + rows(z, 0) * cw_ref[1:2, :] + rows(z, 1) * cw_ref[2:3, :]
    y_conv = rows(ext[:, 2 * bw:3 * bw], 0) * conv

    hxb = hx_ref[0]
    d = hxb.shape[1]
    gates = jax.nn.sigmoid(_dot(hxb, wg_ref[...]) + bg_ref[...])
    ys = (y_ml.astype(BF16), ya_ref[0], y_pool.astype(BF16), y_conv.astype(BF16))
    merged = None
    for i, y in enumerate(ys):
        term = gates[:, i * d:(i + 1) * d] * _dot(y, wb_ref[i])
        merged = term if merged is None else merged + term
    out = _dot(merged.astype(BF16), wo_ref[...])
    x_new = x_ref[0] + mod_ref[0, 2:3, :] * out
    xo_ref[0] = x_new
    h2_ref[0] = (_rms_rows(x_new) * gffn_ref[...]) * (1.0 + mod_ref[0, 4:5, :]) + mod_ref[0, 3:4, :]


def _merge(hx, hf, hb, o, y_mla, pc, xs, mod, lw, n_lat_tiles):
    bsz, s, d = xs.shape
    tm = ROW_TILE
    nt = s // tm
    halo = POOL_HALO
    per = tm // halo

    def row_spec(width):
        return pl.BlockSpec((1, tm, width), lambda b, t: (b, t, 0))

    def col_spec():
        return pl.BlockSpec((1, BRANCH_W, tm), lambda b, t: (b, 0, t))

    return pl.pallas_call(
        functools.partial(_merge_kernel, n_lat_tiles=n_lat_tiles, n_tiles=nt),
        grid=(bsz, nt),
        in_specs=[
            row_spec(d), col_spec(), col_spec(), col_spec(), row_spec(BRANCH_W),
            row_spec(1024),
            pl.BlockSpec((1, halo, 1024), lambda b, t: (b, jnp.maximum(t * per - 1, 0), 0)),
            pl.BlockSpec((1, halo, 1024), lambda b, t: (b, jnp.minimum((t + 1) * per, s // halo - 1), 0)),
            row_spec(d),
            pl.BlockSpec((1, 6, d), lambda b, t: (jnp.where(t >= n_lat_tiles, bsz, b), 0, 0)),
            _const_spec((BRANCH_W, tm)), _const_spec((BRANCH_W, BRANCH_W)),
            _const_spec((1, BRANCH_W)), _const_spec((3, BRANCH_W)), _const_spec((1, BRANCH_W)),
            _const_spec((d, 4 * d)), _const_spec((1, 4 * d)), _const_spec((4, BRANCH_W, d)),
            _const_spec((d, d)), _const_spec((1, d)),
        ],
        out_specs=[row_spec(d), row_spec(d)],
        out_shape=[jax.ShapeDtypeStruct((bsz, s, d), F32)] * 2,
        compiler_params=_cparams("parallel", "parallel"),
        name="mixer_merge",
    )(hx, hf, hb, o, y_mla, pc, pc, pc, xs, mod, lw["head_gain"], lw["pool_w"], lw["pool_scale"],
      lw["conv_w"], lw["conv_b"], lw["w_gate"], lw["b_gate"], lw["w_branch"], lw["w_out"], lw["g_ffn"])


def _router_kernel(h_ref, wr_ref, br_ref, pe_ref, we_ref, c16_ref):
    tm = h_ref.shape[0]
    scores = jax.nn.sigmoid(_dot(h_ref[...].astype(BF16), wr_ref[...]))
    sel = scores + br_ref[...]
    lane = lax.broadcasted_iota(jnp.int32, (tm, LANES), 1)
    member = jnp.zeros((tm, LANES), F32)
    for _ in range(TOP_K):
        mx = jnp.max(sel, axis=1, keepdims=True)
        ix = jnp.min(jnp.where(sel == mx, lane, LANES), axis=1, keepdims=True)
        hit = lane == ix
        member = jnp.where(hit, 1.0, member)
        sel = jnp.where(hit, -jnp.inf, sel)
    picked = member * scores
    weights = picked / jnp.sum(picked, axis=1, keepdims=True) * ROUTE_SCALE
    counts = jnp.sum(member, axis=0, keepdims=True)
    c16 = jnp.floor((counts + (RUN_ROWS - 1)) * (1.0 / RUN_ROWS))
    e_row = lax.broadcasted_iota(jnp.int32, (LANES, LANES), 0)
    e_col = lax.broadcasted_iota(jnp.int32, (LANES, LANES), 1)
    lower = (e_row < e_col).astype(BF16)
    o16 = _dot(jnp.broadcast_to(c16, (SUBLANES, LANES)).astype(BF16), lower)[0:1]
    row = lax.broadcasted_iota(jnp.int32, (tm, tm), 0)
    col = lax.broadcasted_iota(jnp.int32, (tm, tm), 1)
    rank = _dot((col < row).astype(BF16), member.astype(BF16))
    pos = jnp.where(member > 0.0, RUN_ROWS * o16 + rank, -1.0)
    pe_ref[0] = pos.T[0:N_EXPERTS]
    we_ref[0] = weights.T[0:N_EXPERTS]
    c16_ref[0] = c16.astype(jnp.int32)


def _router(h2, lw):
    n, d = h2.shape
    tm = MOE_TILE
    nt = n // tm

    def t_spec():
        return pl.BlockSpec((1, N_EXPERTS, tm), lambda i: (i, 0, 0))

    return pl.pallas_call(
        _router_kernel,
        grid=(nt,),
        in_specs=[pl.BlockSpec((tm, d), lambda i: (i, 0)), _const_spec((d, LANES)), _const_spec((1, LANES))],
        out_specs=[t_spec(), t_spec(), pl.BlockSpec((1, 1, LANES), lambda i: (i, 0, 0))],
        out_shape=[jax.ShapeDtypeStruct((nt, N_EXPERTS, tm), F32), jax.ShapeDtypeStruct((nt, N_EXPERTS, tm), F32),
                   jax.ShapeDtypeStruct((nt, 1, LANES), jnp.int32)],
        compiler_params=_cparams("parallel"),
        name="moe_router",
    )(h2, lw["w_router"], lw["b_router"])


def _tile_row_bound(tm):
    worst = tm * TOP_K + N_EXPERTS * (RUN_ROWS - 1)
    return -(-worst // MOE_CHUNK) * MOE_CHUNK


def _selection_rows(g, grp_e_ref, pe_ref, tm):
    prow = pe_ref[0, pl.ds(grp_e_ref[0, 0, g], 1), :]
    rows = (lax.broadcasted_iota(jnp.int32, (RUN_ROWS, tm), 0) + g * RUN_ROWS).astype(F32)
    return prow, rows


def _dispatch_kernel(tail_ref, n_act_ref, grp_e_ref, gdst_ref, h_ref, pe_ref, xs_ref,
                     sel_sc, xb_sc, xp_sc, zero_sc, sem, *, n_groups):
    i = pl.program_id(0)
    tm = h_ref.shape[0]
    n_chunks = sel_sc.shape[0] // MOE_CHUNK
    per_chunk = MOE_CHUNK // RUN_ROWS

    @pl.when(i == pl.num_programs(0) - 1)
    def _():
        zero_sc[...] = jnp.zeros_like(zero_sc)

        def fill_copy(q):
            return pltpu.make_async_copy(
                zero_sc.at[pl.ds(0, RUN_ROWS)],
                xs_ref.at[pl.ds(pl.multiple_of(tail_ref[q] * RUN_ROWS, RUN_ROWS), RUN_ROWS)], sem)

        def block_copy(b):
            return pltpu.make_async_copy(
                zero_sc, xs_ref.at[pl.ds(pl.multiple_of(b * EXPERT_ROWS, EXPERT_ROWS), EXPERT_ROWS)], sem)

        def block_start(b, carry):
            block_copy(b).start()
            return carry

        def block_wait(b, carry):
            block_copy(b).wait()
            return carry

        n_blocks = xs_ref.shape[0] // EXPERT_ROWS
        lax.fori_loop(n_act_ref[0], n_blocks, block_start, 0)
        lax.fori_loop(n_act_ref[0], n_blocks, block_wait, 0)

        def fill_start(q, carry):
            @pl.when(tail_ref[q] >= 0)
            def _():
                fill_copy(q).start()
            return carry

        def fill_wait(q, carry):
            @pl.when(tail_ref[q] >= 0)
            def _():
                fill_copy(q).wait()
            return carry

        lax.fori_loop(0, tail_ref.shape[0], fill_start, 0)
        lax.fori_loop(0, tail_ref.shape[0], fill_wait, 0)

    def build(g, carry):
        prow, rows = _selection_rows(g, grp_e_ref, pe_ref, tm)
        sel_sc[pl.ds(pl.multiple_of(g * RUN_ROWS, RUN_ROWS), RUN_ROWS), :] = jnp.where(
            prow == rows, 1.0, 0.0).astype(BF16)
        return carry

    lax.fori_loop(0, n_groups, build, 0, unroll=4)
    xb_sc[...] = h_ref[...].astype(BF16)

    def run_copy(g):
        return pltpu.make_async_copy(
            xp_sc.at[pl.ds(pl.multiple_of(g * RUN_ROWS, RUN_ROWS), RUN_ROWS)],
            xs_ref.at[pl.ds(pl.multiple_of(gdst_ref[0, 0, g] * RUN_ROWS, RUN_ROWS), RUN_ROWS)], sem)

    def permute(ch):
        r0 = pl.multiple_of(ch * MOE_CHUNK, MOE_CHUNK)
        xp_sc[pl.ds(r0, MOE_CHUNK), :] = _dot(sel_sc[pl.ds(r0, MOE_CHUNK), :], xb_sc[...]).astype(BF16)

    def start_chunk(ch):
        for j in range(per_chunk):
            run_copy(ch * per_chunk + j).start()

    permute(0)

    def step(ch, carry):
        start_chunk(ch - 1)
        permute(ch)
        return carry

    lax.fori_loop(1, n_chunks, step, 0)
    start_chunk(n_chunks - 1)
    pltpu.make_async_copy(xp_sc, xs_ref.at[pl.ds(0, xp_sc.shape[0])], sem).wait()


def _dispatch(h2, pe, plan, n_rows):
    n, d = h2.shape
    tm = MOE_TILE
    nt = n // tm
    rb = _tile_row_bound(tm)
    n_groups = rb // RUN_ROWS

    def smem_spec():
        return pl.BlockSpec((1, 1, n_groups), lambda i, *_: (i, 0, 0), memory_space=pltpu.SMEM)

    return pl.pallas_call(
        functools.partial(_dispatch_kernel, n_groups=n_groups),
        grid_spec=pltpu.PrefetchScalarGridSpec(
            num_scalar_prefetch=2,
            grid=(nt,),
            in_specs=[smem_spec(), smem_spec(),
                      pl.BlockSpec((tm, d), lambda i, *_: (i, 0)),
                      pl.BlockSpec((1, N_EXPERTS, tm), lambda i, *_: (i, 0, 0))],
            out_specs=pl.BlockSpec(memory_space=pl.ANY),
            scratch_shapes=[pltpu.VMEM((rb, tm), BF16), pltpu.VMEM((tm, d), BF16), pltpu.VMEM((rb, d), BF16),
                            pltpu.VMEM((EXPERT_ROWS, d), BF16), pltpu.SemaphoreType.DMA(())],
        ),
        out_shape=jax.ShapeDtypeStruct((n_rows, d), BF16),
        compiler_params=_cparams("arbitrary"),
        name="moe_dispatch",
    )(plan["tail"], plan["n_act"].reshape(1), plan["grp_e"], plan["gdst"], h2, pe)


def _expert_kernel(blk_e_ref, n_act_ref, x_ref, w1_ref, w3_ref, w2_ref, y_ref):
    del blk_e_ref

    @pl.when(pl.program_id(0) < n_act_ref[0])
    def _():
        x = x_ref[...]
        a = _dot(x, w1_ref[0, 0].astype(BF16))
        act = (a * jax.nn.sigmoid(a)) * _dot(x, w3_ref[0, 0].astype(BF16))
        y_ref[...] = _dot(act.astype(BF16), w2_ref[0, 0].astype(BF16)).astype(BF16)

    @pl.when(pl.program_id(0) >= n_act_ref[0])
    def _():
        y_ref[...] = jnp.zeros_like(y_ref)


def _experts(xs_sorted, blk_e, n_act, lw):
    n_rows, d = xs_sorted.shape
    bm = EXPERT_ROWS

    def row_map(i, blk_e_ref, n_act_ref):
        return (jnp.minimum(i, n_act_ref[0] - 1), 0)

    layer = lw["layer"]

    def w_map(i, blk_e_ref, n_act_ref):
        return (layer, blk_e_ref[i], 0, 0)

    return pl.pallas_call(
        _expert_kernel,
        grid_spec=pltpu.PrefetchScalarGridSpec(
            num_scalar_prefetch=2,
            grid=(n_rows // bm,),
            in_specs=[pl.BlockSpec((bm, d), row_map),
                      pl.BlockSpec((1, 1, d, EXPERT_FF), w_map),
                      pl.BlockSpec((1, 1, d, EXPERT_FF), w_map),
                      pl.BlockSpec((1, 1, EXPERT_FF, d), w_map)],
            out_specs=pl.BlockSpec((bm, d), lambda i, *_: (i, 0)),
        ),
        out_shape=jax.ShapeDtypeStruct((n_rows, d), BF16),
        compiler_params=_cparams("arbitrary"),
        name="moe_experts",
    )(blk_e, n_act, xs_sorted, lw["w1"], lw["w3"], lw["w2"])


def _combine_kernel(grp_e_ref, gsrc_ref, h_ref, x_ref, pe_ref, we_ref, mod_ref, modc_ref, ws13_ref,
                    ws2_ref, ys_ref, xo_ref, sel_sc, yp_sc, acc_sc, sem, *, n_groups, tiles_per_sample, n_ctx):
    i = pl.program_id(0)
    tm = h_ref.shape[0]

    n_chunks = sel_sc.shape[0] // MOE_CHUNK
    per_chunk = MOE_CHUNK // RUN_ROWS

    def start_chunk(ch):
        for j in range(per_chunk):
            g = ch * per_chunk + j
            pltpu.make_async_copy(
                ys_ref.at[pl.ds(pl.multiple_of(gsrc_ref[0, 0, g] * RUN_ROWS, RUN_ROWS), RUN_ROWS)],
                yp_sc.at[pl.ds(pl.multiple_of(g * RUN_ROWS, RUN_ROWS), RUN_ROWS)], sem).start()

    def wait_chunk(ch):
        r0 = pl.multiple_of(ch * MOE_CHUNK, MOE_CHUNK)
        pltpu.make_async_copy(ys_ref.at[pl.ds(0, MOE_CHUNK)], yp_sc.at[pl.ds(r0, MOE_CHUNK)], sem).wait()

    start_chunk(0)

    def build(g, carry):
        prow, rows = _selection_rows(g, grp_e_ref, pe_ref, tm)
        wrow = we_ref[0, pl.ds(grp_e_ref[0, 0, g], 1), :]
        sel_sc[pl.ds(pl.multiple_of(g * RUN_ROWS, RUN_ROWS), RUN_ROWS), :] = jnp.where(
            prow == rows, wrow, 0.0).astype(BF16)
        return carry

    lax.fori_loop(0, n_groups, build, 0, unroll=4)

    up = _dot(h_ref[...].astype(BF16), ws13_ref[...])
    a = up[:, 0:EXPERT_FF]
    act = (a * jax.nn.sigmoid(a)) * up[:, EXPERT_FF:2 * EXPERT_FF]
    acc_sc[...] = _dot(act.astype(BF16), ws2_ref[...])

    def gather_sum(ch):
        r0 = pl.multiple_of(ch * MOE_CHUNK, MOE_CHUNK)
        acc_sc[...] += lax.dot_general(sel_sc[pl.ds(r0, MOE_CHUNK), :], yp_sc[pl.ds(r0, MOE_CHUNK), :], TN_DIMS,
                                       preferred_element_type=F32)

    def step(ch, carry):
        wait_chunk(ch)
        start_chunk(ch + 1)
        gather_sum(ch)
        return carry

    lax.fori_loop(0, n_chunks - 1, step, 0)
    wait_chunk(n_chunks - 1)
    gather_sum(n_chunks - 1)
    r = lax.broadcasted_iota(jnp.int32, (tm, 1), 0)
    is_ctx = jnp.logical_and(i % tiles_per_sample == tiles_per_sample - 1, r >= tm - n_ctx)
    gate = jnp.where(is_ctx, modc_ref[0, 5:6, :], mod_ref[0, 5:6, :])
    xo_ref[...] = x_ref[...] + gate * acc_sc[...]


def _combine(h2, xs, pe, we, plan, mod, ys_sorted, lw, tiles_per_sample, n_ctx, bsz):
    n, d = h2.shape
    tm = MOE_TILE
    nt = n // tm
    rb = _tile_row_bound(tm)
    n_groups = rb // RUN_ROWS

    def smem_spec():
        return pl.BlockSpec((1, 1, n_groups), lambda i, *_: (i, 0, 0), memory_space=pltpu.SMEM)

    def row_spec():
        return pl.BlockSpec((tm, d), lambda i, *_: (i, 0))

    def t_spec():
        return pl.BlockSpec((1, N_EXPERTS, tm), lambda i, *_: (i, 0, 0))

    return pl.pallas_call(
        functools.partial(_combine_kernel, n_groups=n_groups, tiles_per_sample=tiles_per_sample, n_ctx=n_ctx),
        grid_spec=pltpu.PrefetchScalarGridSpec(
            num_scalar_prefetch=0,
            grid=(nt,),
            in_specs=[smem_spec(), smem_spec(), row_spec(), row_spec(), t_spec(), t_spec(),
                      pl.BlockSpec((1, 6, d), lambda i, *_: (i // tiles_per_sample, 0, 0)),
                      pl.BlockSpec((1, 6, d), lambda i, *_: (bsz, 0, 0)),
                      pl.BlockSpec((d, 2 * EXPERT_FF), lambda i, *_: (0, 0), pipeline_mode=pl.Buffered(1)),
                      pl.BlockSpec((EXPERT_FF, d), lambda i, *_: (0, 0), pipeline_mode=pl.Buffered(1)),
                      pl.BlockSpec(memory_space=pl.ANY)],
            out_specs=row_spec(),
            scratch_shapes=[pltpu.VMEM((rb, tm), BF16), pltpu.VMEM((rb, d), BF16), pltpu.VMEM((tm, d), F32),
                            pltpu.SemaphoreType.DMA(())],
        ),
        out_shape=jax.ShapeDtypeStruct((n, d), F32),
        compiler_params=_cparams("arbitrary"),
        name="moe_combine",
    )(plan["grp_e"], plan["gsrc"], h2, xs, pe, we, mod, mod, lw["ws13"], lw["ws2"], ys_sorted)


def _moe_plan(c16, n_groups, spare16):
    nt = c16.shape[0]
    per_blk = EXPERT_ROWS // RUN_ROWS
    o16 = jnp.cumsum(c16, axis=1) - c16
    before16 = jnp.cumsum(c16, axis=0) - c16
    gtot16 = jnp.sum(c16, axis=0)
    gblk = (gtot16 + per_blk - 1) // per_blk
    gend_blk = jnp.cumsum(gblk)
    gstart16 = per_blk * (gend_blk - gblk)
    g = jnp.arange(n_groups, dtype=jnp.int32)
    grp_e = jnp.minimum(jnp.sum((o16 + c16)[:, None, :] <= g[None, :, None], axis=2), N_EXPERTS - 1)
    run0 = gstart16[None, :] + before16 - o16
    experts = jnp.arange(N_EXPERTS, dtype=jnp.int32)
    gdst = jnp.sum(jnp.where(grp_e[:, :, None] == experts[None, None, :], run0[:, None, :], 0), axis=2) + g[None, :]
    q = jnp.arange(per_blk, dtype=jnp.int32)
    tail = jnp.where(q[None, :] < (per_blk * gblk - gtot16)[:, None],
                     (gstart16 + gtot16)[:, None] + q[None, :], -1)
    used = g[None, :] < jnp.sum(c16, axis=1)[:, None]
    return {
        "grp_e": grp_e.astype(jnp.int32).reshape(nt, 1, n_groups),
        "gdst": jnp.where(used, gdst, spare16 + g[None, :]).astype(jnp.int32).reshape(nt, 1, n_groups),
        "gsrc": jnp.where(used, gdst, 0).astype(jnp.int32).reshape(nt, 1, n_groups),
        "tail": tail.astype(jnp.int32).reshape(-1),
        "gend_blk": gend_blk, "n_act": gend_blk[-1].astype(jnp.int32),
    }


def _moe(h2, xs, mod, lw, tiles_per_sample, n_ctx, bsz):
    n, d = h2.shape
    nt = n // MOE_TILE
    n_groups = _tile_row_bound(MOE_TILE) // RUN_ROWS
    pe, we, c16 = _router(h2, lw)
    worst_rows = n * TOP_K + nt * N_EXPERTS * (RUN_ROWS - 1) + N_EXPERTS * (EXPERT_ROWS - 1)
    run_blocks = -(-worst_rows // EXPERT_ROWS)
    n_blocks = run_blocks + -(-n_groups * RUN_ROWS // EXPERT_ROWS)
    plan = _moe_plan(c16[:, 0, :N_EXPERTS], n_groups, run_blocks * (EXPERT_ROWS // RUN_ROWS))
    blk = jnp.minimum(jnp.arange(n_blocks, dtype=jnp.int32), plan["n_act"] - 1)
    blk_e = jnp.minimum(jnp.sum(plan["gend_blk"][None, :] <= blk[:, None], axis=1), N_EXPERTS - 1).astype(jnp.int32)
    xs_sorted = _dispatch(h2, pe, plan, n_blocks * EXPERT_ROWS)
    ys_sorted = _experts(xs_sorted, blk_e, plan["n_act"].reshape(1), lw)
    return _combine(h2, xs, pe, we, plan, mod, ys_sorted, lw, tiles_per_sample, n_ctx, bsz)


def _slots(w, head_w, real_w):
    rows = w.shape[0]
    w = w.reshape(rows, -1, head_w)[:, :, :real_w]
    return jnp.pad(w, ((0, 0), (0, 0), (0, HEAD_SLOT - real_w))).reshape(rows, -1)


def _block_diag(blocks):
    n, r, c = blocks.shape
    out = jnp.zeros((n * r, n * c), blocks.dtype)
    for i in range(n):
        out = out.at[i * r:(i + 1) * r, i * c:(i + 1) * c].set(blocks[i])
    return out


def _layer_weights(l, p):
    d = p["w_in"].shape[1]
    w_in = p["w_in"][l]
    ml = 4 * BRANCH_W
    o_mla = ml + 4 * ML_HEADS
    o_pool = o_mla + Q_LORA + KV_LORA + QK_ROPE
    w_big = jnp.concatenate([
        w_in[:, :ml],
        w_in[:, o_pool:],
        w_in[:, o_mla:o_mla + Q_LORA + KV_LORA],
        w_in[:, o_mla + Q_LORA + KV_LORA:o_pool],
        w_in[:, ml:o_mla],
        jnp.zeros((d, LANES - QK_ROPE - 4 * ML_HEADS), F32),
    ], axis=1).astype(BF16)
    w_ukv = p["mla_w_ukv"][l].reshape(KV_LORA, MLA_HEADS, QK_NOPE + V_DIM)

    def gain_slots(g):
        return jnp.tile(jnp.pad(g, (0, HEAD_SLOT - QK_DIM)), MLA_HEADS)[None, :]

    return {
        "g_mix": p["g_mix"][l][None, :], "g_ffn": p["g_ffn"][l][None, :],
        "w_big": w_big,
        "g_cq": p["mla_g_cq"][l][None, :], "g_ckv": p["mla_g_ckv"][l][None, :],
        "w_uq": _slots(p["mla_w_uq"][l], QK_DIM, QK_DIM).astype(BF16),
        "w_uk": _slots(w_ukv[:, :, :QK_NOPE].reshape(KV_LORA, -1), QK_NOPE, QK_NOPE).astype(BF16),
        "w_uv": w_ukv[:, :, QK_NOPE:].reshape(KV_LORA, BRANCH_W).astype(BF16),
        "g_qn": gain_slots(p["mla_g_qn"][l]), "g_kn": gain_slots(p["mla_g_kn"][l]),
        "gate_bias": jnp.pad(p["ml_gate_bias"][l], (QK_ROPE, LANES - QK_ROPE - 4 * ML_HEADS))[None, :],
        "head_gain": jnp.broadcast_to(p["ml_head_gain"][l][:, None], (BRANCH_W, ROW_TILE)),
        "pool_w": _block_diag(p["pool_w"][l]).astype(BF16),
        "pool_scale": p["pool_scale"][l][None, :],
        "conv_w": p["conv_w"][l], "conv_b": p["conv_b"][l][None, :],
        "w_gate": jnp.concatenate(list(p["w_gate"][l]), axis=1).astype(BF16),
        "b_gate": p["b_gate"][l].reshape(1, -1),
        "w_branch": p["w_branch"][l].astype(BF16),
        "w_out": p["w_out"][l].astype(BF16),
        "w_router": jnp.pad(p["moe_w_router"][l], ((0, 0), (0, LANES - N_EXPERTS))).astype(BF16),
        "b_router": jnp.pad(p["moe_b_router"][l], (0, LANES - N_EXPERTS), constant_values=-1e30)[None, :],
        "layer": l, "w1": p["moe_w1"], "w3": p["moe_w3"], "w2": p["moe_w2"],
        "ws13": jnp.concatenate([p["moe_ws1"][l], p["moe_ws3"][l]], axis=1).astype(BF16),
        "ws2": p["moe_ws2"][l].astype(BF16),
    }


def _rope_tables(n_lat, n_ctx):
    t = jnp.arange(n_lat)
    n_freq = QK_ROPE // 4
    inv = ROPE_THETA ** (-jnp.arange(n_freq, dtype=F32) / n_freq)
    ang_r = (t // GRID_W).astype(F32)[:, None] * inv
    ang_c = (t % GRID_W).astype(F32)[:, None] * inv
    cos4 = jnp.concatenate([jnp.cos(ang_r)] * 2 + [jnp.cos(ang_c)] * 2, axis=1)
    zero = jnp.zeros_like(ang_r)
    sin_a = jnp.concatenate([-jnp.sin(ang_r), zero, -jnp.sin(ang_c), zero], axis=1)
    sin_b = jnp.concatenate([zero, jnp.sin(ang_r), zero, jnp.sin(ang_c)], axis=1)

    def slot(a, fill):
        a = jnp.pad(a, ((0, 0), (QK_NOPE, 0)), constant_values=fill)
        a = jnp.pad(a, ((0, 0), (0, HEAD_SLOT - QK_DIM)), constant_values=fill)
        return jnp.pad(a, ((0, n_ctx), (0, 0)), constant_values=fill)

    return {"cos": slot(cos4, 1.0), "sin_a": slot(sin_a, 0.0), "sin_b": slot(sin_b, 0.0),
            "ones_slot": _block_diag(jnp.ones((MLA_HEADS, HEAD_SLOT, HEAD_SLOT), BF16))}


def kernel(x, c, ctx, c_ctx, w_mod, b_mod, g_mix, g_ffn, w_in, ml_gate_bias, ml_head_gain, mla_g_cq, mla_g_ckv,
           mla_w_uq, mla_w_ukv, mla_g_qn, mla_g_kn, pool_w, pool_scale, conv_w, conv_b, w_gate, b_gate, w_branch,
           w_out, moe_w_router, moe_b_router, moe_w1, moe_w3, moe_w2, moe_ws1, moe_ws3, moe_ws2):
    p = dict(g_mix=g_mix, g_ffn=g_ffn, w_in=w_in, ml_gate_bias=ml_gate_bias, ml_head_gain=ml_head_gain,
             mla_g_cq=mla_g_cq, mla_g_ckv=mla_g_ckv, mla_w_uq=mla_w_uq, mla_w_ukv=mla_w_ukv, mla_g_qn=mla_g_qn,
             mla_g_kn=mla_g_kn, pool_w=pool_w, pool_scale=pool_scale, conv_w=conv_w, conv_b=conv_b,
             w_gate=w_gate, b_gate=b_gate, w_branch=w_branch, w_out=w_out, moe_w_router=moe_w_router,
             moe_b_router=moe_b_router, moe_w1=moe_w1, moe_w3=moe_w3, moe_w2=moe_w2, moe_ws1=moe_ws1,
             moe_ws3=moe_ws3, moe_ws2=moe_ws2)
    bsz, n_lat, d = x.shape
    n_ctx = ctx.shape[1]
    depth = w_mod.shape[0]
    s = n_lat + n_ctx
    assert n_ctx == ROW_TILE == ATT_TQ and n_lat % ROW_TILE == 0 and n_lat % GRID_W == 0
    assert s % MOE_TILE == 0 and n_ctx <= MOE_TILE
    n_lat_tiles = n_lat // ROW_TILE

    mod_rows = -(-(bsz + 1) // SUBLANES) * SUBLANES
    cc = jnp.concatenate([c, c_ctx[None, :], jnp.zeros((mod_rows - bsz - 1, d), F32)], axis=0)
    mods = _modulation(cc, w_mod, b_mod).reshape(depth, mod_rows, 6, d)
    rope = _rope_tables(n_lat, n_ctx)
    xs = jnp.concatenate([x, ctx], axis=1)

    for l in range(depth):
        lw = _layer_weights(l, p)
        mod = mods[l]
        hx, mq, mk, mv, o, g, pc, q, k, v = _in_proj(xs, mod, lw, rope, n_lat_tiles)
        hf, hb = _mlstm(mq, mk, mv, g, n_lat // ML_CHUNK)
        y_mla = _attention(q, k, v, n_lat)
        xs, h2 = _merge(hx, hf, hb, o, y_mla, pc, xs, mod, lw, n_lat_tiles)
        xs = _moe(h2.reshape(bsz * s, d), xs.reshape(bsz * s, d), mod, lw, s // MOE_TILE, n_ctx,
                  bsz).reshape(bsz, s, d)
    return xs[:, :n_lat]
```

```python
import functools
import math

import jax
import jax.numpy as jnp
from jax import lax
from jax.experimental import pallas as pl
from jax.experimental.pallas import tpu as pltpu

GRID_W = 64
BRANCH_W = 256
EPS = 1e-6
ML_HEADS = 4
ML_DH = BRANCH_W // ML_HEADS
ML_CHUNK = 128
MLA_HEADS = 4
Q_LORA = 256
KV_LORA = 128
QK_NOPE = 64
QK_ROPE = 32
QK_DIM = QK_NOPE + QK_ROPE
V_DIM = BRANCH_W // MLA_HEADS
V_SLOT = V_DIM + 16
ROPE_THETA = 10000.0
POOL_WINDOWS = (2, 4, 8, 16)
POOL_GROUP = BRANCH_W // len(POOL_WINDOWS)
POOL_HALO = max(POOL_WINDOWS) // 2
N_EXPERTS = 64
TOP_K = 6
EXPERT_FF = 256
ROUTE_SCALE = 2.5

LANES = 128
SUBLANES = 8
HEAD_SLOT = LANES
ROW_TILE = 256
ATT_TQ = 256
ATT_TK = 2048
ATT_SUB = 128
ATT_AHEAD = 8
EXPERT_ROWS = 1024
MOE_TILE_MAX = 768
RUN_ROWS = 16
MOE_CHUNK = 512
VMEM_LIMIT = 56 * 1024 * 1024

F32 = jnp.float32
BF16 = jnp.bfloat16
NT_DIMS = (((1,), (1,)), ((), ()))
TN_DIMS = (((0,), (0,)), ((), ()))


def _cparams(*sem):
    return pltpu.CompilerParams(dimension_semantics=sem, vmem_limit_bytes=VMEM_LIMIT)


def _const_spec(shape):
    nd = len(shape)
    return pl.BlockSpec(shape, lambda *_: (0,) * nd, pipeline_mode=pl.Buffered(1))


def _dot(a, b):
    return jnp.dot(a, b, preferred_element_type=F32)


def _split_dot(a_f32, ones_bf16):
    hi = a_f32.astype(BF16)
    r1 = a_f32 - hi.astype(F32)
    mid = r1.astype(BF16)
    lo = (r1 - mid.astype(F32)).astype(BF16)
    return _dot(hi, ones_bf16) + _dot(mid, ones_bf16) + _dot(lo, ones_bf16)


def _rms_rows(x):
    return x * lax.rsqrt(jnp.mean(x * x, axis=-1, keepdims=True) + EPS)


def _mod_kernel(c_ref, w_ref, b_ref, o_ref):
    c = c_ref[...]
    a = (c * jax.nn.sigmoid(c)).astype(BF16)
    o_ref[0] = _dot(a, w_ref[0].astype(BF16)) + b_ref[0]


def _modulation(cc, w_mod, b_mod):
    depth, d, d6 = w_mod.shape
    rows = cc.shape[0]
    tn = 1536
    return pl.pallas_call(
        _mod_kernel,
        grid=(depth, d6 // tn),
        in_specs=[
            pl.BlockSpec((rows, d), lambda l, n: (0, 0)),
            pl.BlockSpec((1, d, tn), lambda l, n: (l, 0, n)),
            pl.BlockSpec((1, 1, tn), lambda l, n: (l, 0, n)),
        ],
        out_specs=pl.BlockSpec((1, rows, tn), lambda l, n: (l, 0, n)),
        out_shape=jax.ShapeDtypeStruct((depth, rows, d6), F32),
        compiler_params=_cparams("parallel", "parallel"),
        name="modulation",
    )(cc, w_mod, b_mod.reshape(depth, 1, d6))


def _group_mean_sq(x, ones_bd, width):
    sq = x * x
    hi = sq.astype(BF16)
    lo = (sq - hi.astype(F32)).astype(BF16)
    return (_dot(hi, ones_bd) + _dot(lo, ones_bd)) * (1.0 / width)


def _in_kernel(x_ref, mod_ref, gmix_ref, w_ref, gb_ref, gcq_ref, gckv_ref, wuq_ref, wuk_ref, wuv_ref,
               gq_ref, gk_ref, ones_ref, cos_ref, sa_ref, sb_ref,
               hx_ref, mq_ref, mk_ref, mv_ref, mo_ref, g_ref, pc_ref, q_ref, k_ref, v_ref):
    x = x_ref[0]
    shift = mod_ref[0, 0:1, :]
    scale = mod_ref[0, 1:2, :]
    hx = (_rms_rows(x) * gmix_ref[...]) * (1.0 + scale) + shift
    hxb = hx.astype(BF16)
    hx_ref[0] = hxb
    p = _dot(hxb, w_ref[...])
    bw = BRANCH_W
    mq_ref[0] = p[:, 0:bw].T.astype(BF16)
    mk_ref[0] = (p[:, bw:2 * bw] * (ML_DH ** -0.5)).astype(BF16)
    mv_ref[0] = p[:, 2 * bw:3 * bw].T.astype(BF16)
    mo_ref[0] = p[:, 3 * bw:4 * bw].T.astype(BF16)
    pc_ref[0] = p[:, 1024:2048]
    cq = p[:, 2048:2048 + Q_LORA]
    ckv = p[:, 2304:2304 + KV_LORA]
    misc = p[:, 2432:2432 + LANES]
    g_ref[0] = (misc + gb_ref[...]).T[QK_ROPE:QK_ROPE + 4 * ML_HEADS]
    lane = lax.broadcasted_iota(jnp.int32, misc.shape, 1)
    kr_slot = jnp.where(jnp.logical_and(lane >= QK_NOPE, lane < QK_DIM), pltpu.roll(misc, QK_NOPE, 1), 0.0)
    kr4 = jnp.concatenate([kr_slot] * MLA_HEADS, axis=1)

    cqn = (_rms_rows(cq) * gcq_ref[...]).astype(BF16)
    ckvn = (_rms_rows(ckv) * gckv_ref[...]).astype(BF16)
    q_pre = _dot(cqn, wuq_ref[...])
    k_pre = _dot(ckvn, wuk_ref[...]) + kr4
    v_t = _dot(ckvn, wuv_ref[...]).T
    tm = v_t.shape[1]
    extra = V_SLOT - V_DIM
    one_row = jnp.where(lax.broadcasted_iota(jnp.int32, (extra, tm), 0) == 0, 1.0, 0.0)
    v_ref[0] = jnp.concatenate(
        sum([[v_t[h * V_DIM:(h + 1) * V_DIM], one_row] for h in range(MLA_HEADS)], []), axis=0).astype(BF16)

    ones_bd = ones_ref[...]
    cos = jnp.concatenate([cos_ref[...]] * MLA_HEADS, axis=1)
    sa = jnp.concatenate([sa_ref[...]] * MLA_HEADS, axis=1)
    sb = jnp.concatenate([sb_ref[...]] * MLA_HEADS, axis=1)
    width = MLA_HEADS * HEAD_SLOT
    half = QK_ROPE // 4

    def norm_rope(t, gain):
        t = t * lax.rsqrt(_group_mean_sq(t, ones_bd, QK_DIM) + EPS) * gain
        return t * cos + pltpu.roll(t, width - half, 1) * sa + pltpu.roll(t, half, 1) * sb

    q = norm_rope(q_pre, gq_ref[...]) * (QK_DIM ** -0.5 * math.log2(math.e))
    q_ref[0] = q.T.astype(BF16)
    k_ref[0] = norm_rope(k_pre, gk_ref[...]).astype(BF16)


def _in_proj(xs, mod, lw, rope, n_lat_tiles):
    bsz, s, d = xs.shape
    tm = ROW_TILE
    nt = s // tm
    wcols = lw["w_big"].shape[1]
    slot_w = MLA_HEADS * HEAD_SLOT

    def row_spec(width):
        return pl.BlockSpec((1, tm, width), lambda t, b: (b, t, 0))

    def tab_spec():
        return pl.BlockSpec((tm, HEAD_SLOT), lambda t, b: (t, 0))

    def col_spec(height):
        return pl.BlockSpec((1, height, tm), lambda t, b: (b, 0, t))

    v_rows = MLA_HEADS * V_SLOT
    outs = [("row", d, BF16), ("col", BRANCH_W, BF16), ("row", BRANCH_W, BF16), ("col", BRANCH_W, BF16),
            ("col", BRANCH_W, BF16), ("col", 4 * ML_HEADS, F32), ("row", 1024, F32),
            ("col", slot_w, BF16), ("row", slot_w, BF16), ("col", v_rows, BF16)]
    out_specs = [row_spec(w) if kind == "row" else col_spec(w) for kind, w, _ in outs]
    out_shape = [jax.ShapeDtypeStruct((bsz, s, w) if kind == "row" else (bsz, w, s), dt) for kind, w, dt in outs]
    return pl.pallas_call(
        _in_kernel,
        grid=(nt, bsz),
        in_specs=[
            row_spec(d),
            pl.BlockSpec((1, 6, d), lambda t, b: (jnp.where(t >= n_lat_tiles, bsz, b), 0, 0)),
            _const_spec((1, d)),
            _const_spec((d, wcols)),
            _const_spec((1, LANES)),
            _const_spec((1, Q_LORA)),
            _const_spec((1, KV_LORA)),
            _const_spec((Q_LORA, slot_w)),
            _const_spec((KV_LORA, slot_w)),
            _const_spec((KV_LORA, BRANCH_W)),
            _const_spec((1, slot_w)),
            _const_spec((1, slot_w)),
            _const_spec((slot_w, slot_w)),
            tab_spec(), tab_spec(), tab_spec(),
        ],
        out_specs=out_specs,
        out_shape=out_shape,
        compiler_params=_cparams("parallel", "parallel"),
        name="norm_in_proj",
    )(xs, mod, lw["g_mix"], lw["w_big"], lw["gate_bias"], lw["g_cq"], lw["g_ckv"], lw["w_uq"], lw["w_uk"], lw["w_uv"],
      lw["g_qn"], lw["g_kn"], rope["ones_slot"], rope["cos"], rope["sin_a"], rope["sin_b"])


def _log_sigmoid(x):
    return jnp.minimum(x, 0.0) - jnp.log1p(jnp.exp(-jnp.abs(x)))


def _mlstm_kernel(qt_f, k_f, vt_f, g_f, qt_b, k_b, vt_b, g_b, hf_ref, hb_ref, c_sc, m_sc):
    @pl.when(pl.program_id(1) == 0)
    def _():
        c_sc[...] = jnp.zeros_like(c_sc)
        m_sc[...] = jnp.zeros_like(m_sc)

    L = ML_CHUNK
    dh = ML_DH
    row = lax.broadcasted_iota(jnp.int32, (L, L), 0)
    col = lax.broadcasted_iota(jnp.int32, (L, L), 1)
    diag = row == col
    ones_ll = jnp.ones((L, L), BF16)
    one_rows = jnp.where(lax.broadcasted_iota(jnp.int32, (LANES - dh, L), 0) == 0, 1.0, 0.0).astype(BF16)

    dirs = ((qt_f, k_f, vt_f, g_f), (qt_b, k_b, vt_b, g_b))
    units = []
    for d, (qt_ref, k_ref, vt_ref, g_ref) in enumerate(dirs):
        a = g_ref[0]
        lf = _log_sigmoid(a)
        valid = (row <= col) if d == 0 else (row >= col)
        b_rows = _split_dot(lf, valid.astype(BF16))
        last = L - 1 if d == 0 else 0
        for h in range(ML_HEADS):
            ci = d * 2 * ML_HEADS + h
            cf = ci + ML_HEADS
            st = d * ML_HEADS + h
            u = {"st": st, "valid": valid}
            bt = b_rows[cf:cf + 1, :]
            li = a[ci:ci + 1, :]
            b_end = bt[:, last:last + 1]
            m_st = m_sc[st:st + 1, 0:1]
            qt = qt_ref[0, h * dh:(h + 1) * dh, :]
            k = k_ref[0, :, h * dh:(h + 1) * dh]
            u["vt"] = vt_ref[0, h * dh:(h + 1) * dh, :]
            u["bt"] = bt
            u["inter"] = bt + m_st
            src = jnp.where(diag, bt - li, 0.0)
            hi = src.astype(BF16)
            lo = (src - hi.astype(F32)).astype(BF16)
            u["src"] = _dot(hi, ones_ll) + _dot(lo, ones_ll)
            w_log = b_end - bt + li
            u["m_new"] = jnp.maximum(b_end + m_st, jnp.max(w_log, axis=1, keepdims=True))
            u["decay"] = jnp.exp(b_end + m_st - u["m_new"])
            v_aug = jnp.concatenate([u["vt"], one_rows], axis=0)
            vw = (v_aug.astype(F32) * jnp.exp(w_log - u["m_new"])).astype(BF16)
            u["s_kq"] = _dot(k, qt)
            u["qc"] = _dot(c_sc[st].astype(BF16), qt)
            u["upd"] = _dot(vw, k)
            units.append(u)

    for u in units:
        d_log = jnp.where(u["valid"], u["bt"] - u["src"], -jnp.inf)
        u["m_t"] = jnp.maximum(u["inter"], jnp.max(d_log, axis=0, keepdims=True))
        w_st = jnp.exp(d_log - u["m_t"]) * u["s_kq"]
        u["w_sum"] = jnp.sum(w_st, axis=0, keepdims=True)
        u["pv"] = _dot(u["vt"], w_st.astype(BF16))

    outs = []
    for u in units:
        a_inter = jnp.exp(u["inter"] - u["m_t"])
        num = a_inter * u["qc"][0:dh, :] + u["pv"]
        den = a_inter * u["qc"][dh:dh + 1, :] + u["w_sum"]
        outs.append(num / jnp.maximum(jnp.abs(den), jnp.exp(-u["m_t"])))
        st = u["st"]
        c_sc[st] = u["decay"] * c_sc[st] + u["upd"]
        m_sc[st:st + 1, :] = jnp.broadcast_to(u["m_new"], (1, LANES))
    hf_ref[0] = jnp.concatenate(outs[0:ML_HEADS], axis=0)
    hb_ref[0] = jnp.concatenate(outs[ML_HEADS:], axis=0)


def _mlstm(qt, k, vt, gt, n_lat_chunks):
    bsz, s, _ = k.shape
    nc = s // ML_CHUNK

    def fwd_chunk(j):
        return (j + n_lat_chunks) % nc

    def bwd_chunk(j):
        return nc - 1 - j

    def specs(chunk):
        def col(height):
            return pl.BlockSpec((1, height, ML_CHUNK), lambda b, j: (b, 0, chunk(j)))
        return [col(BRANCH_W), pl.BlockSpec((1, ML_CHUNK, BRANCH_W), lambda b, j: (b, chunk(j), 0)),
                col(BRANCH_W), col(4 * ML_HEADS)], col(BRANCH_W)

    in_f, out_f = specs(fwd_chunk)
    in_b, out_b = specs(bwd_chunk)
    return pl.pallas_call(
        _mlstm_kernel,
        grid=(bsz, nc),
        in_specs=in_f + in_b,
        out_specs=[out_f, out_b],
        out_shape=[jax.ShapeDtypeStruct((bsz, BRANCH_W, s), F32)] * 2,
        scratch_shapes=[pltpu.VMEM((2 * ML_HEADS, LANES, ML_DH), F32),
                        pltpu.VMEM((2 * ML_HEADS, LANES), F32)],
        compiler_params=_cparams("parallel", "arbitrary"),
        name="mlstm_scan",
    )(qt, k, vt, gt, qt, k, vt, gt)


def _attn_kernel(qt_ref, k_ref, vt_ref, o_ref, m_sc, acc_sc, *, n_lat, n_ctx, tk):
    m_sc[...] = jnp.full_like(m_sc, -jnp.inf)
    acc_sc[...] = jnp.zeros_like(acc_sc)

    def scores(h, start, size):
        sl = slice(h * HEAD_SLOT, (h + 1) * HEAD_SLOT)
        return _dot(k_ref[0, pl.ds(start, size), sl], qt_ref[0, sl, :])

    def keys(start, size):
        n_sub = size // ATT_SUB
        order = [(j, h) for j in range(n_sub) for h in range(MLA_HEADS)]
        pending = [scores(h, start + j * ATT_SUB, ATT_SUB) for j, h in order[:ATT_AHEAD]]
        for i, (j, h) in enumerate(order):
            off = start + j * ATT_SUB
            s = pending.pop(0)
            if i + ATT_AHEAD < len(order):
                jn, hn = order[i + ATT_AHEAD]
                pending.append(scores(hn, start + jn * ATT_SUB, ATT_SUB))
            m_old = m_sc[h]
            m_new = jnp.maximum(m_old, jnp.max(s, axis=0, keepdims=True))
            p = jnp.exp2(s - m_new)
            vt = vt_ref[0, h * V_SLOT:(h + 1) * V_SLOT, pl.ds(off, ATT_SUB)]
            acc_sc[h] = jnp.exp2(m_old - m_new) * acc_sc[h] + _dot(vt, p.astype(BF16))
            m_sc[h] = m_new

    keys(n_lat, n_ctx)

    def chunk(c, carry):
        keys(pl.multiple_of(c * tk, tk), tk)
        return carry

    is_ctx_tile = pl.program_id(1) * ATT_TQ >= n_lat
    lax.fori_loop(0, jnp.where(is_ctx_tile, 0, n_lat // tk), chunk, 0)
    out_t = jnp.concatenate([acc_sc[h, 0:V_DIM, :] / acc_sc[h, V_DIM:V_DIM + 1, :] for h in range(MLA_HEADS)],
                            axis=0)
    o_ref[0] = out_t.T.astype(o_ref.dtype)


def _key_chunk(n_keys):
    return max(t for t in range(ATT_TQ, ATT_TK + 1, ATT_TQ) if n_keys % t == 0)


def _attention(qt, k, vt, n_lat):
    bsz, s, slot_w = k.shape
    tq = ATT_TQ
    return pl.pallas_call(
        functools.partial(_attn_kernel, n_lat=n_lat, n_ctx=s - n_lat, tk=_key_chunk(n_lat)),
        grid=(bsz, s // tq),
        in_specs=[pl.BlockSpec((1, slot_w, tq), lambda b, t: (b, 0, t)),
                  pl.BlockSpec((1, s, slot_w), lambda b, t: (b, 0, 0), pipeline_mode=pl.Buffered(1)),
                  pl.BlockSpec((1, MLA_HEADS * V_SLOT, s), lambda b, t: (b, 0, 0), pipeline_mode=pl.Buffered(1))],
        out_specs=pl.BlockSpec((1, tq, BRANCH_W), lambda b, t: (b, t, 0)),
        out_shape=jax.ShapeDtypeStruct((bsz, s, BRANCH_W), BF16),
        scratch_shapes=[pltpu.VMEM((MLA_HEADS, 1, tq), F32), pltpu.VMEM((MLA_HEADS, V_SLOT, tq), F32)],
        compiler_params=_cparams("parallel", "arbitrary"),
        name="mla_attention",
    )(qt, k, vt)


def _merge_kernel(hx_ref, hf_ref, hb_ref, o_ref, ya_ref, pc_ref, prev_ref, next_ref, x_ref, mod_ref,
                  hg_ref, pw_ref, ps_ref, cw_ref, cb_ref, wg_ref, bg_ref, wb_ref, wo_ref,
                  gffn_ref, xo_ref, h2_ref, *, n_lat_tiles, n_tiles):
    t = pl.program_id(1)
    tm = ROW_TILE
    bw = BRANCH_W
    halo = POOL_HALO
    no_prev = jnp.logical_or(t == 0, t == n_lat_tiles)
    no_next = jnp.logical_or(t == n_lat_tiles - 1, t == n_tiles - 1)

    h_t = hf_ref[0] + hb_ref[0]
    normed = []
    for hd in range(ML_HEADS):
        hh = h_t[hd * ML_DH:(hd + 1) * ML_DH]
        normed.append(hh * lax.rsqrt(jnp.mean(hh * hh, axis=0, keepdims=True) + EPS))
    y_ml = (jnp.concatenate(normed, axis=0) * hg_ref[...] * jax.nn.sigmoid(o_ref[0].astype(F32))).T

    prev = jnp.where(no_prev, 0.0, prev_ref[0])
    nxt = jnp.where(no_next, 0.0, next_ref[0])
    ext = jnp.concatenate([prev, pc_ref[0], nxt], axis=0)
    pe = ext[:, 0:bw]

    def rows(arr, off):
        return arr[halo + off:halo + off + tm, :]

    r = lax.broadcasted_iota(jnp.int32, (tm, 1), 0)
    lane = lax.broadcasted_iota(jnp.int32, (tm, bw), 1)
    centre = rows(pe, 0)
    acc = centre
    mean = jnp.zeros((tm, bw), F32)
    done = 0
    for gi, w in enumerate(POOL_WINDOWS):
        for off in list(range(-(w // 2), -done)) + list(range(max(done, 1), w // 2)):
            acc = acc + rows(pe, off)
        done = w // 2
        before = jnp.where(no_prev, jnp.minimum(r, w // 2), w // 2)
        after = jnp.where(no_next, jnp.minimum(tm - r, w // 2), w // 2)
        inv = 1.0 / (before + after).astype(F32)
        mean = jnp.where(lane >= gi * POOL_GROUP, acc * inv, mean)
    y_pool = _dot((mean - centre).astype(BF16), pw_ref[...]) * ps_ref[...]

    z = ext[:, 3 * bw:4 * bw] * ext[:, bw:2 * bw]
    conv = cb_ref[...] + rows(z, -1) * cw_ref[0:1, :] + rows(z, 0) * cw_ref[1:2, :] + rows(z, 1) * cw_ref[2:3, :]
    y_conv = rows(ext[:, 2 * bw:3 * bw], 0) * conv

    hxb = hx_ref[0]
    d = hxb.shape[1]
    gates = jax.nn.sigmoid(_dot(hxb, wg_ref[...]) + bg_ref[...])
    ys = (y_ml.astype(BF16), ya_ref[0], y_pool.astype(BF16), y_conv.astype(BF16))
    merged = None
    for i, y in enumerate(ys):
        term = gates[:, i * d:(i + 1) * d] * _dot(y, wb_ref[i])
        merged = term if merged is None else merged + term
    out = _dot(merged.astype(BF16), wo_ref[...])
    x_new = x_ref[0] + mod_ref[0, 2:3, :] * out
    xo_ref[0] = x_new
    h2_ref[0] = (_rms_rows(x_new) * gffn_ref[...]) * (1.0 + mod_ref[0, 4:5, :]) + mod_ref[0, 3:4, :]


def _merge(hx, hf, hb, o, y_mla, pc, xs, mod, lw, n_lat_tiles, with_ctx):
    bsz, s, d = xs.shape
    tm = ROW_TILE
    nt = s // tm if with_ctx else n_lat_tiles
    halo = POOL_HALO
    per = tm // halo

    def row_spec(width):
        return pl.BlockSpec((1, tm, width), lambda b, t: (b, t, 0))

    def col_spec():
        return pl.BlockSpec((1, BRANCH_W, tm), lambda b, t: (b, 0, t))

    return pl.pallas_call(
        functools.partial(_merge_kernel, n_lat_tiles=n_lat_tiles, n_tiles=nt),
        grid=(bsz, nt),
        in_specs=[
            row_spec(d), col_spec(), col_spec(), col_spec(), row_spec(BRANCH_W),
            row_spec(1024),
            pl.BlockSpec((1, halo, 1024), lambda b, t: (b, jnp.maximum(t * per - 1, 0), 0)),
            pl.BlockSpec((1, halo, 1024), lambda b, t: (b, jnp.minimum((t + 1) * per, s // halo - 1), 0)),
            row_spec(d),
            pl.BlockSpec((1, 6, d), lambda b, t: (jnp.where(t >= n_lat_tiles, bsz, b), 0, 0)),
            _const_spec((BRANCH_W, tm)), _const_spec((BRANCH_W, BRANCH_W)),
            _const_spec((1, BRANCH_W)), _const_spec((3, BRANCH_W)), _const_spec((1, BRANCH_W)),
            _const_spec((d, 4 * d)), _const_spec((1, 4 * d)), _const_spec((4, BRANCH_W, d)),
            _const_spec((d, d)), _const_spec((1, d)),
        ],
        out_specs=[row_spec(d), row_spec(d)],
        out_shape=[jax.ShapeDtypeStruct((bsz, nt * tm, d), F32)] * 2,
        compiler_params=_cparams("parallel", "parallel"),
        name="mixer_merge",
    )(hx, hf, hb, o, y_mla, pc, pc, pc, xs, mod, lw["head_gain"], lw["pool_w"], lw["pool_scale"],
      lw["conv_w"], lw["conv_b"], lw["w_gate"], lw["b_gate"], lw["w_branch"], lw["w_out"], lw["g_ffn"])


def _router_kernel(h_ref, wr_ref, br_ref, pe_ref, we_ref, c16_ref):
    tm = h_ref.shape[0]
    scores = jax.nn.sigmoid(_dot(h_ref[...].astype(BF16), wr_ref[...]))
    sel = scores + br_ref[...]
    lane = lax.broadcasted_iota(jnp.int32, (tm, LANES), 1)
    member = jnp.zeros((tm, LANES), F32)
    for _ in range(TOP_K):
        mx = jnp.max(sel, axis=1, keepdims=True)
        ix = jnp.min(jnp.where(sel == mx, lane, LANES), axis=1, keepdims=True)
        hit = lane == ix
        member = jnp.where(hit, 1.0, member)
        sel = jnp.where(hit, -jnp.inf, sel)
    picked = member * scores
    weights = picked / jnp.sum(picked, axis=1, keepdims=True) * ROUTE_SCALE
    counts = jnp.sum(member, axis=0, keepdims=True)
    c16 = jnp.floor((counts + (RUN_ROWS - 1)) * (1.0 / RUN_ROWS))
    e_row = lax.broadcasted_iota(jnp.int32, (LANES, LANES), 0)
    e_col = lax.broadcasted_iota(jnp.int32, (LANES, LANES), 1)
    lower = (e_row < e_col).astype(BF16)
    o16 = _dot(jnp.broadcast_to(c16, (SUBLANES, LANES)).astype(BF16), lower)[0:1]
    row = lax.broadcasted_iota(jnp.int32, (tm, tm), 0)
    col = lax.broadcasted_iota(jnp.int32, (tm, tm), 1)
    rank = _dot((col < row).astype(BF16), member.astype(BF16))
    pos = jnp.where(member > 0.0, RUN_ROWS * o16 + rank, -1.0)
    pe_ref[0] = pos.T[0:N_EXPERTS]
    we_ref[0] = weights.T[0:N_EXPERTS]
    c16_ref[0] = c16.astype(jnp.int32)


def _router(h2, lw, tm):
    n, d = h2.shape
    nt = n // tm

    def t_spec():
        return pl.BlockSpec((1, N_EXPERTS, tm), lambda i: (i, 0, 0))

    return pl.pallas_call(
        _router_kernel,
        grid=(nt,),
        in_specs=[pl.BlockSpec((tm, d), lambda i: (i, 0)), _const_spec((d, LANES)), _const_spec((1, LANES))],
        out_specs=[t_spec(), t_spec(), pl.BlockSpec((1, 1, LANES), lambda i: (i, 0, 0))],
        out_shape=[jax.ShapeDtypeStruct((nt, N_EXPERTS, tm), F32), jax.ShapeDtypeStruct((nt, N_EXPERTS, tm), F32),
                   jax.ShapeDtypeStruct((nt, 1, LANES), jnp.int32)],
        compiler_params=_cparams("parallel"),
        name="moe_router",
    )(h2, lw["w_router"], lw["b_router"])


def _tile_row_bound(tm):
    worst = tm * TOP_K + N_EXPERTS * (RUN_ROWS - 1)
    return -(-worst // MOE_CHUNK) * MOE_CHUNK


def _selection_rows(g, grp_e_ref, pe_ref, tm):
    prow = pe_ref[0, pl.ds(grp_e_ref[0, 0, g], 1), :]
    rows = (lax.broadcasted_iota(jnp.int32, (RUN_ROWS, tm), 0) + g * RUN_ROWS).astype(F32)
    return prow, rows


def _dispatch_kernel(tail_ref, n_act_ref, grp_e_ref, gdst_ref, h_ref, pe_ref, xs_ref,
                     sel_sc, xb_sc, xp_sc, zero_sc, sem, *, n_groups):
    i = pl.program_id(0)
    tm = h_ref.shape[0]
    n_chunks = sel_sc.shape[0] // MOE_CHUNK
    per_chunk = MOE_CHUNK // RUN_ROWS

    @pl.when(i == pl.num_programs(0) - 1)
    def _():
        zero_sc[...] = jnp.zeros_like(zero_sc)

        def fill_copy(q):
            return pltpu.make_async_copy(
                zero_sc.at[pl.ds(0, RUN_ROWS)],
                xs_ref.at[pl.ds(pl.multiple_of(tail_ref[q] * RUN_ROWS, RUN_ROWS), RUN_ROWS)], sem)

        def block_copy(b):
            return pltpu.make_async_copy(
                zero_sc, xs_ref.at[pl.ds(pl.multiple_of(b * EXPERT_ROWS, EXPERT_ROWS), EXPERT_ROWS)], sem)

        def block_start(b, carry):
            block_copy(b).start()
            return carry

        def block_wait(b, carry):
            block_copy(b).wait()
            return carry

        n_blocks = xs_ref.shape[0] // EXPERT_ROWS
        lax.fori_loop(n_act_ref[0], n_blocks, block_start, 0)
        lax.fori_loop(n_act_ref[0], n_blocks, block_wait, 0)

        def fill_start(q, carry):
            @pl.when(tail_ref[q] >= 0)
            def _():
                fill_copy(q).start()
            return carry

        def fill_wait(q, carry):
            @pl.when(tail_ref[q] >= 0)
            def _():
                fill_copy(q).wait()
            return carry

        lax.fori_loop(0, tail_ref.shape[0], fill_start, 0)
        lax.fori_loop(0, tail_ref.shape[0], fill_wait, 0)

    def build(g, carry):
        prow, rows = _selection_rows(g, grp_e_ref, pe_ref, tm)
        sel_sc[pl.ds(pl.multiple_of(g * RUN_ROWS, RUN_ROWS), RUN_ROWS), :] = jnp.where(
            prow == rows, 1.0, 0.0).astype(BF16)
        return carry

    lax.fori_loop(0, n_groups, build, 0, unroll=4)
    xb_sc[...] = h_ref[...].astype(BF16)

    def run_copy(g):
        return pltpu.make_async_copy(
            xp_sc.at[pl.ds(pl.multiple_of(g * RUN_ROWS, RUN_ROWS), RUN_ROWS)],
            xs_ref.at[pl.ds(pl.multiple_of(gdst_ref[0, 0, g] * RUN_ROWS, RUN_ROWS), RUN_ROWS)], sem)

    def permute(ch):
        r0 = pl.multiple_of(ch * MOE_CHUNK, MOE_CHUNK)
        xp_sc[pl.ds(r0, MOE_CHUNK), :] = _dot(sel_sc[pl.ds(r0, MOE_CHUNK), :], xb_sc[...]).astype(BF16)

    def start_chunk(ch):
        for j in range(per_chunk):
            run_copy(ch * per_chunk + j).start()

    permute(0)

    def step(ch, carry):
        start_chunk(ch - 1)
        permute(ch)
        return carry

    lax.fori_loop(1, n_chunks, step, 0)
    start_chunk(n_chunks - 1)
    pltpu.make_async_copy(xp_sc, xs_ref.at[pl.ds(0, xp_sc.shape[0])], sem).wait()


def _dispatch(h2, pe, plan, n_rows, tm):
    n, d = h2.shape
    nt = n // tm
    rb = _tile_row_bound(tm)
    n_groups = rb // RUN_ROWS

    def smem_spec():
        return pl.BlockSpec((1, 1, n_groups), lambda i, *_: (i, 0, 0), memory_space=pltpu.SMEM)

    return pl.pallas_call(
        functools.partial(_dispatch_kernel, n_groups=n_groups),
        grid_spec=pltpu.PrefetchScalarGridSpec(
            num_scalar_prefetch=2,
            grid=(nt,),
            in_specs=[smem_spec(), smem_spec(),
                      pl.BlockSpec((tm, d), lambda i, *_: (i, 0)),
                      pl.BlockSpec((1, N_EXPERTS, tm), lambda i, *_: (i, 0, 0))],
            out_specs=pl.BlockSpec(memory_space=pl.ANY),
            scratch_shapes=[pltpu.VMEM((rb, tm), BF16), pltpu.VMEM((tm, d), BF16), pltpu.VMEM((rb, d), BF16),
                            pltpu.VMEM((EXPERT_ROWS, d), BF16), pltpu.SemaphoreType.DMA(())],
        ),
        out_shape=jax.ShapeDtypeStruct((n_rows, d), BF16),
        compiler_params=_cparams("arbitrary"),
        name="moe_dispatch",
    )(plan["tail"], plan["n_act"].reshape(1), plan["grp_e"], plan["gdst"], h2, pe)


def _expert_kernel(blk_e_ref, n_act_ref, x_ref, w1_ref, w3_ref, w2_ref, y_ref):
    del blk_e_ref

    @pl.when(pl.program_id(0) < n_act_ref[0])
    def _():
        x = x_ref[...]
        a = _dot(x, w1_ref[0, 0].astype(BF16))
        act = (a * jax.nn.sigmoid(a)) * _dot(x, w3_ref[0, 0].astype(BF16))
        y_ref[...] = _dot(act.astype(BF16), w2_ref[0, 0].astype(BF16)).astype(BF16)

    @pl.when(pl.program_id(0) >= n_act_ref[0])
    def _():
        y_ref[...] = jnp.zeros_like(y_ref)


def _experts(xs_sorted, blk_e, n_act, lw):
    n_rows, d = xs_sorted.shape
    bm = EXPERT_ROWS

    def row_map(i, blk_e_ref, n_act_ref):
        return (jnp.minimum(i, n_act_ref[0] - 1), 0)

    layer = lw["layer"]

    def w_map(i, blk_e_ref, n_act_ref):
        return (layer, blk_e_ref[i], 0, 0)

    return pl.pallas_call(
        _expert_kernel,
        grid_spec=pltpu.PrefetchScalarGridSpec(
            num_scalar_prefetch=2,
            grid=(n_rows // bm,),
            in_specs=[pl.BlockSpec((bm, d), row_map),
                      pl.BlockSpec((1, 1, d, EXPERT_FF), w_map),
                      pl.BlockSpec((1, 1, d, EXPERT_FF), w_map),
                      pl.BlockSpec((1, 1, EXPERT_FF, d), w_map)],
            out_specs=pl.BlockSpec((bm, d), lambda i, *_: (i, 0)),
        ),
        out_shape=jax.ShapeDtypeStruct((n_rows, d), BF16),
        compiler_params=_cparams("arbitrary"),
        name="moe_experts",
    )(blk_e, n_act, xs_sorted, lw["w1"], lw["w3"], lw["w2"])


def _combine_kernel(grp_e_ref, gsrc_ref, h_ref, x_ref, pe_ref, we_ref, mod_ref, modc_ref, ws13_ref,
                    ws2_ref, ys_ref, xo_ref, sel_sc, yp_sc, acc_sc, sem, *, n_groups, tiles_per_sample, n_ctx):
    i = pl.program_id(0)
    tm = h_ref.shape[0]

    n_chunks = sel_sc.shape[0] // MOE_CHUNK
    per_chunk = MOE_CHUNK // RUN_ROWS

    def start_chunk(ch):
        for j in range(per_chunk):
            g = ch * per_chunk + j
            pltpu.make_async_copy(
                ys_ref.at[pl.ds(pl.multiple_of(gsrc_ref[0, 0, g] * RUN_ROWS, RUN_ROWS), RUN_ROWS)],
                yp_sc.at[pl.ds(pl.multiple_of(g * RUN_ROWS, RUN_ROWS), RUN_ROWS)], sem).start()

    def wait_chunk(ch):
        r0 = pl.multiple_of(ch * MOE_CHUNK, MOE_CHUNK)
        pltpu.make_async_copy(ys_ref.at[pl.ds(0, MOE_CHUNK)], yp_sc.at[pl.ds(r0, MOE_CHUNK)], sem).wait()

    start_chunk(0)

    def build(g, carry):
        prow, rows = _selection_rows(g, grp_e_ref, pe_ref, tm)
        wrow = we_ref[0, pl.ds(grp_e_ref[0, 0, g], 1), :]
        sel_sc[pl.ds(pl.multiple_of(g * RUN_ROWS, RUN_ROWS), RUN_ROWS), :] = jnp.where(
            prow == rows, wrow, 0.0).astype(BF16)
        return carry

    lax.fori_loop(0, n_groups, build, 0, unroll=4)

    up = _dot(h_ref[...].astype(BF16), ws13_ref[...])
    a = up[:, 0:EXPERT_FF]
    act = (a * jax.nn.sigmoid(a)) * up[:, EXPERT_FF:2 * EXPERT_FF]
    acc_sc[...] = _dot(act.astype(BF16), ws2_ref[...])

    def gather_sum(ch):
        r0 = pl.multiple_of(ch * MOE_CHUNK, MOE_CHUNK)
        acc_sc[...] += lax.dot_general(sel_sc[pl.ds(r0, MOE_CHUNK), :], yp_sc[pl.ds(r0, MOE_CHUNK), :], TN_DIMS,
                                       preferred_element_type=F32)

    def step(ch, carry):
        wait_chunk(ch)
        start_chunk(ch + 1)
        gather_sum(ch)
        return carry

    lax.fori_loop(0, n_chunks - 1, step, 0)
    wait_chunk(n_chunks - 1)
    gather_sum(n_chunks - 1)
    r = lax.broadcasted_iota(jnp.int32, (tm, 1), 0)
    is_ctx = jnp.logical_and(i % tiles_per_sample == tiles_per_sample - 1, r >= tm - n_ctx)
    gate = jnp.where(is_ctx, modc_ref[0, 5:6, :], mod_ref[0, 5:6, :])
    xo_ref[...] = x_ref[...] + gate * acc_sc[...]


def _combine(h2, xs, pe, we, plan, mod, ys_sorted, lw, tm, tiles_per_sample, n_ctx, bsz):
    n, d = h2.shape
    nt = n // tm
    rb = _tile_row_bound(tm)
    n_groups = rb // RUN_ROWS

    def smem_spec():
        return pl.BlockSpec((1, 1, n_groups), lambda i, *_: (i, 0, 0), memory_space=pltpu.SMEM)

    def row_spec():
        return pl.BlockSpec((tm, d), lambda i, *_: (i, 0))

    def t_spec():
        return pl.BlockSpec((1, N_EXPERTS, tm), lambda i, *_: (i, 0, 0))

    return pl.pallas_call(
        functools.partial(_combine_kernel, n_groups=n_groups, tiles_per_sample=tiles_per_sample, n_ctx=n_ctx),
        grid_spec=pltpu.PrefetchScalarGridSpec(
            num_scalar_prefetch=0,
            grid=(nt,),
            in_specs=[smem_spec(), smem_spec(), row_spec(), row_spec(), t_spec(), t_spec(),
                      pl.BlockSpec((1, 6, d), lambda i, *_: (i // tiles_per_sample, 0, 0)),
                      pl.BlockSpec((1, 6, d), lambda i, *_: (bsz, 0, 0)),
                      pl.BlockSpec((d, 2 * EXPERT_FF), lambda i, *_: (0, 0), pipeline_mode=pl.Buffered(1)),
                      pl.BlockSpec((EXPERT_FF, d), lambda i, *_: (0, 0), pipeline_mode=pl.Buffered(1)),
                      pl.BlockSpec(memory_space=pl.ANY)],
            out_specs=row_spec(),
            scratch_shapes=[pltpu.VMEM((rb, tm), BF16), pltpu.VMEM((rb, d), BF16), pltpu.VMEM((tm, d), F32),
                            pltpu.SemaphoreType.DMA(())],
        ),
        out_shape=jax.ShapeDtypeStruct((n, d), F32),
        compiler_params=_cparams("arbitrary"),
        name="moe_combine",
    )(plan["grp_e"], plan["gsrc"], h2, xs, pe, we, mod, mod, lw["ws13"], lw["ws2"], ys_sorted)


def _moe_plan(c16, n_groups, spare16):
    nt = c16.shape[0]
    per_blk = EXPERT_ROWS // RUN_ROWS
    o16 = jnp.cumsum(c16, axis=1) - c16
    before16 = jnp.cumsum(c16, axis=0) - c16
    gtot16 = jnp.sum(c16, axis=0)
    gblk = (gtot16 + per_blk - 1) // per_blk
    gend_blk = jnp.cumsum(gblk)
    gstart16 = per_blk * (gend_blk - gblk)
    g = jnp.arange(n_groups, dtype=jnp.int32)
    grp_e = jnp.minimum(jnp.sum((o16 + c16)[:, None, :] <= g[None, :, None], axis=2), N_EXPERTS - 1)
    run0 = gstart16[None, :] + before16 - o16
    experts = jnp.arange(N_EXPERTS, dtype=jnp.int32)
    gdst = jnp.sum(jnp.where(grp_e[:, :, None] == experts[None, None, :], run0[:, None, :], 0), axis=2) + g[None, :]
    q = jnp.arange(per_blk, dtype=jnp.int32)
    tail = jnp.where(q[None, :] < (per_blk * gblk - gtot16)[:, None],
                     (gstart16 + gtot16)[:, None] + q[None, :], -1)
    used = g[None, :] < jnp.sum(c16, axis=1)[:, None]
    return {
        "grp_e": grp_e.astype(jnp.int32).reshape(nt, 1, n_groups),
        "gdst": jnp.where(used, gdst, spare16 + g[None, :]).astype(jnp.int32).reshape(nt, 1, n_groups),
        "gsrc": jnp.where(used, gdst, 0).astype(jnp.int32).reshape(nt, 1, n_groups),
        "tail": tail.astype(jnp.int32).reshape(-1),
        "gend_blk": gend_blk, "n_act": gend_blk[-1].astype(jnp.int32),
    }


def _moe_tile(rows_per_sample):
    return max(t for t in range(ROW_TILE, MOE_TILE_MAX + 1, ROW_TILE) if rows_per_sample % t == 0)


def _moe(h2, xs, mod, lw, rows_per_sample, n_ctx, bsz):
    n, d = h2.shape
    tile = _moe_tile(rows_per_sample)
    tiles_per_sample = rows_per_sample // tile
    nt = n // tile
    n_groups = _tile_row_bound(tile) // RUN_ROWS
    pe, we, c16 = _router(h2, lw, tile)
    worst_rows = n * TOP_K + nt * N_EXPERTS * (RUN_ROWS - 1) + N_EXPERTS * (EXPERT_ROWS - 1)
    run_blocks = -(-worst_rows // EXPERT_ROWS)
    n_blocks = run_blocks + -(-n_groups * RUN_ROWS // EXPERT_ROWS)
    plan = _moe_plan(c16[:, 0, :N_EXPERTS], n_groups, run_blocks * (EXPERT_ROWS // RUN_ROWS))
    blk = jnp.minimum(jnp.arange(n_blocks, dtype=jnp.int32), plan["n_act"] - 1)
    blk_e = jnp.minimum(jnp.sum(plan["gend_blk"][None, :] <= blk[:, None], axis=1), N_EXPERTS - 1).astype(jnp.int32)
    xs_sorted = _dispatch(h2, pe, plan, n_blocks * EXPERT_ROWS, tile)
    ys_sorted = _experts(xs_sorted, blk_e, plan["n_act"].reshape(1), lw)
    return _combine(h2, xs, pe, we, plan, mod, ys_sorted, lw, tile, tiles_per_sample, n_ctx, bsz)


def _slots(w, head_w, real_w):
    rows = w.shape[0]
    w = w.reshape(rows, -1, head_w)[:, :, :real_w]
    return jnp.pad(w, ((0, 0), (0, 0), (0, HEAD_SLOT - real_w))).reshape(rows, -1)


def _block_diag(blocks):
    n, r, c = blocks.shape
    out = jnp.zeros((n * r, n * c), blocks.dtype)
    for i in range(n):
        out = out.at[i * r:(i + 1) * r, i * c:(i + 1) * c].set(blocks[i])
    return out


def _layer_weights(l, p):
    d = p["w_in"].shape[1]
    w_in = p["w_in"][l]
    ml = 4 * BRANCH_W
    o_mla = ml + 4 * ML_HEADS
    o_pool = o_mla + Q_LORA + KV_LORA + QK_ROPE
    w_big = jnp.concatenate([
        w_in[:, :ml],
        w_in[:, o_pool:],
        w_in[:, o_mla:o_mla + Q_LORA + KV_LORA],
        w_in[:, o_mla + Q_LORA + KV_LORA:o_pool],
        w_in[:, ml:o_mla],
        jnp.zeros((d, LANES - QK_ROPE - 4 * ML_HEADS), F32),
    ], axis=1).astype(BF16)
    w_ukv = p["mla_w_ukv"][l].reshape(KV_LORA, MLA_HEADS, QK_NOPE + V_DIM)

    def gain_slots(g):
        return jnp.tile(jnp.pad(g, (0, HEAD_SLOT - QK_DIM)), MLA_HEADS)[None, :]

    return {
        "g_mix": p["g_mix"][l][None, :], "g_ffn": p["g_ffn"][l][None, :],
        "w_big": w_big,
        "g_cq": p["mla_g_cq"][l][None, :], "g_ckv": p["mla_g_ckv"][l][None, :],
        "w_uq": _slots(p["mla_w_uq"][l], QK_DIM, QK_DIM).astype(BF16),
        "w_uk": _slots(w_ukv[:, :, :QK_NOPE].reshape(KV_LORA, -1), QK_NOPE, QK_NOPE).astype(BF16),
        "w_uv": w_ukv[:, :, QK_NOPE:].reshape(KV_LORA, BRANCH_W).astype(BF16),
        "g_qn": gain_slots(p["mla_g_qn"][l]), "g_kn": gain_slots(p["mla_g_kn"][l]),
        "gate_bias": jnp.pad(p["ml_gate_bias"][l], (QK_ROPE, LANES - QK_ROPE - 4 * ML_HEADS))[None, :],
        "head_gain": jnp.broadcast_to(p["ml_head_gain"][l][:, None], (BRANCH_W, ROW_TILE)),
        "pool_w": _block_diag(p["pool_w"][l]).astype(BF16),
        "pool_scale": p["pool_scale"][l][None, :],
        "conv_w": p["conv_w"][l], "conv_b": p["conv_b"][l][None, :],
        "w_gate": jnp.concatenate(list(p["w_gate"][l]), axis=1).astype(BF16),
        "b_gate": p["b_gate"][l].reshape(1, -1),
        "w_branch": p["w_branch"][l].astype(BF16),
        "w_out": p["w_out"][l].astype(BF16),
        "w_router": jnp.pad(p["moe_w_router"][l], ((0, 0), (0, LANES - N_EXPERTS))).astype(BF16),
        "b_router": jnp.pad(p["moe_b_router"][l], (0, LANES - N_EXPERTS), constant_values=-1e30)[None, :],
        "layer": l, "w1": p["moe_w1"], "w3": p["moe_w3"], "w2": p["moe_w2"],
        "ws13": jnp.concatenate([p["moe_ws1"][l], p["moe_ws3"][l]], axis=1).astype(BF16),
        "ws2": p["moe_ws2"][l].astype(BF16),
    }


def _rope_tables(n_lat, n_ctx):
    t = jnp.arange(n_lat)
    n_freq = QK_ROPE // 4
    inv = ROPE_THETA ** (-jnp.arange(n_freq, dtype=F32) / n_freq)
    ang_r = (t // GRID_W).astype(F32)[:, None] * inv
    ang_c = (t % GRID_W).astype(F32)[:, None] * inv
    cos4 = jnp.concatenate([jnp.cos(ang_r)] * 2 + [jnp.cos(ang_c)] * 2, axis=1)
    zero = jnp.zeros_like(ang_r)
    sin_a = jnp.concatenate([-jnp.sin(ang_r), zero, -jnp.sin(ang_c), zero], axis=1)
    sin_b = jnp.concatenate([zero, jnp.sin(ang_r), zero, jnp.sin(ang_c)], axis=1)

    def slot(a, fill):
        a = jnp.pad(a, ((0, 0), (QK_NOPE, 0)), constant_values=fill)
        a = jnp.pad(a, ((0, 0), (0, HEAD_SLOT - QK_DIM)), constant_values=fill)
        return jnp.pad(a, ((0, n_ctx), (0, 0)), constant_values=fill)

    return {"cos": slot(cos4, 1.0), "sin_a": slot(sin_a, 0.0), "sin_b": slot(sin_b, 0.0),
            "ones_slot": _block_diag(jnp.ones((MLA_HEADS, HEAD_SLOT, HEAD_SLOT), BF16))}


def kernel(x, c, ctx, c_ctx, w_mod, b_mod, g_mix, g_ffn, w_in, ml_gate_bias, ml_head_gain, mla_g_cq, mla_g_ckv,
           mla_w_uq, mla_w_ukv, mla_g_qn, mla_g_kn, pool_w, pool_scale, conv_w, conv_b, w_gate, b_gate, w_branch,
           w_out, moe_w_router, moe_b_router, moe_w1, moe_w3, moe_w2, moe_ws1, moe_ws3, moe_ws2):
    p = dict(g_mix=g_mix, g_ffn=g_ffn, w_in=w_in, ml_gate_bias=ml_gate_bias, ml_head_gain=ml_head_gain,
             mla_g_cq=mla_g_cq, mla_g_ckv=mla_g_ckv, mla_w_uq=mla_w_uq, mla_w_ukv=mla_w_ukv, mla_g_qn=mla_g_qn,
             mla_g_kn=mla_g_kn, pool_w=pool_w, pool_scale=pool_scale, conv_w=conv_w, conv_b=conv_b,
             w_gate=w_gate, b_gate=b_gate, w_branch=w_branch, w_out=w_out, moe_w_router=moe_w_router,
             moe_b_router=moe_b_router, moe_w1=moe_w1, moe_w3=moe_w3, moe_w2=moe_w2, moe_ws1=moe_ws1,
             moe_ws3=moe_ws3, moe_ws2=moe_ws2)
    bsz, n_lat, d = x.shape
    n_ctx = ctx.shape[1]
    depth = w_mod.shape[0]
    s = n_lat + n_ctx
    assert n_ctx == ROW_TILE == ATT_TQ and n_lat % ROW_TILE == 0 and n_lat % GRID_W == 0
    n_lat_tiles = n_lat // ROW_TILE

    mod_rows = -(-(bsz + 1) // SUBLANES) * SUBLANES
    cc = jnp.concatenate([c, c_ctx[None, :], jnp.zeros((mod_rows - bsz - 1, d), F32)], axis=0)
    mods = _modulation(cc, w_mod, b_mod).reshape(depth, mod_rows, 6, d)
    rope = _rope_tables(n_lat, n_ctx)
    xs = jnp.concatenate([x, ctx], axis=1)

    for l in range(depth):
        lw = _layer_weights(l, p)
        mod = mods[l]
        hx, mq, mk, mv, o, g, pc, q, k, v = _in_proj(xs, mod, lw, rope, n_lat_tiles)
        hf, hb = _mlstm(mq, mk, mv, g, n_lat // ML_CHUNK)
        y_mla = _attention(q, k, v, n_lat)
        with_ctx = l + 1 < depth
        rows = s if with_ctx else n_lat
        xs, h2 = _merge(hx, hf, hb, o, y_mla, pc, xs, mod, lw, n_lat_tiles, with_ctx)
        xs = _moe(h2.reshape(bsz * rows, d), xs.reshape(bsz * rows, d), mod, lw, rows, n_ctx if with_ctx else 0,
                  bsz).reshape(bsz, rows, d)
    return xs
```

```python
import functools
import math

import jax
import jax.numpy as jnp
from jax import lax
from jax.experimental import pallas as pl
from jax.experimental.pallas import tpu as pltpu

GRID_W = 64
BRANCH_W = 256
EPS = 1e-6
ML_HEADS = 4
ML_DH = BRANCH_W // ML_HEADS
ML_CHUNK = 128
MLA_HEADS = 4
Q_LORA = 256
KV_LORA = 128
QK_NOPE = 64
QK_ROPE = 32
QK_DIM = QK_NOPE + QK_ROPE
V_DIM = BRANCH_W // MLA_HEADS
V_SLOT = V_DIM + 16
ROPE_THETA = 10000.0
POOL_WINDOWS = (2, 4, 8, 16)
POOL_GROUP = BRANCH_W // len(POOL_WINDOWS)
POOL_HALO = max(POOL_WINDOWS) // 2
N_EXPERTS = 64
TOP_K = 6
EXPERT_FF = 256
ROUTE_SCALE = 2.5

LANES = 128
SUBLANES = 8
HEAD_SLOT = LANES
ROW_TILE = 256
ATT_TQ = 256
ATT_TK = 8192
ATT_SUB = 128
ATT_AHEAD = 8
EXPERT_ROWS = 1024
MOE_TILE_MAX = 768
RUN_ROWS = 16
MOE_CHUNK = 512
VMEM_LIMIT = 56 * 1024 * 1024

F32 = jnp.float32
BF16 = jnp.bfloat16
NT_DIMS = (((1,), (1,)), ((), ()))
TN_DIMS = (((0,), (0,)), ((), ()))


def _cparams(*sem):
    return pltpu.CompilerParams(dimension_semantics=sem, vmem_limit_bytes=VMEM_LIMIT)


def _const_spec(shape):
    nd = len(shape)
    return pl.BlockSpec(shape, lambda *_: (0,) * nd, pipeline_mode=pl.Buffered(1))


def _dot(a, b):
    return jnp.dot(a, b, preferred_element_type=F32)


def _split_dot(a_f32, ones_bf16):
    hi = a_f32.astype(BF16)
    r1 = a_f32 - hi.astype(F32)
    mid = r1.astype(BF16)
    lo = (r1 - mid.astype(F32)).astype(BF16)
    return _dot(hi, ones_bf16) + _dot(mid, ones_bf16) + _dot(lo, ones_bf16)


def _rms_rows(x):
    return x * lax.rsqrt(jnp.mean(x * x, axis=-1, keepdims=True) + EPS)


def _mod_kernel(c_ref, w_ref, b_ref, o_ref):
    c = c_ref[...]
    a = (c * jax.nn.sigmoid(c)).astype(BF16)
    o_ref[0] = _dot(a, w_ref[0].astype(BF16)) + b_ref[0]


def _modulation(cc, w_mod, b_mod):
    depth, d, d6 = w_mod.shape
    rows = cc.shape[0]
    tn = 1536
    return pl.pallas_call(
        _mod_kernel,
        grid=(depth, d6 // tn),
        in_specs=[
            pl.BlockSpec((rows, d), lambda l, n: (0, 0)),
            pl.BlockSpec((1, d, tn), lambda l, n: (l, 0, n)),
            pl.BlockSpec((1, 1, tn), lambda l, n: (l, 0, n)),
        ],
        out_specs=pl.BlockSpec((1, rows, tn), lambda l, n: (l, 0, n)),
        out_shape=jax.ShapeDtypeStruct((depth, rows, d6), F32),
        compiler_params=_cparams("parallel", "parallel"),
        name="modulation",
    )(cc, w_mod, b_mod.reshape(depth, 1, d6))


def _group_mean_sq(x, ones_bd, width):
    sq = x * x
    hi = sq.astype(BF16)
    lo = (sq - hi.astype(F32)).astype(BF16)
    return (_dot(hi, ones_bd) + _dot(lo, ones_bd)) * (1.0 / width)


def _in_kernel(x_ref, mod_ref, gmix_ref, w_ref, gb_ref, gcq_ref, gckv_ref, wuq_ref, wuk_ref, wuv_ref,
               gq_ref, gk_ref, ones_ref, cos_ref, sa_ref, sb_ref,
               hx_ref, mq_ref, mk_ref, mv_ref, mo_ref, g_ref, pc_ref, q_ref, k_ref, v_ref):
    x = x_ref[0]
    shift = mod_ref[0, 0:1, :]
    scale = mod_ref[0, 1:2, :]
    hx = (_rms_rows(x) * gmix_ref[...]) * (1.0 + scale) + shift
    hxb = hx.astype(BF16)
    hx_ref[0] = hxb
    p = _dot(hxb, w_ref[...])
    bw = BRANCH_W
    mq_ref[0] = p[:, 0:bw].T.astype(BF16)
    mk_ref[0] = (p[:, bw:2 * bw] * (ML_DH ** -0.5)).astype(BF16)
    mv_ref[0] = p[:, 2 * bw:3 * bw].T.astype(BF16)
    mo_ref[0] = p[:, 3 * bw:4 * bw].T.astype(BF16)
    pc_ref[0] = p[:, 1024:2048]
    cq = p[:, 2048:2048 + Q_LORA]
    ckv = p[:, 2304:2304 + KV_LORA]
    misc = p[:, 2432:2432 + LANES]
    g_ref[0] = (misc + gb_ref[...]).T[QK_ROPE:QK_ROPE + 4 * ML_HEADS]
    lane = lax.broadcasted_iota(jnp.int32, misc.shape, 1)
    kr_slot = jnp.where(jnp.logical_and(lane >= QK_NOPE, lane < QK_DIM), pltpu.roll(misc, QK_NOPE, 1), 0.0)
    kr4 = jnp.concatenate([kr_slot] * MLA_HEADS, axis=1)

    cqn = (_rms_rows(cq) * gcq_ref[...]).astype(BF16)
    ckvn = (_rms_rows(ckv) * gckv_ref[...]).astype(BF16)
    q_pre = _dot(cqn, wuq_ref[...])
    k_pre = _dot(ckvn, wuk_ref[...]) + kr4
    v_t = _dot(ckvn, wuv_ref[...]).T
    tm = v_t.shape[1]
    extra = V_SLOT - V_DIM
    one_row = jnp.where(lax.broadcasted_iota(jnp.int32, (extra, tm), 0) == 0, 1.0, 0.0)
    v_ref[0] = jnp.concatenate(
        sum([[v_t[h * V_DIM:(h + 1) * V_DIM], one_row] for h in range(MLA_HEADS)], []), axis=0).astype(BF16)

    ones_bd = ones_ref[...]
    cos = jnp.concatenate([cos_ref[...]] * MLA_HEADS, axis=1)
    sa = jnp.concatenate([sa_ref[...]] * MLA_HEADS, axis=1)
    sb = jnp.concatenate([sb_ref[...]] * MLA_HEADS, axis=1)
    width = MLA_HEADS * HEAD_SLOT
    half = QK_ROPE // 4

    def norm_rope(t, gain):
        t = t * lax.rsqrt(_group_mean_sq(t, ones_bd, QK_DIM) + EPS) * gain
        return t * cos + pltpu.roll(t, width - half, 1) * sa + pltpu.roll(t, half, 1) * sb

    q = norm_rope(q_pre, gq_ref[...]) * (QK_DIM ** -0.5 * math.log2(math.e))
    q_ref[0] = q.T.astype(BF16)
    k_ref[0] = norm_rope(k_pre, gk_ref[...]).astype(BF16)


def _in_proj(xs, mod, lw, rope, n_lat_tiles):
    bsz, s, d = xs.shape
    tm = ROW_TILE
    nt = s // tm
    wcols = lw["w_big"].shape[1]
    slot_w = MLA_HEADS * HEAD_SLOT

    def row_spec(width):
        return pl.BlockSpec((1, tm, width), lambda t, b: (b, t, 0))

    def tab_spec():
        return pl.BlockSpec((tm, HEAD_SLOT), lambda t, b: (t, 0))

    def col_spec(height):
        return pl.BlockSpec((1, height, tm), lambda t, b: (b, 0, t))

    v_rows = MLA_HEADS * V_SLOT
    outs = [("row", d, BF16), ("col", BRANCH_W, BF16), ("row", BRANCH_W, BF16), ("col", BRANCH_W, BF16),
            ("col", BRANCH_W, BF16), ("col", 4 * ML_HEADS, F32), ("row", 1024, F32),
            ("col", slot_w, BF16), ("row", slot_w, BF16), ("col", v_rows, BF16)]
    out_specs = [row_spec(w) if kind == "row" else col_spec(w) for kind, w, _ in outs]
    out_shape = [jax.ShapeDtypeStruct((bsz, s, w) if kind == "row" else (bsz, w, s), dt) for kind, w, dt in outs]
    return pl.pallas_call(
        _in_kernel,
        grid=(nt, bsz),
        in_specs=[
            row_spec(d),
            pl.BlockSpec((1, 6, d), lambda t, b: (jnp.where(t >= n_lat_tiles, bsz, b), 0, 0)),
            _const_spec((1, d)),
            _const_spec((d, wcols)),
            _const_spec((1, LANES)),
            _const_spec((1, Q_LORA)),
            _const_spec((1, KV_LORA)),
            _const_spec((Q_LORA, slot_w)),
            _const_spec((KV_LORA, slot_w)),
            _const_spec((KV_LORA, BRANCH_W)),
            _const_spec((1, slot_w)),
            _const_spec((1, slot_w)),
            _const_spec((slot_w, slot_w)),
            tab_spec(), tab_spec(), tab_spec(),
        ],
        out_specs=out_specs,
        out_shape=out_shape,
        compiler_params=_cparams("parallel", "parallel"),
        name="norm_in_proj",
    )(xs, mod, lw["g_mix"], lw["w_big"], lw["gate_bias"], lw["g_cq"], lw["g_ckv"], lw["w_uq"], lw["w_uk"], lw["w_uv"],
      lw["g_qn"], lw["g_kn"], rope["ones_slot"], rope["cos"], rope["sin_a"], rope["sin_b"])


def _log_sigmoid(x):
    return jnp.minimum(x, 0.0) - jnp.log1p(jnp.exp(-jnp.abs(x)))


def _mlstm_kernel(qt_f, k_f, vt_f, g_f, qt_b, k_b, vt_b, g_b, hf_ref, hb_ref, c_sc, m_sc):
    @pl.when(pl.program_id(1) == 0)
    def _():
        c_sc[...] = jnp.zeros_like(c_sc)
        m_sc[...] = jnp.zeros_like(m_sc)

    L = ML_CHUNK
    dh = ML_DH
    row = lax.broadcasted_iota(jnp.int32, (L, L), 0)
    col = lax.broadcasted_iota(jnp.int32, (L, L), 1)
    diag = row == col
    ones_ll = jnp.ones((L, L), BF16)
    one_rows = jnp.where(lax.broadcasted_iota(jnp.int32, (LANES - dh, L), 0) == 0, 1.0, 0.0).astype(BF16)

    dirs = ((qt_f, k_f, vt_f, g_f), (qt_b, k_b, vt_b, g_b))
    units = []
    for d, (qt_ref, k_ref, vt_ref, g_ref) in enumerate(dirs):
        a = g_ref[0]
        lf = _log_sigmoid(a)
        valid = (row <= col) if d == 0 else (row >= col)
        b_rows = _split_dot(lf, valid.astype(BF16))
        last = L - 1 if d == 0 else 0
        for h in range(ML_HEADS):
            ci = d * 2 * ML_HEADS + h
            cf = ci + ML_HEADS
            st = d * ML_HEADS + h
            u = {"st": st, "valid": valid}
            bt = b_rows[cf:cf + 1, :]
            li = a[ci:ci + 1, :]
            b_end = bt[:, last:last + 1]
            m_st = m_sc[st:st + 1, 0:1]
            qt = qt_ref[0, h * dh:(h + 1) * dh, :]
            k = k_ref[0, :, h * dh:(h + 1) * dh]
            u["vt"] = vt_ref[0, h * dh:(h + 1) * dh, :]
            u["bt"] = bt
            u["inter"] = bt + m_st
            src = jnp.where(diag, bt - li, 0.0)
            hi = src.astype(BF16)
            lo = (src - hi.astype(F32)).astype(BF16)
            u["src"] = _dot(hi, ones_ll) + _dot(lo, ones_ll)
            w_log = b_end - bt + li
            u["m_new"] = jnp.maximum(b_end + m_st, jnp.max(w_log, axis=1, keepdims=True))
            u["decay"] = jnp.exp(b_end + m_st - u["m_new"])
            v_aug = jnp.concatenate([u["vt"], one_rows], axis=0)
            vw = (v_aug.astype(F32) * jnp.exp(w_log - u["m_new"])).astype(BF16)
            u["s_kq"] = _dot(k, qt)
            u["qc"] = _dot(c_sc[st].astype(BF16), qt)
            u["upd"] = _dot(vw, k)
            units.append(u)

    for u in units:
        d_log = jnp.where(u["valid"], u["bt"] - u["src"], -jnp.inf)
        u["m_t"] = jnp.maximum(u["inter"], jnp.max(d_log, axis=0, keepdims=True))
        w_st = jnp.exp(d_log - u["m_t"]) * u["s_kq"]
        u["w_sum"] = jnp.sum(w_st, axis=0, keepdims=True)
        u["pv"] = _dot(u["vt"], w_st.astype(BF16))

    outs = []
    for u in units:
        a_inter = jnp.exp(u["inter"] - u["m_t"])
        num = a_inter * u["qc"][0:dh, :] + u["pv"]
        den = a_inter * u["qc"][dh:dh + 1, :] + u["w_sum"]
        outs.append(num / jnp.maximum(jnp.abs(den), jnp.exp(-u["m_t"])))
        st = u["st"]
        c_sc[st] = u["decay"] * c_sc[st] + u["upd"]
        m_sc[st:st + 1, :] = jnp.broadcast_to(u["m_new"], (1, LANES))
    hf_ref[0] = jnp.concatenate(outs[0:ML_HEADS], axis=0)
    hb_ref[0] = jnp.concatenate(outs[ML_HEADS:], axis=0)


def _mlstm(qt, k, vt, gt, n_lat_chunks):
    bsz, s, _ = k.shape
    nc = s // ML_CHUNK

    def fwd_chunk(j):
        return (j + n_lat_chunks) % nc

    def bwd_chunk(j):
        return nc - 1 - j

    def specs(chunk):
        def col(height):
            return pl.BlockSpec((1, height, ML_CHUNK), lambda b, j: (b, 0, chunk(j)))
        return [col(BRANCH_W), pl.BlockSpec((1, ML_CHUNK, BRANCH_W), lambda b, j: (b, chunk(j), 0)),
                col(BRANCH_W), col(4 * ML_HEADS)], col(BRANCH_W)

    in_f, out_f = specs(fwd_chunk)
    in_b, out_b = specs(bwd_chunk)
    return pl.pallas_call(
        _mlstm_kernel,
        grid=(bsz, nc),
        in_specs=in_f + in_b,
        out_specs=[out_f, out_b],
        out_shape=[jax.ShapeDtypeStruct((bsz, BRANCH_W, s), F32)] * 2,
        scratch_shapes=[pltpu.VMEM((2 * ML_HEADS, LANES, ML_DH), F32),
                        pltpu.VMEM((2 * ML_HEADS, LANES), F32)],
        compiler_params=_cparams("parallel", "arbitrary"),
        name="mlstm_scan",
    )(qt, k, vt, gt, qt, k, vt, gt)


def _attn_kernel(qt_ref, k_ref, vt_ref, o_ref, m_sc, acc_sc, *, n_lat, n_ctx, tk):
    m_sc[...] = jnp.full_like(m_sc, -jnp.inf)
    acc_sc[...] = jnp.zeros_like(acc_sc)

    def scores(h, start, size):
        sl = slice(h * HEAD_SLOT, (h + 1) * HEAD_SLOT)
        return _dot(k_ref[0, pl.ds(start, size), sl], qt_ref[0, sl, :])

    def keys(start, size):
        n_sub = size // ATT_SUB
        order = [(j, h) for j in range(n_sub) for h in range(MLA_HEADS)]
        pending = [scores(h, start + j * ATT_SUB, ATT_SUB) for j, h in order[:ATT_AHEAD]]
        for i, (j, h) in enumerate(order):
            off = start + j * ATT_SUB
            s = pending.pop(0)
            if i + ATT_AHEAD < len(order):
                jn, hn = order[i + ATT_AHEAD]
                pending.append(scores(hn, start + jn * ATT_SUB, ATT_SUB))
            m_old = m_sc[h]
            m_new = jnp.maximum(m_old, jnp.max(s, axis=0, keepdims=True))
            p = jnp.exp2(s - m_new)
            vt = vt_ref[0, h * V_SLOT:(h + 1) * V_SLOT, pl.ds(off, ATT_SUB)]
            acc_sc[h] = jnp.exp2(m_old - m_new) * acc_sc[h] + _dot(vt, p.astype(BF16))
            m_sc[h] = m_new

    keys(n_lat, n_ctx)

    def chunk(c, carry):
        keys(pl.multiple_of(c * tk, tk), tk)
        return carry

    is_ctx_tile = pl.program_id(1) * ATT_TQ >= n_lat
    lax.fori_loop(0, jnp.where(is_ctx_tile, 0, n_lat // tk), chunk, 0)
    out_t = jnp.concatenate([acc_sc[h, 0:V_DIM, :] / acc_sc[h, V_DIM:V_DIM + 1, :] for h in range(MLA_HEADS)],
                            axis=0)
    o_ref[0] = out_t.T.astype(o_ref.dtype)


def _key_chunk(n_keys):
    return max(t for t in range(ATT_TQ, ATT_TK + 1, ATT_TQ) if n_keys % t == 0)


def _attention(qt, k, vt, n_lat):
    bsz, s, slot_w = k.shape
    tq = ATT_TQ
    return pl.pallas_call(
        functools.partial(_attn_kernel, n_lat=n_lat, n_ctx=s - n_lat, tk=_key_chunk(n_lat)),
        grid=(bsz, s // tq),
        in_specs=[pl.BlockSpec((1, slot_w, tq), lambda b, t: (b, 0, t)),
                  pl.BlockSpec((1, s, slot_w), lambda b, t: (b, 0, 0), pipeline_mode=pl.Buffered(1)),
                  pl.BlockSpec((1, MLA_HEADS * V_SLOT, s), lambda b, t: (b, 0, 0), pipeline_mode=pl.Buffered(1))],
        out_specs=pl.BlockSpec((1, tq, BRANCH_W), lambda b, t: (b, t, 0)),
        out_shape=jax.ShapeDtypeStruct((bsz, s, BRANCH_W), BF16),
        scratch_shapes=[pltpu.VMEM((MLA_HEADS, 1, tq), F32), pltpu.VMEM((MLA_HEADS, V_SLOT, tq), F32)],
        compiler_params=_cparams("parallel", "arbitrary"),
        name="mla_attention",
    )(qt, k, vt)


def _merge_kernel(hx_ref, hf_ref, hb_ref, o_ref, ya_ref, pc_ref, prev_ref, next_ref, x_ref, mod_ref,
                  hg_ref, pw_ref, ps_ref, cw_ref, cb_ref, wg_ref, bg_ref, wb_ref, wo_ref,
                  gffn_ref, xo_ref, h2_ref, *, n_lat_tiles, n_tiles):
    t = pl.program_id(1)
    tm = ROW_TILE
    bw = BRANCH_W
    halo = POOL_HALO
    no_prev = jnp.logical_or(t == 0, t == n_lat_tiles)
    no_next = jnp.logical_or(t == n_lat_tiles - 1, t == n_tiles - 1)

    h_t = hf_ref[0] + hb_ref[0]
    normed = []
    for hd in range(ML_HEADS):
        hh = h_t[hd * ML_DH:(hd + 1) * ML_DH]
        normed.append(hh * lax.rsqrt(jnp.mean(hh * hh, axis=0, keepdims=True) + EPS))
    y_ml = (jnp.concatenate(normed, axis=0) * hg_ref[...] * jax.nn.sigmoid(o_ref[0].astype(F32))).T

    prev = jnp.where(no_prev, 0.0, prev_ref[0])
    nxt = jnp.where(no_next, 0.0, next_ref[0])
    ext = jnp.concatenate([prev, pc_ref[0], nxt], axis=0)
    pe = ext[:, 0:bw]

    def rows(arr, off):
        return arr[halo + off:halo + off + tm, :]

    r = lax.broadcasted_iota(jnp.int32, (tm, 1), 0)
    lane = lax.broadcasted_iota(jnp.int32, (tm, bw), 1)
    centre = rows(pe, 0)
    acc = centre
    mean = jnp.zeros((tm, bw), F32)
    done = 0
    for gi, w in enumerate(POOL_WINDOWS):
        for off in list(range(-(w // 2), -done)) + list(range(max(done, 1), w // 2)):
            acc = acc + rows(pe, off)
        done = w // 2
        before = jnp.where(no_prev, jnp.minimum(r, w // 2), w // 2)
        after = jnp.where(no_next, jnp.minimum(tm - r, w // 2), w // 2)
        inv = 1.0 / (before + after).astype(F32)
        mean = jnp.where(lane >= gi * POOL_GROUP, acc * inv, mean)
    y_pool = _dot((mean - centre).astype(BF16), pw_ref[...]) * ps_ref[...]

    z = ext[:, 3 * bw:4 * bw] * ext[:, bw:2 * bw]
    conv = cb_ref[...] + rows(z, -1) * cw_ref[0:1, :] + rows(z, 0) * cw_ref[1:2, :] + rows(z, 1) * cw_ref[2:3, :]
    y_conv = rows(ext[:, 2 * bw:3 * bw], 0) * conv

    hxb = hx_ref[0]
    d = hxb.shape[1]
    gates = jax.nn.sigmoid(_dot(hxb, wg_ref[...]) + bg_ref[...])
    ys = (y_ml.astype(BF16), ya_ref[0], y_pool.astype(BF16), y_conv.astype(BF16))
    merged = None
    for i, y in enumerate(ys):
        term = gates[:, i * d:(i + 1) * d] * _dot(y, wb_ref[i])
        merged = term if merged is None else merged + term
    out = _dot(merged.astype(BF16), wo_ref[...])
    x_new = x_ref[0] + mod_ref[0, 2:3, :] * out
    xo_ref[0] = x_new
    h2_ref[0] = (_rms_rows(x_new) * gffn_ref[...]) * (1.0 + mod_ref[0, 4:5, :]) + mod_ref[0, 3:4, :]


def _merge(hx, hf, hb, o, y_mla, pc, xs, mod, lw, n_lat_tiles, with_ctx):
    bsz, s, d = xs.shape
    tm = ROW_TILE
    nt = s // tm if with_ctx else n_lat_tiles
    halo = POOL_HALO
    per = tm // halo

    def row_spec(width):
        return pl.BlockSpec((1, tm, width), lambda b, t: (b, t, 0))

    def col_spec():
        return pl.BlockSpec((1, BRANCH_W, tm), lambda b, t: (b, 0, t))

    return pl.pallas_call(
        functools.partial(_merge_kernel, n_lat_tiles=n_lat_tiles, n_tiles=nt),
        grid=(bsz, nt),
        in_specs=[
            row_spec(d), col_spec(), col_spec(), col_spec(), row_spec(BRANCH_W),
            row_spec(1024),
            pl.BlockSpec((1, halo, 1024), lambda b, t: (b, jnp.maximum(t * per - 1, 0), 0)),
            pl.BlockSpec((1, halo, 1024), lambda b, t: (b, jnp.minimum((t + 1) * per, s // halo - 1), 0)),
            row_spec(d),
            pl.BlockSpec((1, 6, d), lambda b, t: (jnp.where(t >= n_lat_tiles, bsz, b), 0, 0)),
            _const_spec((BRANCH_W, tm)), _const_spec((BRANCH_W, BRANCH_W)),
            _const_spec((1, BRANCH_W)), _const_spec((3, BRANCH_W)), _const_spec((1, BRANCH_W)),
            _const_spec((d, 4 * d)), _const_spec((1, 4 * d)), _const_spec((4, BRANCH_W, d)),
            _const_spec((d, d)), _const_spec((1, d)),
        ],
        out_specs=[row_spec(d), row_spec(d)],
        out_shape=[jax.ShapeDtypeStruct((bsz, nt * tm, d), F32)] * 2,
        compiler_params=_cparams("parallel", "parallel"),
        name="mixer_merge",
    )(hx, hf, hb, o, y_mla, pc, pc, pc, xs, mod, lw["head_gain"], lw["pool_w"], lw["pool_scale"],
      lw["conv_w"], lw["conv_b"], lw["w_gate"], lw["b_gate"], lw["w_branch"], lw["w_out"], lw["g_ffn"])


def _router_kernel(h_ref, wr_ref, br_ref, pe_ref, we_ref, c16_ref):
    tm = h_ref.shape[0]
    scores = jax.nn.sigmoid(_dot(h_ref[...].astype(BF16), wr_ref[...]))
    sel = scores + br_ref[...]
    lane = lax.broadcasted_iota(jnp.int32, (tm, LANES), 1)
    member = jnp.zeros((tm, LANES), F32)
    for _ in range(TOP_K):
        mx = jnp.max(sel, axis=1, keepdims=True)
        ix = jnp.min(jnp.where(sel == mx, lane, LANES), axis=1, keepdims=True)
        hit = lane == ix
        member = jnp.where(hit, 1.0, member)
        sel = jnp.where(hit, -jnp.inf, sel)
    picked = member * scores
    weights = picked / jnp.sum(picked, axis=1, keepdims=True) * ROUTE_SCALE
    counts = jnp.sum(member, axis=0, keepdims=True)
    c16 = jnp.floor((counts + (RUN_ROWS - 1)) * (1.0 / RUN_ROWS))
    e_row = lax.broadcasted_iota(jnp.int32, (LANES, LANES), 0)
    e_col = lax.broadcasted_iota(jnp.int32, (LANES, LANES), 1)
    lower = (e_row < e_col).astype(BF16)
    o16 = _dot(jnp.broadcast_to(c16, (SUBLANES, LANES)).astype(BF16), lower)[0:1]
    row = lax.broadcasted_iota(jnp.int32, (tm, tm), 0)
    col = lax.broadcasted_iota(jnp.int32, (tm, tm), 1)
    rank = _dot((col < row).astype(BF16), member.astype(BF16))
    pos = jnp.where(member > 0.0, RUN_ROWS * o16 + rank, -1.0)
    pe_ref[0] = pos.T[0:N_EXPERTS]
    we_ref[0] = weights.T[0:N_EXPERTS]
    c16_ref[0] = c16.astype(jnp.int32)


def _router(h2, lw, tm):
    n, d = h2.shape
    nt = n // tm

    def t_spec():
        return pl.BlockSpec((1, N_EXPERTS, tm), lambda i: (i, 0, 0))

    return pl.pallas_call(
        _router_kernel,
        grid=(nt,),
        in_specs=[pl.BlockSpec((tm, d), lambda i: (i, 0)), _const_spec((d, LANES)), _const_spec((1, LANES))],
        out_specs=[t_spec(), t_spec(), pl.BlockSpec((1, 1, LANES), lambda i: (i, 0, 0))],
        out_shape=[jax.ShapeDtypeStruct((nt, N_EXPERTS, tm), F32), jax.ShapeDtypeStruct((nt, N_EXPERTS, tm), F32),
                   jax.ShapeDtypeStruct((nt, 1, LANES), jnp.int32)],
        compiler_params=_cparams("parallel"),
        name="moe_router",
    )(h2, lw["w_router"], lw["b_router"])


def _tile_row_bound(tm):
    worst = tm * TOP_K + N_EXPERTS * (RUN_ROWS - 1)
    return -(-worst // MOE_CHUNK) * MOE_CHUNK


def _selection_rows(g, grp_e_ref, pe_ref, tm):
    prow = pe_ref[0, pl.ds(grp_e_ref[0, 0, g], 1), :]
    rows = (lax.broadcasted_iota(jnp.int32, (RUN_ROWS, tm), 0) + g * RUN_ROWS).astype(F32)
    return prow, rows


def _dispatch_kernel(tail_ref, n_act_ref, grp_e_ref, gdst_ref, h_ref, pe_ref, xs_ref,
                     sel_sc, xb_sc, xp_sc, zero_sc, sem, *, n_groups):
    i = pl.program_id(0)
    tm = h_ref.shape[0]
    n_chunks = sel_sc.shape[0] // MOE_CHUNK
    per_chunk = MOE_CHUNK // RUN_ROWS

    @pl.when(i == pl.num_programs(0) - 1)
    def _():
        zero_sc[...] = jnp.zeros_like(zero_sc)

        def fill_copy(q):
            return pltpu.make_async_copy(
                zero_sc.at[pl.ds(0, RUN_ROWS)],
                xs_ref.at[pl.ds(pl.multiple_of(tail_ref[q] * RUN_ROWS, RUN_ROWS), RUN_ROWS)], sem)

        def block_copy(b):
            return pltpu.make_async_copy(
                zero_sc, xs_ref.at[pl.ds(pl.multiple_of(b * EXPERT_ROWS, EXPERT_ROWS), EXPERT_ROWS)], sem)

        def block_start(b, carry):
            block_copy(b).start()
            return carry

        def block_wait(b, carry):
            block_copy(b).wait()
            return carry

        n_blocks = xs_ref.shape[0] // EXPERT_ROWS
        lax.fori_loop(n_act_ref[0], n_blocks, block_start, 0)
        lax.fori_loop(n_act_ref[0], n_blocks, block_wait, 0)

        def fill_start(q, carry):
            @pl.when(tail_ref[q] >= 0)
            def _():
                fill_copy(q).start()
            return carry

        def fill_wait(q, carry):
            @pl.when(tail_ref[q] >= 0)
            def _():
                fill_copy(q).wait()
            return carry

        lax.fori_loop(0, tail_ref.shape[0], fill_start, 0)
        lax.fori_loop(0, tail_ref.shape[0], fill_wait, 0)

    def build(g, carry):
        prow, rows = _selection_rows(g, grp_e_ref, pe_ref, tm)
        sel_sc[pl.ds(pl.multiple_of(g * RUN_ROWS, RUN_ROWS), RUN_ROWS), :] = jnp.where(
            prow == rows, 1.0, 0.0).astype(BF16)
        return carry

    lax.fori_loop(0, n_groups, build, 0, unroll=4)
    xb_sc[...] = h_ref[...].astype(BF16)

    def run_copy(g):
        return pltpu.make_async_copy(
            xp_sc.at[pl.ds(pl.multiple_of(g * RUN_ROWS, RUN_ROWS), RUN_ROWS)],
            xs_ref.at[pl.ds(pl.multiple_of(gdst_ref[0, 0, g] * RUN_ROWS, RUN_ROWS), RUN_ROWS)], sem)

    def permute(ch):
        r0 = pl.multiple_of(ch * MOE_CHUNK, MOE_CHUNK)
        xp_sc[pl.ds(r0, MOE_CHUNK), :] = _dot(sel_sc[pl.ds(r0, MOE_CHUNK), :], xb_sc[...]).astype(BF16)

    def start_chunk(ch):
        for j in range(per_chunk):
            run_copy(ch * per_chunk + j).start()

    permute(0)

    def step(ch, carry):
        start_chunk(ch - 1)
        permute(ch)
        return carry

    lax.fori_loop(1, n_chunks, step, 0)
    start_chunk(n_chunks - 1)
    pltpu.make_async_copy(xp_sc, xs_ref.at[pl.ds(0, xp_sc.shape[0])], sem).wait()


def _dispatch(h2, pe, plan, n_rows, tm):
    n, d = h2.shape
    nt = n // tm
    rb = _tile_row_bound(tm)
    n_groups = rb // RUN_ROWS

    def smem_spec():
        return pl.BlockSpec((1, 1, n_groups), lambda i, *_: (i, 0, 0), memory_space=pltpu.SMEM)

    return pl.pallas_call(
        functools.partial(_dispatch_kernel, n_groups=n_groups),
        grid_spec=pltpu.PrefetchScalarGridSpec(
            num_scalar_prefetch=2,
            grid=(nt,),
            in_specs=[smem_spec(), smem_spec(),
                      pl.BlockSpec((tm, d), lambda i, *_: (i, 0)),
                      pl.BlockSpec((1, N_EXPERTS, tm), lambda i, *_: (i, 0, 0))],
            out_specs=pl.BlockSpec(memory_space=pl.ANY),
            scratch_shapes=[pltpu.VMEM((rb, tm), BF16), pltpu.VMEM((tm, d), BF16), pltpu.VMEM((rb, d), BF16),
                            pltpu.VMEM((EXPERT_ROWS, d), BF16), pltpu.SemaphoreType.DMA(())],
        ),
        out_shape=jax.ShapeDtypeStruct((n_rows, d), BF16),
        compiler_params=_cparams("arbitrary"),
        name="moe_dispatch",
    )(plan["tail"], plan["n_act"].reshape(1), plan["grp_e"], plan["gdst"], h2, pe)


def _expert_kernel(blk_e_ref, n_act_ref, x_ref, w1_ref, w3_ref, w2_ref, y_ref):
    del blk_e_ref

    @pl.when(pl.program_id(0) < n_act_ref[0])
    def _():
        x = x_ref[...]
        a = _dot(x, w1_ref[0, 0].astype(BF16))
        act = (a * jax.nn.sigmoid(a)) * _dot(x, w3_ref[0, 0].astype(BF16))
        y_ref[...] = _dot(act.astype(BF16), w2_ref[0, 0].astype(BF16)).astype(BF16)

    @pl.when(pl.program_id(0) >= n_act_ref[0])
    def _():
        y_ref[...] = jnp.zeros_like(y_ref)


def _experts(xs_sorted, blk_e, n_act, lw):
    n_rows, d = xs_sorted.shape
    bm = EXPERT_ROWS

    def row_map(i, blk_e_ref, n_act_ref):
        return (jnp.minimum(i, n_act_ref[0] - 1), 0)

    layer = lw["layer"]

    def w_map(i, blk_e_ref, n_act_ref):
        return (layer, blk_e_ref[i], 0, 0)

    return pl.pallas_call(
        _expert_kernel,
        grid_spec=pltpu.PrefetchScalarGridSpec(
            num_scalar_prefetch=2,
            grid=(n_rows // bm,),
            in_specs=[pl.BlockSpec((bm, d), row_map),
                      pl.BlockSpec((1, 1, d, EXPERT_FF), w_map),
                      pl.BlockSpec((1, 1, d, EXPERT_FF), w_map),
                      pl.BlockSpec((1, 1, EXPERT_FF, d), w_map)],
            out_specs=pl.BlockSpec((bm, d), lambda i, *_: (i, 0)),
        ),
        out_shape=jax.ShapeDtypeStruct((n_rows, d), BF16),
        compiler_params=_cparams("arbitrary"),
        name="moe_experts",
    )(blk_e, n_act, xs_sorted, lw["w1"], lw["w3"], lw["w2"])


def _combine_kernel(grp_e_ref, gsrc_ref, h_ref, x_ref, pe_ref, we_ref, mod_ref, modc_ref, ws13_ref,
                    ws2_ref, ys_ref, xo_ref, sel_sc, yp_sc, acc_sc, sem, *, n_groups, tiles_per_sample, n_ctx):
    i = pl.program_id(0)
    tm = h_ref.shape[0]

    n_chunks = sel_sc.shape[0] // MOE_CHUNK
    per_chunk = MOE_CHUNK // RUN_ROWS

    def start_chunk(ch):
        for j in range(per_chunk):
            g = ch * per_chunk + j
            pltpu.make_async_copy(
                ys_ref.at[pl.ds(pl.multiple_of(gsrc_ref[0, 0, g] * RUN_ROWS, RUN_ROWS), RUN_ROWS)],
                yp_sc.at[pl.ds(pl.multiple_of(g * RUN_ROWS, RUN_ROWS), RUN_ROWS)], sem).start()

    def wait_chunk(ch):
        r0 = pl.multiple_of(ch * MOE_CHUNK, MOE_CHUNK)
        pltpu.make_async_copy(ys_ref.at[pl.ds(0, MOE_CHUNK)], yp_sc.at[pl.ds(r0, MOE_CHUNK)], sem).wait()

    start_chunk(0)

    def build(g, carry):
        prow, rows = _selection_rows(g, grp_e_ref, pe_ref, tm)
        wrow = we_ref[0, pl.ds(grp_e_ref[0, 0, g], 1), :]
        sel_sc[pl.ds(pl.multiple_of(g * RUN_ROWS, RUN_ROWS), RUN_ROWS), :] = jnp.where(
            prow == rows, wrow, 0.0).astype(BF16)
        return carry

    lax.fori_loop(0, n_groups, build, 0, unroll=4)

    up = _dot(h_ref[...].astype(BF16), ws13_ref[...])
    a = up[:, 0:EXPERT_FF]
    act = (a * jax.nn.sigmoid(a)) * up[:, EXPERT_FF:2 * EXPERT_FF]
    acc_sc[...] = _dot(act.astype(BF16), ws2_ref[...])

    def gather_sum(ch):
        r0 = pl.multiple_of(ch * MOE_CHUNK, MOE_CHUNK)
        acc_sc[...] += lax.dot_general(sel_sc[pl.ds(r0, MOE_CHUNK), :], yp_sc[pl.ds(r0, MOE_CHUNK), :], TN_DIMS,
                                       preferred_element_type=F32)

    def step(ch, carry):
        wait_chunk(ch)
        start_chunk(ch + 1)
        gather_sum(ch)
        return carry

    lax.fori_loop(0, n_chunks - 1, step, 0)
    wait_chunk(n_chunks - 1)
    gather_sum(n_chunks - 1)
    r = lax.broadcasted_iota(jnp.int32, (tm, 1), 0)
    is_ctx = jnp.logical_and(i % tiles_per_sample == tiles_per_sample - 1, r >= tm - n_ctx)
    gate = jnp.where(is_ctx, modc_ref[0, 5:6, :], mod_ref[0, 5:6, :])
    xo_ref[...] = x_ref[...] + gate * acc_sc[...]


def _combine(h2, xs, pe, we, plan, mod, ys_sorted, lw, tm, tiles_per_sample, n_ctx, bsz):
    n, d = h2.shape
    nt = n // tm
    rb = _tile_row_bound(tm)
    n_groups = rb // RUN_ROWS

    def smem_spec():
        return pl.BlockSpec((1, 1, n_groups), lambda i, *_: (i, 0, 0), memory_space=pltpu.SMEM)

    def row_spec():
        return pl.BlockSpec((tm, d), lambda i, *_: (i, 0))

    def t_spec():
        return pl.BlockSpec((1, N_EXPERTS, tm), lambda i, *_: (i, 0, 0))

    return pl.pallas_call(
        functools.partial(_combine_kernel, n_groups=n_groups, tiles_per_sample=tiles_per_sample, n_ctx=n_ctx),
        grid_spec=pltpu.PrefetchScalarGridSpec(
            num_scalar_prefetch=0,
            grid=(nt,),
            in_specs=[smem_spec(), smem_spec(), row_spec(), row_spec(), t_spec(), t_spec(),
                      pl.BlockSpec((1, 6, d), lambda i, *_: (i // tiles_per_sample, 0, 0)),
                      pl.BlockSpec((1, 6, d), lambda i, *_: (bsz, 0, 0)),
                      pl.BlockSpec((d, 2 * EXPERT_FF), lambda i, *_: (0, 0), pipeline_mode=pl.Buffered(1)),
                      pl.BlockSpec((EXPERT_FF, d), lambda i, *_: (0, 0), pipeline_mode=pl.Buffered(1)),
                      pl.BlockSpec(memory_space=pl.ANY)],
            out_specs=row_spec(),
            scratch_shapes=[pltpu.VMEM((rb, tm), BF16), pltpu.VMEM((rb, d), BF16), pltpu.VMEM((tm, d), F32),
                            pltpu.SemaphoreType.DMA(())],
        ),
        out_shape=jax.ShapeDtypeStruct((n, d), F32),
        compiler_params=_cparams("arbitrary"),
        name="moe_combine",
    )(plan["grp_e"], plan["gsrc"], h2, xs, pe, we, mod, mod, lw["ws13"], lw["ws2"], ys_sorted)


def _moe_plan(c16, n_groups, spare16):
    nt = c16.shape[0]
    per_blk = EXPERT_ROWS // RUN_ROWS
    o16 = jnp.cumsum(c16, axis=1) - c16
    before16 = jnp.cumsum(c16, axis=0) - c16
    gtot16 = jnp.sum(c16, axis=0)
    gblk = (gtot16 + per_blk - 1) // per_blk
    gend_blk = jnp.cumsum(gblk)
    gstart16 = per_blk * (gend_blk - gblk)
    g = jnp.arange(n_groups, dtype=jnp.int32)
    grp_e = jnp.minimum(jnp.sum((o16 + c16)[:, None, :] <= g[None, :, None], axis=2), N_EXPERTS - 1)
    run0 = gstart16[None, :] + before16 - o16
    experts = jnp.arange(N_EXPERTS, dtype=jnp.int32)
    gdst = jnp.sum(jnp.where(grp_e[:, :, None] == experts[None, None, :], run0[:, None, :], 0), axis=2) + g[None, :]
    q = jnp.arange(per_blk, dtype=jnp.int32)
    tail = jnp.where(q[None, :] < (per_blk * gblk - gtot16)[:, None],
                     (gstart16 + gtot16)[:, None] + q[None, :], -1)
    used = g[None, :] < jnp.sum(c16, axis=1)[:, None]
    return {
        "grp_e": grp_e.astype(jnp.int32).reshape(nt, 1, n_groups),
        "gdst": jnp.where(used, gdst, spare16 + g[None, :]).astype(jnp.int32).reshape(nt, 1, n_groups),
        "gsrc": jnp.where(used, gdst, 0).astype(jnp.int32).reshape(nt, 1, n_groups),
        "tail": tail.astype(jnp.int32).reshape(-1),
        "gend_blk": gend_blk, "n_act": gend_blk[-1].astype(jnp.int32),
    }


def _moe_tile(rows_per_sample):
    return max(t for t in range(ROW_TILE, MOE_TILE_MAX + 1, ROW_TILE) if rows_per_sample % t == 0)


def _moe(h2, xs, mod, lw, rows_per_sample, n_ctx, bsz):
    n, d = h2.shape
    tile = _moe_tile(rows_per_sample)
    tiles_per_sample = rows_per_sample // tile
    nt = n // tile
    n_groups = _tile_row_bound(tile) // RUN_ROWS
    pe, we, c16 = _router(h2, lw, tile)
    worst_rows = n * TOP_K + nt * N_EXPERTS * (RUN_ROWS - 1) + N_EXPERTS * (EXPERT_ROWS - 1)
    run_blocks = -(-worst_rows // EXPERT_ROWS)
    n_blocks = run_blocks + -(-n_groups * RUN_ROWS // EXPERT_ROWS)
    plan = _moe_plan(c16[:, 0, :N_EXPERTS], n_groups, run_blocks * (EXPERT_ROWS // RUN_ROWS))
    blk = jnp.minimum(jnp.arange(n_blocks, dtype=jnp.int32), plan["n_act"] - 1)
    blk_e = jnp.minimum(jnp.sum(plan["gend_blk"][None, :] <= blk[:, None], axis=1), N_EXPERTS - 1).astype(jnp.int32)
    xs_sorted = _dispatch(h2, pe, plan, n_blocks * EXPERT_ROWS, tile)
    ys_sorted = _experts(xs_sorted, blk_e, plan["n_act"].reshape(1), lw)
    return _combine(h2, xs, pe, we, plan, mod, ys_sorted, lw, tile, tiles_per_sample, n_ctx, bsz)


def _slots(w, head_w, real_w):
    rows = w.shape[0]
    w = w.reshape(rows, -1, head_w)[:, :, :real_w]
    return jnp.pad(w, ((0, 0), (0, 0), (0, HEAD_SLOT - real_w))).reshape(rows, -1)


def _block_diag(blocks):
    n, r, c = blocks.shape
    out = jnp.zeros((n * r, n * c), blocks.dtype)
    for i in range(n):
        out = out.at[i * r:(i + 1) * r, i * c:(i + 1) * c].set(blocks[i])
    return out


def _layer_weights(l, p):
    d = p["w_in"].shape[1]
    w_in = p["w_in"][l]
    ml = 4 * BRANCH_W
    o_mla = ml + 4 * ML_HEADS
    o_pool = o_mla + Q_LORA + KV_LORA + QK_ROPE
    w_big = jnp.concatenate([
        w_in[:, :ml],
        w_in[:, o_pool:],
        w_in[:, o_mla:o_mla + Q_LORA + KV_LORA],
        w_in[:, o_mla + Q_LORA + KV_LORA:o_pool],
        w_in[:, ml:o_mla],
        jnp.zeros((d, LANES - QK_ROPE - 4 * ML_HEADS), F32),
    ], axis=1).astype(BF16)
    w_ukv = p["mla_w_ukv"][l].reshape(KV_LORA, MLA_HEADS, QK_NOPE + V_DIM)

    def gain_slots(g):
        return jnp.tile(jnp.pad(g, (0, HEAD_SLOT - QK_DIM)), MLA_HEADS)[None, :]

    return {
        "g_mix": p["g_mix"][l][None, :], "g_ffn": p["g_ffn"][l][None, :],
        "w_big": w_big,
        "g_cq": p["mla_g_cq"][l][None, :], "g_ckv": p["mla_g_ckv"][l][None, :],
        "w_uq": _slots(p["mla_w_uq"][l], QK_DIM, QK_DIM).astype(BF16),
        "w_uk": _slots(w_ukv[:, :, :QK_NOPE].reshape(KV_LORA, -1), QK_NOPE, QK_NOPE).astype(BF16),
        "w_uv": w_ukv[:, :, QK_NOPE:].reshape(KV_LORA, BRANCH_W).astype(BF16),
        "g_qn": gain_slots(p["mla_g_qn"][l]), "g_kn": gain_slots(p["mla_g_kn"][l]),
        "gate_bias": jnp.pad(p["ml_gate_bias"][l], (QK_ROPE, LANES - QK_ROPE - 4 * ML_HEADS))[None, :],
        "head_gain": jnp.broadcast_to(p["ml_head_gain"][l][:, None], (BRANCH_W, ROW_TILE)),
        "pool_w": _block_diag(p["pool_w"][l]).astype(BF16),
        "pool_scale": p["pool_scale"][l][None, :],
        "conv_w": p["conv_w"][l], "conv_b": p["conv_b"][l][None, :],
        "w_gate": jnp.concatenate(list(p["w_gate"][l]), axis=1).astype(BF16),
        "b_gate": p["b_gate"][l].reshape(1, -1),
        "w_branch": p["w_branch"][l].astype(BF16),
        "w_out": p["w_out"][l].astype(BF16),
        "w_router": jnp.pad(p["moe_w_router"][l], ((0, 0), (0, LANES - N_EXPERTS))).astype(BF16),
        "b_router": jnp.pad(p["moe_b_router"][l], (0, LANES - N_EXPERTS), constant_values=-1e30)[None, :],
        "layer": l, "w1": p["moe_w1"], "w3": p["moe_w3"], "w2": p["moe_w2"],
        "ws13": jnp.concatenate([p["moe_ws1"][l], p["moe_ws3"][l]], axis=1).astype(BF16),
        "ws2": p["moe_ws2"][l].astype(BF16),
    }


def _rope_tables(n_lat, n_ctx):
    t = jnp.arange(n_lat)
    n_freq = QK_ROPE // 4
    inv = ROPE_THETA ** (-jnp.arange(n_freq, dtype=F32) / n_freq)
    ang_r = (t // GRID_W).astype(F32)[:, None] * inv
    ang_c = (t % GRID_W).astype(F32)[:, None] * inv
    cos4 = jnp.concatenate([jnp.cos(ang_r)] * 2 + [jnp.cos(ang_c)] * 2, axis=1)
    zero = jnp.zeros_like(ang_r)
    sin_a = jnp.concatenate([-jnp.sin(ang_r), zero, -jnp.sin(ang_c), zero], axis=1)
    sin_b = jnp.concatenate([zero, jnp.sin(ang_r), zero, jnp.sin(ang_c)], axis=1)

    def slot(a, fill):
        a = jnp.pad(a, ((0, 0), (QK_NOPE, 0)), constant_values=fill)
        a = jnp.pad(a, ((0, 0), (0, HEAD_SLOT - QK_DIM)), constant_values=fill)
        return jnp.pad(a, ((0, n_ctx), (0, 0)), constant_values=fill)

    return {"cos": slot(cos4, 1.0), "sin_a": slot(sin_a, 0.0), "sin_b": slot(sin_b, 0.0),
            "ones_slot": _block_diag(jnp.ones((MLA_HEADS, HEAD_SLOT, HEAD_SLOT), BF16))}


def kernel(x, c, ctx, c_ctx, w_mod, b_mod, g_mix, g_ffn, w_in, ml_gate_bias, ml_head_gain, mla_g_cq, mla_g_ckv,
           mla_w_uq, mla_w_ukv, mla_g_qn, mla_g_kn, pool_w, pool_scale, conv_w, conv_b, w_gate, b_gate, w_branch,
           w_out, moe_w_router, moe_b_router, moe_w1, moe_w3, moe_w2, moe_ws1, moe_ws3, moe_ws2):
    p = dict(g_mix=g_mix, g_ffn=g_ffn, w_in=w_in, ml_gate_bias=ml_gate_bias, ml_head_gain=ml_head_gain,
             mla_g_cq=mla_g_cq, mla_g_ckv=mla_g_ckv, mla_w_uq=mla_w_uq, mla_w_ukv=mla_w_ukv, mla_g_qn=mla_g_qn,
             mla_g_kn=mla_g_kn, pool_w=pool_w, pool_scale=pool_scale, conv_w=conv_w, conv_b=conv_b,
             w_gate=w_gate, b_gate=b_gate, w_branch=w_branch, w_out=w_out, moe_w_router=moe_w_router,
             moe_b_router=moe_b_router, moe_w1=moe_w1, moe_w3=moe_w3, moe_w2=moe_w2, moe_ws1=moe_ws1,
             moe_ws3=moe_ws3, moe_ws2=moe_ws2)
    bsz, n_lat, d = x.shape
    n_ctx = ctx.shape[1]
    depth = w_mod.shape[0]
    s = n_lat + n_ctx
    assert n_ctx == ROW_TILE == ATT_TQ and n_lat % ROW_TILE == 0 and n_lat % GRID_W == 0
    n_lat_tiles = n_lat // ROW_TILE

    mod_rows = -(-(bsz + 1) // SUBLANES) * SUBLANES
    cc = jnp.concatenate([c, c_ctx[None, :], jnp.zeros((mod_rows - bsz - 1, d), F32)], axis=0)
    mods = _modulation(cc, w_mod, b_mod).reshape(depth, mod_rows, 6, d)
    rope = _rope_tables(n_lat, n_ctx)
    xs = jnp.concatenate([x, ctx], axis=1)

    for l in range(depth):
        lw = _layer_weights(l, p)
        mod = mods[l]
        hx, mq, mk, mv, o, g, pc, q, k, v = _in_proj(xs, mod, lw, rope, n_lat_tiles)
        hf, hb = _mlstm(mq, mk, mv, g, n_lat // ML_CHUNK)
        y_mla = _attention(q, k, v, n_lat)
        with_ctx = l + 1 < depth
        rows = s if with_ctx else n_lat
        xs, h2 = _merge(hx, hf, hb, o, y_mla, pc, xs, mod, lw, n_lat_tiles, with_ctx)
        xs = _moe(h2.reshape(bsz * rows, d), xs.reshape(bsz * rows, d), mod, lw, rows, n_ctx if with_ctx else 0,
                  bsz).reshape(bsz, rows, d)
    return xs
```

```python
import functools
import math

import jax
import jax.numpy as jnp
from jax import lax
from jax.experimental import pallas as pl
from jax.experimental.pallas import tpu as pltpu

GRID_W = 64
BRANCH_W = 256
EPS = 1e-6
ML_HEADS = 4
ML_DH = BRANCH_W // ML_HEADS
ML_CHUNK = 128
MLA_HEADS = 4
Q_LORA = 256
KV_LORA = 128
QK_NOPE = 64
QK_ROPE = 32
QK_DIM = QK_NOPE + QK_ROPE
V_DIM = BRANCH_W // MLA_HEADS
V_SLOT = V_DIM + 16
ROPE_THETA = 10000.0
POOL_WINDOWS = (2, 4, 8, 16)
POOL_GROUP = BRANCH_W // len(POOL_WINDOWS)
POOL_HALO = max(POOL_WINDOWS) // 2
N_EXPERTS = 64
TOP_K = 6
EXPERT_FF = 256
ROUTE_SCALE = 2.5

LANES = 128
SUBLANES = 8
HEAD_SLOT = LANES
ROW_TILE = 256
ATT_TQ = 256
ATT_TK = 8192
ATT_SUB = 128
ATT_AHEAD = 8
EXPERT_ROWS = 1024
MOE_TILE_MAX = 768
RUN_ROWS = 16
MOE_CHUNK = 512
VMEM_LIMIT = 56 * 1024 * 1024

F32 = jnp.float32
BF16 = jnp.bfloat16
NT_DIMS = (((1,), (1,)), ((), ()))
TN_DIMS = (((0,), (0,)), ((), ()))


def _cparams(*sem):
    return pltpu.CompilerParams(dimension_semantics=sem, vmem_limit_bytes=VMEM_LIMIT)


def _const_spec(shape):
    nd = len(shape)
    return pl.BlockSpec(shape, lambda *_: (0,) * nd, pipeline_mode=pl.Buffered(1))


def _dot(a, b):
    return jnp.dot(a, b, preferred_element_type=F32)


def _split_dot(a_f32, ones_bf16):
    hi = a_f32.astype(BF16)
    r1 = a_f32 - hi.astype(F32)
    mid = r1.astype(BF16)
    lo = (r1 - mid.astype(F32)).astype(BF16)
    return _dot(hi, ones_bf16) + _dot(mid, ones_bf16) + _dot(lo, ones_bf16)


def _rms_rows(x):
    return x * lax.rsqrt(jnp.mean(x * x, axis=-1, keepdims=True) + EPS)


def _mod_kernel(c_ref, w_ref, b_ref, o_ref):
    c = c_ref[...]
    a = (c * jax.nn.sigmoid(c)).astype(BF16)
    o_ref[0] = _dot(a, w_ref[0].astype(BF16)) + b_ref[0]


def _modulation(cc, w_mod, b_mod):
    depth, d, d6 = w_mod.shape
    rows = cc.shape[0]
    tn = 1536
    return pl.pallas_call(
        _mod_kernel,
        grid=(depth, d6 // tn),
        in_specs=[
            pl.BlockSpec((rows, d), lambda l, n: (0, 0)),
            pl.BlockSpec((1, d, tn), lambda l, n: (l, 0, n)),
            pl.BlockSpec((1, 1, tn), lambda l, n: (l, 0, n)),
        ],
        out_specs=pl.BlockSpec((1, rows, tn), lambda l, n: (l, 0, n)),
        out_shape=jax.ShapeDtypeStruct((depth, rows, d6), F32),
        compiler_params=_cparams("parallel", "parallel"),
        name="modulation",
    )(cc, w_mod, b_mod.reshape(depth, 1, d6))


def _group_mean_sq(x, ones_bd, width):
    sq = x * x
    hi = sq.astype(BF16)
    lo = (sq - hi.astype(F32)).astype(BF16)
    return (_dot(hi, ones_bd) + _dot(lo, ones_bd)) * (1.0 / width)


def _in_kernel(x_ref, mod_ref, gmix_ref, w_ref, gb_ref, gcq_ref, gckv_ref, wuq_ref, wuk_ref, wuv_ref,
               gq_ref, gk_ref, ones_ref, cos_ref, sa_ref, sb_ref,
               hx_ref, mq_ref, mk_ref, mv_ref, mo_ref, g_ref, pc_ref, q_ref, k_ref, v_ref):
    x = x_ref[0]
    shift = mod_ref[0, 0:1, :]
    scale = mod_ref[0, 1:2, :]
    hx = (_rms_rows(x) * gmix_ref[...]) * (1.0 + scale) + shift
    hxb = hx.astype(BF16)
    hx_ref[0] = hxb
    p = _dot(hxb, w_ref[...])
    bw = BRANCH_W
    mq_ref[0] = p[:, 0:bw].T.astype(BF16)
    mk_ref[0] = (p[:, bw:2 * bw] * (ML_DH ** -0.5)).astype(BF16)
    mv_ref[0] = p[:, 2 * bw:3 * bw].T.astype(BF16)
    mo_ref[0] = p[:, 3 * bw:4 * bw].T.astype(BF16)
    pc_ref[0] = p[:, 1024:2048]
    cq = p[:, 2048:2048 + Q_LORA]
    ckv = p[:, 2304:2304 + KV_LORA]
    misc = p[:, 2432:2432 + LANES]
    g_ref[0] = (misc + gb_ref[...]).T[QK_ROPE:QK_ROPE + 4 * ML_HEADS]
    lane = lax.broadcasted_iota(jnp.int32, misc.shape, 1)
    kr_slot = jnp.where(jnp.logical_and(lane >= QK_NOPE, lane < QK_DIM), pltpu.roll(misc, QK_NOPE, 1), 0.0)
    kr4 = jnp.concatenate([kr_slot] * MLA_HEADS, axis=1)

    cqn = (_rms_rows(cq) * gcq_ref[...]).astype(BF16)
    ckvn = (_rms_rows(ckv) * gckv_ref[...]).astype(BF16)
    q_pre = _dot(cqn, wuq_ref[...])
    k_pre = _dot(ckvn, wuk_ref[...]) + kr4
    v_t = _dot(ckvn, wuv_ref[...]).T
    tm = v_t.shape[1]
    extra = V_SLOT - V_DIM
    one_row = jnp.where(lax.broadcasted_iota(jnp.int32, (extra, tm), 0) == 0, 1.0, 0.0)
    v_ref[0] = jnp.concatenate(
        sum([[v_t[h * V_DIM:(h + 1) * V_DIM], one_row] for h in range(MLA_HEADS)], []), axis=0).astype(BF16)

    ones_bd = ones_ref[...]
    cos = jnp.concatenate([cos_ref[...]] * MLA_HEADS, axis=1)
    sa = jnp.concatenate([sa_ref[...]] * MLA_HEADS, axis=1)
    sb = jnp.concatenate([sb_ref[...]] * MLA_HEADS, axis=1)
    width = MLA_HEADS * HEAD_SLOT
    half = QK_ROPE // 4

    def norm_rope(t, gain):
        t = t * lax.rsqrt(_group_mean_sq(t, ones_bd, QK_DIM) + EPS) * gain
        return t * cos + pltpu.roll(t, width - half, 1) * sa + pltpu.roll(t, half, 1) * sb

    q = norm_rope(q_pre, gq_ref[...]) * (QK_DIM ** -0.5 * math.log2(math.e))
    q_ref[0] = q.T.astype(BF16)
    k_ref[0] = norm_rope(k_pre, gk_ref[...]).astype(BF16)


def _in_proj(xs, mod, lw, rope, n_lat_tiles):
    bsz, s, d = xs.shape
    tm = ROW_TILE
    nt = s // tm
    wcols = lw["w_big"].shape[1]
    slot_w = MLA_HEADS * HEAD_SLOT

    def row_spec(width):
        return pl.BlockSpec((1, tm, width), lambda t, b: (b, t, 0))

    def tab_spec():
        return pl.BlockSpec((tm, HEAD_SLOT), lambda t, b: (t, 0))

    def col_spec(height):
        return pl.BlockSpec((1, height, tm), lambda t, b: (b, 0, t))

    v_rows = MLA_HEADS * V_SLOT
    outs = [("row", d, BF16), ("col", BRANCH_W, BF16), ("row", BRANCH_W, BF16), ("col", BRANCH_W, BF16),
            ("col", BRANCH_W, BF16), ("col", 4 * ML_HEADS, F32), ("row", 1024, F32),
            ("col", slot_w, BF16), ("row", slot_w, BF16), ("col", v_rows, BF16)]
    out_specs = [row_spec(w) if kind == "row" else col_spec(w) for kind, w, _ in outs]
    out_shape = [jax.ShapeDtypeStruct((bsz, s, w) if kind == "row" else (bsz, w, s), dt) for kind, w, dt in outs]
    return pl.pallas_call(
        _in_kernel,
        grid=(nt, bsz),
        in_specs=[
            row_spec(d),
            pl.BlockSpec((1, 6, d), lambda t, b: (jnp.where(t >= n_lat_tiles, bsz, b), 0, 0)),
            _const_spec((1, d)),
            _const_spec((d, wcols)),
            _const_spec((1, LANES)),
            _const_spec((1, Q_LORA)),
            _const_spec((1, KV_LORA)),
            _const_spec((Q_LORA, slot_w)),
            _const_spec((KV_LORA, slot_w)),
            _const_spec((KV_LORA, BRANCH_W)),
            _const_spec((1, slot_w)),
            _const_spec((1, slot_w)),
            _const_spec((slot_w, slot_w)),
            tab_spec(), tab_spec(), tab_spec(),
        ],
        out_specs=out_specs,
        out_shape=out_shape,
        compiler_params=_cparams("parallel", "parallel"),
        name="norm_in_proj",
    )(xs, mod, lw["g_mix"], lw["w_big"], lw["gate_bias"], lw["g_cq"], lw["g_ckv"], lw["w_uq"], lw["w_uk"], lw["w_uv"],
      lw["g_qn"], lw["g_kn"], rope["ones_slot"], rope["cos"], rope["sin_a"], rope["sin_b"])


def _log_sigmoid(x):
    return jnp.minimum(x, 0.0) - jnp.log1p(jnp.exp(-jnp.abs(x)))


def _mlstm_kernel(qt_f, k_f, vt_f, g_f, qt_b, k_b, vt_b, g_b, hf_ref, hb_ref, c_sc, m_sc):
    @pl.when(pl.program_id(1) == 0)
    def _():
        c_sc[...] = jnp.zeros_like(c_sc)
        m_sc[...] = jnp.zeros_like(m_sc)

    L = ML_CHUNK
    dh = ML_DH
    row = lax.broadcasted_iota(jnp.int32, (L, L), 0)
    col = lax.broadcasted_iota(jnp.int32, (L, L), 1)
    diag = row == col
    ones_ll = jnp.ones((L, L), BF16)
    one_rows = jnp.where(lax.broadcasted_iota(jnp.int32, (LANES - dh, L), 0) == 0, 1.0, 0.0).astype(BF16)

    dirs = ((qt_f, k_f, vt_f, g_f), (qt_b, k_b, vt_b, g_b))
    units = []
    for d, (qt_ref, k_ref, vt_ref, g_ref) in enumerate(dirs):
        a = g_ref[0]
        lf = _log_sigmoid(a)
        valid = (row <= col) if d == 0 else (row >= col)
        b_rows = _split_dot(lf, valid.astype(BF16))
        last = L - 1 if d == 0 else 0
        for h in range(ML_HEADS):
            ci = d * 2 * ML_HEADS + h
            cf = ci + ML_HEADS
            st = d * ML_HEADS + h
            u = {"st": st, "valid": valid}
            bt = b_rows[cf:cf + 1, :]
            li = a[ci:ci + 1, :]
            b_end = bt[:, last:last + 1]
            m_st = m_sc[st:st + 1, 0:1]
            qt = qt_ref[0, h * dh:(h + 1) * dh, :]
            k = k_ref[0, :, h * dh:(h + 1) * dh]
            u["vt"] = vt_ref[0, h * dh:(h + 1) * dh, :]
            u["bt"] = bt
            u["inter"] = bt + m_st
            src = jnp.where(diag, bt - li, 0.0)
            hi = src.astype(BF16)
            lo = (src - hi.astype(F32)).astype(BF16)
            u["src"] = _dot(hi, ones_ll) + _dot(lo, ones_ll)
            w_log = b_end - bt + li
            u["m_new"] = jnp.maximum(b_end + m_st, jnp.max(w_log, axis=1, keepdims=True))
            u["decay"] = jnp.exp(b_end + m_st - u["m_new"])
            v_aug = jnp.concatenate([u["vt"], one_rows], axis=0)
            vw = (v_aug.astype(F32) * jnp.exp(w_log - u["m_new"])).astype(BF16)
            u["s_kq"] = _dot(k, qt)
            u["qc"] = _dot(c_sc[st].astype(BF16), qt)
            u["upd"] = _dot(vw, k)
            units.append(u)

    for u in units:
        d_log = jnp.where(u["valid"], u["bt"] - u["src"], -jnp.inf)
        u["m_t"] = jnp.maximum(u["inter"], jnp.max(d_log, axis=0, keepdims=True))
        w_st = jnp.exp(d_log - u["m_t"]) * u["s_kq"]
        u["w_sum"] = jnp.sum(w_st, axis=0, keepdims=True)
        u["pv"] = _dot(u["vt"], w_st.astype(BF16))

    outs = []
    for u in units:
        a_inter = jnp.exp(u["inter"] - u["m_t"])
        num = a_inter * u["qc"][0:dh, :] + u["pv"]
        den = a_inter * u["qc"][dh:dh + 1, :] + u["w_sum"]
        outs.append(num / jnp.maximum(jnp.abs(den), jnp.exp(-u["m_t"])))
        st = u["st"]
        c_sc[st] = u["decay"] * c_sc[st] + u["upd"]
        m_sc[st:st + 1, :] = jnp.broadcast_to(u["m_new"], (1, LANES))
    hf_ref[0] = jnp.concatenate(outs[0:ML_HEADS], axis=0)
    hb_ref[0] = jnp.concatenate(outs[ML_HEADS:], axis=0)


def _mlstm(qt, k, vt, gt, n_lat_chunks):
    bsz, s, _ = k.shape
    nc = s // ML_CHUNK

    def fwd_chunk(j):
        return (j + n_lat_chunks) % nc

    def bwd_chunk(j):
        return nc - 1 - j

    def specs(chunk):
        def col(height):
            return pl.BlockSpec((1, height, ML_CHUNK), lambda b, j: (b, 0, chunk(j)))
        return [col(BRANCH_W), pl.BlockSpec((1, ML_CHUNK, BRANCH_W), lambda b, j: (b, chunk(j), 0)),
                col(BRANCH_W), col(4 * ML_HEADS)], col(BRANCH_W)

    in_f, out_f = specs(fwd_chunk)
    in_b, out_b = specs(bwd_chunk)
    return pl.pallas_call(
        _mlstm_kernel,
        grid=(bsz, nc),
        in_specs=in_f + in_b,
        out_specs=[out_f, out_b],
        out_shape=[jax.ShapeDtypeStruct((bsz, BRANCH_W, s), F32)] * 2,
        scratch_shapes=[pltpu.VMEM((2 * ML_HEADS, LANES, ML_DH), F32),
                        pltpu.VMEM((2 * ML_HEADS, LANES), F32)],
        compiler_params=_cparams("parallel", "arbitrary"),
        name="mlstm_scan",
    )(qt, k, vt, gt, qt, k, vt, gt)


def _attn_kernel(qt_ref, k_ref, vt_ref, o_ref, m_sc, acc_sc, *, n_lat, n_ctx, tk):
    m_sc[...] = jnp.full_like(m_sc, -jnp.inf)
    acc_sc[...] = jnp.zeros_like(acc_sc)

    def scores(h, start, size):
        sl = slice(h * HEAD_SLOT, (h + 1) * HEAD_SLOT)
        return _dot(k_ref[0, pl.ds(start, size), sl], qt_ref[0, sl, :])

    def keys(start, size):
        n_sub = size // ATT_SUB
        order = [(j, h) for j in range(n_sub) for h in range(MLA_HEADS)]
        pending = [scores(h, start + j * ATT_SUB, ATT_SUB) for j, h in order[:ATT_AHEAD]]
        for i, (j, h) in enumerate(order):
            off = start + j * ATT_SUB
            s = pending.pop(0)
            if i + ATT_AHEAD < len(order):
                jn, hn = order[i + ATT_AHEAD]
                pending.append(scores(hn, start + jn * ATT_SUB, ATT_SUB))
            m_old = m_sc[h]
            m_new = jnp.maximum(m_old, jnp.max(s, axis=0, keepdims=True))
            p = jnp.exp2(s - m_new)
            vt = vt_ref[0, h * V_SLOT:(h + 1) * V_SLOT, pl.ds(off, ATT_SUB)]
            acc_sc[h] = jnp.exp2(m_old - m_new) * acc_sc[h] + _dot(vt, p.astype(BF16))
            m_sc[h] = m_new

    keys(n_lat, n_ctx)

    def chunk(c, carry):
        keys(pl.multiple_of(c * tk, tk), tk)
        return carry

    is_ctx_tile = pl.program_id(1) * ATT_TQ >= n_lat
    lax.fori_loop(0, jnp.where(is_ctx_tile, 0, n_lat // tk), chunk, 0)
    out_t = jnp.concatenate([acc_sc[h, 0:V_DIM, :] / acc_sc[h, V_DIM:V_DIM + 1, :] for h in range(MLA_HEADS)],
                            axis=0)
    o_ref[0] = out_t.T.astype(o_ref.dtype)


def _key_chunk(n_keys):
    return max(t for t in range(ATT_TQ, ATT_TK + 1, ATT_TQ) if n_keys % t == 0)


def _attention(qt, k, vt, n_lat):
    bsz, s, slot_w = k.shape
    tq = ATT_TQ
    return pl.pallas_call(
        functools.partial(_attn_kernel, n_lat=n_lat, n_ctx=s - n_lat, tk=_key_chunk(n_lat)),
        grid=(bsz, s // tq),
        in_specs=[pl.BlockSpec((1, slot_w, tq), lambda b, t: (b, 0, t)),
                  pl.BlockSpec((1, s, slot_w), lambda b, t: (b, 0, 0), pipeline_mode=pl.Buffered(1)),
                  pl.BlockSpec((1, MLA_HEADS * V_SLOT, s), lambda b, t: (b, 0, 0), pipeline_mode=pl.Buffered(1))],
        out_specs=pl.BlockSpec((1, tq, BRANCH_W), lambda b, t: (b, t, 0)),
        out_shape=jax.ShapeDtypeStruct((bsz, s, BRANCH_W), BF16),
        scratch_shapes=[pltpu.VMEM((MLA_HEADS, 1, tq), F32), pltpu.VMEM((MLA_HEADS, V_SLOT, tq), F32)],
        compiler_params=_cparams("parallel", "arbitrary"),
        name="mla_attention",
    )(qt, k, vt)


def _merge_kernel(hx_ref, hf_ref, hb_ref, o_ref, ya_ref, pc_ref, prev_ref, next_ref, x_ref, mod_ref,
                  hg_ref, pw_ref, ps_ref, cw_ref, cb_ref, wg_ref, bg_ref, wb_ref, wo_ref,
                  gffn_ref, xo_ref, h2_ref, *, n_lat_tiles, n_tiles):
    t = pl.program_id(1)
    tm = ROW_TILE
    bw = BRANCH_W
    halo = POOL_HALO
    no_prev = jnp.logical_or(t == 0, t == n_lat_tiles)
    no_next = jnp.logical_or(t == n_lat_tiles - 1, t == n_tiles - 1)

    h_t = hf_ref[0] + hb_ref[0]
    normed = []
    for hd in range(ML_HEADS):
        hh = h_t[hd * ML_DH:(hd + 1) * ML_DH]
        normed.append(hh * lax.rsqrt(jnp.mean(hh * hh, axis=0, keepdims=True) + EPS))
    y_ml = (jnp.concatenate(normed, axis=0) * hg_ref[...] * jax.nn.sigmoid(o_ref[0].astype(F32))).T

    prev = jnp.where(no_prev, 0.0, prev_ref[0])
    nxt = jnp.where(no_next, 0.0, next_ref[0])
    ext = jnp.concatenate([prev, pc_ref[0], nxt], axis=0)
    pe = ext[:, 0:bw]

    def rows(arr, off):
        return arr[halo + off:halo + off + tm, :]

    r = lax.broadcasted_iota(jnp.int32, (tm, 1), 0)
    lane = lax.broadcasted_iota(jnp.int32, (tm, bw), 1)
    centre = rows(pe, 0)
    acc = centre
    mean = jnp.zeros((tm, bw), F32)
    done = 0
    for gi, w in enumerate(POOL_WINDOWS):
        for off in list(range(-(w // 2), -done)) + list(range(max(done, 1), w // 2)):
            acc = acc + rows(pe, off)
        done = w // 2
        before = jnp.where(no_prev, jnp.minimum(r, w // 2), w // 2)
        after = jnp.where(no_next, jnp.minimum(tm - r, w // 2), w // 2)
        inv = 1.0 / (before + after).astype(F32)
        mean = jnp.where(lane >= gi * POOL_GROUP, acc * inv, mean)
    y_pool = _dot((mean - centre).astype(BF16), pw_ref[...]) * ps_ref[...]

    z = ext[:, 3 * bw:4 * bw] * ext[:, bw:2 * bw]
    conv = cb_ref[...] + rows(z, -1) * cw_ref[0:1, :] + rows(z, 0) * cw_ref[1:2, :] + rows(z, 1) * cw_ref[2:3, :]
    y_conv = rows(ext[:, 2 * bw:3 * bw], 0) * conv

    hxb = hx_ref[0]
    d = hxb.shape[1]
    gates = jax.nn.sigmoid(_dot(hxb, wg_ref[...]) + bg_ref[...])
    ys = (y_ml.astype(BF16), ya_ref[0], y_pool.astype(BF16), y_conv.astype(BF16))
    merged = None
    for i, y in enumerate(ys):
        term = gates[:, i * d:(i + 1) * d] * _dot(y, wb_ref[i])
        merged = term if merged is None else merged + term
    out = _dot(merged.astype(BF16), wo_ref[...])
    x_new = x_ref[0] + mod_ref[0, 2:3, :] * out
    xo_ref[0] = x_new
    h2_ref[0] = (_rms_rows(x_new) * gffn_ref[...]) * (1.0 + mod_ref[0, 4:5, :]) + mod_ref[0, 3:4, :]


def _merge(hx, hf, hb, o, y_mla, pc, xs, mod, lw, n_lat_tiles, with_ctx):
    bsz, s, d = xs.shape
    tm = ROW_TILE
    nt = s // tm if with_ctx else n_lat_tiles
    halo = POOL_HALO
    per = tm // halo

    def row_spec(width):
        return pl.BlockSpec((1, tm, width), lambda b, t: (b, t, 0))

    def col_spec():
        return pl.BlockSpec((1, BRANCH_W, tm), lambda b, t: (b, 0, t))

    return pl.pallas_call(
        functools.partial(_merge_kernel, n_lat_tiles=n_lat_tiles, n_tiles=nt),
        grid=(bsz, nt),
        in_specs=[
            row_spec(d), col_spec(), col_spec(), col_spec(), row_spec(BRANCH_W),
            row_spec(1024),
            pl.BlockSpec((1, halo, 1024), lambda b, t: (b, jnp.maximum(t * per - 1, 0), 0)),
            pl.BlockSpec((1, halo, 1024), lambda b, t: (b, jnp.minimum((t + 1) * per, s // halo - 1), 0)),
            row_spec(d),
            pl.BlockSpec((1, 6, d), lambda b, t: (jnp.where(t >= n_lat_tiles, bsz, b), 0, 0)),
            _const_spec((BRANCH_W, tm)), _const_spec((BRANCH_W, BRANCH_W)),
            _const_spec((1, BRANCH_W)), _const_spec((3, BRANCH_W)), _const_spec((1, BRANCH_W)),
            _const_spec((d, 4 * d)), _const_spec((1, 4 * d)), _const_spec((4, BRANCH_W, d)),
            _const_spec((d, d)), _const_spec((1, d)),
        ],
        out_specs=[row_spec(d), row_spec(d)],
        out_shape=[jax.ShapeDtypeStruct((bsz, nt * tm, d), F32)] * 2,
        compiler_params=_cparams("parallel", "parallel"),
        name="mixer_merge",
    )(hx, hf, hb, o, y_mla, pc, pc, pc, xs, mod, lw["head_gain"], lw["pool_w"], lw["pool_scale"],
      lw["conv_w"], lw["conv_b"], lw["w_gate"], lw["b_gate"], lw["w_branch"], lw["w_out"], lw["g_ffn"])


def _router_kernel(h_ref, wr_ref, br_ref, pe_ref, we_ref, c16_ref):
    tm = h_ref.shape[0]
    scores = jax.nn.sigmoid(_dot(h_ref[...].astype(BF16), wr_ref[...]))
    sel = scores + br_ref[...]
    lane = lax.broadcasted_iota(jnp.int32, (tm, LANES), 1)
    member = jnp.zeros((tm, LANES), F32)
    for _ in range(TOP_K):
        mx = jnp.max(sel, axis=1, keepdims=True)
        ix = jnp.min(jnp.where(sel == mx, lane, LANES), axis=1, keepdims=True)
        hit = lane == ix
        member = jnp.where(hit, 1.0, member)
        sel = jnp.where(hit, -jnp.inf, sel)
    picked = member * scores
    weights = picked / jnp.sum(picked, axis=1, keepdims=True) * ROUTE_SCALE
    counts = jnp.sum(member, axis=0, keepdims=True)
    c16 = jnp.floor((counts + (RUN_ROWS - 1)) * (1.0 / RUN_ROWS))
    e_row = lax.broadcasted_iota(jnp.int32, (LANES, LANES), 0)
    e_col = lax.broadcasted_iota(jnp.int32, (LANES, LANES), 1)
    lower = (e_row < e_col).astype(BF16)
    o16 = _dot(jnp.broadcast_to(c16, (SUBLANES, LANES)).astype(BF16), lower)[0:1]
    row = lax.broadcasted_iota(jnp.int32, (tm, tm), 0)
    col = lax.broadcasted_iota(jnp.int32, (tm, tm), 1)
    rank = _dot((col < row).astype(BF16), member.astype(BF16))
    pos = jnp.where(member > 0.0, RUN_ROWS * o16 + rank, -1.0)
    pe_ref[0] = pos.T[0:N_EXPERTS]
    we_ref[0] = weights.T[0:N_EXPERTS]
    c16_ref[0] = c16.astype(jnp.int32)


def _router(h2, lw, tm):
    n, d = h2.shape
    nt = n // tm

    def t_spec():
        return pl.BlockSpec((1, N_EXPERTS, tm), lambda i: (i, 0, 0))

    return pl.pallas_call(
        _router_kernel,
        grid=(nt,),
        in_specs=[pl.BlockSpec((tm, d), lambda i: (i, 0)), _const_spec((d, LANES)), _const_spec((1, LANES))],
        out_specs=[t_spec(), t_spec(), pl.BlockSpec((1, 1, LANES), lambda i: (i, 0, 0))],
        out_shape=[jax.ShapeDtypeStruct((nt, N_EXPERTS, tm), F32), jax.ShapeDtypeStruct((nt, N_EXPERTS, tm), F32),
                   jax.ShapeDtypeStruct((nt, 1, LANES), jnp.int32)],
        compiler_params=_cparams("parallel"),
        name="moe_router",
    )(h2, lw["w_router"], lw["b_router"])


def _tile_row_bound(tm):
    worst = tm * TOP_K + N_EXPERTS * (RUN_ROWS - 1)
    return -(-worst // MOE_CHUNK) * MOE_CHUNK


def _selection_rows(g, grp_e_ref, pe_ref, tm):
    prow = pe_ref[0, pl.ds(grp_e_ref[0, 0, g], 1), :]
    rows = (lax.broadcasted_iota(jnp.int32, (RUN_ROWS, tm), 0) + g * RUN_ROWS).astype(F32)
    return prow, rows


def _dispatch_kernel(tail_ref, n_act_ref, grp_e_ref, gdst_ref, h_ref, pe_ref, xs_ref,
                     sel_sc, xb_sc, xp_sc, zero_sc, sem, *, n_groups):
    i = pl.program_id(0)
    tm = h_ref.shape[0]
    n_chunks = sel_sc.shape[0] // MOE_CHUNK
    per_chunk = MOE_CHUNK // RUN_ROWS

    @pl.when(i == pl.num_programs(0) - 1)
    def _():
        zero_sc[...] = jnp.zeros_like(zero_sc)

        def fill_copy(q):
            return pltpu.make_async_copy(
                zero_sc.at[pl.ds(0, RUN_ROWS)],
                xs_ref.at[pl.ds(pl.multiple_of(tail_ref[q] * RUN_ROWS, RUN_ROWS), RUN_ROWS)], sem)

        def block_copy(b):
            return pltpu.make_async_copy(
                zero_sc, xs_ref.at[pl.ds(pl.multiple_of(b * EXPERT_ROWS, EXPERT_ROWS), EXPERT_ROWS)], sem)

        def block_start(b, carry):
            block_copy(b).start()
            return carry

        def block_wait(b, carry):
            block_copy(b).wait()
            return carry

        n_blocks = xs_ref.shape[0] // EXPERT_ROWS
        lax.fori_loop(n_act_ref[0], n_blocks, block_start, 0)
        lax.fori_loop(n_act_ref[0], n_blocks, block_wait, 0)

        def fill_start(q, carry):
            @pl.when(tail_ref[q] >= 0)
            def _():
                fill_copy(q).start()
            return carry

        def fill_wait(q, carry):
            @pl.when(tail_ref[q] >= 0)
            def _():
                fill_copy(q).wait()
            return carry

        lax.fori_loop(0, tail_ref.shape[0], fill_start, 0)
        lax.fori_loop(0, tail_ref.shape[0], fill_wait, 0)

    def build(g, carry):
        prow, rows = _selection_rows(g, grp_e_ref, pe_ref, tm)
        sel_sc[pl.ds(pl.multiple_of(g * RUN_ROWS, RUN_ROWS), RUN_ROWS), :] = jnp.where(
            prow == rows, 1.0, 0.0).astype(BF16)
        return carry

    lax.fori_loop(0, n_groups, build, 0, unroll=4)
    xb_sc[...] = h_ref[...].astype(BF16)

    def run_copy(g):
        return pltpu.make_async_copy(
            xp_sc.at[pl.ds(pl.multiple_of(g * RUN_ROWS, RUN_ROWS), RUN_ROWS)],
            xs_ref.at[pl.ds(pl.multiple_of(gdst_ref[0, 0, g] * RUN_ROWS, RUN_ROWS), RUN_ROWS)], sem)

    def permute(ch):
        r0 = pl.multiple_of(ch * MOE_CHUNK, MOE_CHUNK)
        xp_sc[pl.ds(r0, MOE_CHUNK), :] = _dot(sel_sc[pl.ds(r0, MOE_CHUNK), :], xb_sc[...]).astype(BF16)

    def start_chunk(ch):
        for j in range(per_chunk):
            run_copy(ch * per_chunk + j).start()

    permute(0)

    def step(ch, carry):
        start_chunk(ch - 1)
        permute(ch)
        return carry

    lax.fori_loop(1, n_chunks, step, 0)
    start_chunk(n_chunks - 1)
    pltpu.make_async_copy(xp_sc, xs_ref.at[pl.ds(0, xp_sc.shape[0])], sem).wait()


def _dispatch(h2, pe, plan, n_rows, tm):
    n, d = h2.shape
    nt = n // tm
    rb = _tile_row_bound(tm)
    n_groups = rb // RUN_ROWS

    def smem_spec():
        return pl.BlockSpec((1, 1, n_groups), lambda i, *_: (i, 0, 0), memory_space=pltpu.SMEM)

    return pl.pallas_call(
        functools.partial(_dispatch_kernel, n_groups=n_groups),
        grid_spec=pltpu.PrefetchScalarGridSpec(
            num_scalar_prefetch=2,
            grid=(nt,),
            in_specs=[smem_spec(), smem_spec(),
                      pl.BlockSpec((tm, d), lambda i, *_: (i, 0)),
                      pl.BlockSpec((1, N_EXPERTS, tm), lambda i, *_: (i, 0, 0))],
            out_specs=pl.BlockSpec(memory_space=pl.ANY),
            scratch_shapes=[pltpu.VMEM((rb, tm), BF16), pltpu.VMEM((tm, d), BF16), pltpu.VMEM((rb, d), BF16),
                            pltpu.VMEM((EXPERT_ROWS, d), BF16), pltpu.SemaphoreType.DMA(())],
        ),
        out_shape=jax.ShapeDtypeStruct((n_rows, d), BF16),
        compiler_params=_cparams("arbitrary"),
        name="moe_dispatch",
    )(plan["tail"], plan["n_act"].reshape(1), plan["grp_e"], plan["gdst"], h2, pe)


def _expert_kernel(blk_e_ref, n_act_ref, x_ref, w1_ref, w3_ref, w2_ref, y_ref):
    del blk_e_ref

    @pl.when(pl.program_id(0) < n_act_ref[0])
    def _():
        x = x_ref[...]
        a = _dot(x, w1_ref[0, 0].astype(BF16))
        act = (a * jax.nn.sigmoid(a)) * _dot(x, w3_ref[0, 0].astype(BF16))
        y_ref[...] = _dot(act.astype(BF16), w2_ref[0, 0].astype(BF16)).astype(BF16)

    @pl.when(pl.program_id(0) >= n_act_ref[0])
    def _():
        y_ref[...] = jnp.zeros_like(y_ref)


def _experts(xs_sorted, blk_e, n_act, lw):
    n_rows, d = xs_sorted.shape
    bm = EXPERT_ROWS

    def row_map(i, blk_e_ref, n_act_ref):
        return (jnp.minimum(i, n_act_ref[0] - 1), 0)

    layer = lw["layer"]

    def w_map(i, blk_e_ref, n_act_ref):
        return (layer, blk_e_ref[i], 0, 0)

    return pl.pallas_call(
        _expert_kernel,
        grid_spec=pltpu.PrefetchScalarGridSpec(
            num_scalar_prefetch=2,
            grid=(n_rows // bm,),
            in_specs=[pl.BlockSpec((bm, d), row_map),
                      pl.BlockSpec((1, 1, d, EXPERT_FF), w_map),
                      pl.BlockSpec((1, 1, d, EXPERT_FF), w_map),
                      pl.BlockSpec((1, 1, EXPERT_FF, d), w_map)],
            out_specs=pl.BlockSpec((bm, d), lambda i, *_: (i, 0)),
        ),
        out_shape=jax.ShapeDtypeStruct((n_rows, d), BF16),
        compiler_params=_cparams("arbitrary"),
        name="moe_experts",
    )(blk_e, n_act, xs_sorted, lw["w1"], lw["w3"], lw["w2"])


def _combine_kernel(nch_ref, grp_e_ref, gsrc_ref, h_ref, x_ref, pe_ref, we_ref, mod_ref, modc_ref, ws13_ref,
                    ws2_ref, ys_ref, xo_ref, sel_sc, yp_sc, acc_sc, sem, *, n_groups, tiles_per_sample, n_ctx):
    i = pl.program_id(0)
    tm = h_ref.shape[0]
    per_chunk = MOE_CHUNK // RUN_ROWS

    def start_chunk(ch):
        for j in range(per_chunk):
            g = ch * per_chunk + j
            pltpu.make_async_copy(
                ys_ref.at[pl.ds(pl.multiple_of(gsrc_ref[0, 0, g] * RUN_ROWS, RUN_ROWS), RUN_ROWS)],
                yp_sc.at[pl.ds(pl.multiple_of(g * RUN_ROWS, RUN_ROWS), RUN_ROWS)], sem).start()

    def wait_chunk(ch):
        r0 = pl.multiple_of(ch * MOE_CHUNK, MOE_CHUNK)
        pltpu.make_async_copy(ys_ref.at[pl.ds(0, MOE_CHUNK)], yp_sc.at[pl.ds(r0, MOE_CHUNK)], sem).wait()

    start_chunk(0)

    def build(g, carry):
        prow, rows = _selection_rows(g, grp_e_ref, pe_ref, tm)
        wrow = we_ref[0, pl.ds(grp_e_ref[0, 0, g], 1), :]
        sel_sc[pl.ds(pl.multiple_of(g * RUN_ROWS, RUN_ROWS), RUN_ROWS), :] = jnp.where(
            prow == rows, wrow, 0.0).astype(BF16)
        return carry

    lax.fori_loop(0, n_groups, build, 0, unroll=4)

    up = _dot(h_ref[...].astype(BF16), ws13_ref[...])
    a = up[:, 0:EXPERT_FF]
    act = (a * jax.nn.sigmoid(a)) * up[:, EXPERT_FF:2 * EXPERT_FF]
    acc_sc[...] = _dot(act.astype(BF16), ws2_ref[...])

    def gather_sum(ch):
        r0 = pl.multiple_of(ch * MOE_CHUNK, MOE_CHUNK)
        acc_sc[...] += lax.dot_general(sel_sc[pl.ds(r0, MOE_CHUNK), :], yp_sc[pl.ds(r0, MOE_CHUNK), :], TN_DIMS,
                                       preferred_element_type=F32)

    def step(ch, carry):
        wait_chunk(ch)
        start_chunk(ch + 1)
        gather_sum(ch)
        return carry

    last = nch_ref[i] - 1
    lax.fori_loop(0, last, step, 0)
    wait_chunk(last)
    gather_sum(last)
    r = lax.broadcasted_iota(jnp.int32, (tm, 1), 0)
    is_ctx = jnp.logical_and(i % tiles_per_sample == tiles_per_sample - 1, r >= tm - n_ctx)
    gate = jnp.where(is_ctx, modc_ref[0, 5:6, :], mod_ref[0, 5:6, :])
    xo_ref[...] = x_ref[...] + gate * acc_sc[...]


def _combine(h2, xs, pe, we, plan, mod, ys_sorted, lw, tm, tiles_per_sample, n_ctx, bsz):
    n, d = h2.shape
    nt = n // tm
    rb = _tile_row_bound(tm)
    n_groups = rb // RUN_ROWS

    def smem_spec():
        return pl.BlockSpec((1, 1, n_groups), lambda i, *_: (i, 0, 0), memory_space=pltpu.SMEM)

    def row_spec():
        return pl.BlockSpec((tm, d), lambda i, *_: (i, 0))

    def t_spec():
        return pl.BlockSpec((1, N_EXPERTS, tm), lambda i, *_: (i, 0, 0))

    return pl.pallas_call(
        functools.partial(_combine_kernel, n_groups=n_groups, tiles_per_sample=tiles_per_sample, n_ctx=n_ctx),
        grid_spec=pltpu.PrefetchScalarGridSpec(
            num_scalar_prefetch=1,
            grid=(nt,),
            in_specs=[smem_spec(), smem_spec(), row_spec(), row_spec(), t_spec(), t_spec(),
                      pl.BlockSpec((1, 6, d), lambda i, *_: (i // tiles_per_sample, 0, 0)),
                      pl.BlockSpec((1, 6, d), lambda i, *_: (bsz, 0, 0)),
                      pl.BlockSpec((d, 2 * EXPERT_FF), lambda i, *_: (0, 0), pipeline_mode=pl.Buffered(1)),
                      pl.BlockSpec((EXPERT_FF, d), lambda i, *_: (0, 0), pipeline_mode=pl.Buffered(1)),
                      pl.BlockSpec(memory_space=pl.ANY)],
            out_specs=row_spec(),
            scratch_shapes=[pltpu.VMEM((rb, tm), BF16), pltpu.VMEM((rb, d), BF16), pltpu.VMEM((tm, d), F32),
                            pltpu.SemaphoreType.DMA(())],
        ),
        out_shape=jax.ShapeDtypeStruct((n, d), F32),
        compiler_params=_cparams("arbitrary"),
        name="moe_combine",
    )(plan["n_ch"], plan["grp_e"], plan["gsrc"], h2, xs, pe, we, mod, mod, lw["ws13"], lw["ws2"], ys_sorted)


def _moe_plan(c16, n_groups, spare16):
    nt = c16.shape[0]
    per_blk = EXPERT_ROWS // RUN_ROWS
    o16 = jnp.cumsum(c16, axis=1) - c16
    before16 = jnp.cumsum(c16, axis=0) - c16
    gtot16 = jnp.sum(c16, axis=0)
    gblk = (gtot16 + per_blk - 1) // per_blk
    gend_blk = jnp.cumsum(gblk)
    gstart16 = per_blk * (gend_blk - gblk)
    g = jnp.arange(n_groups, dtype=jnp.int32)
    grp_e = jnp.minimum(jnp.sum((o16 + c16)[:, None, :] <= g[None, :, None], axis=2), N_EXPERTS - 1)
    run0 = gstart16[None, :] + before16 - o16
    experts = jnp.arange(N_EXPERTS, dtype=jnp.int32)
    gdst = jnp.sum(jnp.where(grp_e[:, :, None] == experts[None, None, :], run0[:, None, :], 0), axis=2) + g[None, :]
    q = jnp.arange(per_blk, dtype=jnp.int32)
    tail = jnp.where(q[None, :] < (per_blk * gblk - gtot16)[:, None],
                     (gstart16 + gtot16)[:, None] + q[None, :], -1)
    used = g[None, :] < jnp.sum(c16, axis=1)[:, None]
    return {
        "grp_e": grp_e.astype(jnp.int32).reshape(nt, 1, n_groups),
        "gdst": jnp.where(used, gdst, spare16 + g[None, :]).astype(jnp.int32).reshape(nt, 1, n_groups),
        "gsrc": jnp.where(used, gdst, 0).astype(jnp.int32).reshape(nt, 1, n_groups),
        "n_ch": ((jnp.sum(c16, axis=1) * RUN_ROWS + MOE_CHUNK - 1) // MOE_CHUNK).astype(jnp.int32),
        "tail": tail.astype(jnp.int32).reshape(-1),
        "gend_blk": gend_blk, "n_act": gend_blk[-1].astype(jnp.int32),
    }


def _moe_tile(rows_per_sample):
    return max(t for t in range(ROW_TILE, MOE_TILE_MAX + 1, ROW_TILE) if rows_per_sample % t == 0)


def _moe(h2, xs, mod, lw, rows_per_sample, n_ctx, bsz):
    n, d = h2.shape
    tile = _moe_tile(rows_per_sample)
    tiles_per_sample = rows_per_sample // tile
    nt = n // tile
    n_groups = _tile_row_bound(tile) // RUN_ROWS
    pe, we, c16 = _router(h2, lw, tile)
    worst_rows = n * TOP_K + nt * N_EXPERTS * (RUN_ROWS - 1) + N_EXPERTS * (EXPERT_ROWS - 1)
    run_blocks = -(-worst_rows // EXPERT_ROWS)
    n_blocks = run_blocks + -(-n_groups * RUN_ROWS // EXPERT_ROWS)
    plan = _moe_plan(c16[:, 0, :N_EXPERTS], n_groups, run_blocks * (EXPERT_ROWS // RUN_ROWS))
    blk = jnp.minimum(jnp.arange(n_blocks, dtype=jnp.int32), plan["n_act"] - 1)
    blk_e = jnp.minimum(jnp.sum(plan["gend_blk"][None, :] <= blk[:, None], axis=1), N_EXPERTS - 1).astype(jnp.int32)
    xs_sorted = _dispatch(h2, pe, plan, n_blocks * EXPERT_ROWS, tile)
    ys_sorted = _experts(xs_sorted, blk_e, plan["n_act"].reshape(1), lw)
    return _combine(h2, xs, pe, we, plan, mod, ys_sorted, lw, tile, tiles_per_sample, n_ctx, bsz)


def _slots(w, head_w, real_w):
    rows = w.shape[0]
    w = w.reshape(rows, -1, head_w)[:, :, :real_w]
    return jnp.pad(w, ((0, 0), (0, 0), (0, HEAD_SLOT - real_w))).reshape(rows, -1)


def _block_diag(blocks):
    n, r, c = blocks.shape
    out = jnp.zeros((n * r, n * c), blocks.dtype)
    for i in range(n):
        out = out.at[i * r:(i + 1) * r, i * c:(i + 1) * c].set(blocks[i])
    return out


def _layer_weights(l, p):
    d = p["w_in"].shape[1]
    w_in = p["w_in"][l]
    ml = 4 * BRANCH_W
    o_mla = ml + 4 * ML_HEADS
    o_pool = o_mla + Q_LORA + KV_LORA + QK_ROPE
    w_big = jnp.concatenate([
        w_in[:, :ml],
        w_in[:, o_pool:],
        w_in[:, o_mla:o_mla + Q_LORA + KV_LORA],
        w_in[:, o_mla + Q_LORA + KV_LORA:o_pool],
        w_in[:, ml:o_mla],
        jnp.zeros((d, LANES - QK_ROPE - 4 * ML_HEADS), F32),
    ], axis=1).astype(BF16)
    w_ukv = p["mla_w_ukv"][l].reshape(KV_LORA, MLA_HEADS, QK_NOPE + V_DIM)

    def gain_slots(g):
        return jnp.tile(jnp.pad(g, (0, HEAD_SLOT - QK_DIM)), MLA_HEADS)[None, :]

    return {
        "g_mix": p["g_mix"][l][None, :], "g_ffn": p["g_ffn"][l][None, :],
        "w_big": w_big,
        "g_cq": p["mla_g_cq"][l][None, :], "g_ckv": p["mla_g_ckv"][l][None, :],
        "w_uq": _slots(p["mla_w_uq"][l], QK_DIM, QK_DIM).astype(BF16),
        "w_uk": _slots(w_ukv[:, :, :QK_NOPE].reshape(KV_LORA, -1), QK_NOPE, QK_NOPE).astype(BF16),
        "w_uv": w_ukv[:, :, QK_NOPE:].reshape(KV_LORA, BRANCH_W).astype(BF16),
        "g_qn": gain_slots(p["mla_g_qn"][l]), "g_kn": gain_slots(p["mla_g_kn"][l]),
        "gate_bias": jnp.pad(p["ml_gate_bias"][l], (QK_ROPE, LANES - QK_ROPE - 4 * ML_HEADS))[None, :],
        "head_gain": jnp.broadcast_to(p["ml_head_gain"][l][:, None], (BRANCH_W, ROW_TILE)),
        "pool_w": _block_diag(p["pool_w"][l]).astype(BF16),
        "pool_scale": p["pool_scale"][l][None, :],
        "conv_w": p["conv_w"][l], "conv_b": p["conv_b"][l][None, :],
        "w_gate": jnp.concatenate(list(p["w_gate"][l]), axis=1).astype(BF16),
        "b_gate": p["b_gate"][l].reshape(1, -1),
        "w_branch": p["w_branch"][l].astype(BF16),
        "w_out": p["w_out"][l].astype(BF16),
        "w_router": jnp.pad(p["moe_w_router"][l], ((0, 0), (0, LANES - N_EXPERTS))).astype(BF16),
        "b_router": jnp.pad(p["moe_b_router"][l], (0, LANES - N_EXPERTS), constant_values=-1e30)[None, :],
        "layer": l, "w1": p["moe_w1"], "w3": p["moe_w3"], "w2": p["moe_w2"],
        "ws13": jnp.concatenate([p["moe_ws1"][l], p["moe_ws3"][l]], axis=1).astype(BF16),
        "ws2": p["moe_ws2"][l].astype(BF16),
    }


def _rope_tables(n_lat, n_ctx):
    t = jnp.arange(n_lat)
    n_freq = QK_ROPE // 4
    inv = ROPE_THETA ** (-jnp.arange(n_freq, dtype=F32) / n_freq)
    ang_r = (t // GRID_W).astype(F32)[:, None] * inv
    ang_c = (t % GRID_W).astype(F32)[:, None] * inv
    cos4 = jnp.concatenate([jnp.cos(ang_r)] * 2 + [jnp.cos(ang_c)] * 2, axis=1)
    zero = jnp.zeros_like(ang_r)
    sin_a = jnp.concatenate([-jnp.sin(ang_r), zero, -jnp.sin(ang_c), zero], axis=1)
    sin_b = jnp.concatenate([zero, jnp.sin(ang_r), zero, jnp.sin(ang_c)], axis=1)

    def slot(a, fill):
        a = jnp.pad(a, ((0, 0), (QK_NOPE, 0)), constant_values=fill)
        a = jnp.pad(a, ((0, 0), (0, HEAD_SLOT - QK_DIM)), constant_values=fill)
        return jnp.pad(a, ((0, n_ctx), (0, 0)), constant_values=fill)

    return {"cos": slot(cos4, 1.0), "sin_a": slot(sin_a, 0.0), "sin_b": slot(sin_b, 0.0),
            "ones_slot": _block_diag(jnp.ones((MLA_HEADS, HEAD_SLOT, HEAD_SLOT), BF16))}


def kernel(x, c, ctx, c_ctx, w_mod, b_mod, g_mix, g_ffn, w_in, ml_gate_bias, ml_head_gain, mla_g_cq, mla_g_ckv,
           mla_w_uq, mla_w_ukv, mla_g_qn, mla_g_kn, pool_w, pool_scale, conv_w, conv_b, w_gate, b_gate, w_branch,
           w_out, moe_w_router, moe_b_router, moe_w1, moe_w3, moe_w2, moe_ws1, moe_ws3, moe_ws2):
    p = dict(g_mix=g_mix, g_ffn=g_ffn, w_in=w_in, ml_gate_bias=ml_gate_bias, ml_head_gain=ml_head_gain,
             mla_g_cq=mla_g_cq, mla_g_ckv=mla_g_ckv, mla_w_uq=mla_w_uq, mla_w_ukv=mla_w_ukv, mla_g_qn=mla_g_qn,
             mla_g_kn=mla_g_kn, pool_w=pool_w, pool_scale=pool_scale, conv_w=conv_w, conv_b=conv_b,
             w_gate=w_gate, b_gate=b_gate, w_branch=w_branch, w_out=w_out, moe_w_router=moe_w_router,
             moe_b_router=moe_b_router, moe_w1=moe_w1, moe_w3=moe_w3, moe_w2=moe_w2, moe_ws1=moe_ws1,
             moe_ws3=moe_ws3, moe_ws2=moe_ws2)
    bsz, n_lat, d = x.shape
    n_ctx = ctx.shape[1]
    depth = w_mod.shape[0]
    s = n_lat + n_ctx
    assert n_ctx == ROW_TILE == ATT_TQ and n_lat % ROW_TILE == 0 and n_lat % GRID_W == 0
    n_lat_tiles = n_lat // ROW_TILE

    mod_rows = -(-(bsz + 1) // SUBLANES) * SUBLANES
    cc = jnp.concatenate([c, c_ctx[None, :], jnp.zeros((mod_rows - bsz - 1, d), F32)], axis=0)
    mods = _modulation(cc, w_mod, b_mod).reshape(depth, mod_rows, 6, d)
    rope = _rope_tables(n_lat, n_ctx)
    xs = jnp.concatenate([x, ctx], axis=1)

    for l in range(depth):
        lw = _layer_weights(l, p)
        mod = mods[l]
        hx, mq, mk, mv, o, g, pc, q, k, v = _in_proj(xs, mod, lw, rope, n_lat_tiles)
        hf, hb = _mlstm(mq, mk, mv, g, n_lat // ML_CHUNK)
        y_mla = _attention(q, k, v, n_lat)
        with_ctx = l + 1 < depth
        rows = s if with_ctx else n_lat
        xs, h2 = _merge(hx, hf, hb, o, y_mla, pc, xs, mod, lw, n_lat_tiles, with_ctx)
        xs = _moe(h2.reshape(bsz * rows, d), xs.reshape(bsz * rows, d), mod, lw, rows, n_ctx if with_ctx else 0,
                  bsz).reshape(bsz, rows, d)
    return xs
```

```python
import functools
import math

import jax
import jax.numpy as jnp
from jax import lax
from jax.experimental import pallas as pl
from jax.experimental.pallas import tpu as pltpu

GRID_W = 64
BRANCH_W = 256
EPS = 1e-6
ML_HEADS = 4
ML_DH = BRANCH_W // ML_HEADS
ML_CHUNK = 128
MLA_HEADS = 4
Q_LORA = 256
KV_LORA = 128
QK_NOPE = 64
QK_ROPE = 32
QK_DIM = QK_NOPE + QK_ROPE
V_DIM = BRANCH_W // MLA_HEADS
V_SLOT = V_DIM + 16
ROPE_THETA = 10000.0
POOL_WINDOWS = (2, 4, 8, 16)
POOL_GROUP = BRANCH_W // len(POOL_WINDOWS)
POOL_HALO = max(POOL_WINDOWS) // 2
N_EXPERTS = 64
TOP_K = 6
EXPERT_FF = 256
ROUTE_SCALE = 2.5

LANES = 128
SUBLANES = 8
HEAD_SLOT = LANES
ROW_TILE = 256
ATT_TQ = 256
ATT_TK = 8192
ATT_SUB = 128
ATT_AHEAD = 8
EXPERT_ROWS = 1024
MOE_TILE_MAX = 768
RUN_ROWS = 16
MOE_CHUNK = 512
VMEM_LIMIT = 56 * 1024 * 1024

F32 = jnp.float32
BF16 = jnp.bfloat16
NT_DIMS = (((1,), (1,)), ((), ()))
TN_DIMS = (((0,), (0,)), ((), ()))


def _cparams(*sem):
    return pltpu.CompilerParams(dimension_semantics=sem, vmem_limit_bytes=VMEM_LIMIT)


def _const_spec(shape):
    nd = len(shape)
    return pl.BlockSpec(shape, lambda *_: (0,) * nd, pipeline_mode=pl.Buffered(1))


def _dot(a, b):
    return jnp.dot(a, b, preferred_element_type=F32)


def _split_dot(a_f32, ones_bf16):
    hi = a_f32.astype(BF16)
    r1 = a_f32 - hi.astype(F32)
    mid = r1.astype(BF16)
    lo = (r1 - mid.astype(F32)).astype(BF16)
    return _dot(hi, ones_bf16) + _dot(mid, ones_bf16) + _dot(lo, ones_bf16)


def _rms_rows(x):
    return x * lax.rsqrt(jnp.mean(x * x, axis=-1, keepdims=True) + EPS)


def _mod_kernel(c_ref, w_ref, b_ref, o_ref):
    c = c_ref[...]
    a = (c * jax.nn.sigmoid(c)).astype(BF16)
    o_ref[0] = _dot(a, w_ref[0].astype(BF16)) + b_ref[0]


def _modulation(cc, w_mod, b_mod):
    depth, d, d6 = w_mod.shape
    rows = cc.shape[0]
    tn = 1536
    return pl.pallas_call(
        _mod_kernel,
        grid=(depth, d6 // tn),
        in_specs=[
            pl.BlockSpec((rows, d), lambda l, n: (0, 0)),
            pl.BlockSpec((1, d, tn), lambda l, n: (l, 0, n)),
            pl.BlockSpec((1, 1, tn), lambda l, n: (l, 0, n)),
        ],
        out_specs=pl.BlockSpec((1, rows, tn), lambda l, n: (l, 0, n)),
        out_shape=jax.ShapeDtypeStruct((depth, rows, d6), F32),
        compiler_params=_cparams("parallel", "parallel"),
        name="modulation",
    )(cc, w_mod, b_mod.reshape(depth, 1, d6))


def _group_mean_sq(x, ones_bd, width):
    sq = x * x
    hi = sq.astype(BF16)
    lo = (sq - hi.astype(F32)).astype(BF16)
    return (_dot(hi, ones_bd) + _dot(lo, ones_bd)) * (1.0 / width)


def _in_kernel(x_ref, mod_ref, gmix_ref, w_ref, gb_ref, gcq_ref, gckv_ref, wuq_ref, wuk_ref, wuv_ref,
               gq_ref, gk_ref, ones_ref, cos_ref, sa_ref, sb_ref,
               hx_ref, mq_ref, mk_ref, mv_ref, mo_ref, g_ref, pc_ref, q_ref, k_ref, v_ref):
    x = x_ref[0]
    shift = mod_ref[0, 0:1, :]
    scale = mod_ref[0, 1:2, :]
    hx = (_rms_rows(x) * gmix_ref[...]) * (1.0 + scale) + shift
    hxb = hx.astype(BF16)
    hx_ref[0] = hxb
    p = _dot(hxb, w_ref[...])
    bw = BRANCH_W
    mq_ref[0] = p[:, 0:bw].T.astype(BF16)
    mk_ref[0] = (p[:, bw:2 * bw] * (ML_DH ** -0.5)).astype(BF16)
    mv_ref[0] = p[:, 2 * bw:3 * bw].T.astype(BF16)
    mo_ref[0] = p[:, 3 * bw:4 * bw].T.astype(BF16)
    pc_ref[0] = p[:, 1024:2048]
    cq = p[:, 2048:2048 + Q_LORA]
    ckv = p[:, 2304:2304 + KV_LORA]
    misc = p[:, 2432:2432 + LANES]
    g_ref[0] = (misc + gb_ref[...]).T[QK_ROPE:QK_ROPE + 4 * ML_HEADS]
    lane = lax.broadcasted_iota(jnp.int32, misc.shape, 1)
    kr_slot = jnp.where(jnp.logical_and(lane >= QK_NOPE, lane < QK_DIM), pltpu.roll(misc, QK_NOPE, 1), 0.0)
    kr4 = jnp.concatenate([kr_slot] * MLA_HEADS, axis=1)

    cqn = (_rms_rows(cq) * gcq_ref[...]).astype(BF16)
    ckvn = (_rms_rows(ckv) * gckv_ref[...]).astype(BF16)
    q_pre = _dot(cqn, wuq_ref[...])
    k_pre = _dot(ckvn, wuk_ref[...]) + kr4
    v_t = _dot(ckvn, wuv_ref[...]).T
    tm = v_t.shape[1]
    extra = V_SLOT - V_DIM
    one_row = jnp.where(lax.broadcasted_iota(jnp.int32, (extra, tm), 0) == 0, 1.0, 0.0)
    v_ref[0] = jnp.concatenate(
        sum([[v_t[h * V_DIM:(h + 1) * V_DIM], one_row] for h in range(MLA_HEADS)], []), axis=0).astype(BF16)

    ones_bd = ones_ref[...]
    cos = jnp.concatenate([cos_ref[...]] * MLA_HEADS, axis=1)
    sa = jnp.concatenate([sa_ref[...]] * MLA_HEADS, axis=1)
    sb = jnp.concatenate([sb_ref[...]] * MLA_HEADS, axis=1)
    width = MLA_HEADS * HEAD_SLOT
    half = QK_ROPE // 4

    def norm_rope(t, gain):
        t = t * lax.rsqrt(_group_mean_sq(t, ones_bd, QK_DIM) + EPS) * gain
        return t * cos + pltpu.roll(t, width - half, 1) * sa + pltpu.roll(t, half, 1) * sb

    q = norm_rope(q_pre, gq_ref[...]) * (QK_DIM ** -0.5 * math.log2(math.e))
    q_ref[0] = q.T.astype(BF16)
    k_ref[0] = norm_rope(k_pre, gk_ref[...]).astype(BF16)


def _in_proj(xs, mod, lw, rope, n_lat_tiles):
    bsz, s, d = xs.shape
    tm = ROW_TILE
    nt = s // tm
    wcols = lw["w_big"].shape[1]
    slot_w = MLA_HEADS * HEAD_SLOT

    def row_spec(width):
        return pl.BlockSpec((1, tm, width), lambda t, b: (b, t, 0))

    def tab_spec():
        return pl.BlockSpec((tm, HEAD_SLOT), lambda t, b: (t, 0))

    def col_spec(height):
        return pl.BlockSpec((1, height, tm), lambda t, b: (b, 0, t))

    v_rows = MLA_HEADS * V_SLOT
    outs = [("row", d, BF16), ("col", BRANCH_W, BF16), ("row", BRANCH_W, BF16), ("col", BRANCH_W, BF16),
            ("col", BRANCH_W, BF16), ("col", 4 * ML_HEADS, F32), ("row", 1024, F32),
            ("col", slot_w, BF16), ("row", slot_w, BF16), ("col", v_rows, BF16)]
    out_specs = [row_spec(w) if kind == "row" else col_spec(w) for kind, w, _ in outs]
    out_shape = [jax.ShapeDtypeStruct((bsz, s, w) if kind == "row" else (bsz, w, s), dt) for kind, w, dt in outs]
    return pl.pallas_call(
        _in_kernel,
        grid=(nt, bsz),
        in_specs=[
            row_spec(d),
            pl.BlockSpec((1, 6, d), lambda t, b: (jnp.where(t >= n_lat_tiles, bsz, b), 0, 0)),
            _const_spec((1, d)),
            _const_spec((d, wcols)),
            _const_spec((1, LANES)),
            _const_spec((1, Q_LORA)),
            _const_spec((1, KV_LORA)),
            _const_spec((Q_LORA, slot_w)),
            _const_spec((KV_LORA, slot_w)),
            _const_spec((KV_LORA, BRANCH_W)),
            _const_spec((1, slot_w)),
            _const_spec((1, slot_w)),
            _const_spec((slot_w, slot_w)),
            tab_spec(), tab_spec(), tab_spec(),
        ],
        out_specs=out_specs,
        out_shape=out_shape,
        compiler_params=_cparams("parallel", "parallel"),
        name="norm_in_proj",
    )(xs, mod, lw["g_mix"], lw["w_big"], lw["gate_bias"], lw["g_cq"], lw["g_ckv"], lw["w_uq"], lw["w_uk"], lw["w_uv"],
      lw["g_qn"], lw["g_kn"], rope["ones_slot"], rope["cos"], rope["sin_a"], rope["sin_b"])


def _log_sigmoid(x):
    return jnp.minimum(x, 0.0) - jnp.log1p(jnp.exp(-jnp.abs(x)))


def _mlstm_kernel(qt_f, k_f, vt_f, g_f, qt_b, k_b, vt_b, g_b, hf_ref, hb_ref, c_sc, m_sc):
    @pl.when(pl.program_id(1) == 0)
    def _():
        c_sc[...] = jnp.zeros_like(c_sc)
        m_sc[...] = jnp.zeros_like(m_sc)

    L = ML_CHUNK
    dh = ML_DH
    row = lax.broadcasted_iota(jnp.int32, (L, L), 0)
    col = lax.broadcasted_iota(jnp.int32, (L, L), 1)
    diag = row == col
    ones_ll = jnp.ones((L, L), BF16)
    one_rows = jnp.where(lax.broadcasted_iota(jnp.int32, (LANES - dh, L), 0) == 0, 1.0, 0.0).astype(BF16)

    dirs = ((qt_f, k_f, vt_f, g_f), (qt_b, k_b, vt_b, g_b))
    units = []
    for d, (qt_ref, k_ref, vt_ref, g_ref) in enumerate(dirs):
        a = g_ref[0]
        lf = _log_sigmoid(a)
        valid = (row <= col) if d == 0 else (row >= col)
        b_rows = _split_dot(lf, valid.astype(BF16))
        last = L - 1 if d == 0 else 0
        for h in range(ML_HEADS):
            ci = d * 2 * ML_HEADS + h
            cf = ci + ML_HEADS
            st = d * ML_HEADS + h
            u = {"st": st, "valid": valid}
            bt = b_rows[cf:cf + 1, :]
            li = a[ci:ci + 1, :]
            b_end = bt[:, last:last + 1]
            m_st = m_sc[st:st + 1, 0:1]
            qt = qt_ref[0, h * dh:(h + 1) * dh, :]
            k = k_ref[0, :, h * dh:(h + 1) * dh]
            u["vt"] = vt_ref[0, h * dh:(h + 1) * dh, :]
            u["bt"] = bt
            u["inter"] = bt + m_st
            src = jnp.where(diag, bt - li, 0.0)
            hi = src.astype(BF16)
            lo = (src - hi.astype(F32)).astype(BF16)
            u["src"] = _dot(hi, ones_ll) + _dot(lo, ones_ll)
            w_log = b_end - bt + li
            u["m_new"] = jnp.maximum(b_end + m_st, jnp.max(w_log, axis=1, keepdims=True))
            u["decay"] = jnp.exp(b_end + m_st - u["m_new"])
            v_aug = jnp.concatenate([u["vt"], one_rows], axis=0)
            vw = (v_aug.astype(F32) * jnp.exp(w_log - u["m_new"])).astype(BF16)
            u["s_kq"] = _dot(k, qt)
            u["qc"] = _dot(c_sc[st].astype(BF16), qt)
            u["upd"] = _dot(vw, k)
            units.append(u)

    for u in units:
        d_log = jnp.where(u["valid"], u["bt"] - u["src"], -jnp.inf)
        u["m_t"] = jnp.maximum(u["inter"], jnp.max(d_log, axis=0, keepdims=True))
        w_st = jnp.exp(d_log - u["m_t"]) * u["s_kq"]
        u["w_sum"] = jnp.sum(w_st, axis=0, keepdims=True)
        u["pv"] = _dot(u["vt"], w_st.astype(BF16))

    outs = []
    for u in units:
        a_inter = jnp.exp(u["inter"] - u["m_t"])
        num = a_inter * u["qc"][0:dh, :] + u["pv"]
        den = a_inter * u["qc"][dh:dh + 1, :] + u["w_sum"]
        outs.append(num / jnp.maximum(jnp.abs(den), jnp.exp(-u["m_t"])))
        st = u["st"]
        c_sc[st] = u["decay"] * c_sc[st] + u["upd"]
        m_sc[st:st + 1, :] = jnp.broadcast_to(u["m_new"], (1, LANES))
    hf_ref[0] = jnp.concatenate(outs[0:ML_HEADS], axis=0)
    hb_ref[0] = jnp.concatenate(outs[ML_HEADS:], axis=0)


def _mlstm(qt, k, vt, gt, n_lat_chunks):
    bsz, s, _ = k.shape
    nc = s // ML_CHUNK

    def fwd_chunk(j):
        return (j + n_lat_chunks) % nc

    def bwd_chunk(j):
        return nc - 1 - j

    def specs(chunk):
        def col(height):
            return pl.BlockSpec((1, height, ML_CHUNK), lambda b, j: (b, 0, chunk(j)))
        return [col(BRANCH_W), pl.BlockSpec((1, ML_CHUNK, BRANCH_W), lambda b, j: (b, chunk(j), 0)),
                col(BRANCH_W), col(4 * ML_HEADS)], col(BRANCH_W)

    in_f, out_f = specs(fwd_chunk)
    in_b, out_b = specs(bwd_chunk)
    return pl.pallas_call(
        _mlstm_kernel,
        grid=(bsz, nc),
        in_specs=in_f + in_b,
        out_specs=[out_f, out_b],
        out_shape=[jax.ShapeDtypeStruct((bsz, BRANCH_W, s), F32)] * 2,
        scratch_shapes=[pltpu.VMEM((2 * ML_HEADS, LANES, ML_DH), F32),
                        pltpu.VMEM((2 * ML_HEADS, LANES), F32)],
        compiler_params=_cparams("parallel", "arbitrary"),
        name="mlstm_scan",
    )(qt, k, vt, gt, qt, k, vt, gt)


def _attn_kernel(qt_ref, k_ref, vt_ref, o_ref, m_sc, acc_sc, *, n_lat, n_ctx, tk):
    m_sc[...] = jnp.full_like(m_sc, -jnp.inf)
    acc_sc[...] = jnp.zeros_like(acc_sc)

    def scores(h, start, size):
        sl = slice(h * HEAD_SLOT, (h + 1) * HEAD_SLOT)
        return _dot(k_ref[0, pl.ds(start, size), sl], qt_ref[0, sl, :])

    def keys(start, size):
        n_sub = size // ATT_SUB
        order = [(j, h) for j in range(n_sub) for h in range(MLA_HEADS)]
        pending = [scores(h, start + j * ATT_SUB, ATT_SUB) for j, h in order[:ATT_AHEAD]]
        for i, (j, h) in enumerate(order):
            off = start + j * ATT_SUB
            s = pending.pop(0)
            if i + ATT_AHEAD < len(order):
                jn, hn = order[i + ATT_AHEAD]
                pending.append(scores(hn, start + jn * ATT_SUB, ATT_SUB))
            m_old = m_sc[h]
            m_new = jnp.maximum(m_old, jnp.max(s, axis=0, keepdims=True))
            p = jnp.exp2(s - m_new)
            vt = vt_ref[0, h * V_SLOT:(h + 1) * V_SLOT, pl.ds(off, ATT_SUB)]
            acc_sc[h] = jnp.exp2(m_old - m_new) * acc_sc[h] + _dot(vt, p.astype(BF16))
            m_sc[h] = m_new

    keys(n_lat, n_ctx)

    def chunk(c, carry):
        keys(pl.multiple_of(c * tk, tk), tk)
        return carry

    is_ctx_tile = pl.program_id(1) * ATT_TQ >= n_lat
    lax.fori_loop(0, jnp.where(is_ctx_tile, 0, n_lat // tk), chunk, 0)
    out_t = jnp.concatenate([acc_sc[h, 0:V_DIM, :] / acc_sc[h, V_DIM:V_DIM + 1, :] for h in range(MLA_HEADS)],
                            axis=0)
    o_ref[0] = out_t.T.astype(o_ref.dtype)


def _key_chunk(n_keys):
    return max(t for t in range(ATT_TQ, ATT_TK + 1, ATT_TQ) if n_keys % t == 0)


def _attention(qt, k, vt, n_lat):
    bsz, s, slot_w = k.shape
    tq = ATT_TQ
    return pl.pallas_call(
        functools.partial(_attn_kernel, n_lat=n_lat, n_ctx=s - n_lat, tk=_key_chunk(n_lat)),
        grid=(bsz, s // tq),
        in_specs=[pl.BlockSpec((1, slot_w, tq), lambda b, t: (b, 0, t)),
                  pl.BlockSpec((1, s, slot_w), lambda b, t: (b, 0, 0), pipeline_mode=pl.Buffered(1)),
                  pl.BlockSpec((1, MLA_HEADS * V_SLOT, s), lambda b, t: (b, 0, 0), pipeline_mode=pl.Buffered(1))],
        out_specs=pl.BlockSpec((1, tq, BRANCH_W), lambda b, t: (b, t, 0)),
        out_shape=jax.ShapeDtypeStruct((bsz, s, BRANCH_W), BF16),
        scratch_shapes=[pltpu.VMEM((MLA_HEADS, 1, tq), F32), pltpu.VMEM((MLA_HEADS, V_SLOT, tq), F32)],
        compiler_params=_cparams("parallel", "arbitrary"),
        name="mla_attention",
    )(qt, k, vt)


def _merge_kernel(hx_ref, hf_ref, hb_ref, o_ref, ya_ref, pc_ref, prev_ref, next_ref, x_ref, mod_ref,
                  hg_ref, pw_ref, ps_ref, cw_ref, cb_ref, wg_ref, bg_ref, wb_ref, wo_ref,
                  gffn_ref, xo_ref, h2_ref, *, n_lat_tiles, n_tiles):
    t = pl.program_id(1)
    tm = ROW_TILE
    bw = BRANCH_W
    halo = POOL_HALO
    no_prev = jnp.logical_or(t == 0, t == n_lat_tiles)
    no_next = jnp.logical_or(t == n_lat_tiles - 1, t == n_tiles - 1)

    h_t = hf_ref[0] + hb_ref[0]
    normed = []
    for hd in range(ML_HEADS):
        hh = h_t[hd * ML_DH:(hd + 1) * ML_DH]
        normed.append(hh * lax.rsqrt(jnp.mean(hh * hh, axis=0, keepdims=True) + EPS))
    y_ml = (jnp.concatenate(normed, axis=0) * hg_ref[...] * jax.nn.sigmoid(o_ref[0].astype(F32))).T

    prev = jnp.where(no_prev, 0.0, prev_ref[0])
    nxt = jnp.where(no_next, 0.0, next_ref[0])
    ext = jnp.concatenate([prev, pc_ref[0], nxt], axis=0)
    pe = ext[:, 0:bw]

    def rows(arr, off):
        return arr[halo + off:halo + off + tm, :]

    r = lax.broadcasted_iota(jnp.int32, (tm, 1), 0)
    lane = lax.broadcasted_iota(jnp.int32, (tm, bw), 1)
    centre = rows(pe, 0)
    acc = centre
    mean = jnp.zeros((tm, bw), F32)
    done = 0
    for gi, w in enumerate(POOL_WINDOWS):
        for off in list(range(-(w // 2), -done)) + list(range(max(done, 1), w // 2)):
            acc = acc + rows(pe, off)
        done = w // 2
        before = jnp.where(no_prev, jnp.minimum(r, w // 2), w // 2)
        after = jnp.where(no_next, jnp.minimum(tm - r, w // 2), w // 2)
        inv = 1.0 / (before + after).astype(F32)
        mean = jnp.where(lane >= gi * POOL_GROUP, acc * inv, mean)
    y_pool = _dot((mean - centre).astype(BF16), pw_ref[...]) * ps_ref[...]

    z = ext[:, 3 * bw:4 * bw] * ext[:, bw:2 * bw]
    conv = cb_ref[...] + rows(z, -1) * cw_ref[0:1, :] + rows(z, 0) * cw_ref[1:2, :] + rows(z, 1) * cw_ref[2:3, :]
    y_conv = rows(ext[:, 2 * bw:3 * bw], 0) * conv

    hxb = hx_ref[0]
    d = hxb.shape[1]
    gates = jax.nn.sigmoid(_dot(hxb, wg_ref[...]) + bg_ref[...])
    ys = (y_ml.astype(BF16), ya_ref[0], y_pool.astype(BF16), y_conv.astype(BF16))
    merged = None
    for i, y in enumerate(ys):
        term = gates[:, i * d:(i + 1) * d] * _dot(y, wb_ref[i])
        merged = term if merged is None else merged + term
    out = _dot(merged.astype(BF16), wo_ref[...])
    x_new = x_ref[0] + mod_ref[0, 2:3, :] * out
    xo_ref[0] = x_new
    h2_ref[0] = (_rms_rows(x_new) * gffn_ref[...]) * (1.0 + mod_ref[0, 4:5, :]) + mod_ref[0, 3:4, :]


def _merge(hx, hf, hb, o, y_mla, pc, xs, mod, lw, n_lat_tiles, with_ctx):
    bsz, s, d = xs.shape
    tm = ROW_TILE
    nt = s // tm if with_ctx else n_lat_tiles
    halo = POOL_HALO
    per = tm // halo

    def row_spec(width):
        return pl.BlockSpec((1, tm, width), lambda b, t: (b, t, 0))

    def col_spec():
        return pl.BlockSpec((1, BRANCH_W, tm), lambda b, t: (b, 0, t))

    return pl.pallas_call(
        functools.partial(_merge_kernel, n_lat_tiles=n_lat_tiles, n_tiles=nt),
        grid=(bsz, nt),
        in_specs=[
            row_spec(d), col_spec(), col_spec(), col_spec(), row_spec(BRANCH_W),
            row_spec(1024),
            pl.BlockSpec((1, halo, 1024), lambda b, t: (b, jnp.maximum(t * per - 1, 0), 0)),
            pl.BlockSpec((1, halo, 1024), lambda b, t: (b, jnp.minimum((t + 1) * per, s // halo - 1), 0)),
            row_spec(d),
            pl.BlockSpec((1, 6, d), lambda b, t: (jnp.where(t >= n_lat_tiles, bsz, b), 0, 0)),
            _const_spec((BRANCH_W, tm)), _const_spec((BRANCH_W, BRANCH_W)),
            _const_spec((1, BRANCH_W)), _const_spec((3, BRANCH_W)), _const_spec((1, BRANCH_W)),
            _const_spec((d, 4 * d)), _const_spec((1, 4 * d)), _const_spec((4, BRANCH_W, d)),
            _const_spec((d, d)), _const_spec((1, d)),
        ],
        out_specs=[row_spec(d), row_spec(d)],
        out_shape=[jax.ShapeDtypeStruct((bsz, nt * tm, d), F32)] * 2,
        compiler_params=_cparams("parallel", "parallel"),
        name="mixer_merge",
    )(hx, hf, hb, o, y_mla, pc, pc, pc, xs, mod, lw["head_gain"], lw["pool_w"], lw["pool_scale"],
      lw["conv_w"], lw["conv_b"], lw["w_gate"], lw["b_gate"], lw["w_branch"], lw["w_out"], lw["g_ffn"])


def _router_kernel(h_ref, wr_ref, br_ref, pe_ref, we_ref, c16_ref):
    tm = h_ref.shape[0]
    scores = jax.nn.sigmoid(_dot(h_ref[...].astype(BF16), wr_ref[...]))
    sel = scores + br_ref[...]
    lane = lax.broadcasted_iota(jnp.int32, (tm, LANES), 1)
    member = jnp.zeros((tm, LANES), F32)
    for _ in range(TOP_K):
        mx = jnp.max(sel, axis=1, keepdims=True)
        ix = jnp.min(jnp.where(sel == mx, lane, LANES), axis=1, keepdims=True)
        hit = lane == ix
        member = jnp.where(hit, 1.0, member)
        sel = jnp.where(hit, -jnp.inf, sel)
    picked = member * scores
    weights = picked / jnp.sum(picked, axis=1, keepdims=True) * ROUTE_SCALE
    counts = jnp.sum(member, axis=0, keepdims=True)
    c16 = jnp.floor((counts + (RUN_ROWS - 1)) * (1.0 / RUN_ROWS))
    e_row = lax.broadcasted_iota(jnp.int32, (LANES, LANES), 0)
    e_col = lax.broadcasted_iota(jnp.int32, (LANES, LANES), 1)
    lower = (e_row < e_col).astype(BF16)
    o16 = _dot(jnp.broadcast_to(c16, (SUBLANES, LANES)).astype(BF16), lower)[0:1]
    row = lax.broadcasted_iota(jnp.int32, (tm, tm), 0)
    col = lax.broadcasted_iota(jnp.int32, (tm, tm), 1)
    rank = _dot((col < row).astype(BF16), member.astype(BF16))
    pos = jnp.where(member > 0.0, RUN_ROWS * o16 + rank, -1.0)
    pe_ref[0] = pos.T[0:N_EXPERTS]
    we_ref[0] = weights.T[0:N_EXPERTS]
    c16_ref[0] = c16.astype(jnp.int32)


def _router(h2, lw, tm):
    n, d = h2.shape
    nt = n // tm

    def t_spec():
        return pl.BlockSpec((1, N_EXPERTS, tm), lambda i: (i, 0, 0))

    return pl.pallas_call(
        _router_kernel,
        grid=(nt,),
        in_specs=[pl.BlockSpec((tm, d), lambda i: (i, 0)), _const_spec((d, LANES)), _const_spec((1, LANES))],
        out_specs=[t_spec(), t_spec(), pl.BlockSpec((1, 1, LANES), lambda i: (i, 0, 0))],
        out_shape=[jax.ShapeDtypeStruct((nt, N_EXPERTS, tm), F32), jax.ShapeDtypeStruct((nt, N_EXPERTS, tm), F32),
                   jax.ShapeDtypeStruct((nt, 1, LANES), jnp.int32)],
        compiler_params=_cparams("parallel"),
        name="moe_router",
    )(h2, lw["w_router"], lw["b_router"])


def _tile_row_bound(tm):
    worst = tm * TOP_K + N_EXPERTS * (RUN_ROWS - 1)
    return -(-worst // MOE_CHUNK) * MOE_CHUNK


def _selection_rows(g, grp_e_ref, pe_ref, tm):
    prow = pe_ref[0, pl.ds(grp_e_ref[0, 0, g], 1), :]
    rows = (lax.broadcasted_iota(jnp.int32, (RUN_ROWS, tm), 0) + g * RUN_ROWS).astype(F32)
    return prow, rows


def _dispatch_kernel(tail_ref, n_act_ref, grp_e_ref, gdst_ref, h_ref, pe_ref, xs_ref,
                     sel_sc, xb_sc, xp_sc, zero_sc, sem, *, n_groups):
    i = pl.program_id(0)
    tm = h_ref.shape[0]
    n_chunks = sel_sc.shape[0] // MOE_CHUNK
    per_chunk = MOE_CHUNK // RUN_ROWS

    @pl.when(i == pl.num_programs(0) - 1)
    def _():
        zero_sc[...] = jnp.zeros_like(zero_sc)

        def fill_copy(q):
            return pltpu.make_async_copy(
                zero_sc.at[pl.ds(0, RUN_ROWS)],
                xs_ref.at[pl.ds(pl.multiple_of(tail_ref[q] * RUN_ROWS, RUN_ROWS), RUN_ROWS)], sem)

        def block_copy(b):
            return pltpu.make_async_copy(
                zero_sc, xs_ref.at[pl.ds(pl.multiple_of(b * EXPERT_ROWS, EXPERT_ROWS), EXPERT_ROWS)], sem)

        def block_start(b, carry):
            block_copy(b).start()
            return carry

        def block_wait(b, carry):
            block_copy(b).wait()
            return carry

        n_blocks = xs_ref.shape[0] // EXPERT_ROWS
        lax.fori_loop(n_act_ref[0], n_blocks, block_start, 0)
        lax.fori_loop(n_act_ref[0], n_blocks, block_wait, 0)

        def fill_start(q, carry):
            @pl.when(tail_ref[q] >= 0)
            def _():
                fill_copy(q).start()
            return carry

        def fill_wait(q, carry):
            @pl.when(tail_ref[q] >= 0)
            def _():
                fill_copy(q).wait()
            return carry

        lax.fori_loop(0, tail_ref.shape[0], fill_start, 0)
        lax.fori_loop(0, tail_ref.shape[0], fill_wait, 0)

    def build(g, carry):
        prow, rows = _selection_rows(g, grp_e_ref, pe_ref, tm)
        sel_sc[pl.ds(pl.multiple_of(g * RUN_ROWS, RUN_ROWS), RUN_ROWS), :] = jnp.where(
            prow == rows, 1.0, 0.0).astype(BF16)
        return carry

    lax.fori_loop(0, n_groups, build, 0, unroll=4)
    xb_sc[...] = h_ref[...].astype(BF16)

    def run_copy(g):
        return pltpu.make_async_copy(
            xp_sc.at[pl.ds(pl.multiple_of(g * RUN_ROWS, RUN_ROWS), RUN_ROWS)],
            xs_ref.at[pl.ds(pl.multiple_of(gdst_ref[0, 0, g] * RUN_ROWS, RUN_ROWS), RUN_ROWS)], sem)

    def permute(ch):
        r0 = pl.multiple_of(ch * MOE_CHUNK, MOE_CHUNK)
        xp_sc[pl.ds(r0, MOE_CHUNK), :] = _dot(sel_sc[pl.ds(r0, MOE_CHUNK), :], xb_sc[...]).astype(BF16)

    def start_chunk(ch):
        for j in range(per_chunk):
            run_copy(ch * per_chunk + j).start(priority=j % 2)

    permute(0)

    def step(ch, carry):
        start_chunk(ch - 1)
        permute(ch)
        return carry

    lax.fori_loop(1, n_chunks, step, 0)
    start_chunk(n_chunks - 1)
    pltpu.make_async_copy(xp_sc, xs_ref.at[pl.ds(0, xp_sc.shape[0])], sem).wait()


def _dispatch(h2, pe, plan, n_rows, tm):
    n, d = h2.shape
    nt = n // tm
    rb = _tile_row_bound(tm)
    n_groups = rb // RUN_ROWS

    def smem_spec():
        return pl.BlockSpec((1, 1, n_groups), lambda i, *_: (i, 0, 0), memory_space=pltpu.SMEM)

    return pl.pallas_call(
        functools.partial(_dispatch_kernel, n_groups=n_groups),
        grid_spec=pltpu.PrefetchScalarGridSpec(
            num_scalar_prefetch=2,
            grid=(nt,),
            in_specs=[smem_spec(), smem_spec(),
                      pl.BlockSpec((tm, d), lambda i, *_: (i, 0)),
                      pl.BlockSpec((1, N_EXPERTS, tm), lambda i, *_: (i, 0, 0))],
            out_specs=pl.BlockSpec(memory_space=pl.ANY),
            scratch_shapes=[pltpu.VMEM((rb, tm), BF16), pltpu.VMEM((tm, d), BF16), pltpu.VMEM((rb, d), BF16),
                            pltpu.VMEM((EXPERT_ROWS, d), BF16), pltpu.SemaphoreType.DMA(())],
        ),
        out_shape=jax.ShapeDtypeStruct((n_rows, d), BF16),
        compiler_params=_cparams("arbitrary"),
        name="moe_dispatch",
    )(plan["tail"], plan["n_act"].reshape(1), plan["grp_e"], plan["gdst"], h2, pe)


def _expert_kernel(blk_e_ref, n_act_ref, x_ref, w1_ref, w3_ref, w2_ref, y_ref):
    del blk_e_ref

    @pl.when(pl.program_id(0) < n_act_ref[0])
    def _():
        x = x_ref[...]
        a = _dot(x, w1_ref[0, 0].astype(BF16))
        act = (a * jax.nn.sigmoid(a)) * _dot(x, w3_ref[0, 0].astype(BF16))
        y_ref[...] = _dot(act.astype(BF16), w2_ref[0, 0].astype(BF16)).astype(BF16)

    @pl.when(pl.program_id(0) >= n_act_ref[0])
    def _():
        y_ref[...] = jnp.zeros_like(y_ref)


def _experts(xs_sorted, blk_e, n_act, lw):
    n_rows, d = xs_sorted.shape
    bm = EXPERT_ROWS

    def row_map(i, blk_e_ref, n_act_ref):
        return (jnp.minimum(i, n_act_ref[0] - 1), 0)

    layer = lw["layer"]

    def w_map(i, blk_e_ref, n_act_ref):
        return (layer, blk_e_ref[i], 0, 0)

    return pl.pallas_call(
        _expert_kernel,
        grid_spec=pltpu.PrefetchScalarGridSpec(
            num_scalar_prefetch=2,
            grid=(n_rows // bm,),
            in_specs=[pl.BlockSpec((bm, d), row_map),
                      pl.BlockSpec((1, 1, d, EXPERT_FF), w_map),
                      pl.BlockSpec((1, 1, d, EXPERT_FF), w_map),
                      pl.BlockSpec((1, 1, EXPERT_FF, d), w_map)],
            out_specs=pl.BlockSpec((bm, d), lambda i, *_: (i, 0)),
        ),
        out_shape=jax.ShapeDtypeStruct((n_rows, d), BF16),
        compiler_params=_cparams("arbitrary"),
        name="moe_experts",
    )(blk_e, n_act, xs_sorted, lw["w1"], lw["w3"], lw["w2"])


def _combine_kernel(nch_ref, grp_e_ref, gsrc_ref, h_ref, x_ref, pe_ref, we_ref, mod_ref, modc_ref, ws13_ref,
                    ws2_ref, ys_ref, xo_ref, sel_sc, yp_sc, acc_sc, sem, *, n_groups, tiles_per_sample, n_ctx):
    i = pl.program_id(0)
    tm = h_ref.shape[0]
    per_chunk = MOE_CHUNK // RUN_ROWS

    def start_chunk(ch):
        for j in range(per_chunk):
            g = ch * per_chunk + j
            pltpu.make_async_copy(
                ys_ref.at[pl.ds(pl.multiple_of(gsrc_ref[0, 0, g] * RUN_ROWS, RUN_ROWS), RUN_ROWS)],
                yp_sc.at[pl.ds(pl.multiple_of(g * RUN_ROWS, RUN_ROWS), RUN_ROWS)], sem).start(priority=j % 2)

    def wait_chunk(ch):
        r0 = pl.multiple_of(ch * MOE_CHUNK, MOE_CHUNK)
        pltpu.make_async_copy(ys_ref.at[pl.ds(0, MOE_CHUNK)], yp_sc.at[pl.ds(r0, MOE_CHUNK)], sem).wait()

    start_chunk(0)

    def build(g, carry):
        prow, rows = _selection_rows(g, grp_e_ref, pe_ref, tm)
        wrow = we_ref[0, pl.ds(grp_e_ref[0, 0, g], 1), :]
        sel_sc[pl.ds(pl.multiple_of(g * RUN_ROWS, RUN_ROWS), RUN_ROWS), :] = jnp.where(
            prow == rows, wrow, 0.0).astype(BF16)
        return carry

    lax.fori_loop(0, n_groups, build, 0, unroll=4)

    up = _dot(h_ref[...].astype(BF16), ws13_ref[...])
    a = up[:, 0:EXPERT_FF]
    act = (a * jax.nn.sigmoid(a)) * up[:, EXPERT_FF:2 * EXPERT_FF]
    acc_sc[...] = _dot(act.astype(BF16), ws2_ref[...])

    def gather_sum(ch):
        r0 = pl.multiple_of(ch * MOE_CHUNK, MOE_CHUNK)
        acc_sc[...] += lax.dot_general(sel_sc[pl.ds(r0, MOE_CHUNK), :], yp_sc[pl.ds(r0, MOE_CHUNK), :], TN_DIMS,
                                       preferred_element_type=F32)

    def step(ch, carry):
        wait_chunk(ch)
        start_chunk(ch + 1)
        gather_sum(ch)
        return carry

    last = nch_ref[i] - 1
    lax.fori_loop(0, last, step, 0)
    wait_chunk(last)
    gather_sum(last)
    r = lax.broadcasted_iota(jnp.int32, (tm, 1), 0)
    is_ctx = jnp.logical_and(i % tiles_per_sample == tiles_per_sample - 1, r >= tm - n_ctx)
    gate = jnp.where(is_ctx, modc_ref[0, 5:6, :], mod_ref[0, 5:6, :])
    xo_ref[...] = x_ref[...] + gate * acc_sc[...]


def _combine(h2, xs, pe, we, plan, mod, ys_sorted, lw, tm, tiles_per_sample, n_ctx, bsz):
    n, d = h2.shape
    nt = n // tm
    rb = _tile_row_bound(tm)
    n_groups = rb // RUN_ROWS

    def smem_spec():
        return pl.BlockSpec((1, 1, n_groups), lambda i, *_: (i, 0, 0), memory_space=pltpu.SMEM)

    def row_spec():
        return pl.BlockSpec((tm, d), lambda i, *_: (i, 0))

    def t_spec():
        return pl.BlockSpec((1, N_EXPERTS, tm), lambda i, *_: (i, 0, 0))

    return pl.pallas_call(
        functools.partial(_combine_kernel, n_groups=n_groups, tiles_per_sample=tiles_per_sample, n_ctx=n_ctx),
        grid_spec=pltpu.PrefetchScalarGridSpec(
            num_scalar_prefetch=1,
            grid=(nt,),
            in_specs=[smem_spec(), smem_spec(), row_spec(), row_spec(), t_spec(), t_spec(),
                      pl.BlockSpec((1, 6, d), lambda i, *_: (i // tiles_per_sample, 0, 0)),
                      pl.BlockSpec((1, 6, d), lambda i, *_: (bsz, 0, 0)),
                      pl.BlockSpec((d, 2 * EXPERT_FF), lambda i, *_: (0, 0), pipeline_mode=pl.Buffered(1)),
                      pl.BlockSpec((EXPERT_FF, d), lambda i, *_: (0, 0), pipeline_mode=pl.Buffered(1)),
                      pl.BlockSpec(memory_space=pl.ANY)],
            out_specs=row_spec(),
            scratch_shapes=[pltpu.VMEM((rb, tm), BF16), pltpu.VMEM((rb, d), BF16), pltpu.VMEM((tm, d), F32),
                            pltpu.SemaphoreType.DMA(())],
        ),
        out_shape=jax.ShapeDtypeStruct((n, d), F32),
        compiler_params=_cparams("arbitrary"),
        name="moe_combine",
    )(plan["n_ch"], plan["grp_e"], plan["gsrc"], h2, xs, pe, we, mod, mod, lw["ws13"], lw["ws2"], ys_sorted)


def _moe_plan(c16, n_groups, spare16):
    nt = c16.shape[0]
    per_blk = EXPERT_ROWS // RUN_ROWS
    o16 = jnp.cumsum(c16, axis=1) - c16
    before16 = jnp.cumsum(c16, axis=0) - c16
    gtot16 = jnp.sum(c16, axis=0)
    gblk = (gtot16 + per_blk - 1) // per_blk
    gend_blk = jnp.cumsum(gblk)
    gstart16 = per_blk * (gend_blk - gblk)
    g = jnp.arange(n_groups, dtype=jnp.int32)
    grp_e = jnp.minimum(jnp.sum((o16 + c16)[:, None, :] <= g[None, :, None], axis=2), N_EXPERTS - 1)
    run0 = gstart16[None, :] + before16 - o16
    experts = jnp.arange(N_EXPERTS, dtype=jnp.int32)
    gdst = jnp.sum(jnp.where(grp_e[:, :, None] == experts[None, None, :], run0[:, None, :], 0), axis=2) + g[None, :]
    q = jnp.arange(per_blk, dtype=jnp.int32)
    tail = jnp.where(q[None, :] < (per_blk * gblk - gtot16)[:, None],
                     (gstart16 + gtot16)[:, None] + q[None, :], -1)
    used = g[None, :] < jnp.sum(c16, axis=1)[:, None]
    return {
        "grp_e": grp_e.astype(jnp.int32).reshape(nt, 1, n_groups),
        "gdst": jnp.where(used, gdst, spare16 + g[None, :]).astype(jnp.int32).reshape(nt, 1, n_groups),
        "gsrc": jnp.where(used, gdst, 0).astype(jnp.int32).reshape(nt, 1, n_groups),
        "n_ch": ((jnp.sum(c16, axis=1) * RUN_ROWS + MOE_CHUNK - 1) // MOE_CHUNK).astype(jnp.int32),
        "tail": tail.astype(jnp.int32).reshape(-1),
        "gend_blk": gend_blk, "n_act": gend_blk[-1].astype(jnp.int32),
    }


def _moe_tile(rows_per_sample):
    return max(t for t in range(ROW_TILE, MOE_TILE_MAX + 1, ROW_TILE) if rows_per_sample % t == 0)


def _moe(h2, xs, mod, lw, rows_per_sample, n_ctx, bsz):
    n, d = h2.shape
    tile = _moe_tile(rows_per_sample)
    tiles_per_sample = rows_per_sample // tile
    nt = n // tile
    n_groups = _tile_row_bound(tile) // RUN_ROWS
    pe, we, c16 = _router(h2, lw, tile)
    worst_rows = n * TOP_K + nt * N_EXPERTS * (RUN_ROWS - 1) + N_EXPERTS * (EXPERT_ROWS - 1)
    run_blocks = -(-worst_rows // EXPERT_ROWS)
    n_blocks = run_blocks + -(-n_groups * RUN_ROWS // EXPERT_ROWS)
    plan = _moe_plan(c16[:, 0, :N_EXPERTS], n_groups, run_blocks * (EXPERT_ROWS // RUN_ROWS))
    blk = jnp.minimum(jnp.arange(n_blocks, dtype=jnp.int32), plan["n_act"] - 1)
    blk_e = jnp.minimum(jnp.sum(plan["gend_blk"][None, :] <= blk[:, None], axis=1), N_EXPERTS - 1).astype(jnp.int32)
    xs_sorted = _dispatch(h2, pe, plan, n_blocks * EXPERT_ROWS, tile)
    ys_sorted = _experts(xs_sorted, blk_e, plan["n_act"].reshape(1), lw)
    return _combine(h2, xs, pe, we, plan, mod, ys_sorted, lw, tile, tiles_per_sample, n_ctx, bsz)


def _slots(w, head_w, real_w):
    rows = w.shape[0]
    w = w.reshape(rows, -1, head_w)[:, :, :real_w]
    return jnp.pad(w, ((0, 0), (0, 0), (0, HEAD_SLOT - real_w))).reshape(rows, -1)


def _block_diag(blocks):
    n, r, c = blocks.shape
    out = jnp.zeros((n * r, n * c), blocks.dtype)
    for i in range(n):
        out = out.at[i * r:(i + 1) * r, i * c:(i + 1) * c].set(blocks[i])
    return out


def _layer_weights(l, p):
    d = p["w_in"].shape[1]
    w_in = p["w_in"][l]
    ml = 4 * BRANCH_W
    o_mla = ml + 4 * ML_HEADS
    o_pool = o_mla + Q_LORA + KV_LORA + QK_ROPE
    w_big = jnp.concatenate([
        w_in[:, :ml],
        w_in[:, o_pool:],
        w_in[:, o_mla:o_mla + Q_LORA + KV_LORA],
        w_in[:, o_mla + Q_LORA + KV_LORA:o_pool],
        w_in[:, ml:o_mla],
        jnp.zeros((d, LANES - QK_ROPE - 4 * ML_HEADS), F32),
    ], axis=1).astype(BF16)
    w_ukv = p["mla_w_ukv"][l].reshape(KV_LORA, MLA_HEADS, QK_NOPE + V_DIM)

    def gain_slots(g):
        return jnp.tile(jnp.pad(g, (0, HEAD_SLOT - QK_DIM)), MLA_HEADS)[None, :]

    return {
        "g_mix": p["g_mix"][l][None, :], "g_ffn": p["g_ffn"][l][None, :],
        "w_big": w_big,
        "g_cq": p["mla_g_cq"][l][None, :], "g_ckv": p["mla_g_ckv"][l][None, :],
        "w_uq": _slots(p["mla_w_uq"][l], QK_DIM, QK_DIM).astype(BF16),
        "w_uk": _slots(w_ukv[:, :, :QK_NOPE].reshape(KV_LORA, -1), QK_NOPE, QK_NOPE).astype(BF16),
        "w_uv": w_ukv[:, :, QK_NOPE:].reshape(KV_LORA, BRANCH_W).astype(BF16),
        "g_qn": gain_slots(p["mla_g_qn"][l]), "g_kn": gain_slots(p["mla_g_kn"][l]),
        "gate_bias": jnp.pad(p["ml_gate_bias"][l], (QK_ROPE, LANES - QK_ROPE - 4 * ML_HEADS))[None, :],
        "head_gain": jnp.broadcast_to(p["ml_head_gain"][l][:, None], (BRANCH_W, ROW_TILE)),
        "pool_w": _block_diag(p["pool_w"][l]).astype(BF16),
        "pool_scale": p["pool_scale"][l][None, :],
        "conv_w": p["conv_w"][l], "conv_b": p["conv_b"][l][None, :],
        "w_gate": jnp.concatenate(list(p["w_gate"][l]), axis=1).astype(BF16),
        "b_gate": p["b_gate"][l].reshape(1, -1),
        "w_branch": p["w_branch"][l].astype(BF16),
        "w_out": p["w_out"][l].astype(BF16),
        "w_router": jnp.pad(p["moe_w_router"][l], ((0, 0), (0, LANES - N_EXPERTS))).astype(BF16),
        "b_router": jnp.pad(p["moe_b_router"][l], (0, LANES - N_EXPERTS), constant_values=-1e30)[None, :],
        "layer": l, "w1": p["moe_w1"], "w3": p["moe_w3"], "w2": p["moe_w2"],
        "ws13": jnp.concatenate([p["moe_ws1"][l], p["moe_ws3"][l]], axis=1).astype(BF16),
        "ws2": p["moe_ws2"][l].astype(BF16),
    }


def _rope_tables(n_lat, n_ctx):
    t = jnp.arange(n_lat)
    n_freq = QK_ROPE // 4
    inv = ROPE_THETA ** (-jnp.arange(n_freq, dtype=F32) / n_freq)
    ang_r = (t // GRID_W).astype(F32)[:, None] * inv
    ang_c = (t % GRID_W).astype(F32)[:, None] * inv
    cos4 = jnp.concatenate([jnp.cos(ang_r)] * 2 + [jnp.cos(ang_c)] * 2, axis=1)
    zero = jnp.zeros_like(ang_r)
    sin_a = jnp.concatenate([-jnp.sin(ang_r), zero, -jnp.sin(ang_c), zero], axis=1)
    sin_b = jnp.concatenate([zero, jnp.sin(ang_r), zero, jnp.sin(ang_c)], axis=1)

    def slot(a, fill):
        a = jnp.pad(a, ((0, 0), (QK_NOPE, 0)), constant_values=fill)
        a = jnp.pad(a, ((0, 0), (0, HEAD_SLOT - QK_DIM)), constant_values=fill)
        return jnp.pad(a, ((0, n_ctx), (0, 0)), constant_values=fill)

    return {"cos": slot(cos4, 1.0), "sin_a": slot(sin_a, 0.0), "sin_b": slot(sin_b, 0.0),
            "ones_slot": _block_diag(jnp.ones((MLA_HEADS, HEAD_SLOT, HEAD_SLOT), BF16))}


def kernel(x, c, ctx, c_ctx, w_mod, b_mod, g_mix, g_ffn, w_in, ml_gate_bias, ml_head_gain, mla_g_cq, mla_g_ckv,
           mla_w_uq, mla_w_ukv, mla_g_qn, mla_g_kn, pool_w, pool_scale, conv_w, conv_b, w_gate, b_gate, w_branch,
           w_out, moe_w_router, moe_b_router, moe_w1, moe_w3, moe_w2, moe_ws1, moe_ws3, moe_ws2):
    p = dict(g_mix=g_mix, g_ffn=g_ffn, w_in=w_in, ml_gate_bias=ml_gate_bias, ml_head_gain=ml_head_gain,
             mla_g_cq=mla_g_cq, mla_g_ckv=mla_g_ckv, mla_w_uq=mla_w_uq, mla_w_ukv=mla_w_ukv, mla_g_qn=mla_g_qn,
             mla_g_kn=mla_g_kn, pool_w=pool_w, pool_scale=pool_scale, conv_w=conv_w, conv_b=conv_b,
             w_gate=w_gate, b_gate=b_gate, w_branch=w_branch, w_out=w_out, moe_w_router=moe_w_router,
             moe_b_router=moe_b_router, moe_w1=moe_w1, moe_w3=moe_w3, moe_w2=moe_w2, moe_ws1=moe_ws1,
             moe_ws3=moe_ws3, moe_ws2=moe_ws2)
    bsz, n_lat, d = x.shape
    n_ctx = ctx.shape[1]
    depth = w_mod.shape[0]
    s = n_lat + n_ctx
    assert n_ctx == ROW_TILE == ATT_TQ and n_lat % ROW_TILE == 0 and n_lat % GRID_W == 0
    n_lat_tiles = n_lat // ROW_TILE

    mod_rows = -(-(bsz + 1) // SUBLANES) * SUBLANES
    cc = jnp.concatenate([c, c_ctx[None, :], jnp.zeros((mod_rows - bsz - 1, d), F32)], axis=0)
    mods = _modulation(cc, w_mod, b_mod).reshape(depth, mod_rows, 6, d)
    rope = _rope_tables(n_lat, n_ctx)
    xs = jnp.concatenate([x, ctx], axis=1)

    for l in range(depth):
        lw = _layer_weights(l, p)
        mod = mods[l]
        hx, mq, mk, mv, o, g, pc, q, k, v = _in_proj(xs, mod, lw, rope, n_lat_tiles)
        hf, hb = _mlstm(mq, mk, mv, g, n_lat // ML_CHUNK)
        y_mla = _attention(q, k, v, n_lat)
        with_ctx = l + 1 < depth
        rows = s if with_ctx else n_lat
        xs, h2 = _merge(hx, hf, hb, o, y_mla, pc, xs, mod, lw, n_lat_tiles, with_ctx)
        xs = _moe(h2.reshape(bsz * rows, d), xs.reshape(bsz * rows, d), mod, lw, rows, n_ctx if with_ctx else 0,
                  bsz).reshape(bsz, rows, d)
    return xs
```
